```python
import jax, jax.numpy as jnp
from jax import lax
import numpy as np

D_MODEL = 1024
BATCH = 8
SEQ = 8192
DEPTH = 2

CHUNK = 64
MEM_LEN = 256
FOX_HEADS = 8
FOX_HEAD_DIM = 64
FOX_WIDTH = FOX_HEADS * FOX_HEAD_DIM
POOL_WIDTH = D_MODEL - FOX_WIDTH
POOL_WINDOWS = (2, 4, 8, 16)
POOL_GROUPS = len(POOL_WINDOWS)
POOL_GROUP_DIM = POOL_WIDTH // POOL_GROUPS
IN_COLS = 3 * FOX_WIDTH + POOL_WIDTH + FOX_HEADS
MEM_HEADS = 4
MEM_HEAD_DIM = 128
MEM_WIDTH = MEM_HEADS * MEM_HEAD_DIM
D_FF = ((-(-8 * D_MODEL // 3) + 255) // 256) * 256
Q_BLOCK = 128
EPS = 1e-6

kernel_name = "fox_pool_hybrid_encoder"


def rmsnorm(x, g):
    xf = x.astype(jnp.float32)
    y = xf * lax.rsqrt(jnp.mean(xf * xf, axis=-1, keepdims=True) + EPS)
    return (y * g.astype(jnp.float32)).astype(x.dtype)


def forgetting_attention(q, k, v, f_logit):
    c = jnp.cumsum(jax.nn.log_sigmoid(f_logit.astype(jnp.float32)), axis=-1)
    scale = FOX_HEAD_DIM ** -0.5
    S = q.shape[2]
    outs = []
    for start in range(0, S, Q_BLOCK):
        end = start + Q_BLOCK
        qb = q[:, :, start:end]
        kb = k[:, :, :end]
        vb = v[:, :, :end]
        s = jnp.einsum('bhqd,bhkd->bhqk', qb, kb, preferred_element_type=jnp.float32) * scale
        s = s + c[:, :, start:end, None] - c[:, :, None, :end]
        qpos = start + jnp.arange(Q_BLOCK)
        kpos = jnp.arange(end)
        s = jnp.where(kpos[None, :] <= qpos[:, None], s, -jnp.inf)
        p = jax.nn.softmax(s, axis=-1)
        outs.append(jnp.einsum('bhqk,bhkd->bhqd', p.astype(vb.dtype), vb))
    return jnp.concatenate(outs, axis=2)


def multiscale_pool(p, w_pool, pool_scale):
    B, S, _ = p.shape
    pf = p.astype(jnp.float32).reshape(B, S, POOL_GROUPS, POOL_GROUP_DIM)
    cs = jnp.concatenate([jnp.zeros((B, 1, POOL_GROUPS, POOL_GROUP_DIM), jnp.float32),
                          jnp.cumsum(pf, axis=1)], axis=1)
    t = jnp.arange(S)
    outs = []
    for g, w in enumerate(POOL_WINDOWS):
        csg = cs[:, :, g]
        lower = jnp.pad(csg, ((0, 0), (w - 1, 0), (0, 0)))[:, :S]
        cnt = jnp.minimum(t + 1, w).astype(jnp.float32)[None, :, None]
        mixed = (csg[:, 1:] - lower) / cnt - pf[:, :, g]
        outs.append(jnp.einsum('bsc,cd->bsd', mixed, w_pool[g].astype(jnp.float32)))
    y = jnp.concatenate(outs, axis=-1) * pool_scale.astype(jnp.float32)
    return y.astype(p.dtype)


def head_split(t, n_heads, head_dim):
    B, S, _ = t.shape
    return t.reshape(B, S, n_heads, head_dim)


def _fwd_setup_inputs(seed: int = 0) -> dict:
    key = jax.random.key(seed)
    ks = jax.random.split(key, 24)
    f32 = jnp.float32
    D = D_MODEL

    def nrm(k, shape, fan_in):
        return jax.random.normal(k, shape, f32) * (fan_in ** -0.5)

    def gain(k, shape):
        return 1.0 + 0.05 * jax.random.normal(k, shape, f32)

    return {
        "x": jax.random.normal(ks[0], (BATCH, SEQ, D), f32),
        "mem": jax.random.normal(ks[1], (BATCH, MEM_LEN, D), f32),
        "g_mix": gain(ks[2], (DEPTH, D)),
        "w_in": nrm(ks[3], (DEPTH, D, IN_COLS), D),
        "b_forget": 3.0 + 0.5 * jax.random.normal(ks[4], (DEPTH, FOX_HEADS), f32),
        "g_q_fox": gain(ks[5], (DEPTH, FOX_HEAD_DIM)),
        "g_k_fox": gain(ks[6], (DEPTH, FOX_HEAD_DIM)),
        "w_pool": nrm(ks[7], (DEPTH, POOL_GROUPS, POOL_GROUP_DIM, POOL_GROUP_DIM), POOL_GROUP_DIM),
        "pool_scale": gain(ks[8], (DEPTH, POOL_WIDTH)),
        "w_out": nrm(ks[9], (DEPTH, D, D), D),
        "g_mem_q": gain(ks[10], (DEPTH, D)),
        "g_mem_kv": gain(ks[11], (DEPTH, D)),
        "w_mem_q": nrm(ks[12], (DEPTH, D, MEM_WIDTH), D),
        "w_mem_kv": nrm(ks[13], (DEPTH, D, 2 * MEM_WIDTH), D),
        "g_q_mem": gain(ks[14], (DEPTH, MEM_HEAD_DIM)),
        "g_k_mem": gain(ks[15], (DEPTH, MEM_HEAD_DIM)),
        "w_mem_out": nrm(ks[16], (DEPTH, MEM_WIDTH, D), MEM_WIDTH),
        "g_ffn": gain(ks[17], (DEPTH, D)),
        "w_gate_up": nrm(ks[18], (DEPTH, D, 2 * D_FF), D),
        "w_down": nrm(ks[19], (DEPTH, D_FF, D), D_FF),
    }


def _fwd_reference(x, mem, g_mix, w_in, b_forget, g_q_fox, g_k_fox, w_pool, pool_scale, w_out,
              g_mem_q, g_mem_kv, w_mem_q, w_mem_kv, g_q_mem, g_k_mem, w_mem_out,
              g_ffn, w_gate_up, w_down):
    B, S, _ = x.shape
    h = x
    for l in range(DEPTH):
        xn = rmsnorm(h, g_mix[l])
        z = jnp.einsum('bsd,dc->bsc', xn, w_in[l])
        q = z[..., :FOX_WIDTH]
        k = z[..., FOX_WIDTH:2 * FOX_WIDTH]
        v = z[..., 2 * FOX_WIDTH:3 * FOX_WIDTH]
        p_in = z[..., 3 * FOX_WIDTH:3 * FOX_WIDTH + POOL_WIDTH]
        f_logit = z[..., 3 * FOX_WIDTH + POOL_WIDTH:] + b_forget[l]
        q = rmsnorm(head_split(q, FOX_HEADS, FOX_HEAD_DIM), g_q_fox[l]).transpose(0, 2, 1, 3)
        k = rmsnorm(head_split(k, FOX_HEADS, FOX_HEAD_DIM), g_k_fox[l]).transpose(0, 2, 1, 3)
        v = head_split(v, FOX_HEADS, FOX_HEAD_DIM).transpose(0, 2, 1, 3)
        fox = forgetting_attention(q, k, v, f_logit.transpose(0, 2, 1))
        fox = fox.transpose(0, 2, 1, 3).reshape(B, S, FOX_WIDTH)
        pool = multiscale_pool(p_in, w_pool[l], pool_scale[l])
        h = h + jnp.einsum('bsc,cd->bsd', jnp.concatenate([fox, pool], axis=-1), w_out[l])

        hn = rmsnorm(h, g_mem_q[l])
        mn = rmsnorm(mem, g_mem_kv[l])
        mq = rmsnorm(head_split(jnp.einsum('bsd,dc->bsc', hn, w_mem_q[l]), MEM_HEADS, MEM_HEAD_DIM), g_q_mem[l])
        mkv = jnp.einsum('bmd,dc->bmc', mn, w_mem_kv[l])
        mk = rmsnorm(head_split(mkv[..., :MEM_WIDTH], MEM_HEADS, MEM_HEAD_DIM), g_k_mem[l])
        mv = head_split(mkv[..., MEM_WIDTH:], MEM_HEADS, MEM_HEAD_DIM)
        sc = jnp.einsum('bshd,bmhd->bhsm', mq, mk, preferred_element_type=jnp.float32) * (MEM_HEAD_DIM ** -0.5)
        pm = jax.nn.softmax(sc, axis=-1).astype(mv.dtype)
        mo = jnp.einsum('bhsm,bmhd->bshd', pm, mv).reshape(B, S, MEM_WIDTH)
        h = h + jnp.einsum('bsc,cd->bsd', mo, w_mem_out[l])

        hn = rmsnorm(h, g_ffn[l])
        gu = jnp.einsum('bsd,df->bsf', hn, w_gate_up[l])
        act = jax.nn.silu(gu[..., :D_FF]) * gu[..., D_FF:]
        h = h + jnp.einsum('bsf,fd->bsd', act, w_down[l])
    return h


import jax as _jax
import jax.numpy as _jnp

TWIN_FORMAT = 'train_step'
FWD_PARAMS = ['x', 'mem', 'g_mix', 'w_in', 'b_forget', 'g_q_fox', 'g_k_fox', 'w_pool', 'pool_scale', 'w_out', 'g_mem_q', 'g_mem_kv', 'w_mem_q', 'w_mem_kv', 'g_q_mem', 'g_k_mem', 'w_mem_out', 'g_ffn', 'w_gate_up', 'w_down']
TWIN_WEIGHTS = ['g_mix', 'w_in', 'b_forget', 'g_q_fox', 'g_k_fox', 'w_pool', 'pool_scale', 'w_out', 'g_mem_q', 'g_mem_kv', 'w_mem_q', 'w_mem_kv', 'g_q_mem', 'g_k_mem', 'w_mem_out', 'g_ffn', 'w_gate_up', 'w_down']
TWIN_DIFF_INPUT = 'x'
TWIN_INPUTS = ['x', 'mem', 'g_mix', 'w_in', 'b_forget', 'g_q_fox', 'g_k_fox', 'w_pool', 'pool_scale', 'w_out', 'g_mem_q', 'g_mem_kv', 'w_mem_q', 'w_mem_kv', 'g_q_mem', 'g_k_mem', 'w_mem_out', 'g_ffn', 'w_gate_up', 'w_down', 'loss_target', 'm_g_mix', 'm_w_in', 'm_b_forget', 'm_g_q_fox', 'm_g_k_fox', 'm_w_pool', 'm_pool_scale', 'm_w_out', 'm_g_mem_q', 'm_g_mem_kv', 'm_w_mem_q', 'm_w_mem_kv', 'm_g_q_mem', 'm_g_k_mem', 'm_w_mem_out', 'm_g_ffn', 'm_w_gate_up', 'm_w_down', 'v_g_mix', 'v_w_in', 'v_b_forget', 'v_g_q_fox', 'v_g_k_fox', 'v_w_pool', 'v_pool_scale', 'v_w_out', 'v_g_mem_q', 'v_g_mem_kv', 'v_w_mem_q', 'v_w_mem_kv', 'v_g_q_mem', 'v_g_k_mem', 'v_w_mem_out', 'v_g_ffn', 'v_w_gate_up', 'v_w_down']
TWIN_OUTPUTS = ['loss', 'grad_x', 'grad_g_mix', 'grad_w_in', 'grad_b_forget', 'grad_g_q_fox', 'grad_g_k_fox', 'grad_w_pool', 'grad_pool_scale', 'grad_w_out', 'grad_g_mem_q', 'grad_g_mem_kv', 'grad_w_mem_q', 'grad_w_mem_kv', 'grad_g_q_mem', 'grad_g_k_mem', 'grad_w_mem_out', 'grad_g_ffn', 'grad_w_gate_up', 'grad_w_down', 'delta_g_mix', 'delta_w_in', 'delta_b_forget', 'delta_g_q_fox', 'delta_g_k_fox', 'delta_w_pool', 'delta_pool_scale', 'delta_w_out', 'delta_g_mem_q', 'delta_g_mem_kv', 'delta_w_mem_q', 'delta_w_mem_kv', 'delta_g_q_mem', 'delta_g_k_mem', 'delta_w_mem_out', 'delta_g_ffn', 'delta_w_gate_up', 'delta_w_down', 'new_m_g_mix', 'new_m_w_in', 'new_m_b_forget', 'new_m_g_q_fox', 'new_m_g_k_fox', 'new_m_w_pool', 'new_m_pool_scale', 'new_m_w_out', 'new_m_g_mem_q', 'new_m_g_mem_kv', 'new_m_w_mem_q', 'new_m_w_mem_kv', 'new_m_g_q_mem', 'new_m_g_k_mem', 'new_m_w_mem_out', 'new_m_g_ffn', 'new_m_w_gate_up', 'new_m_w_down', 'new_v_g_mix', 'new_v_w_in', 'new_v_b_forget', 'new_v_g_q_fox', 'new_v_g_k_fox', 'new_v_w_pool', 'new_v_pool_scale', 'new_v_w_out', 'new_v_g_mem_q', 'new_v_g_mem_kv', 'new_v_w_mem_q', 'new_v_w_mem_kv', 'new_v_g_q_mem', 'new_v_g_k_mem', 'new_v_w_mem_out', 'new_v_g_ffn', 'new_v_w_gate_up', 'new_v_w_down']
TWIN_LEAF_KINDS = {'loss': 'loss', 'grad_x': 'grad_x', 'grad_g_mix': 'grad_w', 'grad_w_in': 'grad_w', 'grad_b_forget': 'grad_w', 'grad_g_q_fox': 'grad_w', 'grad_g_k_fox': 'grad_w', 'grad_w_pool': 'grad_w', 'grad_pool_scale': 'grad_w', 'grad_w_out': 'grad_w', 'grad_g_mem_q': 'grad_w', 'grad_g_mem_kv': 'grad_w', 'grad_w_mem_q': 'grad_w', 'grad_w_mem_kv': 'grad_w', 'grad_g_q_mem': 'grad_w', 'grad_g_k_mem': 'grad_w', 'grad_w_mem_out': 'grad_w', 'grad_g_ffn': 'grad_w', 'grad_w_gate_up': 'grad_w', 'grad_w_down': 'grad_w', 'delta_g_mix': 'delta_w', 'delta_w_in': 'delta_w', 'delta_b_forget': 'delta_w', 'delta_g_q_fox': 'delta_w', 'delta_g_k_fox': 'delta_w', 'delta_w_pool': 'delta_w', 'delta_pool_scale': 'delta_w', 'delta_w_out': 'delta_w', 'delta_g_mem_q': 'delta_w', 'delta_g_mem_kv': 'delta_w', 'delta_w_mem_q': 'delta_w', 'delta_w_mem_kv': 'delta_w', 'delta_g_q_mem': 'delta_w', 'delta_g_k_mem': 'delta_w', 'delta_w_mem_out': 'delta_w', 'delta_g_ffn': 'delta_w', 'delta_w_gate_up': 'delta_w', 'delta_w_down': 'delta_w', 'new_m_g_mix': 'new_m', 'new_m_w_in': 'new_m', 'new_m_b_forget': 'new_m', 'new_m_g_q_fox': 'new_m', 'new_m_g_k_fox': 'new_m', 'new_m_w_pool': 'new_m', 'new_m_pool_scale': 'new_m', 'new_m_w_out': 'new_m', 'new_m_g_mem_q': 'new_m', 'new_m_g_mem_kv': 'new_m', 'new_m_w_mem_q': 'new_m', 'new_m_w_mem_kv': 'new_m', 'new_m_g_q_mem': 'new_m', 'new_m_g_k_mem': 'new_m', 'new_m_w_mem_out': 'new_m', 'new_m_g_ffn': 'new_m', 'new_m_w_gate_up': 'new_m', 'new_m_w_down': 'new_m', 'new_v_g_mix': 'new_v', 'new_v_w_in': 'new_v', 'new_v_b_forget': 'new_v', 'new_v_g_q_fox': 'new_v', 'new_v_g_k_fox': 'new_v', 'new_v_w_pool': 'new_v', 'new_v_pool_scale': 'new_v', 'new_v_w_out': 'new_v', 'new_v_g_mem_q': 'new_v', 'new_v_g_mem_kv': 'new_v', 'new_v_w_mem_q': 'new_v', 'new_v_w_mem_kv': 'new_v', 'new_v_g_q_mem': 'new_v', 'new_v_g_k_mem': 'new_v', 'new_v_w_mem_out': 'new_v', 'new_v_g_ffn': 'new_v', 'new_v_w_gate_up': 'new_v', 'new_v_w_down': 'new_v'}


def _forward(args):
    return _fwd_reference(*[args[k] for k in FWD_PARAMS])


def _output_shape():
    out = _jax.eval_shape(lambda: _forward(_fwd_setup_inputs(0)))
    return out.shape, out.dtype

N_MICROBATCH = 1
ADAM_LR = 0.001
ADAM_B1 = 0.9
ADAM_B2 = 0.999
ADAM_EPS = 1e-08
ADAM_WD = 0.01
ADAM_STEP = 10
PER_EXAMPLE_BATCH_AXIS = {'x': 0, 'mem': 0, 'loss_target': 0}
SHARED_INPUTS = []
_WEIGHT_DTYPES = {'g_mix': _jnp.float32, 'w_in': _jnp.float32, 'b_forget': _jnp.float32, 'g_q_fox': _jnp.float32, 'g_k_fox': _jnp.float32, 'w_pool': _jnp.float32, 'pool_scale': _jnp.float32, 'w_out': _jnp.float32, 'g_mem_q': _jnp.float32, 'g_mem_kv': _jnp.float32, 'w_mem_q': _jnp.float32, 'w_mem_kv': _jnp.float32, 'g_q_mem': _jnp.float32, 'g_k_mem': _jnp.float32, 'w_mem_out': _jnp.float32, 'g_ffn': _jnp.float32, 'w_gate_up': _jnp.float32, 'w_down': _jnp.float32}
MOMENT_SCALE = {'g_mix': 2.772891e+01, 'w_in': 1.117086e+00, 'b_forget': 1.298659e+02, 'g_q_fox': 2.308936e+01, 'g_k_fox': 2.281044e+01, 'w_pool': 4.309891e+00, 'pool_scale': 5.106340e+01, 'w_out': 1.710665e+00, 'g_mem_q': 1.148565e-01, 'g_mem_kv': 6.447019e-01, 'w_mem_q': 1.574313e-01, 'w_mem_kv': 3.020883e-01, 'g_q_mem': 4.906012e+00, 'g_k_mem': 4.924720e+00, 'w_mem_out': 2.933668e-01, 'g_ffn': 4.947780e+01, 'w_gate_up': 3.385587e-01, 'w_down': 5.829368e-01}


def _to_microbatches(a, axis):
    t = _jnp.moveaxis(a, axis, 0)
    t = t.reshape((N_MICROBATCH, t.shape[0] // N_MICROBATCH) + t.shape[1:])
    return _jnp.moveaxis(t, 1, axis + 1)


def setup_inputs(seed: int = 0) -> dict:
    inp = _fwd_setup_inputs(seed)
    key = _jax.random.fold_in(_jax.random.key(seed), 7919)
    shape, _ = _output_shape()
    out = dict(inp)
    out["loss_target"] = _jax.random.normal(_jax.random.fold_in(key, 0), shape, _jnp.float32)
    for i, name in enumerate(TWIN_WEIGHTS):
        w = inp[name].astype(_jnp.float32)
        if MOMENT_SCALE is None:
            s = _jnp.sqrt(_jnp.mean(_jnp.square(w)) + 1e-30)
        else:
            s = MOMENT_SCALE[name]
        km, kv = _jax.random.split(_jax.random.fold_in(key, i + 1))
        out[name] = w
        out["m_" + name] = s * _jax.random.normal(km, w.shape, _jnp.float32)
        out["v_" + name] = (s * s) * _jax.random.uniform(kv, w.shape, _jnp.float32, 0.5, 1.5)
    if N_MICROBATCH > 1:
        for name, axis in PER_EXAMPLE_BATCH_AXIS.items():
            out[name] = _to_microbatches(out[name], axis)
    return {'x': out['x'], 'mem': out['mem'], 'g_mix': out['g_mix'], 'w_in': out['w_in'], 'b_forget': out['b_forget'], 'g_q_fox': out['g_q_fox'], 'g_k_fox': out['g_k_fox'], 'w_pool': out['w_pool'], 'pool_scale': out['pool_scale'], 'w_out': out['w_out'], 'g_mem_q': out['g_mem_q'], 'g_mem_kv': out['g_mem_kv'], 'w_mem_q': out['w_mem_q'], 'w_mem_kv': out['w_mem_kv'], 'g_q_mem': out['g_q_mem'], 'g_k_mem': out['g_k_mem'], 'w_mem_out': out['w_mem_out'], 'g_ffn': out['g_ffn'], 'w_gate_up': out['w_gate_up'], 'w_down': out['w_down'], 'loss_target': out['loss_target'], 'm_g_mix': out['m_g_mix'], 'm_w_in': out['m_w_in'], 'm_b_forget': out['m_b_forget'], 'm_g_q_fox': out['m_g_q_fox'], 'm_g_k_fox': out['m_g_k_fox'], 'm_w_pool': out['m_w_pool'], 'm_pool_scale': out['m_pool_scale'], 'm_w_out': out['m_w_out'], 'm_g_mem_q': out['m_g_mem_q'], 'm_g_mem_kv': out['m_g_mem_kv'], 'm_w_mem_q': out['m_w_mem_q'], 'm_w_mem_kv': out['m_w_mem_kv'], 'm_g_q_mem': out['m_g_q_mem'], 'm_g_k_mem': out['m_g_k_mem'], 'm_w_mem_out': out['m_w_mem_out'], 'm_g_ffn': out['m_g_ffn'], 'm_w_gate_up': out['m_w_gate_up'], 'm_w_down': out['m_w_down'], 'v_g_mix': out['v_g_mix'], 'v_w_in': out['v_w_in'], 'v_b_forget': out['v_b_forget'], 'v_g_q_fox': out['v_g_q_fox'], 'v_g_k_fox': out['v_g_k_fox'], 'v_w_pool': out['v_w_pool'], 'v_pool_scale': out['v_pool_scale'], 'v_w_out': out['v_w_out'], 'v_g_mem_q': out['v_g_mem_q'], 'v_g_mem_kv': out['v_g_mem_kv'], 'v_w_mem_q': out['v_w_mem_q'], 'v_w_mem_kv': out['v_w_mem_kv'], 'v_g_q_mem': out['v_g_q_mem'], 'v_g_k_mem': out['v_g_k_mem'], 'v_w_mem_out': out['v_w_mem_out'], 'v_g_ffn': out['v_g_ffn'], 'v_w_gate_up': out['v_w_gate_up'], 'v_w_down': out['v_w_down']}


def _loss(weights, diff, rest, loss_target):
    with _jax.named_scope("forward"):
        args = {**rest, TWIN_DIFF_INPUT: diff, **{k: w.astype(_WEIGHT_DTYPES[k]) for k, w in weights.items()}}
        y = _forward(args)
    with _jax.named_scope("loss_head"):
        err = _jnp.square(y.astype(_jnp.float32) - loss_target)
        return 0.5 * _jnp.sum(_jnp.mean(err, axis=-1)) if err.ndim else 0.5 * err


def _adamw(w, g, m, v):
    m = ADAM_B1 * m + (1.0 - ADAM_B1) * g
    v = ADAM_B2 * v + (1.0 - ADAM_B2) * _jnp.square(g)
    m_hat = m / (1.0 - ADAM_B1 ** ADAM_STEP)
    v_hat = v / (1.0 - ADAM_B2 ** ADAM_STEP)
    delta = -ADAM_LR * (m_hat / (_jnp.sqrt(v_hat) + ADAM_EPS) + ADAM_WD * w)
    return delta, m, v


def reference(x, mem, g_mix, w_in, b_forget, g_q_fox, g_k_fox, w_pool, pool_scale, w_out, g_mem_q, g_mem_kv, w_mem_q, w_mem_kv, g_q_mem, g_k_mem, w_mem_out, g_ffn, w_gate_up, w_down, loss_target, m_g_mix, m_w_in, m_b_forget, m_g_q_fox, m_g_k_fox, m_w_pool, m_pool_scale, m_w_out, m_g_mem_q, m_g_mem_kv, m_w_mem_q, m_w_mem_kv, m_g_q_mem, m_g_k_mem, m_w_mem_out, m_g_ffn, m_w_gate_up, m_w_down, v_g_mix, v_w_in, v_b_forget, v_g_q_fox, v_g_k_fox, v_w_pool, v_pool_scale, v_w_out, v_g_mem_q, v_g_mem_kv, v_w_mem_q, v_w_mem_kv, v_g_q_mem, v_g_k_mem, v_w_mem_out, v_g_ffn, v_w_gate_up, v_w_down):
    given = dict(x=x, mem=mem, g_mix=g_mix, w_in=w_in, b_forget=b_forget, g_q_fox=g_q_fox, g_k_fox=g_k_fox, w_pool=w_pool, pool_scale=pool_scale, w_out=w_out, g_mem_q=g_mem_q, g_mem_kv=g_mem_kv, w_mem_q=w_mem_q, w_mem_kv=w_mem_kv, g_q_mem=g_q_mem, g_k_mem=g_k_mem, w_mem_out=w_mem_out, g_ffn=g_ffn, w_gate_up=w_gate_up, w_down=w_down, loss_target=loss_target, m_g_mix=m_g_mix, m_w_in=m_w_in, m_b_forget=m_b_forget, m_g_q_fox=m_g_q_fox, m_g_k_fox=m_g_k_fox, m_w_pool=m_w_pool, m_pool_scale=m_pool_scale, m_w_out=m_w_out, m_g_mem_q=m_g_mem_q, m_g_mem_kv=m_g_mem_kv, m_w_mem_q=m_w_mem_q, m_w_mem_kv=m_w_mem_kv, m_g_q_mem=m_g_q_mem, m_g_k_mem=m_g_k_mem, m_w_mem_out=m_w_mem_out, m_g_ffn=m_g_ffn, m_w_gate_up=m_w_gate_up, m_w_down=m_w_down, v_g_mix=v_g_mix, v_w_in=v_w_in, v_b_forget=v_b_forget, v_g_q_fox=v_g_q_fox, v_g_k_fox=v_g_k_fox, v_w_pool=v_w_pool, v_pool_scale=v_pool_scale, v_w_out=v_w_out, v_g_mem_q=v_g_mem_q, v_g_mem_kv=v_g_mem_kv, v_w_mem_q=v_w_mem_q, v_w_mem_kv=v_w_mem_kv, v_g_q_mem=v_g_q_mem, v_g_k_mem=v_g_k_mem, v_w_mem_out=v_w_mem_out, v_g_ffn=v_g_ffn, v_w_gate_up=v_w_gate_up, v_w_down=v_w_down)
    weights = {n: given[n] for n in TWIN_WEIGHTS}
    shared = {n: given[n] for n in SHARED_INPUTS}
    per_example = {n: given[n] for n in ['x', 'mem']}
    grad_fn = _jax.value_and_grad(_loss, argnums=(0, 1))

    def one_microbatch(ex, loss_target):
        ex = dict(ex)
        diff = ex.pop(TWIN_DIFF_INPUT)
        return grad_fn(weights, diff, {**shared, **ex}, loss_target)

    if N_MICROBATCH == 1:
        loss, (grad_w, grad_x) = one_microbatch(per_example, given["loss_target"])
    else:
        def body(carry, xs):
            loss_sum, grad_sum = carry
            l_k, (gw_k, gx_k) = one_microbatch(xs[0], xs[1])
            with _jax.named_scope("update"):
                return (loss_sum + l_k, _jax.tree.map(_jnp.add, grad_sum, gw_k)), gx_k

        init = (_jnp.zeros((), _jnp.float32), _jax.tree.map(_jnp.zeros_like, weights))
        (loss, grad_w), grad_x = _jax.lax.scan(body, init, (per_example, given["loss_target"]))
    with _jax.named_scope("update"):
        delta_w, new_m, new_v = {}, {}, {}
        for n in TWIN_WEIGHTS:
            delta_w[n], new_m[n], new_v[n] = _adamw(weights[n], grad_w[n], given["m_" + n], given["v_" + n])
    return (loss, grad_x, *[grad_w[n] for n in TWIN_WEIGHTS], *[delta_w[n] for n in TWIN_WEIGHTS],
            *[new_m[n] for n in TWIN_WEIGHTS], *[new_v[n] for n in TWIN_WEIGHTS])
```

```python
import functools

import jax
import jax.numpy as jnp
from jax import lax
from jax.experimental import pallas as pl
from jax.experimental.pallas import tpu as pltpu

_F32 = jnp.float32
_MXU = jnp.bfloat16

D_MODEL = 1024
DEPTH = 2
FOX_HEADS = 8
FOX_HEAD_DIM = 64
FOX_WIDTH = FOX_HEADS * FOX_HEAD_DIM
POOL_WINDOWS = (2, 4, 8, 16)
POOL_GROUP_DIM = 128
POOL_WIDTH = len(POOL_WINDOWS) * POOL_GROUP_DIM
MEM_HEADS = 4
MEM_HEAD_DIM = 128
MEM_WIDTH = MEM_HEADS * MEM_HEAD_DIM
D_FF = 2816
EPS = 1e-6
LANES = 128
HALO = 16

ADAM_LR = 0.001
ADAM_B1 = 0.9
ADAM_B2 = 0.999
ADAM_EPS = 1e-08
ADAM_WD = 0.01
ADAM_STEP = 10

N_CHIPS = 4
MESH = pl.DeviceIdType.MESH

_TM = 256
_TQ = 1024
_TMF = 512
_TF = 256
_TT = 512
_TB = 256
_TA = 512
_TH = 384

BIG = ("w_in", "w_out", "w_mem_q", "w_mem_kv", "w_mem_out", "w_gate_up", "w_down")
SHARD_AXIS = {"w_in": 2, "w_out": 1, "w_mem_q": 1, "w_mem_kv": 1, "w_mem_out": 2, "w_gate_up": 2, "w_down": 1}
SMALL = ("g_mix", "b_forget", "g_q_fox", "g_k_fox", "w_pool", "pool_scale", "g_mem_q", "g_mem_kv", "g_q_mem",
         "g_k_mem", "g_ffn")
WEIGHTS = ("g_mix", "w_in", "b_forget", "g_q_fox", "g_k_fox", "w_pool", "pool_scale", "w_out", "g_mem_q", "g_mem_kv",
           "w_mem_q", "w_mem_kv", "g_q_mem", "g_k_mem", "w_mem_out", "g_ffn", "w_gate_up", "w_down")


def _dot(a, b):
    return jnp.dot(a, b, preferred_element_type=_F32)


def _dot_nt(a, b):
    return lax.dot_general(a, b, (((1,), (1,)), ((), ())), preferred_element_type=_F32)


def _dot_tn(a, b):
    return lax.dot_general(a, b, (((0,), (0,)), ((), ())), preferred_element_type=_F32)


def _group_sum(x, ones_blockdiag):
    hi = x.astype(_MXU)
    lo = (x - hi.astype(_F32)).astype(_MXU)
    return _dot(hi, ones_blockdiag) + _dot(lo, ones_blockdiag)


def _tri_dot(tri, x):
    h1 = x.astype(jnp.bfloat16)
    r1 = x - h1.astype(_F32)
    h2 = r1.astype(jnp.bfloat16)
    h3 = (r1 - h2.astype(_F32)).astype(jnp.bfloat16)
    return _dot(tri, h1) + _dot(tri, h2) + _dot(tri, h3)


def _rstd(x):
    return lax.rsqrt(jnp.mean(x * x, axis=-1, keepdims=True) + EPS)


def _norm_bwd(dy, x, g):
    r = _rstd(x)
    xhat = x * r
    u = dy * g
    dx = r * (u - xhat * jnp.mean(u * xhat, axis=-1, keepdims=True))
    return dx, jnp.sum(dy * xhat, axis=0, keepdims=True)


def _headnorm_bwd(dy, x, g, ones_blockdiag, width):
    r = lax.rsqrt(_group_sum(x * x, ones_blockdiag) * (1.0 / width) + EPS)
    xhat = x * r
    u = dy * g
    dx = r * (u - xhat * (_group_sum(u * xhat, ones_blockdiag) * (1.0 / width)))
    return dx, jnp.sum(dy * xhat, axis=0, keepdims=True)


def _fold_heads(row, heads, width):
    acc = row[:, 0:width]
    for h in range(1, heads):
        acc = acc + row[:, h * width:(h + 1) * width]
    return acc


def _resident(shape):
    return pl.BlockSpec(shape, lambda *_: (0,) * len(shape), pipeline_mode=pl.Buffered(1))


def _rows(tm, width):
    return pl.BlockSpec((tm, width), lambda i: (i, 0))


def _blockdiag_ones(groups, width):
    return jnp.kron(jnp.eye(groups, dtype=_F32), jnp.ones((width, width), _F32)).astype(_MXU)


def _sds(shape, dtype):
    return jax.ShapeDtypeStruct(shape, dtype)


def _mix_in_fwd(h, g_mix, w_main, w_f, b_f, gq, gk, bd64):
    T = h.shape[0]
    tm = min(_TM, T)

    def body(h_ref, g_ref, wm_ref, wf_ref, bf_ref, gq_ref, gk_ref, bd_ref,
             xn_ref, zqk_ref, qn_ref, kn_ref, v_ref, pin_ref, fl_ref):
        x = h_ref[...]
        xn = ((x * _rstd(x)) * g_ref[...]).astype(_MXU)
        xn_ref[...] = xn
        z = _dot(xn, wm_ref[...])
        q = z[:, :FOX_WIDTH]
        k = z[:, FOX_WIDTH:2 * FOX_WIDTH]
        zqk_ref[...] = z[:, :2 * FOX_WIDTH]
        bd = bd_ref[...]
        rq = lax.rsqrt(_group_sum(q * q, bd) * (1.0 / FOX_HEAD_DIM) + EPS)
        rk = lax.rsqrt(_group_sum(k * k, bd) * (1.0 / FOX_HEAD_DIM) + EPS)
        qn_ref[...] = ((q * rq) * gq_ref[...]).astype(_MXU)
        kn_ref[...] = ((k * rk) * gk_ref[...]).astype(_MXU)
        v_ref[...] = z[:, 2 * FOX_WIDTH:3 * FOX_WIDTH].astype(_MXU)
        pin_ref[...] = z[:, 3 * FOX_WIDTH:]
        fl_ref[...] = _dot(xn, wf_ref[...]) + bf_ref[...]

    return pl.pallas_call(
        body, name="mix_in_fwd", grid=(T // tm,),
        in_specs=[_rows(tm, D_MODEL), _resident((1, D_MODEL)), _resident(w_main.shape), _resident(w_f.shape),
                  _resident((1, LANES)), _resident((1, FOX_WIDTH)), _resident((1, FOX_WIDTH)), _resident(bd64.shape)],
        out_specs=[_rows(tm, D_MODEL), _rows(tm, 2 * FOX_WIDTH), _rows(tm, FOX_WIDTH), _rows(tm, FOX_WIDTH),
                   _rows(tm, FOX_WIDTH), _rows(tm, POOL_WIDTH), _rows(tm, LANES)],
        out_shape=[_sds((T, D_MODEL), _MXU), _sds((T, 2 * FOX_WIDTH), _F32), _sds((T, FOX_WIDTH), _MXU),
                   _sds((T, FOX_WIDTH), _MXU), _sds((T, FOX_WIDTH), _MXU), _sds((T, POOL_WIDTH), _F32),
                   _sds((T, LANES), _F32)],
    )(h, g_mix, w_main, w_f, b_f, gq, gk, bd64)


def _gate_fwd(fl):
    T = fl.shape[0]
    tb = min(_TB, T)

    def body(fl_ref, c_ref, carry):
        @pl.when(pl.program_id(0) == 0)
        def _():
            carry[...] = jnp.zeros_like(carry)

        x = fl_ref[...]
        ls = jnp.minimum(x, 0.0) - jnp.log1p(jnp.exp(-jnp.abs(x)))
        row = lax.broadcasted_iota(jnp.int32, (tb, tb), 0)
        col = lax.broadcasted_iota(jnp.int32, (tb, tb), 1)
        tri = jnp.where(col <= row, 1.0, 0.0).astype(jnp.bfloat16)
        cs = _tri_dot(tri, ls) + carry[...]
        c_ref[...] = cs
        carry[...] = cs[tb - 1:tb, :]

    return pl.pallas_call(
        body, name="gate_fwd", grid=(T // tb,), in_specs=[_rows(tb, LANES)], out_specs=_rows(tb, LANES),
        out_shape=_sds((T, LANES), _F32), scratch_shapes=[pltpu.VMEM((1, LANES), _F32)],
    )(fl)


def _gate_bwd(drs, dcs, fl):
    T = fl.shape[0]
    tb = min(_TB, T)
    nb = T // tb

    def body(r_ref, d_ref, fl_ref, df_ref, carry):
        @pl.when(pl.program_id(0) == 0)
        def _():
            carry[...] = jnp.zeros_like(carry)

        row = lax.broadcasted_iota(jnp.int32, (tb, tb), 0)
        col = lax.broadcasted_iota(jnp.int32, (tb, tb), 1)
        tri = jnp.where(col >= row, 1.0, 0.0).astype(jnp.bfloat16)
        rc = _tri_dot(tri, r_ref[...] - d_ref[...]) + carry[...]
        carry[...] = rc[0:1, :]
        df_ref[...] = rc * (1.0 / (1.0 + jnp.exp(fl_ref[...])))

    rev = pl.BlockSpec((tb, LANES), lambda i: (nb - 1 - i, 0))
    return pl.pallas_call(
        body, name="gate_bwd", grid=(nb,), in_specs=[rev, rev, rev], out_specs=rev,
        out_shape=_sds((T, LANES), _F32), scratch_shapes=[pltpu.VMEM((1, LANES), _F32)],
    )(drs, dcs, fl)


def _fox_scores(q_ref, k_ref, cc, cr, hh, masked):
    sl = slice(FOX_HEAD_DIM * hh, FOX_HEAD_DIM * (hh + 1))
    s = _dot_nt(q_ref[:, sl], k_ref[:, sl]) * (FOX_HEAD_DIM ** -0.5) + (cc[:, hh:hh + 1] - cr[hh:hh + 1, :])
    if masked:
        row = lax.broadcasted_iota(jnp.int32, s.shape, 0)
        col = lax.broadcasted_iota(jnp.int32, s.shape, 1)
        s = jnp.where(col <= row, s, -jnp.inf)
    return s


def _fox_fwd(qn, kn, v, ccol, crow):
    T = qn.shape[0]
    tq = min(_TQ, T)
    nq = T // tq
    pair = 2 * FOX_HEAD_DIM

    def body(q_ref, k_ref, v_ref, cc_ref, cr_ref, o_ref, lse_ref, m_s, l_s, acc_s):
        i = pl.program_id(1)
        j = pl.program_id(2)

        @pl.when(j == 0)
        def _():
            m_s[...] = jnp.full(m_s.shape, -jnp.inf, _F32)
            l_s[...] = jnp.zeros_like(l_s)
            acc_s[...] = jnp.zeros_like(acc_s)

        def step(masked):
            cc = cc_ref[0]
            cr = cr_ref[0]
            for hh in range(2):
                s = _fox_scores(q_ref, k_ref, cc, cr, hh, masked)
                m_prev = m_s[hh]
                m_new = jnp.maximum(m_prev, jnp.max(s, axis=-1, keepdims=True))
                alpha = jnp.exp(m_prev - m_new)
                p = jnp.exp(s - m_new)
                l_s[hh] = alpha * l_s[hh] + jnp.sum(p, axis=-1, keepdims=True)
                vv = v_ref[:, FOX_HEAD_DIM * hh:FOX_HEAD_DIM * (hh + 1)]
                acc_s[hh] = alpha * acc_s[hh] + _dot(p.astype(_MXU), vv)
                m_s[hh] = m_new

        @pl.when(j < i)
        def _():
            step(False)

        @pl.when(j == i)
        def _():
            step(True)
            outs, lses = [], []
            for hh in range(2):
                l = l_s[hh]
                outs.append(acc_s[hh] / l)
                lses.append(jnp.broadcast_to(m_s[hh] + jnp.log(l), (tq, FOX_HEAD_DIM)))
            o_ref[...] = jnp.concatenate(outs, axis=-1).astype(_MXU)
            lse_ref[...] = jnp.concatenate(lses, axis=-1)

    qspec = pl.BlockSpec((tq, pair), lambda p, i, j: (i, p))
    kspec = pl.BlockSpec((tq, pair), lambda p, i, j: (jnp.minimum(j, i), p))
    return pl.pallas_call(
        body, name="fox_fwd", grid=(FOX_HEADS // 2, nq, nq),
        in_specs=[qspec, kspec, kspec,
                  pl.BlockSpec((1, tq, LANES), lambda p, i, j: (p, i, 0)),
                  pl.BlockSpec((1, 8, tq), lambda p, i, j: (p, 0, jnp.minimum(j, i)))],
        out_specs=[qspec, qspec],
        out_shape=[_sds((T, FOX_WIDTH), _MXU), _sds((T, FOX_WIDTH), _F32)],
        scratch_shapes=[pltpu.VMEM((2, tq, 1), _F32), pltpu.VMEM((2, tq, 1), _F32),
                        pltpu.VMEM((2, tq, FOX_HEAD_DIM), _F32)],
    )(qn, kn, v, ccol, crow)


def _pool_window_sum(ext, w, forward):
    n = ext.shape[0]
    sm = ext
    k = 1
    while k < w:
        sm = sm + pltpu.roll(sm, (n - k) if forward else k, axis=0)
        k *= 2
    return sm


def _out_proj_fwd(h, o, pin, w_pool, pscale, w_out):
    T = h.shape[0]
    tm = min(_TM, T)
    hb = tm // HALO

    def body(h_ref, o_ref, pin_ref, halo_ref, wp_ref, ps_ref, wo_ref, h1_ref, mixed_ref, y_ref):
        i = pl.program_id(0)
        pin_t = pin_ref[...]
        halo = jnp.where(i == 0, 0.0, halo_ref[...])
        ext = jnp.concatenate([halo, pin_t], axis=0)
        t = (i * tm + lax.broadcasted_iota(jnp.int32, (tm, 1), 0) + 1).astype(_F32)
        mixed, ys = [], []
        for g, w in enumerate(POOL_WINDOWS):
            sl = slice(g * POOL_GROUP_DIM, (g + 1) * POOL_GROUP_DIM)
            win = _pool_window_sum(ext[:, sl], w, False)[HALO:, :]
            mg = (win / jnp.minimum(t, float(w)) - pin_t[:, sl]).astype(_MXU)
            mixed.append(mg)
            ys.append(_dot(mg, wp_ref[g]))
        mixed_ref[...] = jnp.concatenate(mixed, axis=-1)
        y = (jnp.concatenate(ys, axis=-1) * ps_ref[...]).astype(_MXU)
        y_ref[...] = y
        h1_ref[...] = h_ref[...] + _dot(o_ref[...], wo_ref[:FOX_WIDTH, :]) + _dot(y, wo_ref[FOX_WIDTH:, :])

    return pl.pallas_call(
        body, name="out_proj_fwd", grid=(T // tm,),
        in_specs=[_rows(tm, D_MODEL), _rows(tm, FOX_WIDTH), _rows(tm, POOL_WIDTH),
                  pl.BlockSpec((HALO, POOL_WIDTH), lambda i: (jnp.maximum(i * hb - 1, 0), 0)),
                  _resident(w_pool.shape), _resident((1, POOL_WIDTH)), _resident(w_out.shape)],
        out_specs=[_rows(tm, D_MODEL), _rows(tm, POOL_WIDTH), _rows(tm, POOL_WIDTH)],
        out_shape=[_sds((T, D_MODEL), _F32), _sds((T, POOL_WIDTH), _MXU), _sds((T, POOL_WIDTH), _MXU)],
    )(h, o, pin, pin, w_pool, pscale, w_out)


def _mem_kv_fwd(mem, g_kv, w_kv, gkm, bd128):
    M = mem.shape[0]

    def body(mem_ref, g_ref, w_ref, gk_ref, bd_ref, mn_ref, mkv_ref, mk_ref, mv_ref):
        x = mem_ref[...]
        mn = ((x * _rstd(x)) * g_ref[...]).astype(_MXU)
        mn_ref[...] = mn
        z = _dot(mn, w_ref[...])
        mkv_ref[...] = z
        k = z[:, :MEM_WIDTH]
        rk = lax.rsqrt(_group_sum(k * k, bd_ref[...]) * (1.0 / MEM_HEAD_DIM) + EPS)
        mk_ref[...] = ((k * rk) * gk_ref[...]).astype(_MXU)
        mv_ref[...] = z[:, MEM_WIDTH:].astype(_MXU)

    return pl.pallas_call(
        body, name="mem_kv_fwd",
        out_shape=[_sds((M, D_MODEL), _MXU), _sds((M, 2 * MEM_WIDTH), _F32), _sds((M, MEM_WIDTH), _MXU),
                   _sds((M, MEM_WIDTH), _MXU)],
    )(mem, g_kv, w_kv, gkm, bd128)


def _mem_softmax(qn, mk_ref, hd):
    sl = slice(hd * MEM_HEAD_DIM, (hd + 1) * MEM_HEAD_DIM)
    s = _dot_nt(qn[:, sl], mk_ref[:, sl]) * (MEM_HEAD_DIM ** -0.5)
    e = jnp.exp(s - jnp.max(s, axis=-1, keepdims=True))
    return e / jnp.sum(e, axis=-1, keepdims=True)


def _mem_attn_fwd(h1, g_q, w_q, gqm, bd128, mk, mv, w_mo):
    T = h1.shape[0]
    tm = min(_TM, T)

    def body(h_ref, g_ref, wq_ref, gq_ref, bd_ref, mk_ref, mv_ref, wo_ref, h2_ref, hn_ref, mo_ref):
        x = h_ref[...]
        hn = ((x * _rstd(x)) * g_ref[...]).astype(_MXU)
        hn_ref[...] = hn
        mq = _dot(hn, wq_ref[...])
        rq = lax.rsqrt(_group_sum(mq * mq, bd_ref[...]) * (1.0 / MEM_HEAD_DIM) + EPS)
        qn = ((mq * rq) * gq_ref[...]).astype(_MXU)
        outs = []
        for hd in range(MEM_HEADS):
            p = _mem_softmax(qn, mk_ref, hd).astype(_MXU)
            outs.append(_dot(p, mv_ref[:, hd * MEM_HEAD_DIM:(hd + 1) * MEM_HEAD_DIM]))
        mo = jnp.concatenate(outs, axis=-1).astype(_MXU)
        mo_ref[...] = mo
        h2_ref[...] = x + _dot(mo, wo_ref[...])

    return pl.pallas_call(
        body, name="mem_attn_fwd", grid=(T // tm,),
        in_specs=[_rows(tm, D_MODEL), _resident((1, D_MODEL)), _resident(w_q.shape), _resident((1, MEM_WIDTH)),
                  _resident(bd128.shape), _resident(mk.shape), _resident(mv.shape), _resident(w_mo.shape)],
        out_specs=[_rows(tm, D_MODEL), _rows(tm, D_MODEL), _rows(tm, MEM_WIDTH)],
        out_shape=[_sds((T, D_MODEL), _F32), _sds((T, D_MODEL), _MXU), _sds((T, MEM_WIDTH), _MXU)],
    )(h1, g_q, w_q, gqm, bd128, mk, mv, w_mo)


def _ffn_fwd(h2, g_ffn, w_gu, w_d):
    T = h2.shape[0]
    tm = min(_TMF, T)
    nf = D_FF // _TF

    def body(h_ref, g_ref, wg_ref, wu_ref, wd_ref, h3_ref, hn_ref, acc, xn_s):
        j = pl.program_id(1)

        @pl.when(j == 0)
        def _():
            x = h_ref[...]
            xn = ((x * _rstd(x)) * g_ref[...]).astype(_MXU)
            xn_s[...] = xn
            hn_ref[...] = xn
            acc[...] = jnp.zeros_like(acc)

        xn = xn_s[...]
        g = _dot(xn, wg_ref[...])
        u = _dot(xn, wu_ref[...])
        a = ((g * jax.nn.sigmoid(g)) * u).astype(_MXU)
        acc[...] += _dot(a, wd_ref[...])

        @pl.when(j == nf - 1)
        def _():
            h3_ref[...] = h_ref[...] + acc[...]

    tok = pl.BlockSpec((tm, D_MODEL), lambda i, j: (i, 0))
    return pl.pallas_call(
        body, name="ffn_fwd", grid=(T // tm, nf),
        in_specs=[tok, pl.BlockSpec((1, D_MODEL), lambda i, j: (0, 0)),
                  pl.BlockSpec((D_MODEL, _TF), lambda i, j: (0, j)),
                  pl.BlockSpec((D_MODEL, _TF), lambda i, j: (0, j + nf)),
                  pl.BlockSpec((_TF, D_MODEL), lambda i, j: (j, 0))],
        out_specs=[tok, tok],
        out_shape=[_sds((T, D_MODEL), _F32), _sds((T, D_MODEL), _MXU)],
        scratch_shapes=[pltpu.VMEM((tm, D_MODEL), _F32), pltpu.VMEM((tm, D_MODEL), _MXU)],
    )(h2, g_ffn, w_gu, w_gu, w_d)


def _loss_grad(y, tgt):
    T = y.shape[0]
    tm = min(_TA, T)

    def body(y_ref, t_ref, dy_ref, loss_ref):
        @pl.when(pl.program_id(0) == 0)
        def _():
            loss_ref[...] = jnp.zeros_like(loss_ref)

        err = y_ref[...] - t_ref[...]
        dy_ref[...] = err * (1.0 / D_MODEL)
        part = jnp.sum(jnp.sum(err * err, axis=0, keepdims=True), axis=1, keepdims=True)
        loss_ref[...] += part * (0.5 / D_MODEL)

    return pl.pallas_call(
        body, name="loss_grad", grid=(T // tm,), in_specs=[_rows(tm, D_MODEL), _rows(tm, D_MODEL)],
        out_specs=[_rows(tm, D_MODEL), pl.BlockSpec((1, 1), lambda i: (0, 0))],
        out_shape=[_sds((T, D_MODEL), _F32), _sds((1, 1), _F32)],
    )(y, tgt)


def _pick_tile(n, candidates=(1408, 1024, 512, 256, 128)):
    for c in candidates:
        if n % c == 0:
            return c
    return n


def _matmul_tn(a, b, name):
    T, K = a.shape
    N = b.shape[1]
    tk, tn, tt = _pick_tile(K), _pick_tile(N), min(_TT, T)

    def body(a_ref, b_ref, o_ref):
        @pl.when(pl.program_id(2) == 0)
        def _():
            o_ref[...] = jnp.zeros_like(o_ref)

        o_ref[...] += _dot_tn(a_ref[...].astype(_MXU), b_ref[...].astype(_MXU))

    return pl.pallas_call(
        body, name=name, grid=(K // tk, N // tn, T // tt),
        in_specs=[pl.BlockSpec((tt, tk), lambda i, j, t: (t, i)), pl.BlockSpec((tt, tn), lambda i, j, t: (t, j))],
        out_specs=pl.BlockSpec((tk, tn), lambda i, j, t: (i, j)),
        out_shape=_sds((K, N), _F32),
    )(a, b)


def _ffn_bwd(dh3, h2, hn, g_ffn, w_gu, w_d):
    T = h2.shape[0]
    tm = min(_TMF, T)
    nf = D_FF // _TF

    def body(dh_ref, h_ref, hn_ref, g_ref, wg_ref, wu_ref, wd_ref, dh2_ref, act_ref, dg_ref, du_ref, dgain_ref,
             acc, dyb):
        i = pl.program_id(0)
        j = pl.program_id(1)

        @pl.when((i == 0) & (j == 0))
        def _():
            dgain_ref[...] = jnp.zeros_like(dgain_ref)

        @pl.when(j == 0)
        def _():
            dyb[...] = dh_ref[...].astype(_MXU)
            acc[...] = jnp.zeros_like(acc)

        xn = hn_ref[...]
        g = _dot(xn, wg_ref[...])
        u = _dot(xn, wu_ref[...])
        sg = jax.nn.sigmoid(g)
        sl = g * sg
        act_ref[...] = (sl * u).astype(_MXU)
        da = _dot_nt(dyb[...], wd_ref[...])
        dgate = (da * u * (sg * (1.0 + g * (1.0 - sg)))).astype(_MXU)
        dup = (da * sl).astype(_MXU)
        dg_ref[...] = dgate
        du_ref[...] = dup
        acc[...] += _dot_nt(dgate, wg_ref[...]) + _dot_nt(dup, wu_ref[...])

        @pl.when(j == nf - 1)
        def _():
            dhn = acc[...]
            dx, dgain = _norm_bwd(dhn, h_ref[...], g_ref[...])
            dh2_ref[...] = dh_ref[...] + dx
            dgain_ref[...] += dgain

    tok = pl.BlockSpec((tm, D_MODEL), lambda i, j: (i, 0))
    ffb = pl.BlockSpec((tm, _TF), lambda i, j: (i, j))
    row = pl.BlockSpec((1, D_MODEL), lambda i, j: (0, 0))
    return pl.pallas_call(
        body, name="ffn_bwd", grid=(T // tm, nf),
        in_specs=[tok, tok, tok, row,
                  pl.BlockSpec((D_MODEL, _TF), lambda i, j: (0, j)),
                  pl.BlockSpec((D_MODEL, _TF), lambda i, j: (0, j + nf)),
                  pl.BlockSpec((_TF, D_MODEL), lambda i, j: (j, 0))],
        out_specs=[tok, ffb, ffb, ffb, row],
        out_shape=[_sds((T, D_MODEL), _F32), _sds((T, D_FF), _MXU), _sds((T, D_FF), _MXU), _sds((T, D_FF), _MXU),
                   _sds((1, D_MODEL), _F32)],
        scratch_shapes=[pltpu.VMEM((tm, D_MODEL), _F32), pltpu.VMEM((tm, D_MODEL), _MXU)],
    )(dh3, h2, hn, g_ffn, w_gu, w_gu, w_d)


def _mem_attn_bwd(dh2, h1, g_q, w_q, gqm, bd128, mk, mv, w_mo):
    T = h1.shape[0]
    M = mk.shape[0]
    tm = min(_TM, T)
    nt = T // tm

    def body(dh_ref, h_ref, g_ref, wq_ref, gq_ref, bd_ref, mk_ref, mv_ref, wo_ref,
             dh1_ref, dmq_ref, dmk_ref, dmv_ref, dgq_ref, dgain_ref, dgq_acc):
        i = pl.program_id(0)

        @pl.when(i == 0)
        def _():
            dmk_ref[...] = jnp.zeros_like(dmk_ref)
            dmv_ref[...] = jnp.zeros_like(dmv_ref)
            dgain_ref[...] = jnp.zeros_like(dgain_ref)
            dgq_acc[...] = jnp.zeros_like(dgq_acc)

        x = h_ref[...]
        g = g_ref[...]
        bd = bd_ref[...]
        hn = ((x * _rstd(x)) * g).astype(_MXU)
        mq = _dot(hn, wq_ref[...])
        rq = lax.rsqrt(_group_sum(mq * mq, bd) * (1.0 / MEM_HEAD_DIM) + EPS)
        qn = ((mq * rq) * gq_ref[...]).astype(_MXU)
        dmo = _dot_nt(dh_ref[...].astype(_MXU), wo_ref[...])
        dqn = []
        for hd in range(MEM_HEADS):
            sl = slice(hd * MEM_HEAD_DIM, (hd + 1) * MEM_HEAD_DIM)
            p = _mem_softmax(qn, mk_ref, hd)
            dmo_h = dmo[:, sl].astype(_MXU)
            dp = _dot_nt(dmo_h, mv_ref[:, sl])
            ds = (p * (dp - jnp.sum(p * dp, axis=-1, keepdims=True)) * (MEM_HEAD_DIM ** -0.5)).astype(_MXU)
            dqn.append(_dot(ds, mk_ref[:, sl]))
            dmk_ref[:, sl] += _dot_tn(ds, qn[:, sl])
            dmv_ref[:, sl] += _dot_tn(p.astype(_MXU), dmo_h)
        dqn = jnp.concatenate(dqn, axis=-1)
        dmq, dgq = _headnorm_bwd(dqn, mq, gq_ref[...], bd, MEM_HEAD_DIM)
        dgq_acc[...] += dgq
        dmq = dmq.astype(_MXU)
        dmq_ref[...] = dmq
        dhn = _dot_nt(dmq, wq_ref[...])
        dx, dgain = _norm_bwd(dhn, x, g)
        dh1_ref[...] = dh_ref[...] + dx
        dgain_ref[...] += dgain

        @pl.when(i == nt - 1)
        def _():
            dgq_ref[...] = _fold_heads(dgq_acc[...], MEM_HEADS, MEM_HEAD_DIM)

    const2 = lambda shape: pl.BlockSpec(shape, lambda i: (0, 0))
    return pl.pallas_call(
        body, name="mem_attn_bwd", grid=(nt,),
        in_specs=[_rows(tm, D_MODEL), _rows(tm, D_MODEL), _resident((1, D_MODEL)), _resident(w_q.shape),
                  _resident((1, MEM_WIDTH)), _resident(bd128.shape), _resident(mk.shape), _resident(mv.shape),
                  _resident(w_mo.shape)],
        out_specs=[_rows(tm, D_MODEL), _rows(tm, MEM_WIDTH), const2((M, MEM_WIDTH)), const2((M, MEM_WIDTH)),
                   const2((1, MEM_HEAD_DIM)), const2((1, D_MODEL))],
        out_shape=[_sds((T, D_MODEL), _F32), _sds((T, MEM_WIDTH), _MXU), _sds((M, MEM_WIDTH), _F32),
                   _sds((M, MEM_WIDTH), _F32), _sds((1, MEM_HEAD_DIM), _F32), _sds((1, D_MODEL), _F32)],
        scratch_shapes=[pltpu.VMEM((1, MEM_WIDTH), _F32)],
    )(dh2, h1, g_q, w_q, gqm, bd128, mk, mv, w_mo)


def _mem_kv_bwd(dmk, dmv, mkv, mn, mem, g_kv, gkm, bd128, w_kv):
    def body(dmk_ref, dmv_ref, mkv_ref, mn_ref, mem_ref, g_ref, gk_ref, bd_ref, w_ref, dw_ref, dgk_ref, dgain_ref):
        kraw = mkv_ref[:, :MEM_WIDTH]
        dk, dgk = _headnorm_bwd(dmk_ref[...], kraw, gk_ref[...], bd_ref[...], MEM_HEAD_DIM)
        dgk_ref[...] = _fold_heads(dgk, MEM_HEADS, MEM_HEAD_DIM)
        dmkv = jnp.concatenate([dk, dmv_ref[...]], axis=-1).astype(_MXU)
        dw_ref[...] = _dot_tn(mn_ref[...], dmkv)
        dmn = _dot_nt(dmkv, w_ref[...])
        _, dgain = _norm_bwd(dmn, mem_ref[...], g_ref[...])
        dgain_ref[...] = dgain

    return pl.pallas_call(
        body, name="mem_kv_bwd",
        out_shape=[_sds((D_MODEL, 2 * MEM_WIDTH), _F32), _sds((1, MEM_HEAD_DIM), _F32), _sds((1, D_MODEL), _F32)],
    )(dmk, dmv, mkv, mn, mem, g_kv, gkm, bd128, w_kv)


def _out_proj_bwd(dh1, mixed, w_pool, pscale, w_out):
    T = dh1.shape[0]
    tm = min(_TM, T)
    hb = tm // HALO
    nt = T // tm

    def body(dh_ref, halo_ref, mx_ref, wp_ref, ps_ref, wo_ref, do_ref, dpin_ref, dwp_ref, dps_ref):
        i = pl.program_id(0)

        @pl.when(i == 0)
        def _():
            dwp_ref[...] = jnp.zeros_like(dwp_ref)
            dps_ref[...] = jnp.zeros_like(dps_ref)

        dcat = _dot_nt(dh_ref[...].astype(_MXU), wo_ref[...])
        do_ref[...] = dcat[:, :FOX_WIDTH].astype(_MXU)
        dy = dcat[:, FOX_WIDTH:]
        dyh = _dot_nt(halo_ref[...].astype(_MXU), wo_ref[FOX_WIDTH:, :])
        dyh = jnp.where(i == nt - 1, 0.0, dyh)
        ps = ps_ref[...]
        t = (i * tm + lax.broadcasted_iota(jnp.int32, (tm + HALO, 1), 0) + 1).astype(_F32)
        mixed_t = mx_ref[...]
        dpin, dps = [], []
        for g, w in enumerate(POOL_WINDOWS):
            sl = slice(g * POOL_GROUP_DIM, (g + 1) * POOL_GROUP_DIM)
            mg = mixed_t[:, sl]
            wg = wp_ref[g]
            dps.append(jnp.sum(dy[:, sl] * _dot(mg, wg), axis=0, keepdims=True))
            dyl = (dy[:, sl] * ps[:, sl]).astype(_MXU)
            dylh = (dyh[:, sl] * ps[:, sl]).astype(_MXU)
            dwp_ref[g] += _dot_tn(mg, dyl)
            dmx = _dot_nt(dyl, wg)
            ext = jnp.concatenate([dmx, _dot_nt(dylh, wg)], axis=0) / jnp.minimum(t, float(w))
            dpin.append(_pool_window_sum(ext, w, True)[:tm, :] - dmx)
        dpin_ref[...] = jnp.concatenate(dpin, axis=-1)
        dps_ref[...] += jnp.concatenate(dps, axis=-1)

    return pl.pallas_call(
        body, name="out_proj_bwd", grid=(nt,),
        in_specs=[_rows(tm, D_MODEL),
                  pl.BlockSpec((HALO, D_MODEL), lambda i: (jnp.minimum((i + 1) * hb, T // HALO - 1), 0)),
                  _rows(tm, POOL_WIDTH), _resident(w_pool.shape), _resident((1, POOL_WIDTH)), _resident(w_out.shape)],
        out_specs=[_rows(tm, FOX_WIDTH), _rows(tm, POOL_WIDTH),
                   pl.BlockSpec(w_pool.shape, lambda i: (0, 0, 0)), pl.BlockSpec((1, POOL_WIDTH), lambda i: (0, 0))],
        out_shape=[_sds((T, FOX_WIDTH), _MXU), _sds((T, POOL_WIDTH), _F32), _sds(w_pool.shape, _F32),
                   _sds((1, POOL_WIDTH), _F32)],
    )(dh1, dh1, mixed, w_pool, pscale, w_out)


def _fox_bwd(qn, kn, v, o, do, lse, ccol, crow):
    T = qn.shape[0]
    tq = min(_TQ, T)
    nq = T // tq
    pair = 2 * FOX_HEAD_DIM

    def body(q_ref, k_ref, v_ref, o_ref, do_ref, lse_ref, cc_ref, cr_ref, dq_ref, dk_ref, dv_ref, dcs_ref, drs_ref,
             dk_acc, dv_acc, dcs_acc):
        j = pl.program_id(1)
        i = pl.program_id(2)

        @pl.when((j == 0) & (i == 0))
        def _():
            dq_ref[...] = jnp.zeros_like(dq_ref)
            drs_ref[...] = jnp.zeros_like(drs_ref)

        @pl.when(i == 0)
        def _():
            dk_acc[...] = jnp.zeros_like(dk_acc)
            dv_acc[...] = jnp.zeros_like(dv_acc)
            dcs_acc[...] = jnp.zeros_like(dcs_acc)

        def step(masked):
            cc = cc_ref[0]
            cr = cr_ref[0]
            lse = lse_ref[...]
            dqs, rs = [], []
            for hh in range(2):
                sl = slice(FOX_HEAD_DIM * hh, FOX_HEAD_DIM * (hh + 1))
                s = _fox_scores(q_ref, k_ref, cc, cr, hh, masked)
                p = jnp.exp(s - lse[:, FOX_HEAD_DIM * hh:FOX_HEAD_DIM * hh + 1])
                dob = do_ref[:, sl]
                dv_acc[hh] += _dot_tn(p.astype(_MXU), dob)
                dp = _dot_nt(dob, v_ref[:, sl])
                delta = jnp.sum(dob.astype(_F32) * o_ref[:, sl].astype(_F32), axis=-1, keepdims=True)
                ds = p * (dp - delta)
                dcs_acc[hh:hh + 1, :] += jnp.sum(ds, axis=0, keepdims=True)
                rs.append(jnp.sum(ds, axis=-1, keepdims=True))
                dsb = (ds * (FOX_HEAD_DIM ** -0.5)).astype(_MXU)
                dk_acc[hh] += _dot_tn(dsb, q_ref[:, sl])
                dqs.append(_dot(dsb, k_ref[:, sl]))
            rows = pl.ds(pl.multiple_of(i * tq, tq), tq)
            dq_ref[rows, :] += jnp.concatenate(dqs, axis=-1)
            lane = lax.broadcasted_iota(jnp.int32, (tq, LANES), 1)
            drs_ref[0, rows, :] += jnp.where(lane == 0, rs[0], jnp.where(lane == 1, rs[1], 0.0))

        @pl.when(i > j)
        def _():
            step(False)

        @pl.when(i == j)
        def _():
            step(True)

        @pl.when(i == nq - 1)
        def _():
            dk_ref[...] = jnp.concatenate([dk_acc[0], dk_acc[1]], axis=-1)
            dv_ref[...] = jnp.concatenate([dv_acc[0], dv_acc[1]], axis=-1)
            dcs_ref[0] = dcs_acc[...]

    qspec = pl.BlockSpec((tq, pair), lambda p, j, i: (jnp.maximum(i, j), p))
    kspec = pl.BlockSpec((tq, pair), lambda p, j, i: (j, p))
    return pl.pallas_call(
        body, name="fox_bwd", grid=(FOX_HEADS // 2, nq, nq),
        in_specs=[qspec, kspec, kspec, qspec, qspec, qspec,
                  pl.BlockSpec((1, tq, LANES), lambda p, j, i: (p, jnp.maximum(i, j), 0)),
                  pl.BlockSpec((1, 8, tq), lambda p, j, i: (p, 0, j))],
        out_specs=[pl.BlockSpec((T, pair), lambda p, j, i: (0, p)), kspec, kspec,
                   pl.BlockSpec((1, 8, tq), lambda p, j, i: (p, 0, j)),
                   pl.BlockSpec((1, T, LANES), lambda p, j, i: (p, 0, 0))],
        out_shape=[_sds((T, FOX_WIDTH), _F32), _sds((T, FOX_WIDTH), _F32), _sds((T, FOX_WIDTH), _F32),
                   _sds((FOX_HEADS // 2, 8, T), _F32), _sds((FOX_HEADS // 2, T, LANES), _F32)],
        scratch_shapes=[pltpu.VMEM((2, tq, FOX_HEAD_DIM), _F32), pltpu.VMEM((2, tq, FOX_HEAD_DIM), _F32),
                        pltpu.VMEM((8, tq), _F32)],
    )(qn, kn, v, o, do, lse, ccol, crow)


def _mix_in_bwd(dh1, h, g_mix, zqk, dq, dk, dv, dpin, df, gq, gk, bd64, w_main, w_f):
    T = h.shape[0]
    tm = min(_TM, T)
    nt = T // tm

    def body(dh1_ref, h_ref, g_ref, zqk_ref, dq_ref, dk_ref, dv_ref, dpin_ref, df_ref, gq_ref, gk_ref, bd_ref,
             wm_ref, wf_ref, dh_ref, dz_ref, dzf_ref, dgq_ref, dgk_ref, dgain_ref, dbf_ref, dgq_acc, dgk_acc):
        i = pl.program_id(0)

        @pl.when(i == 0)
        def _():
            dgain_ref[...] = jnp.zeros_like(dgain_ref)
            dbf_ref[...] = jnp.zeros_like(dbf_ref)
            dgq_acc[...] = jnp.zeros_like(dgq_acc)
            dgk_acc[...] = jnp.zeros_like(dgk_acc)

        bd = bd_ref[...]
        dqr, dgq = _headnorm_bwd(dq_ref[...], zqk_ref[:, :FOX_WIDTH], gq_ref[...], bd, FOX_HEAD_DIM)
        dkr, dgk = _headnorm_bwd(dk_ref[...], zqk_ref[:, FOX_WIDTH:], gk_ref[...], bd, FOX_HEAD_DIM)
        dgq_acc[...] += dgq
        dgk_acc[...] += dgk
        dz = jnp.concatenate([dqr, dkr, dv_ref[...], dpin_ref[...]], axis=-1).astype(_MXU)
        dz_ref[...] = dz
        df = df_ref[...]
        dzf = df.astype(_MXU)
        dzf_ref[...] = dzf
        dbf_ref[...] += jnp.sum(df, axis=0, keepdims=True)
        dxn = _dot_nt(dz, wm_ref[...]) + _dot_nt(dzf, wf_ref[...])
        dx, dgain = _norm_bwd(dxn, h_ref[...], g_ref[...])
        dh_ref[...] = dh1_ref[...] + dx
        dgain_ref[...] += dgain

        @pl.when(i == nt - 1)
        def _():
            dgq_ref[...] = _fold_heads(dgq_acc[...], FOX_HEADS, FOX_HEAD_DIM)
            dgk_ref[...] = _fold_heads(dgk_acc[...], FOX_HEADS, FOX_HEAD_DIM)

    const2 = lambda shape: pl.BlockSpec(shape, lambda i: (0, 0))
    return pl.pallas_call(
        body, name="mix_in_bwd", grid=(nt,),
        in_specs=[_rows(tm, D_MODEL), _rows(tm, D_MODEL), _resident((1, D_MODEL)), _rows(tm, 2 * FOX_WIDTH),
                  _rows(tm, FOX_WIDTH), _rows(tm, FOX_WIDTH), _rows(tm, FOX_WIDTH), _rows(tm, POOL_WIDTH),
                  _rows(tm, LANES), _resident((1, FOX_WIDTH)), _resident((1, FOX_WIDTH)), _resident(bd64.shape),
                  _resident(w_main.shape), _resident(w_f.shape)],
        out_specs=[_rows(tm, D_MODEL), _rows(tm, 4 * FOX_WIDTH), _rows(tm, LANES), const2((1, FOX_HEAD_DIM)),
                   const2((1, FOX_HEAD_DIM)), const2((1, D_MODEL)), const2((1, LANES))],
        out_shape=[_sds((T, D_MODEL), _F32), _sds((T, 4 * FOX_WIDTH), _MXU), _sds((T, LANES), _MXU),
                   _sds((1, FOX_HEAD_DIM), _F32), _sds((1, FOX_HEAD_DIM), _F32), _sds((1, D_MODEL), _F32),
                   _sds((1, LANES), _F32)],
        scratch_shapes=[pltpu.VMEM((1, FOX_WIDTH), _F32), pltpu.VMEM((1, FOX_WIDTH), _F32)],
    )(dh1, h, g_mix, zqk, dq, dk, dv, dpin, df, gq, gk, bd64, w_main, w_f)


def _layer_params(W, l):
    w_in = W["w_in"][l]
    n_main = 3 * FOX_WIDTH + POOL_WIDTH
    row = lambda a: a.reshape(1, -1).astype(_F32)
    return dict(
        g_mix=row(W["g_mix"][l]),
        w_main=w_in[:, :n_main],
        w_f=jnp.pad(w_in[:, n_main:], ((0, 0), (0, LANES - FOX_HEADS))),
        b_f=jnp.pad(row(W["b_forget"][l]), ((0, 0), (0, LANES - FOX_HEADS))),
        gq=jnp.tile(row(W["g_q_fox"][l]), (1, FOX_HEADS)),
        gk=jnp.tile(row(W["g_k_fox"][l]), (1, FOX_HEADS)),
        w_pool=W["w_pool"][l].astype(_MXU),
        pscale=row(W["pool_scale"][l]),
        w_out=W["w_out"][l],
        g_mem_q=row(W["g_mem_q"][l]),
        g_mem_kv=row(W["g_mem_kv"][l]),
        w_mem_q=W["w_mem_q"][l],
        w_mem_kv=W["w_mem_kv"][l],
        gqm=jnp.tile(row(W["g_q_mem"][l]), (1, MEM_HEADS)),
        gkm=jnp.tile(row(W["g_k_mem"][l]), (1, MEM_HEADS)),
        w_mem_out=W["w_mem_out"][l],
        g_ffn=row(W["g_ffn"][l]),
        w_gu=W["w_gate_up"][l],
        w_d=W["w_down"][l],
    )


def _pair_layouts(c):
    T = c.shape[0]
    cp = c[:, :FOX_HEADS].reshape(T, FOX_HEADS // 2, 2)
    ccol = jnp.pad(cp.transpose(1, 0, 2), ((0, 0), (0, 0), (0, LANES - 2)))
    crow = jnp.pad(cp.transpose(1, 2, 0), ((0, 0), (0, 6), (0, 0)))
    return ccol, crow


def _layer_fwd(h, mem, P, bd64, bd128):
    s = dict(h=h)
    s["xn"], s["zqk"], s["qn"], s["kn"], s["v"], pin, s["fl"] = _mix_in_fwd(
        h, P["g_mix"], P["w_main"], P["w_f"], P["b_f"], P["gq"], P["gk"], bd64)
    s["ccol"], s["crow"] = _pair_layouts(_gate_fwd(s["fl"]))
    s["o"], s["lse"] = _fox_fwd(s["qn"], s["kn"], s["v"], s["ccol"], s["crow"])
    s["h1"], s["mixed"], s["y"] = _out_proj_fwd(h, s["o"], pin, P["w_pool"], P["pscale"], P["w_out"])
    s["mn"], s["mkv"], s["mk"], s["mv"] = _mem_kv_fwd(mem, P["g_mem_kv"], P["w_mem_kv"], P["gkm"], bd128)
    s["h2"], s["hn_mem"], s["mo"] = _mem_attn_fwd(s["h1"], P["g_mem_q"], P["w_mem_q"], P["gqm"], bd128, s["mk"],
                                                  s["mv"], P["w_mem_out"])
    h3, s["hn_ffn"] = _ffn_fwd(s["h2"], P["g_ffn"], P["w_gu"], P["w_d"])
    return h3, s


def _layer_bwd(dh3, mem, P, s, bd64, bd128):
    T = dh3.shape[0]
    g = {}
    dh2, act, dgate, dup, g["g_ffn"] = _ffn_bwd(dh3, s["h2"], s["hn_ffn"], P["g_ffn"], P["w_gu"], P["w_d"])
    g["w_down"] = _matmul_tn(act, dh3, "dw_down")
    g["w_gate_up"] = jnp.concatenate([_matmul_tn(s["hn_ffn"], dgate, "dw_gate"),
                                      _matmul_tn(s["hn_ffn"], dup, "dw_up")], axis=1)

    dh1, dmq, dmk, dmv, g["g_q_mem"], g["g_mem_q"] = _mem_attn_bwd(
        dh2, s["h1"], P["g_mem_q"], P["w_mem_q"], P["gqm"], bd128, s["mk"], s["mv"], P["w_mem_out"])
    g["w_mem_out"] = _matmul_tn(s["mo"], dh2, "dw_mem_out")
    g["w_mem_q"] = _matmul_tn(s["hn_mem"], dmq, "dw_mem_q")
    g["w_mem_kv"], g["g_k_mem"], g["g_mem_kv"] = _mem_kv_bwd(dmk, dmv, s["mkv"], s["mn"], mem, P["g_mem_kv"],
                                                             P["gkm"], bd128, P["w_mem_kv"])

    do, dpin, g["w_pool"], g["pool_scale"] = _out_proj_bwd(dh1, s["mixed"], P["w_pool"], P["pscale"], P["w_out"])
    g["w_out"] = jnp.concatenate([_matmul_tn(s["o"], dh1, "dw_out_fox"), _matmul_tn(s["y"], dh1, "dw_out_pool")],
                                 axis=0)
    dq, dk, dv, dcs, drs = _fox_bwd(s["qn"], s["kn"], s["v"], s["o"], do, s["lse"], s["ccol"], s["crow"])
    pad = ((0, 0), (0, LANES - FOX_HEADS))
    dcs = jnp.pad(dcs[:, :2, :].transpose(2, 0, 1).reshape(T, FOX_HEADS), pad)
    drs = jnp.pad(drs[:, :, :2].transpose(1, 0, 2).reshape(T, FOX_HEADS), pad)
    df = _gate_bwd(drs, dcs, s["fl"])
    dh, dz, dzf, g["g_q_fox"], g["g_k_fox"], g["g_mix"], dbf = _mix_in_bwd(
        dh1, s["h"], P["g_mix"], s["zqk"], dq, dk, dv, dpin, df, P["gq"], P["gk"], bd64, P["w_main"], P["w_f"])
    g["b_forget"] = dbf[:, :FOX_HEADS]
    g["w_in"] = jnp.concatenate([_matmul_tn(s["xn"], dz, "dw_in_main"),
                                 _matmul_tn(s["xn"], dzf, "dw_in_gate")[:, :FOX_HEADS]], axis=1)
    return dh, g


def _device_grads(x, mem, tgt, W):
    bd64 = _blockdiag_ones(FOX_HEADS, FOX_HEAD_DIM)
    bd128 = _blockdiag_ones(MEM_HEADS, MEM_HEAD_DIM)
    params = [_layer_params(W, l) for l in range(DEPTH)]
    h, saved = x, []
    for l in range(DEPTH):
        h, s = _layer_fwd(h, mem, params[l], bd64, bd128)
        saved.append(s)
    dh, loss = _loss_grad(h, tgt)
    per_layer = [None] * DEPTH
    for l in reversed(range(DEPTH)):
        dh, per_layer[l] = _layer_bwd(dh, mem, params[l], saved[l], bd64, bd128)
    grads = {n: jnp.stack([per_layer[l][n].reshape(W[n].shape[1:]) for l in range(DEPTH)]) for n in WEIGHTS}
    return loss, dh, grads


def _shard_rows(shard_shape):
    n = 1
    for d in shard_shape:
        n *= d
    return n // D_MODEL


def _to_shards(full, ax):
    shp = full.shape
    parts = full.reshape(shp[:ax] + (N_CHIPS, shp[ax] // N_CHIPS) + shp[ax + 1:])
    return jnp.moveaxis(parts, ax, 0)


def _from_shards(parts, ax):
    full = jnp.moveaxis(parts, 0, ax)
    shp = full.shape
    return full.reshape(shp[:ax] + (shp[ax] * shp[ax + 1],) + shp[ax + 2:])


def _flat_rows(total_rows):
    return -(-total_rows // (2 * _TH)) * (2 * _TH)


def _pack_small(arrs):
    flat = jnp.concatenate([a.reshape(-1).astype(_F32) for a in arrs])
    rows = -(-flat.shape[0] // (8 * D_MODEL)) * 8
    return jnp.pad(flat, (0, rows * D_MODEL - flat.shape[0])).reshape(rows, D_MODEL)


def _unpack_small(flat, shapes):
    flat = flat.reshape(-1)
    out, off = [], 0
    for shp in shapes:
        n = 1
        for d in shp:
            n *= d
        out.append(flat[off:off + n].reshape(shp))
        off += n
    return out


def _mesh_pos():
    return lax.axis_index("x"), lax.axis_index("y"), lax.axis_index("c")


def _other_chips(x, y):
    return [(1 - x, y), (x, 1 - y), (1 - x, 1 - y)]


_ANY = pl.BlockSpec(memory_space=pl.ANY)


def _allgather_weights(flat):
    R = flat.shape[0]
    half = R // 2

    def body(src, out, local_sem, send_sems, recv_sems):
        x, y, c = _mesh_pos()
        me = 2 * x + y
        sibling = (x, y, 1 - c)
        chips = _other_chips(x, y)
        mine = pl.ds(pl.multiple_of(c * half, 16), half)
        theirs = pl.ds(pl.multiple_of((1 - c) * half, 16), half)

        def copy(k, src_ref, dst_ref, to):
            return pltpu.make_async_remote_copy(src_ref=src_ref, dst_ref=dst_ref, send_sem=send_sems.at[k],
                                                recv_sem=recv_sems.at[k], device_id=to, device_id_type=MESH)

        own = pltpu.make_async_copy(src, out.at[me], local_sem)
        own.start()
        first = [copy(j, src.at[mine], out.at[me, mine], (*chip, c)) for j, chip in enumerate(chips)]
        for cp in first:
            cp.start()
        passed = []
        for j, (cx, cy) in enumerate(chips):
            slab = out.at[2 * cx + cy, mine]
            copy(j, slab, slab, (cx, cy, c)).wait_recv()
            fwd = copy(3 + j, slab, slab, sibling)
            fwd.start()
            passed.append(fwd)
        for j, (cx, cy) in enumerate(chips):
            slab = out.at[2 * cx + cy, theirs]
            copy(3 + j, slab, slab, sibling).wait_recv()
        for cp in first + passed:
            cp.wait_send()
        own.wait()

    return pl.pallas_call(
        body, name="allgather_weights", in_specs=[_ANY], out_specs=_ANY,
        out_shape=_sds((N_CHIPS,) + flat.shape, flat.dtype),
        scratch_shapes=[pltpu.SemaphoreType.DMA, pltpu.SemaphoreType.DMA((6,)), pltpu.SemaphoreType.DMA((6,))],
    )(flat)


def _exchange_halves(g):
    half = g.shape[2]

    def body(g_ref, got_ref, send_sems, recv_sems):
        x, y, c = _mesh_pos()
        sibling = (x, y, 1 - c)
        copies = []
        for k in range(N_CHIPS):
            cp = pltpu.make_async_remote_copy(src_ref=g_ref.at[k, 1 - c], dst_ref=got_ref.at[k],
                                              send_sem=send_sems.at[k], recv_sem=recv_sems.at[k],
                                              device_id=sibling, device_id_type=MESH)
            cp.start()
            copies.append(cp)
        for cp in copies:
            cp.wait()

    return pl.pallas_call(
        body, name="grad_exchange_halves", in_specs=[_ANY], out_specs=_ANY,
        out_shape=_sds((N_CHIPS, half, D_MODEL), _F32),
        scratch_shapes=[pltpu.SemaphoreType.DMA((N_CHIPS,)), pltpu.SemaphoreType.DMA((N_CHIPS,))],
    )(g)


def _add_halves(g, got, c_idx):
    half = g.shape[2]
    ta = _TH

    def body(c_ref, a_ref, b_ref, o_ref):
        o_ref[...] = a_ref[0] + b_ref[...]

    return pl.pallas_call(
        body, name="grad_add_halves",
        grid_spec=pltpu.PrefetchScalarGridSpec(
            num_scalar_prefetch=1, grid=(N_CHIPS, half // ta),
            in_specs=[pl.BlockSpec((1, 1, ta, D_MODEL), lambda k, i, c: (k, c[0], i, 0)),
                      pl.BlockSpec((1, ta, D_MODEL), lambda k, i, c: (k, i, 0))],
            out_specs=pl.BlockSpec((1, ta, D_MODEL), lambda k, i, c: (k, i, 0))),
        out_shape=_sds((N_CHIPS, half, D_MODEL), _F32),
    )(c_idx, g, got)


def _scatter_to_chips(p):
    def body(p_ref, got_ref, local_sem, send_sems, recv_sems):
        x, y, c = _mesh_pos()
        me = 2 * x + y
        own = pltpu.make_async_copy(p_ref.at[me], got_ref.at[me], local_sem)
        own.start()
        copies = []
        for j, (cx, cy) in enumerate(_other_chips(x, y)):
            cp = pltpu.make_async_remote_copy(src_ref=p_ref.at[2 * cx + cy], dst_ref=got_ref.at[me],
                                              send_sem=send_sems.at[j], recv_sem=recv_sems.at[j],
                                              device_id=(cx, cy, c), device_id_type=MESH)
            cp.start()
            copies.append(cp)
        for j, (cx, cy) in enumerate(_other_chips(x, y)):
            slab = got_ref.at[2 * cx + cy]
            pltpu.make_async_remote_copy(src_ref=slab, dst_ref=slab, send_sem=send_sems.at[j],
                                         recv_sem=recv_sems.at[j], device_id=(cx, cy, c),
                                         device_id_type=MESH).wait_recv()
        for cp in copies:
            cp.wait_send()
        own.wait()

    return pl.pallas_call(
        body, name="grad_scatter_chips", in_specs=[_ANY], out_specs=_ANY, out_shape=_sds(p.shape, _F32),
        scratch_shapes=[pltpu.SemaphoreType.DMA, pltpu.SemaphoreType.DMA((3,)), pltpu.SemaphoreType.DMA((3,))],
    )(p)


def _sum_chips(got):
    half = got.shape[1]
    ta = _TH

    def body(a_ref, o_ref):
        o_ref[...] = ((a_ref[0] + a_ref[1]) + a_ref[2]) + a_ref[3]

    return pl.pallas_call(
        body, name="grad_sum_chips", grid=(half // ta,),
        in_specs=[pl.BlockSpec((N_CHIPS, ta, D_MODEL), lambda i: (0, i, 0))],
        out_specs=pl.BlockSpec((ta, D_MODEL), lambda i: (i, 0)), out_shape=_sds((half, D_MODEL), _F32),
    )(got)


def _share_with_sibling(s):
    def body(s_ref, out_ref, local_sem, send_sem, recv_sem):
        x, y, c = _mesh_pos()
        own = pltpu.make_async_copy(s_ref, out_ref.at[c], local_sem)
        own.start()
        cp = pltpu.make_async_remote_copy(src_ref=s_ref, dst_ref=out_ref.at[c], send_sem=send_sem, recv_sem=recv_sem,
                                          device_id=(x, y, 1 - c), device_id_type=MESH)
        cp.start()
        theirs = out_ref.at[1 - c]
        pltpu.make_async_remote_copy(src_ref=theirs, dst_ref=theirs, send_sem=send_sem, recv_sem=recv_sem,
                                     device_id=(x, y, 1 - c), device_id_type=MESH).wait_recv()
        cp.wait_send()
        own.wait()

    return pl.pallas_call(
        body, name="grad_share_sibling", in_specs=[_ANY], out_specs=_ANY, out_shape=_sds((2,) + s.shape, _F32),
        scratch_shapes=[pltpu.SemaphoreType.DMA, pltpu.SemaphoreType.DMA, pltpu.SemaphoreType.DMA],
    )(s)


def _allreduce_small(g):
    rows = g.shape[0]
    n_dev = 2 * N_CHIPS

    def body(g_ref, out_ref, gathered, local_sem, send_sems, recv_sems):
        x, y, c = _mesh_pos()
        me = 4 * x + 2 * y + c
        own = pltpu.make_async_copy(g_ref, gathered.at[me], local_sem)
        own.start()
        copies = []
        for k in range(1, n_dev):
            fx, fy, fc = (k >> 2) & 1, (k >> 1) & 1, k & 1
            cp = pltpu.make_async_remote_copy(
                src_ref=g_ref, dst_ref=gathered.at[me], send_sem=send_sems.at[k - 1], recv_sem=recv_sems.at[k - 1],
                device_id=(x ^ fx, y ^ fy, c ^ fc), device_id_type=MESH)
            cp.start()
            copies.append(cp)
        for k in range(1, n_dev):
            fx, fy, fc = (k >> 2) & 1, (k >> 1) & 1, k & 1
            px, py, pc = x ^ fx, y ^ fy, c ^ fc
            slab = gathered.at[4 * px + 2 * py + pc]
            pltpu.make_async_remote_copy(src_ref=slab, dst_ref=slab, send_sem=send_sems.at[k - 1],
                                         recv_sem=recv_sems.at[k - 1], device_id=(px, py, pc),
                                         device_id_type=MESH).wait_recv()
        for cp in copies:
            cp.wait_send()
        own.wait()
        acc = gathered[0]
        for d in range(1, n_dev):
            acc = acc + gathered[d]
        out_ref[...] = acc

    vmem = pl.BlockSpec(memory_space=pltpu.VMEM)
    return pl.pallas_call(
        body, name="allreduce_small", in_specs=[vmem], out_specs=vmem, out_shape=_sds((rows, D_MODEL), _F32),
        scratch_shapes=[pltpu.VMEM((n_dev, rows, D_MODEL), _F32), pltpu.SemaphoreType.DMA,
                        pltpu.SemaphoreType.DMA((n_dev - 1,)), pltpu.SemaphoreType.DMA((n_dev - 1,))],
    )(g)


def _adamw(w, g, m, v, name):
    shape = w.shape
    cols = shape[-1]
    rows = 1
    for d in shape[:-1]:
        rows *= d
    w2, g2, m2, v2 = (a.reshape(rows, cols) for a in (w, g, m, v))
    tr = _pick_tile(rows, (512, 256, 128, 64, 32, 16, 8))

    def body(w_ref, g_ref, m_ref, v_ref, d_ref, nm_ref, nv_ref):
        gg = g_ref[...]
        nm = ADAM_B1 * m_ref[...] + (1.0 - ADAM_B1) * gg
        nv = ADAM_B2 * v_ref[...] + (1.0 - ADAM_B2) * (gg * gg)
        m_hat = nm / (1.0 - ADAM_B1 ** ADAM_STEP)
        v_hat = nv / (1.0 - ADAM_B2 ** ADAM_STEP)
        d_ref[...] = -ADAM_LR * (m_hat / (jnp.sqrt(v_hat) + ADAM_EPS) + ADAM_WD * w_ref[...])
        nm_ref[...] = nm
        nv_ref[...] = nv

    spec = pl.BlockSpec((tr, cols), lambda i: (i, 0))
    outs = pl.pallas_call(
        body, name=name, grid=(rows // tr,), in_specs=[spec] * 4, out_specs=[spec] * 3,
        out_shape=[_sds((rows, cols), _F32)] * 3,
    )(w2, g2, m2, v2)
    return tuple(o.reshape(shape) for o in outs)


def kernel(x, mem, g_mix, w_in, b_forget, g_q_fox, g_k_fox, w_pool, pool_scale, w_out, g_mem_q, g_mem_kv, w_mem_q, w_mem_kv, g_q_mem, g_k_mem, w_mem_out, g_ffn, w_gate_up, w_down, loss_target, m_g_mix, m_w_in, m_b_forget, m_g_q_fox, m_g_k_fox, m_w_pool, m_pool_scale, m_w_out, m_g_mem_q, m_g_mem_kv, m_w_mem_q, m_w_mem_kv, m_g_q_mem, m_g_k_mem, m_w_mem_out, m_g_ffn, m_w_gate_up, m_w_down, v_g_mix, v_w_in, v_b_forget, v_g_q_fox, v_g_k_fox, v_w_pool, v_pool_scale, v_w_out, v_g_mem_q, v_g_mem_kv, v_w_mem_q, v_w_mem_kv, v_g_q_mem, v_g_k_mem, v_w_mem_out, v_g_ffn, v_w_gate_up, v_w_down):
    w = dict(g_mix=g_mix, w_in=w_in, b_forget=b_forget, g_q_fox=g_q_fox, g_k_fox=g_k_fox, w_pool=w_pool,
             pool_scale=pool_scale, w_out=w_out, g_mem_q=g_mem_q, g_mem_kv=g_mem_kv, w_mem_q=w_mem_q,
             w_mem_kv=w_mem_kv, g_q_mem=g_q_mem, g_k_mem=g_k_mem, w_mem_out=w_mem_out, g_ffn=g_ffn,
             w_gate_up=w_gate_up, w_down=w_down)
    m = dict(g_mix=m_g_mix, w_in=m_w_in, b_forget=m_b_forget, g_q_fox=m_g_q_fox, g_k_fox=m_g_k_fox, w_pool=m_w_pool,
             pool_scale=m_pool_scale, w_out=m_w_out, g_mem_q=m_g_mem_q, g_mem_kv=m_g_mem_kv, w_mem_q=m_w_mem_q,
             w_mem_kv=m_w_mem_kv, g_q_mem=m_g_q_mem, g_k_mem=m_g_k_mem, w_mem_out=m_w_mem_out, g_ffn=m_g_ffn,
             w_gate_up=m_w_gate_up, w_down=m_w_down)
    v = dict(g_mix=v_g_mix, w_in=v_w_in, b_forget=v_b_forget, g_q_fox=v_g_q_fox, g_k_fox=v_g_k_fox, w_pool=v_w_pool,
             pool_scale=v_pool_scale, w_out=v_w_out, g_mem_q=v_g_mem_q, g_mem_kv=v_g_mem_kv, w_mem_q=v_w_mem_q,
             w_mem_kv=v_w_mem_kv, g_q_mem=v_g_q_mem, g_k_mem=v_g_k_mem, w_mem_out=v_w_mem_out, g_ffn=v_g_ffn,
             w_gate_up=v_w_gate_up, w_down=v_w_down)

    shard_rows = [_shard_rows(w[n].shape) for n in BIG]
    slot_rows = [-(-r // 16) * 16 for r in shard_rows]
    total = sum(slot_rows)
    R = _flat_rows(total)
    flat = jnp.concatenate([jnp.pad(w[n].astype(_MXU).reshape(-1, D_MODEL), ((0, s - r), (0, 0)))
                            for n, r, s in zip(BIG, shard_rows, slot_rows)]
                           + [jnp.zeros((R - total, D_MODEL), _MXU)])
    gathered = _allgather_weights(flat)
    W = {n: w[n] for n in SMALL}
    off = 0
    for n, r, s in zip(BIG, shard_rows, slot_rows):
        parts = gathered[:, off:off + r].reshape((N_CHIPS,) + w[n].shape)
        W[n] = _from_shards(parts, SHARD_AXIS[n])
        off += s

    loss, grad_x, grads = _device_grads(x[0], mem[0], loss_target[0], W)
    loss = lax.psum(loss[0, 0], ("x", "y", "c"))

    half = R // 2
    gflat = jnp.concatenate(
        [jnp.pad(_to_shards(grads[n], SHARD_AXIS[n]).reshape(N_CHIPS, -1, D_MODEL), ((0, 0), (0, s - r), (0, 0)))
         for n, r, s in zip(BIG, shard_rows, slot_rows)]
        + [jnp.zeros((N_CHIPS, R - total, D_MODEL), _F32)], axis=1).reshape(N_CHIPS, 2, half, D_MODEL)
    c_idx = lax.axis_index("c").astype(jnp.int32).reshape(1)
    partial = _add_halves(gflat, _exchange_halves(gflat), c_idx)
    reduced = _share_with_sibling(_sum_chips(_scatter_to_chips(partial))).reshape(R, D_MODEL)

    small_shapes = [w[n].shape for n in SMALL]
    gsmall = _unpack_small(_allreduce_small(_pack_small([grads[n] for n in SMALL])), small_shapes)

    g_out, d_out, m_out, v_out = {}, {}, {}, {}
    off = 0
    for n, r, s in zip(BIG, shard_rows, slot_rows):
        g_out[n] = reduced[off:off + r].reshape(w[n].shape)
        d_out[n], m_out[n], v_out[n] = _adamw(w[n], g_out[n], m[n], v[n], "adamw_" + n)
        off += s
    packed = [_pack_small([t[n] for n in SMALL]) for t in (w, m, v)]
    ds, ms, vs = _adamw(packed[0], _pack_small(gsmall), packed[1], packed[2], "adamw_small")
    for n, gi, di, mi, vi in zip(SMALL, gsmall, _unpack_small(ds, small_shapes), _unpack_small(ms, small_shapes),
                                 _unpack_small(vs, small_shapes)):
        g_out[n], d_out[n], m_out[n], v_out[n] = gi, di, mi, vi

    return (loss, grad_x[None], *[g_out[n] for n in WEIGHTS], *[d_out[n] for n in WEIGHTS],
            *[m_out[n] for n in WEIGHTS], *[v_out[n] for n in WEIGHTS])
```

```python
import functools

import jax
import jax.numpy as jnp
from jax import lax
from jax.experimental import pallas as pl
from jax.experimental.pallas import tpu as pltpu

_F32 = jnp.float32
_MXU = jnp.bfloat16

D_MODEL = 1024
DEPTH = 2
FOX_HEADS = 8
FOX_HEAD_DIM = 64
FOX_WIDTH = FOX_HEADS * FOX_HEAD_DIM
POOL_WINDOWS = (2, 4, 8, 16)
POOL_GROUP_DIM = 128
POOL_WIDTH = len(POOL_WINDOWS) * POOL_GROUP_DIM
MEM_HEADS = 4
MEM_HEAD_DIM = 128
MEM_WIDTH = MEM_HEADS * MEM_HEAD_DIM
D_FF = 2816
EPS = 1e-6
LANES = 128
HALO = 16

ADAM_LR = 0.001
ADAM_B1 = 0.9
ADAM_B2 = 0.999
ADAM_EPS = 1e-08
ADAM_WD = 0.01
ADAM_STEP = 10

N_CHIPS = 4
MESH = pl.DeviceIdType.MESH

_TM = 256
_TQ = 1024
_TMF = 512
_TF = 256
_TT = 512
_TB = 256
_TA = 512
_TH = 384

BIG = ("w_in", "w_out", "w_mem_q", "w_mem_kv", "w_mem_out", "w_gate_up", "w_down")
SHARD_AXIS = {"w_in": 2, "w_out": 1, "w_mem_q": 1, "w_mem_kv": 1, "w_mem_out": 2, "w_gate_up": 2, "w_down": 1}
SMALL = ("g_mix", "b_forget", "g_q_fox", "g_k_fox", "w_pool", "pool_scale", "g_mem_q", "g_mem_kv", "g_q_mem",
         "g_k_mem", "g_ffn")
WEIGHTS = ("g_mix", "w_in", "b_forget", "g_q_fox", "g_k_fox", "w_pool", "pool_scale", "w_out", "g_mem_q", "g_mem_kv",
           "w_mem_q", "w_mem_kv", "g_q_mem", "g_k_mem", "w_mem_out", "g_ffn", "w_gate_up", "w_down")


def _dot(a, b):
    return jnp.dot(a, b, preferred_element_type=_F32)


def _dot_nt(a, b):
    return lax.dot_general(a, b, (((1,), (1,)), ((), ())), preferred_element_type=_F32)


def _dot_tn(a, b):
    return lax.dot_general(a, b, (((0,), (0,)), ((), ())), preferred_element_type=_F32)


def _group_sum(x, ones_blockdiag):
    hi = x.astype(_MXU)
    lo = (x - hi.astype(_F32)).astype(_MXU)
    return _dot(hi, ones_blockdiag) + _dot(lo, ones_blockdiag)


def _tri_dot(tri, x):
    h1 = x.astype(jnp.bfloat16)
    r1 = x - h1.astype(_F32)
    h2 = r1.astype(jnp.bfloat16)
    h3 = (r1 - h2.astype(_F32)).astype(jnp.bfloat16)
    return _dot(tri, h1) + _dot(tri, h2) + _dot(tri, h3)


def _rstd(x):
    return lax.rsqrt(jnp.mean(x * x, axis=-1, keepdims=True) + EPS)


def _norm_bwd(dy, x, g):
    r = _rstd(x)
    xhat = x * r
    u = dy * g
    dx = r * (u - xhat * jnp.mean(u * xhat, axis=-1, keepdims=True))
    return dx, jnp.sum(dy * xhat, axis=0, keepdims=True)


def _headnorm_bwd(dy, x, g, ones_blockdiag, width):
    r = lax.rsqrt(_group_sum(x * x, ones_blockdiag) * (1.0 / width) + EPS)
    xhat = x * r
    u = dy * g
    dx = r * (u - xhat * (_group_sum(u * xhat, ones_blockdiag) * (1.0 / width)))
    return dx, jnp.sum(dy * xhat, axis=0, keepdims=True)


def _fold_heads(row, heads, width):
    acc = row[:, 0:width]
    for h in range(1, heads):
        acc = acc + row[:, h * width:(h + 1) * width]
    return acc


def _resident(shape):
    return pl.BlockSpec(shape, lambda *_: (0,) * len(shape), pipeline_mode=pl.Buffered(1))


def _rows(tm, width):
    return pl.BlockSpec((tm, width), lambda i: (i, 0))


def _blockdiag_ones(groups, width):
    return jnp.kron(jnp.eye(groups, dtype=_F32), jnp.ones((width, width), _F32)).astype(_MXU)


def _sds(shape, dtype):
    return jax.ShapeDtypeStruct(shape, dtype)


def _mix_in_fwd(h, g_mix, w_main, w_f, b_f, gq, gk, bd64):
    T = h.shape[0]
    tm = min(_TM, T)

    def body(h_ref, g_ref, wm_ref, wf_ref, bf_ref, gq_ref, gk_ref, bd_ref,
             xn_ref, zqk_ref, qn_ref, kn_ref, v_ref, pin_ref, fl_ref):
        x = h_ref[...]
        xn = ((x * _rstd(x)) * g_ref[...]).astype(_MXU)
        xn_ref[...] = xn
        z = _dot(xn, wm_ref[...])
        q = z[:, :FOX_WIDTH]
        k = z[:, FOX_WIDTH:2 * FOX_WIDTH]
        zqk_ref[...] = z[:, :2 * FOX_WIDTH]
        bd = bd_ref[...]
        rq = lax.rsqrt(_group_sum(q * q, bd) * (1.0 / FOX_HEAD_DIM) + EPS)
        rk = lax.rsqrt(_group_sum(k * k, bd) * (1.0 / FOX_HEAD_DIM) + EPS)
        qn_ref[...] = ((q * rq) * gq_ref[...]).astype(_MXU)
        kn_ref[...] = ((k * rk) * gk_ref[...]).astype(_MXU)
        v_ref[...] = z[:, 2 * FOX_WIDTH:3 * FOX_WIDTH].astype(_MXU)
        pin_ref[...] = z[:, 3 * FOX_WIDTH:]
        fl_ref[...] = _dot(xn, wf_ref[...]) + bf_ref[...]

    return pl.pallas_call(
        body, name="mix_in_fwd", grid=(T // tm,),
        in_specs=[_rows(tm, D_MODEL), _resident((1, D_MODEL)), _resident(w_main.shape), _resident(w_f.shape),
                  _resident((1, LANES)), _resident((1, FOX_WIDTH)), _resident((1, FOX_WIDTH)), _resident(bd64.shape)],
        out_specs=[_rows(tm, D_MODEL), _rows(tm, 2 * FOX_WIDTH), _rows(tm, FOX_WIDTH), _rows(tm, FOX_WIDTH),
                   _rows(tm, FOX_WIDTH), _rows(tm, POOL_WIDTH), _rows(tm, LANES)],
        out_shape=[_sds((T, D_MODEL), _MXU), _sds((T, 2 * FOX_WIDTH), _F32), _sds((T, FOX_WIDTH), _MXU),
                   _sds((T, FOX_WIDTH), _MXU), _sds((T, FOX_WIDTH), _MXU), _sds((T, POOL_WIDTH), _F32),
                   _sds((T, LANES), _F32)],
    )(h, g_mix, w_main, w_f, b_f, gq, gk, bd64)


def _gate_fwd(fl):
    T = fl.shape[0]
    tb = min(_TB, T)

    def body(fl_ref, c_ref, carry):
        @pl.when(pl.program_id(0) == 0)
        def _():
            carry[...] = jnp.zeros_like(carry)

        x = fl_ref[...]
        ls = jnp.minimum(x, 0.0) - jnp.log1p(jnp.exp(-jnp.abs(x)))
        row = lax.broadcasted_iota(jnp.int32, (tb, tb), 0)
        col = lax.broadcasted_iota(jnp.int32, (tb, tb), 1)
        tri = jnp.where(col <= row, 1.0, 0.0).astype(jnp.bfloat16)
        cs = _tri_dot(tri, ls) + carry[...]
        c_ref[...] = cs
        carry[...] = cs[tb - 1:tb, :]

    return pl.pallas_call(
        body, name="gate_fwd", grid=(T // tb,), in_specs=[_rows(tb, LANES)], out_specs=_rows(tb, LANES),
        out_shape=_sds((T, LANES), _F32), scratch_shapes=[pltpu.VMEM((1, LANES), _F32)],
    )(fl)


def _gate_bwd(drs, dcs, fl):
    T = fl.shape[0]
    tb = min(_TB, T)
    nb = T // tb

    def body(r_ref, d_ref, fl_ref, df_ref, carry):
        @pl.when(pl.program_id(0) == 0)
        def _():
            carry[...] = jnp.zeros_like(carry)

        row = lax.broadcasted_iota(jnp.int32, (tb, tb), 0)
        col = lax.broadcasted_iota(jnp.int32, (tb, tb), 1)
        tri = jnp.where(col >= row, 1.0, 0.0).astype(jnp.bfloat16)
        rc = _tri_dot(tri, r_ref[...] - d_ref[...]) + carry[...]
        carry[...] = rc[0:1, :]
        df_ref[...] = rc * (1.0 / (1.0 + jnp.exp(fl_ref[...])))

    rev = pl.BlockSpec((tb, LANES), lambda i: (nb - 1 - i, 0))
    return pl.pallas_call(
        body, name="gate_bwd", grid=(nb,), in_specs=[rev, rev, rev], out_specs=rev,
        out_shape=_sds((T, LANES), _F32), scratch_shapes=[pltpu.VMEM((1, LANES), _F32)],
    )(drs, dcs, fl)


def _fox_scores(q_ref, k_ref, cc, cr, hh, masked):
    sl = slice(FOX_HEAD_DIM * hh, FOX_HEAD_DIM * (hh + 1))
    s = _dot_nt(q_ref[:, sl], k_ref[:, sl]) * (FOX_HEAD_DIM ** -0.5) + (cc[:, hh:hh + 1] - cr[hh:hh + 1, :])
    if masked:
        row = lax.broadcasted_iota(jnp.int32, s.shape, 0)
        col = lax.broadcasted_iota(jnp.int32, s.shape, 1)
        s = jnp.where(col <= row, s, -jnp.inf)
    return s


def _fox_fwd(qn, kn, v, ccol, crow):
    T = qn.shape[0]
    tq = min(_TQ, T)
    nq = T // tq
    pair = 2 * FOX_HEAD_DIM

    def body(q_ref, k_ref, v_ref, cc_ref, cr_ref, o_ref, lse_ref, m_s, l_s, acc_s):
        i = pl.program_id(1)
        j = pl.program_id(2)

        @pl.when(j == 0)
        def _():
            m_s[...] = jnp.full(m_s.shape, -jnp.inf, _F32)
            l_s[...] = jnp.zeros_like(l_s)
            acc_s[...] = jnp.zeros_like(acc_s)

        def step(masked):
            cc = cc_ref[0]
            cr = cr_ref[0]
            for hh in range(2):
                s = _fox_scores(q_ref, k_ref, cc, cr, hh, masked)
                m_prev = m_s[hh]
                m_new = jnp.maximum(m_prev, jnp.max(s, axis=-1, keepdims=True))
                alpha = jnp.exp(m_prev - m_new)
                p = jnp.exp(s - m_new)
                l_s[hh] = alpha * l_s[hh] + jnp.sum(p, axis=-1, keepdims=True)
                vv = v_ref[:, FOX_HEAD_DIM * hh:FOX_HEAD_DIM * (hh + 1)]
                acc_s[hh] = alpha * acc_s[hh] + _dot(p.astype(_MXU), vv)
                m_s[hh] = m_new

        @pl.when(j < i)
        def _():
            step(False)

        @pl.when(j == i)
        def _():
            step(True)
            outs, lses = [], []
            for hh in range(2):
                l = l_s[hh]
                outs.append(acc_s[hh] / l)
                lses.append(jnp.broadcast_to(m_s[hh] + jnp.log(l), (tq, FOX_HEAD_DIM)))
            o_ref[...] = jnp.concatenate(outs, axis=-1).astype(_MXU)
            lse_ref[...] = jnp.concatenate(lses, axis=-1)

    qspec = pl.BlockSpec((tq, pair), lambda p, i, j: (i, p))
    kspec = pl.BlockSpec((tq, pair), lambda p, i, j: (jnp.minimum(j, i), p))
    return pl.pallas_call(
        body, name="fox_fwd", grid=(FOX_HEADS // 2, nq, nq),
        in_specs=[qspec, kspec, kspec,
                  pl.BlockSpec((1, tq, LANES), lambda p, i, j: (p, i, 0)),
                  pl.BlockSpec((1, 8, tq), lambda p, i, j: (p, 0, jnp.minimum(j, i)))],
        out_specs=[qspec, qspec],
        out_shape=[_sds((T, FOX_WIDTH), _MXU), _sds((T, FOX_WIDTH), _F32)],
        scratch_shapes=[pltpu.VMEM((2, tq, 1), _F32), pltpu.VMEM((2, tq, 1), _F32),
                        pltpu.VMEM((2, tq, FOX_HEAD_DIM), _F32)],
    )(qn, kn, v, ccol, crow)


def _pool_window_sum(ext, w, forward):
    n = ext.shape[0]
    sm = ext
    k = 1
    while k < w:
        sm = sm + pltpu.roll(sm, (n - k) if forward else k, axis=0)
        k *= 2
    return sm


def _out_proj_fwd(h, o, pin, w_pool, pscale, w_out):
    T = h.shape[0]
    tm = min(_TM, T)
    hb = tm // HALO

    def body(h_ref, o_ref, pin_ref, halo_ref, wp_ref, ps_ref, wo_ref, h1_ref, mixed_ref, y_ref):
        i = pl.program_id(0)
        pin_t = pin_ref[...]
        halo = jnp.where(i == 0, 0.0, halo_ref[...])
        ext = jnp.concatenate([halo, pin_t], axis=0)
        t = (i * tm + lax.broadcasted_iota(jnp.int32, (tm, 1), 0) + 1).astype(_F32)
        mixed, ys = [], []
        for g, w in enumerate(POOL_WINDOWS):
            sl = slice(g * POOL_GROUP_DIM, (g + 1) * POOL_GROUP_DIM)
            win = _pool_window_sum(ext[:, sl], w, False)[HALO:, :]
            mg = (win / jnp.minimum(t, float(w)) - pin_t[:, sl]).astype(_MXU)
            mixed.append(mg)
            ys.append(_dot(mg, wp_ref[g]))
        mixed_ref[...] = jnp.concatenate(mixed, axis=-1)
        y = (jnp.concatenate(ys, axis=-1) * ps_ref[...]).astype(_MXU)
        y_ref[...] = y
        h1_ref[...] = h_ref[...] + _dot(o_ref[...], wo_ref[:FOX_WIDTH, :]) + _dot(y, wo_ref[FOX_WIDTH:, :])

    return pl.pallas_call(
        body, name="out_proj_fwd", grid=(T // tm,),
        in_specs=[_rows(tm, D_MODEL), _rows(tm, FOX_WIDTH), _rows(tm, POOL_WIDTH),
                  pl.BlockSpec((HALO, POOL_WIDTH), lambda i: (jnp.maximum(i * hb - 1, 0), 0)),
                  _resident(w_pool.shape), _resident((1, POOL_WIDTH)), _resident(w_out.shape)],
        out_specs=[_rows(tm, D_MODEL), _rows(tm, POOL_WIDTH), _rows(tm, POOL_WIDTH)],
        out_shape=[_sds((T, D_MODEL), _F32), _sds((T, POOL_WIDTH), _MXU), _sds((T, POOL_WIDTH), _MXU)],
    )(h, o, pin, pin, w_pool, pscale, w_out)


def _mem_kv_fwd(mem, g_kv, w_kv, gkm, bd128):
    M = mem.shape[0]

    def body(mem_ref, g_ref, w_ref, gk_ref, bd_ref, mn_ref, mkv_ref, mk_ref, mv_ref):
        x = mem_ref[...]
        mn = ((x * _rstd(x)) * g_ref[...]).astype(_MXU)
        mn_ref[...] = mn
        z = _dot(mn, w_ref[...])
        mkv_ref[...] = z
        k = z[:, :MEM_WIDTH]
        rk = lax.rsqrt(_group_sum(k * k, bd_ref[...]) * (1.0 / MEM_HEAD_DIM) + EPS)
        mk_ref[...] = ((k * rk) * gk_ref[...]).astype(_MXU)
        mv_ref[...] = z[:, MEM_WIDTH:].astype(_MXU)

    return pl.pallas_call(
        body, name="mem_kv_fwd",
        out_shape=[_sds((M, D_MODEL), _MXU), _sds((M, 2 * MEM_WIDTH), _F32), _sds((M, MEM_WIDTH), _MXU),
                   _sds((M, MEM_WIDTH), _MXU)],
    )(mem, g_kv, w_kv, gkm, bd128)


def _mem_softmax(qn, mk_ref, hd):
    sl = slice(hd * MEM_HEAD_DIM, (hd + 1) * MEM_HEAD_DIM)
    s = _dot_nt(qn[:, sl], mk_ref[:, sl]) * (MEM_HEAD_DIM ** -0.5)
    e = jnp.exp(s - jnp.max(s, axis=-1, keepdims=True))
    return e / jnp.sum(e, axis=-1, keepdims=True)


def _mem_attn_fwd(h1, g_q, w_q, gqm, bd128, mk, mv, w_mo):
    T = h1.shape[0]
    tm = min(_TM, T)

    def body(h_ref, g_ref, wq_ref, gq_ref, bd_ref, mk_ref, mv_ref, wo_ref, h2_ref, hn_ref, mo_ref):
        x = h_ref[...]
        hn = ((x * _rstd(x)) * g_ref[...]).astype(_MXU)
        hn_ref[...] = hn
        mq = _dot(hn, wq_ref[...])
        rq = lax.rsqrt(_group_sum(mq * mq, bd_ref[...]) * (1.0 / MEM_HEAD_DIM) + EPS)
        qn = ((mq * rq) * gq_ref[...]).astype(_MXU)
        outs = []
        for hd in range(MEM_HEADS):
            p = _mem_softmax(qn, mk_ref, hd).astype(_MXU)
            outs.append(_dot(p, mv_ref[:, hd * MEM_HEAD_DIM:(hd + 1) * MEM_HEAD_DIM]))
        mo = jnp.concatenate(outs, axis=-1).astype(_MXU)
        mo_ref[...] = mo
        h2_ref[...] = x + _dot(mo, wo_ref[...])

    return pl.pallas_call(
        body, name="mem_attn_fwd", grid=(T // tm,),
        in_specs=[_rows(tm, D_MODEL), _resident((1, D_MODEL)), _resident(w_q.shape), _resident((1, MEM_WIDTH)),
                  _resident(bd128.shape), _resident(mk.shape), _resident(mv.shape), _resident(w_mo.shape)],
        out_specs=[_rows(tm, D_MODEL), _rows(tm, D_MODEL), _rows(tm, MEM_WIDTH)],
        out_shape=[_sds((T, D_MODEL), _F32), _sds((T, D_MODEL), _MXU), _sds((T, MEM_WIDTH), _MXU)],
    )(h1, g_q, w_q, gqm, bd128, mk, mv, w_mo)


def _ffn_fwd(h2, g_ffn, w_gu, w_d):
    T = h2.shape[0]
    tm = min(_TMF, T)
    nf = D_FF // _TF

    def body(h_ref, g_ref, wg_ref, wu_ref, wd_ref, h3_ref, hn_ref, acc, xn_s):
        j = pl.program_id(1)

        @pl.when(j == 0)
        def _():
            x = h_ref[...]
            xn = ((x * _rstd(x)) * g_ref[...]).astype(_MXU)
            xn_s[...] = xn
            hn_ref[...] = xn
            acc[...] = jnp.zeros_like(acc)

        xn = xn_s[...]
        g = _dot(xn, wg_ref[...])
        u = _dot(xn, wu_ref[...])
        a = ((g * jax.nn.sigmoid(g)) * u).astype(_MXU)
        acc[...] += _dot(a, wd_ref[...])

        @pl.when(j == nf - 1)
        def _():
            h3_ref[...] = h_ref[...] + acc[...]

    tok = pl.BlockSpec((tm, D_MODEL), lambda i, j: (i, 0))
    return pl.pallas_call(
        body, name="ffn_fwd", grid=(T // tm, nf),
        in_specs=[tok, pl.BlockSpec((1, D_MODEL), lambda i, j: (0, 0)),
                  pl.BlockSpec((D_MODEL, _TF), lambda i, j: (0, j)),
                  pl.BlockSpec((D_MODEL, _TF), lambda i, j: (0, j + nf)),
                  pl.BlockSpec((_TF, D_MODEL), lambda i, j: (j, 0))],
        out_specs=[tok, tok],
        out_shape=[_sds((T, D_MODEL), _F32), _sds((T, D_MODEL), _MXU)],
        scratch_shapes=[pltpu.VMEM((tm, D_MODEL), _F32), pltpu.VMEM((tm, D_MODEL), _MXU)],
    )(h2, g_ffn, w_gu, w_gu, w_d)


def _loss_grad(y, tgt):
    T = y.shape[0]
    tm = min(_TA, T)

    def body(y_ref, t_ref, dy_ref, loss_ref):
        @pl.when(pl.program_id(0) == 0)
        def _():
            loss_ref[...] = jnp.zeros_like(loss_ref)

        err = y_ref[...] - t_ref[...]
        dy_ref[...] = err * (1.0 / D_MODEL)
        part = jnp.sum(jnp.sum(err * err, axis=0, keepdims=True), axis=1, keepdims=True)
        loss_ref[...] += part * (0.5 / D_MODEL)

    return pl.pallas_call(
        body, name="loss_grad", grid=(T // tm,), in_specs=[_rows(tm, D_MODEL), _rows(tm, D_MODEL)],
        out_specs=[_rows(tm, D_MODEL), pl.BlockSpec((1, 1), lambda i: (0, 0))],
        out_shape=[_sds((T, D_MODEL), _F32), _sds((1, 1), _F32)],
    )(y, tgt)


def _pick_tile(n, candidates=(1408, 1024, 512, 256, 128)):
    for c in candidates:
        if n % c == 0:
            return c
    return n


def _matmul_tn(a, b, name):
    T, K = a.shape
    N = b.shape[1]
    tk, tn, tt = _pick_tile(K), _pick_tile(N), min(_TT, T)

    def body(a_ref, b_ref, o_ref):
        @pl.when(pl.program_id(2) == 0)
        def _():
            o_ref[...] = jnp.zeros_like(o_ref)

        o_ref[...] += _dot_tn(a_ref[...].astype(_MXU), b_ref[...].astype(_MXU))

    return pl.pallas_call(
        body, name=name, grid=(K // tk, N // tn, T // tt),
        in_specs=[pl.BlockSpec((tt, tk), lambda i, j, t: (t, i)), pl.BlockSpec((tt, tn), lambda i, j, t: (t, j))],
        out_specs=pl.BlockSpec((tk, tn), lambda i, j, t: (i, j)),
        out_shape=_sds((K, N), _F32),
    )(a, b)


def _ffn_bwd(dh3, h2, hn, g_ffn, w_gu, w_d):
    T = h2.shape[0]
    tm = min(_TMF, T)
    nf = D_FF // _TF

    def body(dh_ref, h_ref, hn_ref, g_ref, wg_ref, wu_ref, wd_ref, dh2_ref, act_ref, dg_ref, du_ref, dgain_ref,
             acc, dyb):
        i = pl.program_id(0)
        j = pl.program_id(1)

        @pl.when((i == 0) & (j == 0))
        def _():
            dgain_ref[...] = jnp.zeros_like(dgain_ref)

        @pl.when(j == 0)
        def _():
            dyb[...] = dh_ref[...].astype(_MXU)
            acc[...] = jnp.zeros_like(acc)

        xn = hn_ref[...]
        g = _dot(xn, wg_ref[...])
        u = _dot(xn, wu_ref[...])
        sg = jax.nn.sigmoid(g)
        sl = g * sg
        act_ref[...] = (sl * u).astype(_MXU)
        da = _dot_nt(dyb[...], wd_ref[...])
        dgate = (da * u * (sg * (1.0 + g * (1.0 - sg)))).astype(_MXU)
        dup = (da * sl).astype(_MXU)
        dg_ref[...] = dgate
        du_ref[...] = dup
        acc[...] += _dot_nt(dgate, wg_ref[...]) + _dot_nt(dup, wu_ref[...])

        @pl.when(j == nf - 1)
        def _():
            dhn = acc[...]
            dx, dgain = _norm_bwd(dhn, h_ref[...], g_ref[...])
            dh2_ref[...] = dh_ref[...] + dx
            dgain_ref[...] += dgain

    tok = pl.BlockSpec((tm, D_MODEL), lambda i, j: (i, 0))
    ffb = pl.BlockSpec((tm, _TF), lambda i, j: (i, j))
    row = pl.BlockSpec((1, D_MODEL), lambda i, j: (0, 0))
    return pl.pallas_call(
        body, name="ffn_bwd", grid=(T // tm, nf),
        in_specs=[tok, tok, tok, row,
                  pl.BlockSpec((D_MODEL, _TF), lambda i, j: (0, j)),
                  pl.BlockSpec((D_MODEL, _TF), lambda i, j: (0, j + nf)),
                  pl.BlockSpec((_TF, D_MODEL), lambda i, j: (j, 0))],
        out_specs=[tok, ffb, ffb, ffb, row],
        out_shape=[_sds((T, D_MODEL), _F32), _sds((T, D_FF), _MXU), _sds((T, D_FF), _MXU), _sds((T, D_FF), _MXU),
                   _sds((1, D_MODEL), _F32)],
        scratch_shapes=[pltpu.VMEM((tm, D_MODEL), _F32), pltpu.VMEM((tm, D_MODEL), _MXU)],
    )(dh3, h2, hn, g_ffn, w_gu, w_gu, w_d)


def _mem_attn_bwd(dh2, h1, g_q, w_q, gqm, bd128, mk, mv, w_mo):
    T = h1.shape[0]
    M = mk.shape[0]
    tm = min(_TM, T)
    nt = T // tm

    def body(dh_ref, h_ref, g_ref, wq_ref, gq_ref, bd_ref, mk_ref, mv_ref, wo_ref,
             dh1_ref, dmq_ref, dmk_ref, dmv_ref, dgq_ref, dgain_ref, dgq_acc):
        i = pl.program_id(0)

        @pl.when(i == 0)
        def _():
            dmk_ref[...] = jnp.zeros_like(dmk_ref)
            dmv_ref[...] = jnp.zeros_like(dmv_ref)
            dgain_ref[...] = jnp.zeros_like(dgain_ref)
            dgq_acc[...] = jnp.zeros_like(dgq_acc)

        x = h_ref[...]
        g = g_ref[...]
        bd = bd_ref[...]
        hn = ((x * _rstd(x)) * g).astype(_MXU)
        mq = _dot(hn, wq_ref[...])
        rq = lax.rsqrt(_group_sum(mq * mq, bd) * (1.0 / MEM_HEAD_DIM) + EPS)
        qn = ((mq * rq) * gq_ref[...]).astype(_MXU)
        dmo = _dot_nt(dh_ref[...].astype(_MXU), wo_ref[...])
        dqn = []
        for hd in range(MEM_HEADS):
            sl = slice(hd * MEM_HEAD_DIM, (hd + 1) * MEM_HEAD_DIM)
            p = _mem_softmax(qn, mk_ref, hd)
            dmo_h = dmo[:, sl].astype(_MXU)
            dp = _dot_nt(dmo_h, mv_ref[:, sl])
            ds = (p * (dp - jnp.sum(p * dp, axis=-1, keepdims=True)) * (MEM_HEAD_DIM ** -0.5)).astype(_MXU)
            dqn.append(_dot(ds, mk_ref[:, sl]))
            dmk_ref[:, sl] += _dot_tn(ds, qn[:, sl])
            dmv_ref[:, sl] += _dot_tn(p.astype(_MXU), dmo_h)
        dqn = jnp.concatenate(dqn, axis=-1)
        dmq, dgq = _headnorm_bwd(dqn, mq, gq_ref[...], bd, MEM_HEAD_DIM)
        dgq_acc[...] += dgq
        dmq = dmq.astype(_MXU)
        dmq_ref[...] = dmq
        dhn = _dot_nt(dmq, wq_ref[...])
        dx, dgain = _norm_bwd(dhn, x, g)
        dh1_ref[...] = dh_ref[...] + dx
        dgain_ref[...] += dgain

        @pl.when(i == nt - 1)
        def _():
            dgq_ref[...] = _fold_heads(dgq_acc[...], MEM_HEADS, MEM_HEAD_DIM)

    const2 = lambda shape: pl.BlockSpec(shape, lambda i: (0, 0))
    return pl.pallas_call(
        body, name="mem_attn_bwd", grid=(nt,),
        in_specs=[_rows(tm, D_MODEL), _rows(tm, D_MODEL), _resident((1, D_MODEL)), _resident(w_q.shape),
                  _resident((1, MEM_WIDTH)), _resident(bd128.shape), _resident(mk.shape), _resident(mv.shape),
                  _resident(w_mo.shape)],
        out_specs=[_rows(tm, D_MODEL), _rows(tm, MEM_WIDTH), const2((M, MEM_WIDTH)), const2((M, MEM_WIDTH)),
                   const2((1, MEM_HEAD_DIM)), const2((1, D_MODEL))],
        out_shape=[_sds((T, D_MODEL), _F32), _sds((T, MEM_WIDTH), _MXU), _sds((M, MEM_WIDTH), _F32),
                   _sds((M, MEM_WIDTH), _F32), _sds((1, MEM_HEAD_DIM), _F32), _sds((1, D_MODEL), _F32)],
        scratch_shapes=[pltpu.VMEM((1, MEM_WIDTH), _F32)],
    )(dh2, h1, g_q, w_q, gqm, bd128, mk, mv, w_mo)


def _mem_kv_bwd(dmk, dmv, mkv, mn, mem, g_kv, gkm, bd128, w_kv):
    def body(dmk_ref, dmv_ref, mkv_ref, mn_ref, mem_ref, g_ref, gk_ref, bd_ref, w_ref, dw_ref, dgk_ref, dgain_ref):
        kraw = mkv_ref[:, :MEM_WIDTH]
        dk, dgk = _headnorm_bwd(dmk_ref[...], kraw, gk_ref[...], bd_ref[...], MEM_HEAD_DIM)
        dgk_ref[...] = _fold_heads(dgk, MEM_HEADS, MEM_HEAD_DIM)
        dmkv = jnp.concatenate([dk, dmv_ref[...]], axis=-1).astype(_MXU)
        dw_ref[...] = _dot_tn(mn_ref[...], dmkv)
        dmn = _dot_nt(dmkv, w_ref[...])
        _, dgain = _norm_bwd(dmn, mem_ref[...], g_ref[...])
        dgain_ref[...] = dgain

    return pl.pallas_call(
        body, name="mem_kv_bwd",
        out_shape=[_sds((D_MODEL, 2 * MEM_WIDTH), _F32), _sds((1, MEM_HEAD_DIM), _F32), _sds((1, D_MODEL), _F32)],
    )(dmk, dmv, mkv, mn, mem, g_kv, gkm, bd128, w_kv)


def _out_proj_bwd(dh1, mixed, w_pool, pscale, w_out):
    T = dh1.shape[0]
    tm = min(_TM, T)
    hb = tm // HALO
    nt = T // tm

    def body(dh_ref, halo_ref, mx_ref, wp_ref, ps_ref, wo_ref, do_ref, dpin_ref, dwp_ref, dps_ref):
        i = pl.program_id(0)

        @pl.when(i == 0)
        def _():
            dwp_ref[...] = jnp.zeros_like(dwp_ref)
            dps_ref[...] = jnp.zeros_like(dps_ref)

        dcat = _dot_nt(dh_ref[...].astype(_MXU), wo_ref[...])
        do_ref[...] = dcat[:, :FOX_WIDTH].astype(_MXU)
        dy = dcat[:, FOX_WIDTH:]
        dyh = _dot_nt(halo_ref[...].astype(_MXU), wo_ref[FOX_WIDTH:, :])
        dyh = jnp.where(i == nt - 1, 0.0, dyh)
        ps = ps_ref[...]
        t = (i * tm + lax.broadcasted_iota(jnp.int32, (tm + HALO, 1), 0) + 1).astype(_F32)
        mixed_t = mx_ref[...]
        dpin, dps = [], []
        for g, w in enumerate(POOL_WINDOWS):
            sl = slice(g * POOL_GROUP_DIM, (g + 1) * POOL_GROUP_DIM)
            mg = mixed_t[:, sl]
            wg = wp_ref[g]
            dps.append(jnp.sum(dy[:, sl] * _dot(mg, wg), axis=0, keepdims=True))
            dyl = (dy[:, sl] * ps[:, sl]).astype(_MXU)
            dylh = (dyh[:, sl] * ps[:, sl]).astype(_MXU)
            dwp_ref[g] += _dot_tn(mg, dyl)
            dmx = _dot_nt(dyl, wg)
            ext = jnp.concatenate([dmx, _dot_nt(dylh, wg)], axis=0) / jnp.minimum(t, float(w))
            dpin.append(_pool_window_sum(ext, w, True)[:tm, :] - dmx)
        dpin_ref[...] = jnp.concatenate(dpin, axis=-1)
        dps_ref[...] += jnp.concatenate(dps, axis=-1)

    return pl.pallas_call(
        body, name="out_proj_bwd", grid=(nt,),
        in_specs=[_rows(tm, D_MODEL),
                  pl.BlockSpec((HALO, D_MODEL), lambda i: (jnp.minimum((i + 1) * hb, T // HALO - 1), 0)),
                  _rows(tm, POOL_WIDTH), _resident(w_pool.shape), _resident((1, POOL_WIDTH)), _resident(w_out.shape)],
        out_specs=[_rows(tm, FOX_WIDTH), _rows(tm, POOL_WIDTH),
                   pl.BlockSpec(w_pool.shape, lambda i: (0, 0, 0)), pl.BlockSpec((1, POOL_WIDTH), lambda i: (0, 0))],
        out_shape=[_sds((T, FOX_WIDTH), _MXU), _sds((T, POOL_WIDTH), _F32), _sds(w_pool.shape, _F32),
                   _sds((1, POOL_WIDTH), _F32)],
    )(dh1, dh1, mixed, w_pool, pscale, w_out)


def _fox_bwd(qn, kn, v, o, do, lse, ccol, crow):
    T = qn.shape[0]
    tq = min(_TQ, T)
    nq = T // tq
    pair = 2 * FOX_HEAD_DIM

    def body(q_ref, k_ref, v_ref, o_ref, do_ref, lse_ref, cc_ref, cr_ref, dq_ref, dk_ref, dv_ref, dcs_ref, drs_ref,
             dk_acc, dv_acc, dcs_acc):
        j = pl.program_id(1)
        i = pl.program_id(2)

        @pl.when((j == 0) & (i == 0))
        def _():
            dq_ref[...] = jnp.zeros_like(dq_ref)
            drs_ref[...] = jnp.zeros_like(drs_ref)

        @pl.when(i == 0)
        def _():
            dk_acc[...] = jnp.zeros_like(dk_acc)
            dv_acc[...] = jnp.zeros_like(dv_acc)
            dcs_acc[...] = jnp.zeros_like(dcs_acc)

        def step(masked):
            cc = cc_ref[0]
            cr = cr_ref[0]
            lse = lse_ref[...]
            dqs, rs = [], []
            for hh in range(2):
                sl = slice(FOX_HEAD_DIM * hh, FOX_HEAD_DIM * (hh + 1))
                s = _fox_scores(q_ref, k_ref, cc, cr, hh, masked)
                p = jnp.exp(s - lse[:, FOX_HEAD_DIM * hh:FOX_HEAD_DIM * hh + 1])
                dob = do_ref[:, sl]
                dv_acc[hh] += _dot_tn(p.astype(_MXU), dob)
                dp = _dot_nt(dob, v_ref[:, sl])
                delta = jnp.sum(dob.astype(_F32) * o_ref[:, sl].astype(_F32), axis=-1, keepdims=True)
                ds = p * (dp - delta)
                dcs_acc[hh:hh + 1, :] += jnp.sum(ds, axis=0, keepdims=True)
                rs.append(jnp.sum(ds, axis=-1, keepdims=True))
                dsb = (ds * (FOX_HEAD_DIM ** -0.5)).astype(_MXU)
                dk_acc[hh] += _dot_tn(dsb, q_ref[:, sl])
                dqs.append(_dot(dsb, k_ref[:, sl]))
            rows = pl.ds(pl.multiple_of(i * tq, tq), tq)
            dq_ref[rows, :] += jnp.concatenate(dqs, axis=-1)
            lane = lax.broadcasted_iota(jnp.int32, (tq, LANES), 1)
            drs_ref[0, rows, :] += jnp.where(lane == 0, rs[0], jnp.where(lane == 1, rs[1], 0.0))

        @pl.when(i > j)
        def _():
            step(False)

        @pl.when(i == j)
        def _():
            step(True)

        @pl.when(i == nq - 1)
        def _():
            dk_ref[...] = jnp.concatenate([dk_acc[0], dk_acc[1]], axis=-1)
            dv_ref[...] = jnp.concatenate([dv_acc[0], dv_acc[1]], axis=-1)
            dcs_ref[0] = dcs_acc[...]

    qspec = pl.BlockSpec((tq, pair), lambda p, j, i: (jnp.maximum(i, j), p))
    kspec = pl.BlockSpec((tq, pair), lambda p, j, i: (j, p))
    return pl.pallas_call(
        body, name="fox_bwd", grid=(FOX_HEADS // 2, nq, nq),
        in_specs=[qspec, kspec, kspec, qspec, qspec, qspec,
                  pl.BlockSpec((1, tq, LANES), lambda p, j, i: (p, jnp.maximum(i, j), 0)),
                  pl.BlockSpec((1, 8, tq), lambda p, j, i: (p, 0, j))],
        out_specs=[pl.BlockSpec((T, pair), lambda p, j, i: (0, p)), kspec, kspec,
                   pl.BlockSpec((1, 8, tq), lambda p, j, i: (p, 0, j)),
                   pl.BlockSpec((1, T, LANES), lambda p, j, i: (p, 0, 0))],
        out_shape=[_sds((T, FOX_WIDTH), _F32), _sds((T, FOX_WIDTH), _F32), _sds((T, FOX_WIDTH), _F32),
                   _sds((FOX_HEADS // 2, 8, T), _F32), _sds((FOX_HEADS // 2, T, LANES), _F32)],
        scratch_shapes=[pltpu.VMEM((2, tq, FOX_HEAD_DIM), _F32), pltpu.VMEM((2, tq, FOX_HEAD_DIM), _F32),
                        pltpu.VMEM((8, tq), _F32)],
    )(qn, kn, v, o, do, lse, ccol, crow)


def _mix_in_bwd(dh1, h, g_mix, zqk, dq, dk, dv, dpin, df, gq, gk, bd64, w_main, w_f):
    T = h.shape[0]
    tm = min(_TM, T)
    nt = T // tm

    def body(dh1_ref, h_ref, g_ref, zqk_ref, dq_ref, dk_ref, dv_ref, dpin_ref, df_ref, gq_ref, gk_ref, bd_ref,
             wm_ref, wf_ref, dh_ref, dz_ref, dzf_ref, dgq_ref, dgk_ref, dgain_ref, dbf_ref, dgq_acc, dgk_acc):
        i = pl.program_id(0)

        @pl.when(i == 0)
        def _():
            dgain_ref[...] = jnp.zeros_like(dgain_ref)
            dbf_ref[...] = jnp.zeros_like(dbf_ref)
            dgq_acc[...] = jnp.zeros_like(dgq_acc)
            dgk_acc[...] = jnp.zeros_like(dgk_acc)

        bd = bd_ref[...]
        dqr, dgq = _headnorm_bwd(dq_ref[...], zqk_ref[:, :FOX_WIDTH], gq_ref[...], bd, FOX_HEAD_DIM)
        dkr, dgk = _headnorm_bwd(dk_ref[...], zqk_ref[:, FOX_WIDTH:], gk_ref[...], bd, FOX_HEAD_DIM)
        dgq_acc[...] += dgq
        dgk_acc[...] += dgk
        dz = jnp.concatenate([dqr, dkr, dv_ref[...], dpin_ref[...]], axis=-1).astype(_MXU)
        dz_ref[...] = dz
        df = df_ref[...]
        dzf = df.astype(_MXU)
        dzf_ref[...] = dzf
        dbf_ref[...] += jnp.sum(df, axis=0, keepdims=True)
        dxn = _dot_nt(dz, wm_ref[...]) + _dot_nt(dzf, wf_ref[...])
        dx, dgain = _norm_bwd(dxn, h_ref[...], g_ref[...])
        dh_ref[...] = dh1_ref[...] + dx
        dgain_ref[...] += dgain

        @pl.when(i == nt - 1)
        def _():
            dgq_ref[...] = _fold_heads(dgq_acc[...], FOX_HEADS, FOX_HEAD_DIM)
            dgk_ref[...] = _fold_heads(dgk_acc[...], FOX_HEADS, FOX_HEAD_DIM)

    const2 = lambda shape: pl.BlockSpec(shape, lambda i: (0, 0))
    return pl.pallas_call(
        body, name="mix_in_bwd", grid=(nt,),
        in_specs=[_rows(tm, D_MODEL), _rows(tm, D_MODEL), _resident((1, D_MODEL)), _rows(tm, 2 * FOX_WIDTH),
                  _rows(tm, FOX_WIDTH), _rows(tm, FOX_WIDTH), _rows(tm, FOX_WIDTH), _rows(tm, POOL_WIDTH),
                  _rows(tm, LANES), _resident((1, FOX_WIDTH)), _resident((1, FOX_WIDTH)), _resident(bd64.shape),
                  _resident(w_main.shape), _resident(w_f.shape)],
        out_specs=[_rows(tm, D_MODEL), _rows(tm, 4 * FOX_WIDTH), _rows(tm, LANES), const2((1, FOX_HEAD_DIM)),
                   const2((1, FOX_HEAD_DIM)), const2((1, D_MODEL)), const2((1, LANES))],
        out_shape=[_sds((T, D_MODEL), _F32), _sds((T, 4 * FOX_WIDTH), _MXU), _sds((T, LANES), _MXU),
                   _sds((1, FOX_HEAD_DIM), _F32), _sds((1, FOX_HEAD_DIM), _F32), _sds((1, D_MODEL), _F32),
                   _sds((1, LANES), _F32)],
        scratch_shapes=[pltpu.VMEM((1, FOX_WIDTH), _F32), pltpu.VMEM((1, FOX_WIDTH), _F32)],
    )(dh1, h, g_mix, zqk, dq, dk, dv, dpin, df, gq, gk, bd64, w_main, w_f)


def _layer_params(W, l):
    w_in = W["w_in"][l]
    n_main = 3 * FOX_WIDTH + POOL_WIDTH
    row = lambda a: a.reshape(1, -1).astype(_F32)
    return dict(
        g_mix=row(W["g_mix"][l]),
        w_main=w_in[:, :n_main],
        w_f=jnp.pad(w_in[:, n_main:], ((0, 0), (0, LANES - FOX_HEADS))),
        b_f=jnp.pad(row(W["b_forget"][l]), ((0, 0), (0, LANES - FOX_HEADS))),
        gq=jnp.tile(row(W["g_q_fox"][l]), (1, FOX_HEADS)),
        gk=jnp.tile(row(W["g_k_fox"][l]), (1, FOX_HEADS)),
        w_pool=W["w_pool"][l].astype(_MXU),
        pscale=row(W["pool_scale"][l]),
        w_out=W["w_out"][l],
        g_mem_q=row(W["g_mem_q"][l]),
        g_mem_kv=row(W["g_mem_kv"][l]),
        w_mem_q=W["w_mem_q"][l],
        w_mem_kv=W["w_mem_kv"][l],
        gqm=jnp.tile(row(W["g_q_mem"][l]), (1, MEM_HEADS)),
        gkm=jnp.tile(row(W["g_k_mem"][l]), (1, MEM_HEADS)),
        w_mem_out=W["w_mem_out"][l],
        g_ffn=row(W["g_ffn"][l]),
        w_gu=W["w_gate_up"][l],
        w_d=W["w_down"][l],
    )


def _pair_layouts(c):
    T = c.shape[0]
    cp = c[:, :FOX_HEADS].reshape(T, FOX_HEADS // 2, 2)
    ccol = jnp.pad(cp.transpose(1, 0, 2), ((0, 0), (0, 0), (0, LANES - 2)))
    crow = jnp.pad(cp.transpose(1, 2, 0), ((0, 0), (0, 6), (0, 0)))
    return ccol, crow


def _layer_fwd(h, mem, P, bd64, bd128):
    s = dict(h=h)
    s["xn"], s["zqk"], s["qn"], s["kn"], s["v"], pin, s["fl"] = _mix_in_fwd(
        h, P["g_mix"], P["w_main"], P["w_f"], P["b_f"], P["gq"], P["gk"], bd64)
    s["ccol"], s["crow"] = _pair_layouts(_gate_fwd(s["fl"]))
    s["o"], s["lse"] = _fox_fwd(s["qn"], s["kn"], s["v"], s["ccol"], s["crow"])
    s["h1"], s["mixed"], s["y"] = _out_proj_fwd(h, s["o"], pin, P["w_pool"], P["pscale"], P["w_out"])
    s["mn"], s["mkv"], s["mk"], s["mv"] = _mem_kv_fwd(mem, P["g_mem_kv"], P["w_mem_kv"], P["gkm"], bd128)
    s["h2"], s["hn_mem"], s["mo"] = _mem_attn_fwd(s["h1"], P["g_mem_q"], P["w_mem_q"], P["gqm"], bd128, s["mk"],
                                                  s["mv"], P["w_mem_out"])
    h3, s["hn_ffn"] = _ffn_fwd(s["h2"], P["g_ffn"], P["w_gu"], P["w_d"])
    return h3, s


def _layer_bwd(dh3, mem, P, s, bd64, bd128):
    T = dh3.shape[0]
    g = {}
    dh2, act, dgate, dup, g["g_ffn"] = _ffn_bwd(dh3, s["h2"], s["hn_ffn"], P["g_ffn"], P["w_gu"], P["w_d"])
    g["w_down"] = _matmul_tn(act, dh3, "dw_down")
    g["w_gate_up"] = jnp.concatenate([_matmul_tn(s["hn_ffn"], dgate, "dw_gate"),
                                      _matmul_tn(s["hn_ffn"], dup, "dw_up")], axis=1)

    dh1, dmq, dmk, dmv, g["g_q_mem"], g["g_mem_q"] = _mem_attn_bwd(
        dh2, s["h1"], P["g_mem_q"], P["w_mem_q"], P["gqm"], bd128, s["mk"], s["mv"], P["w_mem_out"])
    g["w_mem_out"] = _matmul_tn(s["mo"], dh2, "dw_mem_out")
    g["w_mem_q"] = _matmul_tn(s["hn_mem"], dmq, "dw_mem_q")
    g["w_mem_kv"], g["g_k_mem"], g["g_mem_kv"] = _mem_kv_bwd(dmk, dmv, s["mkv"], s["mn"], mem, P["g_mem_kv"],
                                                             P["gkm"], bd128, P["w_mem_kv"])

    do, dpin, g["w_pool"], g["pool_scale"] = _out_proj_bwd(dh1, s["mixed"], P["w_pool"], P["pscale"], P["w_out"])
    g["w_out"] = jnp.concatenate([_matmul_tn(s["o"], dh1, "dw_out_fox"), _matmul_tn(s["y"], dh1, "dw_out_pool")],
                                 axis=0)
    dq, dk, dv, dcs, drs = _fox_bwd(s["qn"], s["kn"], s["v"], s["o"], do, s["lse"], s["ccol"], s["crow"])
    pad = ((0, 0), (0, LANES - FOX_HEADS))
    dcs = jnp.pad(dcs[:, :2, :].transpose(2, 0, 1).reshape(T, FOX_HEADS), pad)
    drs = jnp.pad(drs[:, :, :2].transpose(1, 0, 2).reshape(T, FOX_HEADS), pad)
    df = _gate_bwd(drs, dcs, s["fl"])
    dh, dz, dzf, g["g_q_fox"], g["g_k_fox"], g["g_mix"], dbf = _mix_in_bwd(
        dh1, s["h"], P["g_mix"], s["zqk"], dq, dk, dv, dpin, df, P["gq"], P["gk"], bd64, P["w_main"], P["w_f"])
    g["b_forget"] = dbf[:, :FOX_HEADS]
    g["w_in"] = jnp.concatenate([_matmul_tn(s["xn"], dz, "dw_in_main"),
                                 _matmul_tn(s["xn"], dzf, "dw_in_gate")[:, :FOX_HEADS]], axis=1)
    return dh, g


def _device_grads(x, mem, tgt, W):
    bd64 = _blockdiag_ones(FOX_HEADS, FOX_HEAD_DIM)
    bd128 = _blockdiag_ones(MEM_HEADS, MEM_HEAD_DIM)
    params = [_layer_params(W, l) for l in range(DEPTH)]
    h, saved = x, []
    for l in range(DEPTH):
        h, s = _layer_fwd(h, mem, params[l], bd64, bd128)
        saved.append(s)
    dh, loss = _loss_grad(h, tgt)
    per_layer = [None] * DEPTH
    for l in reversed(range(DEPTH)):
        dh, per_layer[l] = _layer_bwd(dh, mem, params[l], saved[l], bd64, bd128)
    grads = {n: jnp.stack([per_layer[l][n].reshape(W[n].shape[1:]) for l in range(DEPTH)]) for n in WEIGHTS}
    return loss, dh, grads


def _shard_rows(shard_shape):
    n = 1
    for d in shard_shape:
        n *= d
    return n // D_MODEL


def _to_shards(full, ax):
    shp = full.shape
    parts = full.reshape(shp[:ax] + (N_CHIPS, shp[ax] // N_CHIPS) + shp[ax + 1:])
    return jnp.moveaxis(parts, ax, 0)


def _from_shards(parts, ax):
    full = jnp.moveaxis(parts, 0, ax)
    shp = full.shape
    return full.reshape(shp[:ax] + (shp[ax] * shp[ax + 1],) + shp[ax + 2:])


def _flat_rows(total_rows):
    return -(-total_rows // (2 * _TH)) * (2 * _TH)


def _pack_small(arrs):
    flat = jnp.concatenate([a.reshape(-1).astype(_F32) for a in arrs])
    rows = -(-flat.shape[0] // (8 * D_MODEL)) * 8
    return jnp.pad(flat, (0, rows * D_MODEL - flat.shape[0])).reshape(rows, D_MODEL)


def _unpack_small(flat, shapes):
    flat = flat.reshape(-1)
    out, off = [], 0
    for shp in shapes:
        n = 1
        for d in shp:
            n *= d
        out.append(flat[off:off + n].reshape(shp))
        off += n
    return out


def _mesh_pos():
    return lax.axis_index("x"), lax.axis_index("y"), lax.axis_index("c")


def _other_chips(x, y):
    return [(1 - x, y), (x, 1 - y), (1 - x, 1 - y)]


_ANY = pl.BlockSpec(memory_space=pl.ANY)


def _allgather_weights(flat):
    R = flat.shape[0]
    half = R // 2

    def body(src, out, send_sems, recv_sems):
        x, y, c = _mesh_pos()
        me = 2 * x + y
        sibling = (x, y, 1 - c)
        chips = _other_chips(x, y)
        mine = pl.ds(pl.multiple_of(c * half, 16), half)
        theirs = pl.ds(pl.multiple_of((1 - c) * half, 16), half)

        def copy(k, src_ref, dst_ref, to):
            return pltpu.make_async_remote_copy(src_ref=src_ref, dst_ref=dst_ref, send_sem=send_sems.at[k],
                                                recv_sem=recv_sems.at[k], device_id=to, device_id_type=MESH)

        first = [copy(j, src.at[mine], out.at[me, mine], (*chip, c)) for j, chip in enumerate(chips)]
        for cp in first:
            cp.start()
        passed = []
        for j, (cx, cy) in enumerate(chips):
            slab = out.at[2 * cx + cy, mine]
            copy(j, slab, slab, (cx, cy, c)).wait_recv()
            fwd = copy(3 + j, slab, slab, sibling)
            fwd.start()
            passed.append(fwd)
        for j, (cx, cy) in enumerate(chips):
            slab = out.at[2 * cx + cy, theirs]
            copy(3 + j, slab, slab, sibling).wait_recv()
        for cp in first + passed:
            cp.wait_send()

    return pl.pallas_call(
        body, name="allgather_weights", in_specs=[_ANY], out_specs=_ANY,
        out_shape=_sds((N_CHIPS,) + flat.shape, flat.dtype),
        scratch_shapes=[pltpu.SemaphoreType.DMA((6,)), pltpu.SemaphoreType.DMA((6,))],
    )(flat)


def _exchange_halves(g):
    half = g.shape[2]

    def body(g_ref, got_ref, send_sems, recv_sems):
        x, y, c = _mesh_pos()
        sibling = (x, y, 1 - c)
        copies = []
        for k in range(N_CHIPS):
            cp = pltpu.make_async_remote_copy(src_ref=g_ref.at[k, 1 - c], dst_ref=got_ref.at[k],
                                              send_sem=send_sems.at[k], recv_sem=recv_sems.at[k],
                                              device_id=sibling, device_id_type=MESH)
            cp.start()
            copies.append(cp)
        for cp in copies:
            cp.wait()

    return pl.pallas_call(
        body, name="grad_exchange_halves", in_specs=[_ANY], out_specs=_ANY,
        out_shape=_sds((N_CHIPS, half, D_MODEL), _F32),
        scratch_shapes=[pltpu.SemaphoreType.DMA((N_CHIPS,)), pltpu.SemaphoreType.DMA((N_CHIPS,))],
    )(g)


def _add_halves(g, got, c_idx):
    half = g.shape[2]
    ta = _TH

    def body(c_ref, a_ref, b_ref, o_ref):
        o_ref[...] = (a_ref[0] + b_ref[...]).astype(jnp.bfloat16)

    return pl.pallas_call(
        body, name="grad_add_halves",
        grid_spec=pltpu.PrefetchScalarGridSpec(
            num_scalar_prefetch=1, grid=(N_CHIPS, half // ta),
            in_specs=[pl.BlockSpec((1, 1, ta, D_MODEL), lambda k, i, c: (k, c[0], i, 0)),
                      pl.BlockSpec((1, ta, D_MODEL), lambda k, i, c: (k, i, 0))],
            out_specs=pl.BlockSpec((1, ta, D_MODEL), lambda k, i, c: (k, i, 0))),
        out_shape=_sds((N_CHIPS, half, D_MODEL), jnp.bfloat16),
    )(c_idx, g, got)


def _scatter_to_chips(p):
    def body(p_ref, got_ref, send_sems, recv_sems):
        x, y, c = _mesh_pos()
        me = 2 * x + y
        copies = []
        for j, (cx, cy) in enumerate(_other_chips(x, y)):
            cp = pltpu.make_async_remote_copy(src_ref=p_ref.at[2 * cx + cy], dst_ref=got_ref.at[me],
                                              send_sem=send_sems.at[j], recv_sem=recv_sems.at[j],
                                              device_id=(cx, cy, c), device_id_type=MESH)
            cp.start()
            copies.append(cp)
        for j, (cx, cy) in enumerate(_other_chips(x, y)):
            slab = got_ref.at[2 * cx + cy]
            pltpu.make_async_remote_copy(src_ref=slab, dst_ref=slab, send_sem=send_sems.at[j],
                                         recv_sem=recv_sems.at[j], device_id=(cx, cy, c),
                                         device_id_type=MESH).wait_recv()
        for cp in copies:
            cp.wait_send()

    return pl.pallas_call(
        body, name="grad_scatter_chips", in_specs=[_ANY], out_specs=_ANY, out_shape=_sds(p.shape, p.dtype),
        scratch_shapes=[pltpu.SemaphoreType.DMA((3,)), pltpu.SemaphoreType.DMA((3,))],
    )(p)


def _sum_chips(got, c_idx):
    half = got.shape[1]
    ta = _TH

    def body(c_ref, a_ref, o_ref):
        f = lambda k: a_ref[k].astype(_F32)
        o_ref[0] = ((f(0) + f(1)) + f(2)) + f(3)

    return pl.pallas_call(
        body, name="grad_sum_chips",
        grid_spec=pltpu.PrefetchScalarGridSpec(
            num_scalar_prefetch=1, grid=(half // ta,),
            in_specs=[pl.BlockSpec((N_CHIPS, ta, D_MODEL), lambda i, c: (0, i, 0))],
            out_specs=pl.BlockSpec((1, ta, D_MODEL), lambda i, c: (c[0], i, 0))),
        out_shape=_sds((2, half, D_MODEL), _F32),
    )(c_idx, got)


def _share_with_sibling(buf):
    def body(buf_ref, out_ref, send_sem, recv_sem):
        x, y, c = _mesh_pos()
        cp = pltpu.make_async_remote_copy(src_ref=out_ref.at[c], dst_ref=out_ref.at[c], send_sem=send_sem,
                                          recv_sem=recv_sem, device_id=(x, y, 1 - c), device_id_type=MESH)
        cp.start()
        theirs = out_ref.at[1 - c]
        pltpu.make_async_remote_copy(src_ref=theirs, dst_ref=theirs, send_sem=send_sem, recv_sem=recv_sem,
                                     device_id=(x, y, 1 - c), device_id_type=MESH).wait_recv()
        cp.wait_send()

    return pl.pallas_call(
        body, name="grad_share_sibling", in_specs=[_ANY], out_specs=_ANY, out_shape=_sds(buf.shape, _F32),
        input_output_aliases={0: 0},
        scratch_shapes=[pltpu.SemaphoreType.DMA, pltpu.SemaphoreType.DMA],
    )(buf)


def _allreduce_small(g):
    rows = g.shape[0]
    n_dev = 2 * N_CHIPS

    def body(g_ref, out_ref, gathered, local_sem, send_sems, recv_sems):
        x, y, c = _mesh_pos()
        me = 4 * x + 2 * y + c
        own = pltpu.make_async_copy(g_ref, gathered.at[me], local_sem)
        own.start()
        copies = []
        for k in range(1, n_dev):
            fx, fy, fc = (k >> 2) & 1, (k >> 1) & 1, k & 1
            cp = pltpu.make_async_remote_copy(
                src_ref=g_ref, dst_ref=gathered.at[me], send_sem=send_sems.at[k - 1], recv_sem=recv_sems.at[k - 1],
                device_id=(x ^ fx, y ^ fy, c ^ fc), device_id_type=MESH)
            cp.start()
            copies.append(cp)
        for k in range(1, n_dev):
            fx, fy, fc = (k >> 2) & 1, (k >> 1) & 1, k & 1
            px, py, pc = x ^ fx, y ^ fy, c ^ fc
            slab = gathered.at[4 * px + 2 * py + pc]
            pltpu.make_async_remote_copy(src_ref=slab, dst_ref=slab, send_sem=send_sems.at[k - 1],
                                         recv_sem=recv_sems.at[k - 1], device_id=(px, py, pc),
                                         device_id_type=MESH).wait_recv()
        for cp in copies:
            cp.wait_send()
        own.wait()
        acc = gathered[0]
        for d in range(1, n_dev):
            acc = acc + gathered[d]
        out_ref[...] = acc

    vmem = pl.BlockSpec(memory_space=pltpu.VMEM)
    return pl.pallas_call(
        body, name="allreduce_small", in_specs=[vmem], out_specs=vmem, out_shape=_sds((rows, D_MODEL), _F32),
        scratch_shapes=[pltpu.VMEM((n_dev, rows, D_MODEL), _F32), pltpu.SemaphoreType.DMA,
                        pltpu.SemaphoreType.DMA((n_dev - 1,)), pltpu.SemaphoreType.DMA((n_dev - 1,))],
    )(g)


def _adamw(w, g, m, v, name):
    shape = w.shape
    cols = shape[-1]
    rows = 1
    for d in shape[:-1]:
        rows *= d
    w2, g2, m2, v2 = (a.reshape(rows, cols) for a in (w, g, m, v))
    tr = _pick_tile(rows, (512, 256, 128, 64, 32, 16, 8))

    def body(w_ref, g_ref, m_ref, v_ref, d_ref, nm_ref, nv_ref):
        gg = g_ref[...]
        nm = ADAM_B1 * m_ref[...] + (1.0 - ADAM_B1) * gg
        nv = ADAM_B2 * v_ref[...] + (1.0 - ADAM_B2) * (gg * gg)
        m_hat = nm / (1.0 - ADAM_B1 ** ADAM_STEP)
        v_hat = nv / (1.0 - ADAM_B2 ** ADAM_STEP)
        d_ref[...] = -ADAM_LR * (m_hat / (jnp.sqrt(v_hat) + ADAM_EPS) + ADAM_WD * w_ref[...])
        nm_ref[...] = nm
        nv_ref[...] = nv

    spec = pl.BlockSpec((tr, cols), lambda i: (i, 0))
    outs = pl.pallas_call(
        body, name=name, grid=(rows // tr,), in_specs=[spec] * 4, out_specs=[spec] * 3,
        out_shape=[_sds((rows, cols), _F32)] * 3,
    )(w2, g2, m2, v2)
    return tuple(o.reshape(shape) for o in outs)


def kernel(x, mem, g_mix, w_in, b_forget, g_q_fox, g_k_fox, w_pool, pool_scale, w_out, g_mem_q, g_mem_kv, w_mem_q, w_mem_kv, g_q_mem, g_k_mem, w_mem_out, g_ffn, w_gate_up, w_down, loss_target, m_g_mix, m_w_in, m_b_forget, m_g_q_fox, m_g_k_fox, m_w_pool, m_pool_scale, m_w_out, m_g_mem_q, m_g_mem_kv, m_w_mem_q, m_w_mem_kv, m_g_q_mem, m_g_k_mem, m_w_mem_out, m_g_ffn, m_w_gate_up, m_w_down, v_g_mix, v_w_in, v_b_forget, v_g_q_fox, v_g_k_fox, v_w_pool, v_pool_scale, v_w_out, v_g_mem_q, v_g_mem_kv, v_w_mem_q, v_w_mem_kv, v_g_q_mem, v_g_k_mem, v_w_mem_out, v_g_ffn, v_w_gate_up, v_w_down):
    w = dict(g_mix=g_mix, w_in=w_in, b_forget=b_forget, g_q_fox=g_q_fox, g_k_fox=g_k_fox, w_pool=w_pool,
             pool_scale=pool_scale, w_out=w_out, g_mem_q=g_mem_q, g_mem_kv=g_mem_kv, w_mem_q=w_mem_q,
             w_mem_kv=w_mem_kv, g_q_mem=g_q_mem, g_k_mem=g_k_mem, w_mem_out=w_mem_out, g_ffn=g_ffn,
             w_gate_up=w_gate_up, w_down=w_down)
    m = dict(g_mix=m_g_mix, w_in=m_w_in, b_forget=m_b_forget, g_q_fox=m_g_q_fox, g_k_fox=m_g_k_fox, w_pool=m_w_pool,
             pool_scale=m_pool_scale, w_out=m_w_out, g_mem_q=m_g_mem_q, g_mem_kv=m_g_mem_kv, w_mem_q=m_w_mem_q,
             w_mem_kv=m_w_mem_kv, g_q_mem=m_g_q_mem, g_k_mem=m_g_k_mem, w_mem_out=m_w_mem_out, g_ffn=m_g_ffn,
             w_gate_up=m_w_gate_up, w_down=m_w_down)
    v = dict(g_mix=v_g_mix, w_in=v_w_in, b_forget=v_b_forget, g_q_fox=v_g_q_fox, g_k_fox=v_g_k_fox, w_pool=v_w_pool,
             pool_scale=v_pool_scale, w_out=v_w_out, g_mem_q=v_g_mem_q, g_mem_kv=v_g_mem_kv, w_mem_q=v_w_mem_q,
             w_mem_kv=v_w_mem_kv, g_q_mem=v_g_q_mem, g_k_mem=v_g_k_mem, w_mem_out=v_w_mem_out, g_ffn=v_g_ffn,
             w_gate_up=v_w_gate_up, w_down=v_w_down)

    shard_rows = [_shard_rows(w[n].shape) for n in BIG]
    slot_rows = [-(-r // 16) * 16 for r in shard_rows]
    total = sum(slot_rows)
    R = _flat_rows(total)
    flat = jnp.concatenate([jnp.pad(w[n].astype(_MXU).reshape(-1, D_MODEL), ((0, s - r), (0, 0)))
                            for n, r, s in zip(BIG, shard_rows, slot_rows)]
                           + [jnp.zeros((R - total, D_MODEL), _MXU)])
    chip = 2 * lax.axis_index("x") + lax.axis_index("y")
    gathered = lax.dynamic_update_slice(_allgather_weights(flat), flat[None], (chip, 0, 0))
    W = {n: w[n] for n in SMALL}
    off = 0
    for n, r, s in zip(BIG, shard_rows, slot_rows):
        parts = gathered[:, off:off + r].reshape((N_CHIPS,) + w[n].shape)
        W[n] = _from_shards(parts, SHARD_AXIS[n])
        off += s

    loss, grad_x, grads = _device_grads(x[0], mem[0], loss_target[0], W)
    loss = lax.psum(loss[0, 0], ("x", "y", "c"))

    half = R // 2
    gflat = jnp.concatenate(
        [jnp.pad(_to_shards(grads[n], SHARD_AXIS[n]).reshape(N_CHIPS, -1, D_MODEL), ((0, 0), (0, s - r), (0, 0)))
         for n, r, s in zip(BIG, shard_rows, slot_rows)]
        + [jnp.zeros((N_CHIPS, R - total, D_MODEL), _F32)], axis=1).reshape(N_CHIPS, 2, half, D_MODEL)
    c_idx = lax.axis_index("c").astype(jnp.int32).reshape(1)
    partial = _add_halves(gflat, _exchange_halves(gflat), c_idx)
    own = lax.dynamic_slice(partial, (chip, 0, 0), (1, half, D_MODEL))
    partials = lax.dynamic_update_slice(_scatter_to_chips(partial), own, (chip, 0, 0))
    reduced = _share_with_sibling(_sum_chips(partials, c_idx)).reshape(R, D_MODEL)

    small_shapes = [w[n].shape for n in SMALL]
    gsmall = _unpack_small(_allreduce_small(_pack_small([grads[n] for n in SMALL])), small_shapes)

    g_out, d_out, m_out, v_out = {}, {}, {}, {}
    off = 0
    for n, r, s in zip(BIG, shard_rows, slot_rows):
        g_out[n] = reduced[off:off + r].reshape(w[n].shape)
        d_out[n], m_out[n], v_out[n] = _adamw(w[n], g_out[n], m[n], v[n], "adamw_" + n)
        off += s
    packed = [_pack_small([t[n] for n in SMALL]) for t in (w, m, v)]
    ds, ms, vs = _adamw(packed[0], _pack_small(gsmall), packed[1], packed[2], "adamw_small")
    for n, gi, di, mi, vi in zip(SMALL, gsmall, _unpack_small(ds, small_shapes), _unpack_small(ms, small_shapes),
                                 _unpack_small(vs, small_shapes)):
        g_out[n], d_out[n], m_out[n], v_out[n] = gi, di, mi, vi

    return (loss, grad_x[None], *[g_out[n] for n in WEIGHTS], *[d_out[n] for n in WEIGHTS],
            *[m_out[n] for n in WEIGHTS], *[v_out[n] for n in WEIGHTS])
```

```python
import functools

import jax
import jax.numpy as jnp
import numpy as np
from jax import lax
from jax.experimental import pallas as pl
from jax.experimental.pallas import tpu as pltpu

_F32 = jnp.float32
_MXU = jnp.bfloat16

D_MODEL = 1024
DEPTH = 2
FOX_HEADS = 8
FOX_HEAD_DIM = 64
FOX_WIDTH = FOX_HEADS * FOX_HEAD_DIM
POOL_WINDOWS = (2, 4, 8, 16)
POOL_GROUP_DIM = 128
POOL_WIDTH = len(POOL_WINDOWS) * POOL_GROUP_DIM
MEM_HEADS = 4
MEM_HEAD_DIM = 128
MEM_WIDTH = MEM_HEADS * MEM_HEAD_DIM
D_FF = 2816
EPS = 1e-6
LANES = 128
HALO = 16

ADAM_LR = 0.001
ADAM_B1 = 0.9
ADAM_B2 = 0.999
ADAM_EPS = 1e-08
ADAM_WD = 0.01
ADAM_STEP = 10

N_CHIPS = 4
MESH = pl.DeviceIdType.MESH

_TM = 256
_TQ = 1024
_TMF = 512
_TF = 256
_TT = 512
_TB = 256
_TA = 512
_TH = 384

BIG = ("w_in", "w_out", "w_mem_q", "w_mem_kv", "w_mem_out", "w_gate_up", "w_down")
SHARD_AXIS = {"w_in": 2, "w_out": 1, "w_mem_q": 1, "w_mem_kv": 1, "w_mem_out": 2, "w_gate_up": 2, "w_down": 1}
SMALL = ("g_mix", "b_forget", "g_q_fox", "g_k_fox", "w_pool", "pool_scale", "g_mem_q", "g_mem_kv", "g_q_mem",
         "g_k_mem", "g_ffn")
WEIGHTS = ("g_mix", "w_in", "b_forget", "g_q_fox", "g_k_fox", "w_pool", "pool_scale", "w_out", "g_mem_q", "g_mem_kv",
           "w_mem_q", "w_mem_kv", "g_q_mem", "g_k_mem", "w_mem_out", "g_ffn", "w_gate_up", "w_down")


def _dot(a, b):
    return jnp.dot(a, b, preferred_element_type=_F32)


def _dot_nt(a, b):
    return lax.dot_general(a, b, (((1,), (1,)), ((), ())), preferred_element_type=_F32)


def _dot_tn(a, b):
    return lax.dot_general(a, b, (((0,), (0,)), ((), ())), preferred_element_type=_F32)


def _group_sum(x, ones_blockdiag):
    hi = x.astype(_MXU)
    lo = (x - hi.astype(_F32)).astype(_MXU)
    return _dot(hi, ones_blockdiag) + _dot(lo, ones_blockdiag)


def _tri_dot(tri, x):
    h1 = x.astype(jnp.bfloat16)
    r1 = x - h1.astype(_F32)
    h2 = r1.astype(jnp.bfloat16)
    h3 = (r1 - h2.astype(_F32)).astype(jnp.bfloat16)
    return _dot(tri, h1) + _dot(tri, h2) + _dot(tri, h3)


def _rstd(x):
    return lax.rsqrt(jnp.mean(x * x, axis=-1, keepdims=True) + EPS)


def _norm_bwd(dy, x, g):
    r = _rstd(x)
    xhat = x * r
    u = dy * g
    dx = r * (u - xhat * jnp.mean(u * xhat, axis=-1, keepdims=True))
    return dx, jnp.sum(dy * xhat, axis=0, keepdims=True)


def _headnorm_bwd(dy, x, g, ones_blockdiag, width):
    r = lax.rsqrt(_group_sum(x * x, ones_blockdiag) * (1.0 / width) + EPS)
    xhat = x * r
    u = dy * g
    dx = r * (u - xhat * (_group_sum(u * xhat, ones_blockdiag) * (1.0 / width)))
    return dx, jnp.sum(dy * xhat, axis=0, keepdims=True)


def _fold_heads(row, heads, width):
    acc = row[:, 0:width]
    for h in range(1, heads):
        acc = acc + row[:, h * width:(h + 1) * width]
    return acc


def _resident(shape):
    return pl.BlockSpec(shape, lambda *_: (0,) * len(shape), pipeline_mode=pl.Buffered(1))


def _rows(tm, width):
    return pl.BlockSpec((tm, width), lambda i: (i, 0))


def _blockdiag_ones(groups, width):
    return jnp.kron(jnp.eye(groups, dtype=_F32), jnp.ones((width, width), _F32)).astype(_MXU)


def _sds(shape, dtype):
    return jax.ShapeDtypeStruct(shape, dtype)


def _mix_in_fwd(h, g_mix, w_main, w_f, b_f, gq, gk, bd64):
    T = h.shape[0]
    tm = min(_TM, T)

    def body(h_ref, g_ref, wm_ref, wf_ref, bf_ref, gq_ref, gk_ref, bd_ref,
             xn_ref, zqk_ref, qn_ref, kn_ref, v_ref, pin_ref, fl_ref):
        x = h_ref[...]
        xn = ((x * _rstd(x)) * g_ref[...]).astype(_MXU)
        xn_ref[...] = xn
        z = _dot(xn, wm_ref[...])
        q = z[:, :FOX_WIDTH]
        k = z[:, FOX_WIDTH:2 * FOX_WIDTH]
        zqk_ref[...] = z[:, :2 * FOX_WIDTH]
        bd = bd_ref[...]
        rq = lax.rsqrt(_group_sum(q * q, bd) * (1.0 / FOX_HEAD_DIM) + EPS)
        rk = lax.rsqrt(_group_sum(k * k, bd) * (1.0 / FOX_HEAD_DIM) + EPS)
        qn_ref[...] = ((q * rq) * gq_ref[...]).astype(_MXU)
        kn_ref[...] = ((k * rk) * gk_ref[...]).astype(_MXU)
        v_ref[...] = z[:, 2 * FOX_WIDTH:3 * FOX_WIDTH].astype(_MXU)
        pin_ref[...] = z[:, 3 * FOX_WIDTH:]
        fl_ref[...] = _dot(xn, wf_ref[...]) + bf_ref[...]

    return pl.pallas_call(
        body, name="mix_in_fwd", grid=(T // tm,),
        in_specs=[_rows(tm, D_MODEL), _resident((1, D_MODEL)), _resident(w_main.shape), _resident(w_f.shape),
                  _resident((1, LANES)), _resident((1, FOX_WIDTH)), _resident((1, FOX_WIDTH)), _resident(bd64.shape)],
        out_specs=[_rows(tm, D_MODEL), _rows(tm, 2 * FOX_WIDTH), _rows(tm, FOX_WIDTH), _rows(tm, FOX_WIDTH),
                   _rows(tm, FOX_WIDTH), _rows(tm, POOL_WIDTH), _rows(tm, LANES)],
        out_shape=[_sds((T, D_MODEL), _MXU), _sds((T, 2 * FOX_WIDTH), _F32), _sds((T, FOX_WIDTH), _MXU),
                   _sds((T, FOX_WIDTH), _MXU), _sds((T, FOX_WIDTH), _MXU), _sds((T, POOL_WIDTH), _F32),
                   _sds((T, LANES), _F32)],
    )(h, g_mix, w_main, w_f, b_f, gq, gk, bd64)


def _split3(x):
    h1 = x.astype(jnp.bfloat16).astype(_F32)
    r1 = x - h1
    h2 = r1.astype(jnp.bfloat16).astype(_F32)
    h3 = (r1 - h2).astype(jnp.bfloat16).astype(_F32)
    return h1, h2, h3


_ONES3 = (1.0, 1.0, 1.0)
_ZEROS3 = (0.0, 0.0, 0.0)


def _aug_head(feat, first, second):
    rows = feat.shape[0]
    lane = lax.broadcasted_iota(jnp.int32, (rows, LANES - FOX_HEAD_DIM), 1)
    aux = jnp.zeros((rows, LANES - FOX_HEAD_DIM), _F32)
    for k in range(3):
        aux = jnp.where(lane == k, first[k], aux)
        aux = jnp.where(lane == 3 + k, second[k], aux)
    return jnp.concatenate([feat.astype(_MXU), aux.astype(_MXU)], axis=-1)


def _gate_fwd(fl, qn, kn, v):
    T = fl.shape[0]
    tb = min(_TB, T)
    wide = FOX_HEADS * LANES

    def body(fl_ref, q_ref, k_ref, v_ref, qa_ref, ka_ref, va_ref, carry):
        @pl.when(pl.program_id(0) == 0)
        def _():
            carry[...] = jnp.zeros_like(carry)

        x = fl_ref[...]
        ls = jnp.minimum(x, 0.0) - jnp.log1p(jnp.exp(-jnp.abs(x)))
        row = lax.broadcasted_iota(jnp.int32, (tb, tb), 0)
        col = lax.broadcasted_iota(jnp.int32, (tb, tb), 1)
        tri = jnp.where(col <= row, 1.0, 0.0).astype(jnp.bfloat16)
        cs = _tri_dot(tri, ls) + carry[...]
        carry[...] = cs[tb - 1:tb, :]
        for h in range(FOX_HEADS):
            sl = slice(h * FOX_HEAD_DIM, (h + 1) * FOX_HEAD_DIM)
            out = slice(h * LANES, (h + 1) * LANES)
            c3 = _split3(cs[:, h:h + 1])
            qa_ref[:, out] = _aug_head(q_ref[:, sl].astype(_F32) * (FOX_HEAD_DIM ** -0.5), c3, _ONES3)
            ka_ref[:, out] = _aug_head(k_ref[:, sl], _ONES3, tuple(-t for t in c3))
            va_ref[:, out] = _aug_head(v_ref[:, sl], _ONES3, _ZEROS3)

    return pl.pallas_call(
        body, name="gate_fwd", grid=(T // tb,),
        in_specs=[_rows(tb, LANES), _rows(tb, FOX_WIDTH), _rows(tb, FOX_WIDTH), _rows(tb, FOX_WIDTH)],
        out_specs=[_rows(tb, wide)] * 3, out_shape=[_sds((T, wide), _MXU)] * 3,
        scratch_shapes=[pltpu.VMEM((1, LANES), _F32)],
    )(fl, qn, kn, v)


def _gate_bwd(drs, dcs, fl):
    T = fl.shape[0]
    tb = min(_TB, T)
    nb = T // tb

    def body(r_ref, d_ref, fl_ref, df_ref, carry):
        @pl.when(pl.program_id(0) == 0)
        def _():
            carry[...] = jnp.zeros_like(carry)

        row = lax.broadcasted_iota(jnp.int32, (tb, tb), 0)
        col = lax.broadcasted_iota(jnp.int32, (tb, tb), 1)
        tri = jnp.where(col >= row, 1.0, 0.0).astype(jnp.bfloat16)
        rc = _tri_dot(tri, r_ref[...] - d_ref[...]) + carry[...]
        carry[...] = rc[0:1, :]
        df_ref[...] = rc * (1.0 / (1.0 + jnp.exp(fl_ref[...])))

    rev = pl.BlockSpec((tb, LANES), lambda i: (nb - 1 - i, 0))
    return pl.pallas_call(
        body, name="gate_bwd", grid=(nb,), in_specs=[rev, rev, rev], out_specs=rev,
        out_shape=_sds((T, LANES), _F32), scratch_shapes=[pltpu.VMEM((1, LANES), _F32)],
    )(drs, dcs, fl)


def _fox_scores(q_ref, k_ref, hh, masked):
    sl = slice(LANES * hh, LANES * (hh + 1))
    s = _dot_nt(q_ref[:, sl], k_ref[:, sl])
    if masked:
        row = lax.broadcasted_iota(jnp.int32, s.shape, 0)
        col = lax.broadcasted_iota(jnp.int32, s.shape, 1)
        s = jnp.where(col <= row, s, -jnp.inf)
    return s


def _causal_steps(nq, query_major):
    if query_major:
        steps = [(i, j) for i in range(nq) for j in range(i + 1)]
    else:
        steps = [(i, j) for j in range(nq) for i in range(j, nq)]
    return (jnp.asarray(np.array([s[0] for s in steps], np.int32)),
            jnp.asarray(np.array([s[1] for s in steps], np.int32)))


def _fox_fwd(qa, ka, va):
    T = qa.shape[0]
    tq = min(_TQ, T)
    nq = T // tq
    ii, jj = _causal_steps(nq, True)

    def body(ii_ref, jj_ref, q_ref, k_ref, v_ref, o_ref, qb_ref, m_s, acc_s):
        t = pl.program_id(1)
        i = ii_ref[t]
        j = jj_ref[t]

        @pl.when(j == 0)
        def _():
            m_s[...] = jnp.full(m_s.shape, -jnp.inf, _F32)
            acc_s[...] = jnp.zeros_like(acc_s)

        def step(masked):
            for hh in range(2):
                s = _fox_scores(q_ref, k_ref, hh, masked)
                m_prev = m_s[hh]
                m_new = jnp.maximum(m_prev, jnp.max(s, axis=-1, keepdims=True))
                p = jnp.exp(s - m_new).astype(_MXU)
                acc_s[hh] = jnp.exp(m_prev - m_new) * acc_s[hh] + _dot(p, v_ref[:, LANES * hh:LANES * (hh + 1)])
                m_s[hh] = m_new

        @pl.when(j < i)
        def _():
            step(False)

        @pl.when(j == i)
        def _():
            step(True)
            outs = []
            for hh in range(2):
                sl = slice(LANES * hh, LANES * (hh + 1))
                acc = acc_s[hh]
                l = acc[:, FOX_HEAD_DIM:FOX_HEAD_DIM + 1]
                outs.append(acc[:, :FOX_HEAD_DIM] / l)
                qh = q_ref[:, sl]
                cq = (qh[:, FOX_HEAD_DIM:FOX_HEAD_DIM + 1].astype(_F32)
                      + qh[:, FOX_HEAD_DIM + 1:FOX_HEAD_DIM + 2].astype(_F32)
                      + qh[:, FOX_HEAD_DIM + 2:FOX_HEAD_DIM + 3].astype(_F32))
                qb_ref[:, sl] = _aug_head(qh[:, :FOX_HEAD_DIM], _split3(cq - (m_s[hh] + jnp.log(l))), _ONES3)
            o_ref[...] = jnp.concatenate(outs, axis=-1).astype(_MXU)

    qspec = pl.BlockSpec((tq, 2 * LANES), lambda p, t, ii, jj: (ii[t], p))
    kspec = pl.BlockSpec((tq, 2 * LANES), lambda p, t, ii, jj: (jj[t], p))
    return pl.pallas_call(
        body, name="fox_fwd",
        grid_spec=pltpu.PrefetchScalarGridSpec(
            num_scalar_prefetch=2, grid=(FOX_HEADS // 2, int(ii.shape[0])),
            in_specs=[qspec, kspec, kspec],
            out_specs=[pl.BlockSpec((tq, 2 * FOX_HEAD_DIM), lambda p, t, ii, jj: (ii[t], p)), qspec],
            scratch_shapes=[pltpu.VMEM((2, tq, 1), _F32), pltpu.VMEM((2, tq, LANES), _F32)]),
        out_shape=[_sds((T, FOX_WIDTH), _MXU), _sds((T, FOX_HEADS * LANES), _MXU)],
    )(ii, jj, qa, ka, va)


def _pool_window_sum(ext, w, forward):
    n = ext.shape[0]
    sm = ext
    k = 1
    while k < w:
        sm = sm + pltpu.roll(sm, (n - k) if forward else k, axis=0)
        k *= 2
    return sm


def _out_proj_fwd(h, o, pin, w_pool, pscale, w_out):
    T = h.shape[0]
    tm = min(_TM, T)
    hb = tm // HALO

    def body(h_ref, o_ref, pin_ref, halo_ref, wp_ref, ps_ref, wo_ref, h1_ref, mixed_ref, y_ref):
        i = pl.program_id(0)
        pin_t = pin_ref[...]
        halo = jnp.where(i == 0, 0.0, halo_ref[...])
        ext = jnp.concatenate([halo, pin_t], axis=0)
        t = (i * tm + lax.broadcasted_iota(jnp.int32, (tm, 1), 0) + 1).astype(_F32)
        mixed, ys = [], []
        for g, w in enumerate(POOL_WINDOWS):
            sl = slice(g * POOL_GROUP_DIM, (g + 1) * POOL_GROUP_DIM)
            win = _pool_window_sum(ext[:, sl], w, False)[HALO:, :]
            mg = (win / jnp.minimum(t, float(w)) - pin_t[:, sl]).astype(_MXU)
            mixed.append(mg)
            ys.append(_dot(mg, wp_ref[g]))
        mixed_ref[...] = jnp.concatenate(mixed, axis=-1)
        y = (jnp.concatenate(ys, axis=-1) * ps_ref[...]).astype(_MXU)
        y_ref[...] = y
        h1_ref[...] = h_ref[...] + _dot(o_ref[...], wo_ref[:FOX_WIDTH, :]) + _dot(y, wo_ref[FOX_WIDTH:, :])

    return pl.pallas_call(
        body, name="out_proj_fwd", grid=(T // tm,),
        in_specs=[_rows(tm, D_MODEL), _rows(tm, FOX_WIDTH), _rows(tm, POOL_WIDTH),
                  pl.BlockSpec((HALO, POOL_WIDTH), lambda i: (jnp.maximum(i * hb - 1, 0), 0)),
                  _resident(w_pool.shape), _resident((1, POOL_WIDTH)), _resident(w_out.shape)],
        out_specs=[_rows(tm, D_MODEL), _rows(tm, POOL_WIDTH), _rows(tm, POOL_WIDTH)],
        out_shape=[_sds((T, D_MODEL), _F32), _sds((T, POOL_WIDTH), _MXU), _sds((T, POOL_WIDTH), _MXU)],
    )(h, o, pin, pin, w_pool, pscale, w_out)


def _mem_kv_fwd(mem, g_kv, w_kv, gkm, bd128):
    M = mem.shape[0]

    def body(mem_ref, g_ref, w_ref, gk_ref, bd_ref, mn_ref, mkv_ref, mk_ref, mv_ref):
        x = mem_ref[...]
        mn = ((x * _rstd(x)) * g_ref[...]).astype(_MXU)
        mn_ref[...] = mn
        z = _dot(mn, w_ref[...])
        mkv_ref[...] = z
        k = z[:, :MEM_WIDTH]
        rk = lax.rsqrt(_group_sum(k * k, bd_ref[...]) * (1.0 / MEM_HEAD_DIM) + EPS)
        mk_ref[...] = ((k * rk) * gk_ref[...]).astype(_MXU)
        mv_ref[...] = z[:, MEM_WIDTH:].astype(_MXU)

    return pl.pallas_call(
        body, name="mem_kv_fwd",
        out_shape=[_sds((M, D_MODEL), _MXU), _sds((M, 2 * MEM_WIDTH), _F32), _sds((M, MEM_WIDTH), _MXU),
                   _sds((M, MEM_WIDTH), _MXU)],
    )(mem, g_kv, w_kv, gkm, bd128)


def _mem_softmax(qn, mk_ref, hd):
    sl = slice(hd * MEM_HEAD_DIM, (hd + 1) * MEM_HEAD_DIM)
    s = _dot_nt(qn[:, sl], mk_ref[:, sl]) * (MEM_HEAD_DIM ** -0.5)
    e = jnp.exp(s - jnp.max(s, axis=-1, keepdims=True))
    return e / jnp.sum(e, axis=-1, keepdims=True)


def _mem_attn_fwd(h1, g_q, w_q, gqm, bd128, mk, mv, w_mo):
    T = h1.shape[0]
    tm = min(_TM, T)

    def body(h_ref, g_ref, wq_ref, gq_ref, bd_ref, mk_ref, mv_ref, wo_ref, h2_ref, hn_ref, mo_ref):
        x = h_ref[...]
        hn = ((x * _rstd(x)) * g_ref[...]).astype(_MXU)
        hn_ref[...] = hn
        mq = _dot(hn, wq_ref[...])
        rq = lax.rsqrt(_group_sum(mq * mq, bd_ref[...]) * (1.0 / MEM_HEAD_DIM) + EPS)
        qn = ((mq * rq) * gq_ref[...]).astype(_MXU)
        outs = []
        for hd in range(MEM_HEADS):
            p = _mem_softmax(qn, mk_ref, hd).astype(_MXU)
            outs.append(_dot(p, mv_ref[:, hd * MEM_HEAD_DIM:(hd + 1) * MEM_HEAD_DIM]))
        mo = jnp.concatenate(outs, axis=-1).astype(_MXU)
        mo_ref[...] = mo
        h2_ref[...] = x + _dot(mo, wo_ref[...])

    return pl.pallas_call(
        body, name="mem_attn_fwd", grid=(T // tm,),
        in_specs=[_rows(tm, D_MODEL), _resident((1, D_MODEL)), _resident(w_q.shape), _resident((1, MEM_WIDTH)),
                  _resident(bd128.shape), _resident(mk.shape), _resident(mv.shape), _resident(w_mo.shape)],
        out_specs=[_rows(tm, D_MODEL), _rows(tm, D_MODEL), _rows(tm, MEM_WIDTH)],
        out_shape=[_sds((T, D_MODEL), _F32), _sds((T, D_MODEL), _MXU), _sds((T, MEM_WIDTH), _MXU)],
    )(h1, g_q, w_q, gqm, bd128, mk, mv, w_mo)


def _ffn_fwd(h2, g_ffn, w_gu, w_d):
    T = h2.shape[0]
    tm = min(_TMF, T)
    nf = D_FF // _TF

    def body(h_ref, g_ref, wg_ref, wu_ref, wd_ref, h3_ref, hn_ref, acc, xn_s):
        j = pl.program_id(1)

        @pl.when(j == 0)
        def _():
            x = h_ref[...]
            xn = ((x * _rstd(x)) * g_ref[...]).astype(_MXU)
            xn_s[...] = xn
            hn_ref[...] = xn
            acc[...] = jnp.zeros_like(acc)

        xn = xn_s[...]
        g = _dot(xn, wg_ref[...])
        u = _dot(xn, wu_ref[...])
        a = ((g * jax.nn.sigmoid(g)) * u).astype(_MXU)
        acc[...] += _dot(a, wd_ref[...])

        @pl.when(j == nf - 1)
        def _():
            h3_ref[...] = h_ref[...] + acc[...]

    tok = pl.BlockSpec((tm, D_MODEL), lambda i, j: (i, 0))
    return pl.pallas_call(
        body, name="ffn_fwd", grid=(T // tm, nf),
        in_specs=[tok, pl.BlockSpec((1, D_MODEL), lambda i, j: (0, 0)),
                  pl.BlockSpec((D_MODEL, _TF), lambda i, j: (0, j)),
                  pl.BlockSpec((D_MODEL, _TF), lambda i, j: (0, j + nf)),
                  pl.BlockSpec((_TF, D_MODEL), lambda i, j: (j, 0))],
        out_specs=[tok, tok],
        out_shape=[_sds((T, D_MODEL), _F32), _sds((T, D_MODEL), _MXU)],
        scratch_shapes=[pltpu.VMEM((tm, D_MODEL), _F32), pltpu.VMEM((tm, D_MODEL), _MXU)],
    )(h2, g_ffn, w_gu, w_gu, w_d)


def _loss_grad(y, tgt):
    T = y.shape[0]
    tm = min(_TA, T)

    def body(y_ref, t_ref, dy_ref, loss_ref):
        @pl.when(pl.program_id(0) == 0)
        def _():
            loss_ref[...] = jnp.zeros_like(loss_ref)

        err = y_ref[...] - t_ref[...]
        dy_ref[...] = err * (1.0 / D_MODEL)
        part = jnp.sum(jnp.sum(err * err, axis=0, keepdims=True), axis=1, keepdims=True)
        loss_ref[...] += part * (0.5 / D_MODEL)

    return pl.pallas_call(
        body, name="loss_grad", grid=(T // tm,), in_specs=[_rows(tm, D_MODEL), _rows(tm, D_MODEL)],
        out_specs=[_rows(tm, D_MODEL), pl.BlockSpec((1, 1), lambda i: (0, 0))],
        out_shape=[_sds((T, D_MODEL), _F32), _sds((1, 1), _F32)],
    )(y, tgt)


def _pick_tile(n, candidates=(1408, 1024, 512, 256, 128)):
    for c in candidates:
        if n % c == 0:
            return c
    return n


def _matmul_tn(a, b, name):
    T, K = a.shape
    N = b.shape[1]
    tk, tn, tt = _pick_tile(K), _pick_tile(N), min(_TT, T)

    def body(a_ref, b_ref, o_ref):
        @pl.when(pl.program_id(2) == 0)
        def _():
            o_ref[...] = jnp.zeros_like(o_ref)

        o_ref[...] += _dot_tn(a_ref[...].astype(_MXU), b_ref[...].astype(_MXU))

    return pl.pallas_call(
        body, name=name, grid=(K // tk, N // tn, T // tt),
        in_specs=[pl.BlockSpec((tt, tk), lambda i, j, t: (t, i)), pl.BlockSpec((tt, tn), lambda i, j, t: (t, j))],
        out_specs=pl.BlockSpec((tk, tn), lambda i, j, t: (i, j)),
        out_shape=_sds((K, N), _F32),
    )(a, b)


def _ffn_bwd(dh3, h2, hn, g_ffn, w_gu, w_d):
    T = h2.shape[0]
    tm = min(_TMF, T)
    nf = D_FF // _TF

    def body(dh_ref, h_ref, hn_ref, g_ref, wg_ref, wu_ref, wd_ref, dh2_ref, act_ref, dg_ref, du_ref, dgain_ref,
             acc, dyb):
        i = pl.program_id(0)
        j = pl.program_id(1)

        @pl.when((i == 0) & (j == 0))
        def _():
            dgain_ref[...] = jnp.zeros_like(dgain_ref)

        @pl.when(j == 0)
        def _():
            dyb[...] = dh_ref[...].astype(_MXU)
            acc[...] = jnp.zeros_like(acc)

        xn = hn_ref[...]
        g = _dot(xn, wg_ref[...])
        u = _dot(xn, wu_ref[...])
        sg = jax.nn.sigmoid(g)
        sl = g * sg
        act_ref[...] = (sl * u).astype(_MXU)
        da = _dot_nt(dyb[...], wd_ref[...])
        dgate = (da * u * (sg * (1.0 + g * (1.0 - sg)))).astype(_MXU)
        dup = (da * sl).astype(_MXU)
        dg_ref[...] = dgate
        du_ref[...] = dup
        acc[...] += _dot_nt(dgate, wg_ref[...]) + _dot_nt(dup, wu_ref[...])

        @pl.when(j == nf - 1)
        def _():
            dhn = acc[...]
            dx, dgain = _norm_bwd(dhn, h_ref[...], g_ref[...])
            dh2_ref[...] = dh_ref[...] + dx
            dgain_ref[...] += dgain

    tok = pl.BlockSpec((tm, D_MODEL), lambda i, j: (i, 0))
    ffb = pl.BlockSpec((tm, _TF), lambda i, j: (i, j))
    row = pl.BlockSpec((1, D_MODEL), lambda i, j: (0, 0))
    return pl.pallas_call(
        body, name="ffn_bwd", grid=(T // tm, nf),
        in_specs=[tok, tok, tok, row,
                  pl.BlockSpec((D_MODEL, _TF), lambda i, j: (0, j)),
                  pl.BlockSpec((D_MODEL, _TF), lambda i, j: (0, j + nf)),
                  pl.BlockSpec((_TF, D_MODEL), lambda i, j: (j, 0))],
        out_specs=[tok, ffb, ffb, ffb, row],
        out_shape=[_sds((T, D_MODEL), _F32), _sds((T, D_FF), _MXU), _sds((T, D_FF), _MXU), _sds((T, D_FF), _MXU),
                   _sds((1, D_MODEL), _F32)],
        scratch_shapes=[pltpu.VMEM((tm, D_MODEL), _F32), pltpu.VMEM((tm, D_MODEL), _MXU)],
    )(dh3, h2, hn, g_ffn, w_gu, w_gu, w_d)


def _mem_attn_bwd(dh2, h1, g_q, w_q, gqm, bd128, mk, mv, w_mo):
    T = h1.shape[0]
    M = mk.shape[0]
    tm = min(_TM, T)
    nt = T // tm

    def body(dh_ref, h_ref, g_ref, wq_ref, gq_ref, bd_ref, mk_ref, mv_ref, wo_ref,
             dh1_ref, dmq_ref, dmk_ref, dmv_ref, dgq_ref, dgain_ref, dgq_acc):
        i = pl.program_id(0)

        @pl.when(i == 0)
        def _():
            dmk_ref[...] = jnp.zeros_like(dmk_ref)
            dmv_ref[...] = jnp.zeros_like(dmv_ref)
            dgain_ref[...] = jnp.zeros_like(dgain_ref)
            dgq_acc[...] = jnp.zeros_like(dgq_acc)

        x = h_ref[...]
        g = g_ref[...]
        bd = bd_ref[...]
        hn = ((x * _rstd(x)) * g).astype(_MXU)
        mq = _dot(hn, wq_ref[...])
        rq = lax.rsqrt(_group_sum(mq * mq, bd) * (1.0 / MEM_HEAD_DIM) + EPS)
        qn = ((mq * rq) * gq_ref[...]).astype(_MXU)
        dmo = _dot_nt(dh_ref[...].astype(_MXU), wo_ref[...])
        dqn = []
        for hd in range(MEM_HEADS):
            sl = slice(hd * MEM_HEAD_DIM, (hd + 1) * MEM_HEAD_DIM)
            p = _mem_softmax(qn, mk_ref, hd)
            dmo_h = dmo[:, sl].astype(_MXU)
            dp = _dot_nt(dmo_h, mv_ref[:, sl])
            ds = (p * (dp - jnp.sum(p * dp, axis=-1, keepdims=True)) * (MEM_HEAD_DIM ** -0.5)).astype(_MXU)
            dqn.append(_dot(ds, mk_ref[:, sl]))
            dmk_ref[:, sl] += _dot_tn(ds, qn[:, sl])
            dmv_ref[:, sl] += _dot_tn(p.astype(_MXU), dmo_h)
        dqn = jnp.concatenate(dqn, axis=-1)
        dmq, dgq = _headnorm_bwd(dqn, mq, gq_ref[...], bd, MEM_HEAD_DIM)
        dgq_acc[...] += dgq
        dmq = dmq.astype(_MXU)
        dmq_ref[...] = dmq
        dhn = _dot_nt(dmq, wq_ref[...])
        dx, dgain = _norm_bwd(dhn, x, g)
        dh1_ref[...] = dh_ref[...] + dx
        dgain_ref[...] += dgain

        @pl.when(i == nt - 1)
        def _():
            dgq_ref[...] = _fold_heads(dgq_acc[...], MEM_HEADS, MEM_HEAD_DIM)

    const2 = lambda shape: pl.BlockSpec(shape, lambda i: (0, 0))
    return pl.pallas_call(
        body, name="mem_attn_bwd", grid=(nt,),
        in_specs=[_rows(tm, D_MODEL), _rows(tm, D_MODEL), _resident((1, D_MODEL)), _resident(w_q.shape),
                  _resident((1, MEM_WIDTH)), _resident(bd128.shape), _resident(mk.shape), _resident(mv.shape),
                  _resident(w_mo.shape)],
        out_specs=[_rows(tm, D_MODEL), _rows(tm, MEM_WIDTH), const2((M, MEM_WIDTH)), const2((M, MEM_WIDTH)),
                   const2((1, MEM_HEAD_DIM)), const2((1, D_MODEL))],
        out_shape=[_sds((T, D_MODEL), _F32), _sds((T, MEM_WIDTH), _MXU), _sds((M, MEM_WIDTH), _F32),
                   _sds((M, MEM_WIDTH), _F32), _sds((1, MEM_HEAD_DIM), _F32), _sds((1, D_MODEL), _F32)],
        scratch_shapes=[pltpu.VMEM((1, MEM_WIDTH), _F32)],
    )(dh2, h1, g_q, w_q, gqm, bd128, mk, mv, w_mo)


def _mem_kv_bwd(dmk, dmv, mkv, mn, mem, g_kv, gkm, bd128, w_kv):
    def body(dmk_ref, dmv_ref, mkv_ref, mn_ref, mem_ref, g_ref, gk_ref, bd_ref, w_ref, dw_ref, dgk_ref, dgain_ref):
        kraw = mkv_ref[:, :MEM_WIDTH]
        dk, dgk = _headnorm_bwd(dmk_ref[...], kraw, gk_ref[...], bd_ref[...], MEM_HEAD_DIM)
        dgk_ref[...] = _fold_heads(dgk, MEM_HEADS, MEM_HEAD_DIM)
        dmkv = jnp.concatenate([dk, dmv_ref[...]], axis=-1).astype(_MXU)
        dw_ref[...] = _dot_tn(mn_ref[...], dmkv)
        dmn = _dot_nt(dmkv, w_ref[...])
        _, dgain = _norm_bwd(dmn, mem_ref[...], g_ref[...])
        dgain_ref[...] = dgain

    return pl.pallas_call(
        body, name="mem_kv_bwd",
        out_shape=[_sds((D_MODEL, 2 * MEM_WIDTH), _F32), _sds((1, MEM_HEAD_DIM), _F32), _sds((1, D_MODEL), _F32)],
    )(dmk, dmv, mkv, mn, mem, g_kv, gkm, bd128, w_kv)


def _out_proj_bwd(dh1, mixed, o, bd64, w_pool, pscale, w_out):
    T = dh1.shape[0]
    tm = min(_TM, T)
    hb = tm // HALO
    nt = T // tm

    def body(dh_ref, halo_ref, mx_ref, o_ref, bd_ref, wp_ref, ps_ref, wo_ref, doa_ref, dpin_ref, dwp_ref, dps_ref):
        i = pl.program_id(0)

        @pl.when(i == 0)
        def _():
            dwp_ref[...] = jnp.zeros_like(dwp_ref)
            dps_ref[...] = jnp.zeros_like(dps_ref)

        dcat = _dot_nt(dh_ref[...].astype(_MXU), wo_ref[...])
        do = dcat[:, :FOX_WIDTH].astype(_MXU)
        delta = _group_sum(do.astype(_F32) * o_ref[...].astype(_F32), bd_ref[...])
        for h in range(FOX_HEADS):
            sl = slice(h * FOX_HEAD_DIM, (h + 1) * FOX_HEAD_DIM)
            doa_ref[:, h * LANES:(h + 1) * LANES] = _aug_head(
                do[:, sl], _split3(-delta[:, h * FOX_HEAD_DIM:h * FOX_HEAD_DIM + 1]), _ZEROS3)
        dy = dcat[:, FOX_WIDTH:]
        dyh = _dot_nt(halo_ref[...].astype(_MXU), wo_ref[FOX_WIDTH:, :])
        dyh = jnp.where(i == nt - 1, 0.0, dyh)
        ps = ps_ref[...]
        t = (i * tm + lax.broadcasted_iota(jnp.int32, (tm + HALO, 1), 0) + 1).astype(_F32)
        mixed_t = mx_ref[...]
        dpin, dps = [], []
        for g, w in enumerate(POOL_WINDOWS):
            sl = slice(g * POOL_GROUP_DIM, (g + 1) * POOL_GROUP_DIM)
            mg = mixed_t[:, sl]
            wg = wp_ref[g]
            dps.append(jnp.sum(dy[:, sl] * _dot(mg, wg), axis=0, keepdims=True))
            dyl = (dy[:, sl] * ps[:, sl]).astype(_MXU)
            dylh = (dyh[:, sl] * ps[:, sl]).astype(_MXU)
            dwp_ref[g] += _dot_tn(mg, dyl)
            dmx = _dot_nt(dyl, wg)
            ext = jnp.concatenate([dmx, _dot_nt(dylh, wg)], axis=0) / jnp.minimum(t, float(w))
            dpin.append(_pool_window_sum(ext, w, True)[:tm, :] - dmx)
        dpin_ref[...] = jnp.concatenate(dpin, axis=-1)
        dps_ref[...] += jnp.concatenate(dps, axis=-1)

    return pl.pallas_call(
        body, name="out_proj_bwd", grid=(nt,),
        in_specs=[_rows(tm, D_MODEL),
                  pl.BlockSpec((HALO, D_MODEL), lambda i: (jnp.minimum((i + 1) * hb, T // HALO - 1), 0)),
                  _rows(tm, POOL_WIDTH), _rows(tm, FOX_WIDTH), _resident(bd64.shape), _resident(w_pool.shape),
                  _resident((1, POOL_WIDTH)), _resident(w_out.shape)],
        out_specs=[_rows(tm, FOX_HEADS * LANES), _rows(tm, POOL_WIDTH),
                   pl.BlockSpec(w_pool.shape, lambda i: (0, 0, 0)), pl.BlockSpec((1, POOL_WIDTH), lambda i: (0, 0))],
        out_shape=[_sds((T, FOX_HEADS * LANES), _MXU), _sds((T, POOL_WIDTH), _F32), _sds(w_pool.shape, _F32),
                   _sds((1, POOL_WIDTH), _F32)],
    )(dh1, dh1, mixed, o, bd64, w_pool, pscale, w_out)


def _fox_bwd(qb, ka, va, doa):
    T = qb.shape[0]
    tq = min(_TQ, T)
    nq = T // tq
    pair = 2 * FOX_HEAD_DIM
    ii, jj = _causal_steps(nq, False)
    row_lane, col_lane = FOX_HEAD_DIM, FOX_HEAD_DIM + 3

    def two_lanes(a, b):
        lane = lax.broadcasted_iota(jnp.int32, (tq, LANES), 1)
        return jnp.where(lane == 0, a, jnp.where(lane == 1, b, 0.0))

    def body(ii_ref, jj_ref, q_ref, k_ref, v_ref, do_ref, dq_ref, dk_ref, dv_ref, drs_ref, dcs_ref, dk_acc, dv_acc):
        t = pl.program_id(1)
        i = ii_ref[t]
        j = jj_ref[t]

        @pl.when(t == 0)
        def _():
            dq_ref[...] = jnp.zeros_like(dq_ref)
            drs_ref[...] = jnp.zeros_like(drs_ref)

        @pl.when(i == j)
        def _():
            dk_acc[...] = jnp.zeros_like(dk_acc)
            dv_acc[...] = jnp.zeros_like(dv_acc)

        def step(masked):
            dqs, rs = [], []
            for hh in range(2):
                sl = slice(LANES * hh, LANES * (hh + 1))
                p = jnp.exp(_fox_scores(q_ref, k_ref, hh, masked))
                dob = do_ref[:, sl]
                dv_acc[hh] += _dot_tn(p.astype(_MXU), dob)
                dsb = (p * _dot_nt(dob, v_ref[:, sl])).astype(_MXU)
                dk_acc[hh] += _dot_tn(dsb, q_ref[:, sl])
                dqa = _dot(dsb, k_ref[:, sl])
                dqs.append(dqa[:, :FOX_HEAD_DIM] * (FOX_HEAD_DIM ** -0.5))
                rs.append(dqa[:, row_lane:row_lane + 1])
            rows = pl.ds(pl.multiple_of(i * tq, tq), tq)
            dq_ref[rows, :] += jnp.concatenate(dqs, axis=-1)
            drs_ref[0, rows, :] += two_lanes(rs[0], rs[1])

        @pl.when(i > j)
        def _():
            step(False)

        @pl.when(i == j)
        def _():
            step(True)

        @pl.when(i == nq - 1)
        def _():
            dk_ref[...] = jnp.concatenate([dk_acc[0][:, :FOX_HEAD_DIM], dk_acc[1][:, :FOX_HEAD_DIM]], axis=-1)
            dv_ref[...] = jnp.concatenate([dv_acc[0][:, :FOX_HEAD_DIM], dv_acc[1][:, :FOX_HEAD_DIM]], axis=-1)
            dcs_ref[0] = two_lanes(dk_acc[0][:, col_lane:col_lane + 1], dk_acc[1][:, col_lane:col_lane + 1])

    qspec = pl.BlockSpec((tq, 2 * LANES), lambda p, t, ii, jj: (ii[t], p))
    kspec = pl.BlockSpec((tq, 2 * LANES), lambda p, t, ii, jj: (jj[t], p))
    kout = pl.BlockSpec((tq, pair), lambda p, t, ii, jj: (jj[t], p))
    return pl.pallas_call(
        body, name="fox_bwd",
        grid_spec=pltpu.PrefetchScalarGridSpec(
            num_scalar_prefetch=2, grid=(FOX_HEADS // 2, int(ii.shape[0])),
            in_specs=[qspec, kspec, kspec, qspec],
            out_specs=[pl.BlockSpec((T, pair), lambda p, t, ii, jj: (0, p)), kout, kout,
                       pl.BlockSpec((1, T, LANES), lambda p, t, ii, jj: (p, 0, 0)),
                       pl.BlockSpec((1, tq, LANES), lambda p, t, ii, jj: (p, jj[t], 0))],
            scratch_shapes=[pltpu.VMEM((2, tq, LANES), _F32), pltpu.VMEM((2, tq, LANES), _F32)]),
        out_shape=[_sds((T, FOX_WIDTH), _F32), _sds((T, FOX_WIDTH), _F32), _sds((T, FOX_WIDTH), _F32),
                   _sds((FOX_HEADS // 2, T, LANES), _F32), _sds((FOX_HEADS // 2, T, LANES), _F32)],
    )(ii, jj, qb, ka, va, doa)


def _mix_in_bwd(dh1, h, g_mix, zqk, dq, dk, dv, dpin, df, gq, gk, bd64, w_main, w_f):
    T = h.shape[0]
    tm = min(_TM, T)
    nt = T // tm

    def body(dh1_ref, h_ref, g_ref, zqk_ref, dq_ref, dk_ref, dv_ref, dpin_ref, df_ref, gq_ref, gk_ref, bd_ref,
             wm_ref, wf_ref, dh_ref, dz_ref, dzf_ref, dgq_ref, dgk_ref, dgain_ref, dbf_ref, dgq_acc, dgk_acc):
        i = pl.program_id(0)

        @pl.when(i == 0)
        def _():
            dgain_ref[...] = jnp.zeros_like(dgain_ref)
            dbf_ref[...] = jnp.zeros_like(dbf_ref)
            dgq_acc[...] = jnp.zeros_like(dgq_acc)
            dgk_acc[...] = jnp.zeros_like(dgk_acc)

        bd = bd_ref[...]
        dqr, dgq = _headnorm_bwd(dq_ref[...], zqk_ref[:, :FOX_WIDTH], gq_ref[...], bd, FOX_HEAD_DIM)
        dkr, dgk = _headnorm_bwd(dk_ref[...], zqk_ref[:, FOX_WIDTH:], gk_ref[...], bd, FOX_HEAD_DIM)
        dgq_acc[...] += dgq
        dgk_acc[...] += dgk
        dz = jnp.concatenate([dqr, dkr, dv_ref[...], dpin_ref[...]], axis=-1).astype(_MXU)
        dz_ref[...] = dz
        df = df_ref[...]
        dzf = df.astype(_MXU)
        dzf_ref[...] = dzf
        dbf_ref[...] += jnp.sum(df, axis=0, keepdims=True)
        dxn = _dot_nt(dz, wm_ref[...]) + _dot_nt(dzf, wf_ref[...])
        dx, dgain = _norm_bwd(dxn, h_ref[...], g_ref[...])
        dh_ref[...] = dh1_ref[...] + dx
        dgain_ref[...] += dgain

        @pl.when(i == nt - 1)
        def _():
            dgq_ref[...] = _fold_heads(dgq_acc[...], FOX_HEADS, FOX_HEAD_DIM)
            dgk_ref[...] = _fold_heads(dgk_acc[...], FOX_HEADS, FOX_HEAD_DIM)

    const2 = lambda shape: pl.BlockSpec(shape, lambda i: (0, 0))
    return pl.pallas_call(
        body, name="mix_in_bwd", grid=(nt,),
        in_specs=[_rows(tm, D_MODEL), _rows(tm, D_MODEL), _resident((1, D_MODEL)), _rows(tm, 2 * FOX_WIDTH),
                  _rows(tm, FOX_WIDTH), _rows(tm, FOX_WIDTH), _rows(tm, FOX_WIDTH), _rows(tm, POOL_WIDTH),
                  _rows(tm, LANES), _resident((1, FOX_WIDTH)), _resident((1, FOX_WIDTH)), _resident(bd64.shape),
                  _resident(w_main.shape), _resident(w_f.shape)],
        out_specs=[_rows(tm, D_MODEL), _rows(tm, 4 * FOX_WIDTH), _rows(tm, LANES), const2((1, FOX_HEAD_DIM)),
                   const2((1, FOX_HEAD_DIM)), const2((1, D_MODEL)), const2((1, LANES))],
        out_shape=[_sds((T, D_MODEL), _F32), _sds((T, 4 * FOX_WIDTH), _MXU), _sds((T, LANES), _MXU),
                   _sds((1, FOX_HEAD_DIM), _F32), _sds((1, FOX_HEAD_DIM), _F32), _sds((1, D_MODEL), _F32),
                   _sds((1, LANES), _F32)],
        scratch_shapes=[pltpu.VMEM((1, FOX_WIDTH), _F32), pltpu.VMEM((1, FOX_WIDTH), _F32)],
    )(dh1, h, g_mix, zqk, dq, dk, dv, dpin, df, gq, gk, bd64, w_main, w_f)


def _layer_params(W, l):
    w_in = W["w_in"][l]
    n_main = 3 * FOX_WIDTH + POOL_WIDTH
    row = lambda a: a.reshape(1, -1).astype(_F32)
    return dict(
        g_mix=row(W["g_mix"][l]),
        w_main=w_in[:, :n_main],
        w_f=jnp.pad(w_in[:, n_main:], ((0, 0), (0, LANES - FOX_HEADS))),
        b_f=jnp.pad(row(W["b_forget"][l]), ((0, 0), (0, LANES - FOX_HEADS))),
        gq=jnp.tile(row(W["g_q_fox"][l]), (1, FOX_HEADS)),
        gk=jnp.tile(row(W["g_k_fox"][l]), (1, FOX_HEADS)),
        w_pool=W["w_pool"][l].astype(_MXU),
        pscale=row(W["pool_scale"][l]),
        w_out=W["w_out"][l],
        g_mem_q=row(W["g_mem_q"][l]),
        g_mem_kv=row(W["g_mem_kv"][l]),
        w_mem_q=W["w_mem_q"][l],
        w_mem_kv=W["w_mem_kv"][l],
        gqm=jnp.tile(row(W["g_q_mem"][l]), (1, MEM_HEADS)),
        gkm=jnp.tile(row(W["g_k_mem"][l]), (1, MEM_HEADS)),
        w_mem_out=W["w_mem_out"][l],
        g_ffn=row(W["g_ffn"][l]),
        w_gu=W["w_gate_up"][l],
        w_d=W["w_down"][l],
    )


def _per_head_lanes(sums):
    T = sums.shape[1]
    return jnp.pad(sums[:, :, :2].transpose(1, 0, 2).reshape(T, FOX_HEADS), ((0, 0), (0, LANES - FOX_HEADS)))


def _layer_fwd(h, mem, P, bd64, bd128):
    s = dict(h=h)
    s["xn"], s["zqk"], qn, kn, v, pin, s["fl"] = _mix_in_fwd(
        h, P["g_mix"], P["w_main"], P["w_f"], P["b_f"], P["gq"], P["gk"], bd64)
    qa, s["ka"], s["va"] = _gate_fwd(s["fl"], qn, kn, v)
    s["o"], s["qb"] = _fox_fwd(qa, s["ka"], s["va"])
    s["h1"], s["mixed"], s["y"] = _out_proj_fwd(h, s["o"], pin, P["w_pool"], P["pscale"], P["w_out"])
    s["mn"], s["mkv"], s["mk"], s["mv"] = _mem_kv_fwd(mem, P["g_mem_kv"], P["w_mem_kv"], P["gkm"], bd128)
    s["h2"], s["hn_mem"], s["mo"] = _mem_attn_fwd(s["h1"], P["g_mem_q"], P["w_mem_q"], P["gqm"], bd128, s["mk"],
                                                  s["mv"], P["w_mem_out"])
    h3, s["hn_ffn"] = _ffn_fwd(s["h2"], P["g_ffn"], P["w_gu"], P["w_d"])
    return h3, s


def _layer_bwd(dh3, mem, P, s, bd64, bd128):
    g = {}
    dh2, act, dgate, dup, g["g_ffn"] = _ffn_bwd(dh3, s["h2"], s["hn_ffn"], P["g_ffn"], P["w_gu"], P["w_d"])
    g["w_down"] = _matmul_tn(act, dh3, "dw_down")
    g["w_gate_up"] = jnp.concatenate([_matmul_tn(s["hn_ffn"], dgate, "dw_gate"),
                                      _matmul_tn(s["hn_ffn"], dup, "dw_up")], axis=1)

    dh1, dmq, dmk, dmv, g["g_q_mem"], g["g_mem_q"] = _mem_attn_bwd(
        dh2, s["h1"], P["g_mem_q"], P["w_mem_q"], P["gqm"], bd128, s["mk"], s["mv"], P["w_mem_out"])
    g["w_mem_out"] = _matmul_tn(s["mo"], dh2, "dw_mem_out")
    g["w_mem_q"] = _matmul_tn(s["hn_mem"], dmq, "dw_mem_q")
    g["w_mem_kv"], g["g_k_mem"], g["g_mem_kv"] = _mem_kv_bwd(dmk, dmv, s["mkv"], s["mn"], mem, P["g_mem_kv"],
                                                             P["gkm"], bd128, P["w_mem_kv"])

    doa, dpin, g["w_pool"], g["pool_scale"] = _out_proj_bwd(dh1, s["mixed"], s["o"], bd64, P["w_pool"], P["pscale"],
                                                            P["w_out"])
    g["w_out"] = jnp.concatenate([_matmul_tn(s["o"], dh1, "dw_out_fox"), _matmul_tn(s["y"], dh1, "dw_out_pool")],
                                 axis=0)
    dq, dk, dv, drs, dcs = _fox_bwd(s["qb"], s["ka"], s["va"], doa)
    df = _gate_bwd(_per_head_lanes(drs), _per_head_lanes(dcs), s["fl"])
    dh, dz, dzf, g["g_q_fox"], g["g_k_fox"], g["g_mix"], dbf = _mix_in_bwd(
        dh1, s["h"], P["g_mix"], s["zqk"], dq, dk, dv, dpin, df, P["gq"], P["gk"], bd64, P["w_main"], P["w_f"])
    g["b_forget"] = dbf[:, :FOX_HEADS]
    g["w_in"] = jnp.concatenate([_matmul_tn(s["xn"], dz, "dw_in_main"),
                                 _matmul_tn(s["xn"], dzf, "dw_in_gate")[:, :FOX_HEADS]], axis=1)
    return dh, g


def _device_grads(x, mem, tgt, W):
    bd64 = _blockdiag_ones(FOX_HEADS, FOX_HEAD_DIM)
    bd128 = _blockdiag_ones(MEM_HEADS, MEM_HEAD_DIM)
    params = [_layer_params(W, l) for l in range(DEPTH)]
    h, saved = x, []
    for l in range(DEPTH):
        h, s = _layer_fwd(h, mem, params[l], bd64, bd128)
        saved.append(s)
    dh, loss = _loss_grad(h, tgt)
    per_layer = [None] * DEPTH
    for l in reversed(range(DEPTH)):
        dh, per_layer[l] = _layer_bwd(dh, mem, params[l], saved[l], bd64, bd128)
    grads = {n: jnp.stack([per_layer[l][n].reshape(W[n].shape[1:]) for l in range(DEPTH)]) for n in WEIGHTS}
    return loss, dh, grads


def _shard_rows(shard_shape):
    n = 1
    for d in shard_shape:
        n *= d
    return n // D_MODEL


def _to_shards(full, ax):
    shp = full.shape
    parts = full.reshape(shp[:ax] + (N_CHIPS, shp[ax] // N_CHIPS) + shp[ax + 1:])
    return jnp.moveaxis(parts, ax, 0)


def _from_shards(parts, ax):
    full = jnp.moveaxis(parts, 0, ax)
    shp = full.shape
    return full.reshape(shp[:ax] + (shp[ax] * shp[ax + 1],) + shp[ax + 2:])


def _flat_rows(total_rows):
    return -(-total_rows // (2 * _TH)) * (2 * _TH)


def _pack_small(arrs):
    flat = jnp.concatenate([a.reshape(-1).astype(_F32) for a in arrs])
    rows = -(-flat.shape[0] // (8 * D_MODEL)) * 8
    return jnp.pad(flat, (0, rows * D_MODEL - flat.shape[0])).reshape(rows, D_MODEL)


def _unpack_small(flat, shapes):
    flat = flat.reshape(-1)
    out, off = [], 0
    for shp in shapes:
        n = 1
        for d in shp:
            n *= d
        out.append(flat[off:off + n].reshape(shp))
        off += n
    return out


def _mesh_pos():
    return lax.axis_index("x"), lax.axis_index("y"), lax.axis_index("c")


def _other_chips(x, y):
    return [(1 - x, y), (x, 1 - y), (1 - x, 1 - y)]


_ANY = pl.BlockSpec(memory_space=pl.ANY)


def _allgather_weights(flat):
    R = flat.shape[0]
    half = R // 2

    def body(src, out, send_sems, recv_sems):
        x, y, c = _mesh_pos()
        me = 2 * x + y
        sibling = (x, y, 1 - c)
        chips = _other_chips(x, y)
        mine = pl.ds(pl.multiple_of(c * half, 16), half)
        theirs = pl.ds(pl.multiple_of((1 - c) * half, 16), half)

        def copy(k, src_ref, dst_ref, to):
            return pltpu.make_async_remote_copy(src_ref=src_ref, dst_ref=dst_ref, send_sem=send_sems.at[k],
                                                recv_sem=recv_sems.at[k], device_id=to, device_id_type=MESH)

        first = [copy(j, src.at[mine], out.at[me, mine], (*chip, c)) for j, chip in enumerate(chips)]
        for cp in first:
            cp.start()
        passed = []
        for j, (cx, cy) in enumerate(chips):
            slab = out.at[2 * cx + cy, mine]
            copy(j, slab, slab, (cx, cy, c)).wait_recv()
            fwd = copy(3 + j, slab, slab, sibling)
            fwd.start()
            passed.append(fwd)
        for j, (cx, cy) in enumerate(chips):
            slab = out.at[2 * cx + cy, theirs]
            copy(3 + j, slab, slab, sibling).wait_recv()
        for cp in first + passed:
            cp.wait_send()

    return pl.pallas_call(
        body, name="allgather_weights", in_specs=[_ANY], out_specs=_ANY,
        out_shape=_sds((N_CHIPS,) + flat.shape, flat.dtype),
        scratch_shapes=[pltpu.SemaphoreType.DMA((6,)), pltpu.SemaphoreType.DMA((6,))],
    )(flat)


def _exchange_halves(g):
    half = g.shape[2]

    def body(g_ref, got_ref, send_sems, recv_sems):
        x, y, c = _mesh_pos()
        sibling = (x, y, 1 - c)
        copies = []
        for k in range(N_CHIPS):
            cp = pltpu.make_async_remote_copy(src_ref=g_ref.at[k, 1 - c], dst_ref=got_ref.at[k],
                                              send_sem=send_sems.at[k], recv_sem=recv_sems.at[k],
                                              device_id=sibling, device_id_type=MESH)
            cp.start()
            copies.append(cp)
        for cp in copies:
            cp.wait()

    return pl.pallas_call(
        body, name="grad_exchange_halves", in_specs=[_ANY], out_specs=_ANY,
        out_shape=_sds((N_CHIPS, half, D_MODEL), _F32),
        scratch_shapes=[pltpu.SemaphoreType.DMA((N_CHIPS,)), pltpu.SemaphoreType.DMA((N_CHIPS,))],
    )(g)


def _add_halves(g, got, c_idx):
    half = g.shape[2]
    ta = _TH

    def body(c_ref, a_ref, b_ref, o_ref):
        o_ref[...] = (a_ref[0] + b_ref[...]).astype(jnp.bfloat16)

    return pl.pallas_call(
        body, name="grad_add_halves",
        grid_spec=pltpu.PrefetchScalarGridSpec(
            num_scalar_prefetch=1, grid=(N_CHIPS, half // ta),
            in_specs=[pl.BlockSpec((1, 1, ta, D_MODEL), lambda k, i, c: (k, c[0], i, 0)),
                      pl.BlockSpec((1, ta, D_MODEL), lambda k, i, c: (k, i, 0))],
            out_specs=pl.BlockSpec((1, ta, D_MODEL), lambda k, i, c: (k, i, 0))),
        out_shape=_sds((N_CHIPS, half, D_MODEL), jnp.bfloat16),
    )(c_idx, g, got)


def _scatter_to_chips(p):
    def body(p_ref, got_ref, send_sems, recv_sems):
        x, y, c = _mesh_pos()
        me = 2 * x + y
        copies = []
        for j, (cx, cy) in enumerate(_other_chips(x, y)):
            cp = pltpu.make_async_remote_copy(src_ref=p_ref.at[2 * cx + cy], dst_ref=got_ref.at[me],
                                              send_sem=send_sems.at[j], recv_sem=recv_sems.at[j],
                                              device_id=(cx, cy, c), device_id_type=MESH)
            cp.start()
            copies.append(cp)
        for j, (cx, cy) in enumerate(_other_chips(x, y)):
            slab = got_ref.at[2 * cx + cy]
            pltpu.make_async_remote_copy(src_ref=slab, dst_ref=slab, send_sem=send_sems.at[j],
                                         recv_sem=recv_sems.at[j], device_id=(cx, cy, c),
                                         device_id_type=MESH).wait_recv()
        for cp in copies:
            cp.wait_send()

    return pl.pallas_call(
        body, name="grad_scatter_chips", in_specs=[_ANY], out_specs=_ANY, out_shape=_sds(p.shape, p.dtype),
        scratch_shapes=[pltpu.SemaphoreType.DMA((3,)), pltpu.SemaphoreType.DMA((3,))],
    )(p)


def _sum_chips(got, c_idx):
    half = got.shape[1]
    ta = _TH

    def body(c_ref, a_ref, o_ref):
        f = lambda k: a_ref[k].astype(_F32)
        o_ref[0] = ((f(0) + f(1)) + f(2)) + f(3)

    return pl.pallas_call(
        body, name="grad_sum_chips",
        grid_spec=pltpu.PrefetchScalarGridSpec(
            num_scalar_prefetch=1, grid=(half // ta,),
            in_specs=[pl.BlockSpec((N_CHIPS, ta, D_MODEL), lambda i, c: (0, i, 0))],
            out_specs=pl.BlockSpec((1, ta, D_MODEL), lambda i, c: (c[0], i, 0))),
        out_shape=_sds((2, half, D_MODEL), _F32),
    )(c_idx, got)


def _share_with_sibling(buf):
    def body(buf_ref, out_ref, send_sem, recv_sem):
        x, y, c = _mesh_pos()
        cp = pltpu.make_async_remote_copy(src_ref=out_ref.at[c], dst_ref=out_ref.at[c], send_sem=send_sem,
                                          recv_sem=recv_sem, device_id=(x, y, 1 - c), device_id_type=MESH)
        cp.start()
        theirs = out_ref.at[1 - c]
        pltpu.make_async_remote_copy(src_ref=theirs, dst_ref=theirs, send_sem=send_sem, recv_sem=recv_sem,
                                     device_id=(x, y, 1 - c), device_id_type=MESH).wait_recv()
        cp.wait_send()

    return pl.pallas_call(
        body, name="grad_share_sibling", in_specs=[_ANY], out_specs=_ANY, out_shape=_sds(buf.shape, _F32),
        input_output_aliases={0: 0},
        scratch_shapes=[pltpu.SemaphoreType.DMA, pltpu.SemaphoreType.DMA],
    )(buf)


def _allreduce_small(g):
    rows = g.shape[0]
    n_dev = 2 * N_CHIPS

    def body(g_ref, out_ref, gathered, local_sem, send_sems, recv_sems):
        x, y, c = _mesh_pos()
        me = 4 * x + 2 * y + c
        own = pltpu.make_async_copy(g_ref, gathered.at[me], local_sem)
        own.start()
        copies = []
        for k in range(1, n_dev):
            fx, fy, fc = (k >> 2) & 1, (k >> 1) & 1, k & 1
            cp = pltpu.make_async_remote_copy(
                src_ref=g_ref, dst_ref=gathered.at[me], send_sem=send_sems.at[k - 1], recv_sem=recv_sems.at[k - 1],
                device_id=(x ^ fx, y ^ fy, c ^ fc), device_id_type=MESH)
            cp.start()
            copies.append(cp)
        for k in range(1, n_dev):
            fx, fy, fc = (k >> 2) & 1, (k >> 1) & 1, k & 1
            px, py, pc = x ^ fx, y ^ fy, c ^ fc
            slab = gathered.at[4 * px + 2 * py + pc]
            pltpu.make_async_remote_copy(src_ref=slab, dst_ref=slab, send_sem=send_sems.at[k - 1],
                                         recv_sem=recv_sems.at[k - 1], device_id=(px, py, pc),
                                         device_id_type=MESH).wait_recv()
        for cp in copies:
            cp.wait_send()
        own.wait()
        acc = gathered[0]
        for d in range(1, n_dev):
            acc = acc + gathered[d]
        out_ref[...] = acc

    vmem = pl.BlockSpec(memory_space=pltpu.VMEM)
    return pl.pallas_call(
        body, name="allreduce_small", in_specs=[vmem], out_specs=vmem, out_shape=_sds((rows, D_MODEL), _F32),
        scratch_shapes=[pltpu.VMEM((n_dev, rows, D_MODEL), _F32), pltpu.SemaphoreType.DMA,
                        pltpu.SemaphoreType.DMA((n_dev - 1,)), pltpu.SemaphoreType.DMA((n_dev - 1,))],
    )(g)


def _adamw(w, g, m, v, name):
    shape = w.shape
    cols = shape[-1]
    rows = 1
    for d in shape[:-1]:
        rows *= d
    w2, g2, m2, v2 = (a.reshape(rows, cols) for a in (w, g, m, v))
    tr = _pick_tile(rows, (512, 256, 128, 64, 32, 16, 8))

    def body(w_ref, g_ref, m_ref, v_ref, d_ref, nm_ref, nv_ref):
        gg = g_ref[...]
        nm = ADAM_B1 * m_ref[...] + (1.0 - ADAM_B1) * gg
        nv = ADAM_B2 * v_ref[...] + (1.0 - ADAM_B2) * (gg * gg)
        m_hat = nm / (1.0 - ADAM_B1 ** ADAM_STEP)
        v_hat = nv / (1.0 - ADAM_B2 ** ADAM_STEP)
        d_ref[...] = -ADAM_LR * (m_hat / (jnp.sqrt(v_hat) + ADAM_EPS) + ADAM_WD * w_ref[...])
        nm_ref[...] = nm
        nv_ref[...] = nv

    spec = pl.BlockSpec((tr, cols), lambda i: (i, 0))
    outs = pl.pallas_call(
        body, name=name, grid=(rows // tr,), in_specs=[spec] * 4, out_specs=[spec] * 3,
        out_shape=[_sds((rows, cols), _F32)] * 3,
    )(w2, g2, m2, v2)
    return tuple(o.reshape(shape) for o in outs)


def kernel(x, mem, g_mix, w_in, b_forget, g_q_fox, g_k_fox, w_pool, pool_scale, w_out, g_mem_q, g_mem_kv, w_mem_q, w_mem_kv, g_q_mem, g_k_mem, w_mem_out, g_ffn, w_gate_up, w_down, loss_target, m_g_mix, m_w_in, m_b_forget, m_g_q_fox, m_g_k_fox, m_w_pool, m_pool_scale, m_w_out, m_g_mem_q, m_g_mem_kv, m_w_mem_q, m_w_mem_kv, m_g_q_mem, m_g_k_mem, m_w_mem_out, m_g_ffn, m_w_gate_up, m_w_down, v_g_mix, v_w_in, v_b_forget, v_g_q_fox, v_g_k_fox, v_w_pool, v_pool_scale, v_w_out, v_g_mem_q, v_g_mem_kv, v_w_mem_q, v_w_mem_kv, v_g_q_mem, v_g_k_mem, v_w_mem_out, v_g_ffn, v_w_gate_up, v_w_down):
    w = dict(g_mix=g_mix, w_in=w_in, b_forget=b_forget, g_q_fox=g_q_fox, g_k_fox=g_k_fox, w_pool=w_pool,
             pool_scale=pool_scale, w_out=w_out, g_mem_q=g_mem_q, g_mem_kv=g_mem_kv, w_mem_q=w_mem_q,
             w_mem_kv=w_mem_kv, g_q_mem=g_q_mem, g_k_mem=g_k_mem, w_mem_out=w_mem_out, g_ffn=g_ffn,
             w_gate_up=w_gate_up, w_down=w_down)
    m = dict(g_mix=m_g_mix, w_in=m_w_in, b_forget=m_b_forget, g_q_fox=m_g_q_fox, g_k_fox=m_g_k_fox, w_pool=m_w_pool,
             pool_scale=m_pool_scale, w_out=m_w_out, g_mem_q=m_g_mem_q, g_mem_kv=m_g_mem_kv, w_mem_q=m_w_mem_q,
             w_mem_kv=m_w_mem_kv, g_q_mem=m_g_q_mem, g_k_mem=m_g_k_mem, w_mem_out=m_w_mem_out, g_ffn=m_g_ffn,
             w_gate_up=m_w_gate_up, w_down=m_w_down)
    v = dict(g_mix=v_g_mix, w_in=v_w_in, b_forget=v_b_forget, g_q_fox=v_g_q_fox, g_k_fox=v_g_k_fox, w_pool=v_w_pool,
             pool_scale=v_pool_scale, w_out=v_w_out, g_mem_q=v_g_mem_q, g_mem_kv=v_g_mem_kv, w_mem_q=v_w_mem_q,
             w_mem_kv=v_w_mem_kv, g_q_mem=v_g_q_mem, g_k_mem=v_g_k_mem, w_mem_out=v_w_mem_out, g_ffn=v_g_ffn,
             w_gate_up=v_w_gate_up, w_down=v_w_down)

    shard_rows = [_shard_rows(w[n].shape) for n in BIG]
    slot_rows = [-(-r // 16) * 16 for r in shard_rows]
    total = sum(slot_rows)
    R = _flat_rows(total)
    flat = jnp.concatenate([jnp.pad(w[n].astype(_MXU).reshape(-1, D_MODEL), ((0, s - r), (0, 0)))
                            for n, r, s in zip(BIG, shard_rows, slot_rows)]
                           + [jnp.zeros((R - total, D_MODEL), _MXU)])
    chip = 2 * lax.axis_index("x") + lax.axis_index("y")
    gathered = lax.dynamic_update_slice(_allgather_weights(flat), flat[None], (chip, 0, 0))
    W = {n: w[n] for n in SMALL}
    off = 0
    for n, r, s in zip(BIG, shard_rows, slot_rows):
        parts = gathered[:, off:off + r].reshape((N_CHIPS,) + w[n].shape)
        W[n] = _from_shards(parts, SHARD_AXIS[n])
        off += s

    loss, grad_x, grads = _device_grads(x[0], mem[0], loss_target[0], W)
    loss = lax.psum(loss[0, 0], ("x", "y", "c"))

    half = R // 2
    gflat = jnp.concatenate(
        [jnp.pad(_to_shards(grads[n], SHARD_AXIS[n]).reshape(N_CHIPS, -1, D_MODEL), ((0, 0), (0, s - r), (0, 0)))
         for n, r, s in zip(BIG, shard_rows, slot_rows)]
        + [jnp.zeros((N_CHIPS, R - total, D_MODEL), _F32)], axis=1).reshape(N_CHIPS, 2, half, D_MODEL)
    c_idx = lax.axis_index("c").astype(jnp.int32).reshape(1)
    partial = _add_halves(gflat, _exchange_halves(gflat), c_idx)
    own = lax.dynamic_slice(partial, (chip, 0, 0), (1, half, D_MODEL))
    partials = lax.dynamic_update_slice(_scatter_to_chips(partial), own, (chip, 0, 0))
    reduced = _share_with_sibling(_sum_chips(partials, c_idx)).reshape(R, D_MODEL)

    small_shapes = [w[n].shape for n in SMALL]
    gsmall = _unpack_small(_allreduce_small(_pack_small([grads[n] for n in SMALL])), small_shapes)

    g_out, d_out, m_out, v_out = {}, {}, {}, {}
    off = 0
    for n, r, s in zip(BIG, shard_rows, slot_rows):
        g_out[n] = reduced[off:off + r].reshape(w[n].shape)
        d_out[n], m_out[n], v_out[n] = _adamw(w[n], g_out[n], m[n], v[n], "adamw_" + n)
        off += s
    packed = [_pack_small([t[n] for n in SMALL]) for t in (w, m, v)]
    ds, ms, vs = _adamw(packed[0], _pack_small(gsmall), packed[1], packed[2], "adamw_small")
    for n, gi, di, mi, vi in zip(SMALL, gsmall, _unpack_small(ds, small_shapes), _unpack_small(ms, small_shapes),
                                 _unpack_small(vs, small_shapes)):
        g_out[n], d_out[n], m_out[n], v_out[n] = gi, di, mi, vi

    return (loss, grad_x[None], *[g_out[n] for n in WEIGHTS], *[d_out[n] for n in WEIGHTS],
            *[m_out[n] for n in WEIGHTS], *[v_out[n] for n in WEIGHTS])
```

```python
import functools

import jax
import jax.numpy as jnp
import numpy as np
from jax import lax
from jax.experimental import pallas as pl
from jax.experimental.pallas import tpu as pltpu

_F32 = jnp.float32
_MXU = jnp.bfloat16

D_MODEL = 1024
DEPTH = 2
FOX_HEADS = 8
FOX_HEAD_DIM = 64
FOX_WIDTH = FOX_HEADS * FOX_HEAD_DIM
POOL_WINDOWS = (2, 4, 8, 16)
POOL_GROUP_DIM = 128
POOL_WIDTH = len(POOL_WINDOWS) * POOL_GROUP_DIM
MEM_HEADS = 4
MEM_HEAD_DIM = 128
MEM_WIDTH = MEM_HEADS * MEM_HEAD_DIM
D_FF = 2816
EPS = 1e-6
LANES = 128
HALO = 16

ADAM_LR = 0.001
ADAM_B1 = 0.9
ADAM_B2 = 0.999
ADAM_EPS = 1e-08
ADAM_WD = 0.01
ADAM_STEP = 10

N_CHIPS = 4
MESH = pl.DeviceIdType.MESH

_TM = 256
_TQ = 1024
_TMF = 256
_TF = 1408
_TT = 1024
_TB = 256
_TA = 512

BIG = ("w_in", "w_out", "w_mem_q", "w_mem_kv", "w_mem_out", "w_gate_up", "w_down")
SHARD_AXIS = {"w_in": 2, "w_out": 1, "w_mem_q": 1, "w_mem_kv": 1, "w_mem_out": 2, "w_gate_up": 2, "w_down": 1}
SMALL = ("g_mix", "b_forget", "g_q_fox", "g_k_fox", "w_pool", "pool_scale", "g_mem_q", "g_mem_kv", "g_q_mem",
         "g_k_mem", "g_ffn")
WEIGHTS = ("g_mix", "w_in", "b_forget", "g_q_fox", "g_k_fox", "w_pool", "pool_scale", "w_out", "g_mem_q", "g_mem_kv",
           "w_mem_q", "w_mem_kv", "g_q_mem", "g_k_mem", "w_mem_out", "g_ffn", "w_gate_up", "w_down")


def _dot(a, b):
    return jnp.dot(a, b, preferred_element_type=_F32)


def _dot_nt(a, b):
    return lax.dot_general(a, b, (((1,), (1,)), ((), ())), preferred_element_type=_F32)


def _dot_tn(a, b):
    return lax.dot_general(a, b, (((0,), (0,)), ((), ())), preferred_element_type=_F32)


def _group_sum(x, ones_blockdiag):
    hi = x.astype(_MXU)
    lo = (x - hi.astype(_F32)).astype(_MXU)
    return _dot(hi, ones_blockdiag) + _dot(lo, ones_blockdiag)


def _tri_dot(tri, x):
    h1 = x.astype(jnp.bfloat16)
    r1 = x - h1.astype(_F32)
    h2 = r1.astype(jnp.bfloat16)
    h3 = (r1 - h2.astype(_F32)).astype(jnp.bfloat16)
    return _dot(tri, h1) + _dot(tri, h2) + _dot(tri, h3)


def _rstd(x):
    return lax.rsqrt(jnp.mean(x * x, axis=-1, keepdims=True) + EPS)


def _norm_bwd(dy, x, g):
    r = _rstd(x)
    xhat = x * r
    u = dy * g
    dx = r * (u - xhat * jnp.mean(u * xhat, axis=-1, keepdims=True))
    return dx, jnp.sum(dy * xhat, axis=0, keepdims=True)


def _headnorm_bwd(dy, x, g, ones_blockdiag, width):
    r = lax.rsqrt(_group_sum(x * x, ones_blockdiag) * (1.0 / width) + EPS)
    xhat = x * r
    u = dy * g
    dx = r * (u - xhat * (_group_sum(u * xhat, ones_blockdiag) * (1.0 / width)))
    return dx, jnp.sum(dy * xhat, axis=0, keepdims=True)


def _fold_heads(row, heads, width):
    acc = row[:, 0:width]
    for h in range(1, heads):
        acc = acc + row[:, h * width:(h + 1) * width]
    return acc


_ANY = pl.BlockSpec(memory_space=pl.ANY)


def _resident(shape):
    return pl.BlockSpec(shape, lambda *_: (0,) * len(shape), pipeline_mode=pl.Buffered(1))


def _rows(tm, width):
    return pl.BlockSpec((tm, width), lambda i: (i, 0))


def _blockdiag_ones(groups, width):
    return jnp.kron(jnp.eye(groups, dtype=_F32), jnp.ones((width, width), _F32)).astype(_MXU)


def _sds(shape, dtype):
    return jax.ShapeDtypeStruct(shape, dtype)


def _mix_in_fwd(h, g_mix, w_main, w_f, b_f, gq, gk, bd64):
    T = h.shape[0]
    tm = min(_TM, T)

    def body(h_ref, g_ref, wm_ref, wf_ref, bf_ref, gq_ref, gk_ref, bd_ref,
             xn_ref, zqk_ref, qn_ref, kn_ref, v_ref, pin_ref, fl_ref):
        x = h_ref[...]
        xn = ((x * _rstd(x)) * g_ref[...]).astype(_MXU)
        xn_ref[...] = xn
        z = _dot(xn, wm_ref[...])
        q = z[:, :FOX_WIDTH]
        k = z[:, FOX_WIDTH:2 * FOX_WIDTH]
        zqk_ref[...] = z[:, :2 * FOX_WIDTH]
        bd = bd_ref[...]
        rq = lax.rsqrt(_group_sum(q * q, bd) * (1.0 / FOX_HEAD_DIM) + EPS)
        rk = lax.rsqrt(_group_sum(k * k, bd) * (1.0 / FOX_HEAD_DIM) + EPS)
        qn_ref[...] = ((q * rq) * gq_ref[...]).astype(_MXU)
        kn_ref[...] = ((k * rk) * gk_ref[...]).astype(_MXU)
        v_ref[...] = z[:, 2 * FOX_WIDTH:3 * FOX_WIDTH].astype(_MXU)
        pin_ref[...] = z[:, 3 * FOX_WIDTH:]
        fl_ref[...] = _dot(xn, wf_ref[...]) + bf_ref[...]

    return pl.pallas_call(
        body, name="mix_in_fwd", grid=(T // tm,),
        in_specs=[_rows(tm, D_MODEL), _resident((1, D_MODEL)), _resident(w_main.shape), _resident(w_f.shape),
                  _resident((1, LANES)), _resident((1, FOX_WIDTH)), _resident((1, FOX_WIDTH)), _resident(bd64.shape)],
        out_specs=[_rows(tm, D_MODEL), _rows(tm, 2 * FOX_WIDTH), _rows(tm, FOX_WIDTH), _rows(tm, FOX_WIDTH),
                   _rows(tm, FOX_WIDTH), _rows(tm, POOL_WIDTH), _rows(tm, LANES)],
        out_shape=[_sds((T, D_MODEL), _MXU), _sds((T, 2 * FOX_WIDTH), _F32), _sds((T, FOX_WIDTH), _MXU),
                   _sds((T, FOX_WIDTH), _MXU), _sds((T, FOX_WIDTH), _MXU), _sds((T, POOL_WIDTH), _F32),
                   _sds((T, LANES), _F32)],
    )(h, g_mix, w_main, w_f, b_f, gq, gk, bd64)


def _split3(x):
    h1 = x.astype(jnp.bfloat16).astype(_F32)
    r1 = x - h1
    h2 = r1.astype(jnp.bfloat16).astype(_F32)
    h3 = (r1 - h2).astype(jnp.bfloat16).astype(_F32)
    return h1, h2, h3


_ONES3 = (1.0, 1.0, 1.0)
_ZEROS3 = (0.0, 0.0, 0.0)


def _aug_head(feat, first, second):
    rows = feat.shape[0]
    lane = lax.broadcasted_iota(jnp.int32, (rows, LANES - FOX_HEAD_DIM), 1)
    aux = jnp.zeros((rows, LANES - FOX_HEAD_DIM), _F32)
    for k in range(3):
        aux = jnp.where(lane == k, first[k], aux)
        aux = jnp.where(lane == 3 + k, second[k], aux)
    return jnp.concatenate([feat.astype(_MXU), aux.astype(_MXU)], axis=-1)


def _gate_fwd(fl, qn, kn, v):
    T = fl.shape[0]
    tb = min(_TB, T)
    wide = FOX_HEADS * LANES

    def body(fl_ref, q_ref, k_ref, v_ref, qa_ref, ka_ref, va_ref, carry):
        @pl.when(pl.program_id(0) == 0)
        def _():
            carry[...] = jnp.zeros_like(carry)

        x = fl_ref[...]
        ls = jnp.minimum(x, 0.0) - jnp.log1p(jnp.exp(-jnp.abs(x)))
        row = lax.broadcasted_iota(jnp.int32, (tb, tb), 0)
        col = lax.broadcasted_iota(jnp.int32, (tb, tb), 1)
        tri = jnp.where(col <= row, 1.0, 0.0).astype(jnp.bfloat16)
        cs = _tri_dot(tri, ls) + carry[...]
        carry[...] = cs[tb - 1:tb, :]
        for h in range(FOX_HEADS):
            sl = slice(h * FOX_HEAD_DIM, (h + 1) * FOX_HEAD_DIM)
            out = slice(h * LANES, (h + 1) * LANES)
            c3 = _split3(cs[:, h:h + 1])
            qa_ref[:, out] = _aug_head(q_ref[:, sl].astype(_F32) * (FOX_HEAD_DIM ** -0.5), c3, _ONES3)
            ka_ref[:, out] = _aug_head(k_ref[:, sl], _ONES3, tuple(-t for t in c3))
            va_ref[:, out] = _aug_head(v_ref[:, sl], _ONES3, _ZEROS3)

    return pl.pallas_call(
        body, name="gate_fwd", grid=(T // tb,),
        in_specs=[_rows(tb, LANES), _rows(tb, FOX_WIDTH), _rows(tb, FOX_WIDTH), _rows(tb, FOX_WIDTH)],
        out_specs=[_rows(tb, wide)] * 3, out_shape=[_sds((T, wide), _MXU)] * 3,
        scratch_shapes=[pltpu.VMEM((1, LANES), _F32)],
    )(fl, qn, kn, v)


def _gate_bwd(drs, dcs, fl):
    T = fl.shape[0]
    tb = min(_TB, T)
    nb = T // tb

    def body(r_ref, d_ref, fl_ref, df_ref, carry):
        @pl.when(pl.program_id(0) == 0)
        def _():
            carry[...] = jnp.zeros_like(carry)

        row = lax.broadcasted_iota(jnp.int32, (tb, tb), 0)
        col = lax.broadcasted_iota(jnp.int32, (tb, tb), 1)
        tri = jnp.where(col >= row, 1.0, 0.0).astype(jnp.bfloat16)
        rc = _tri_dot(tri, r_ref[...] - d_ref[...]) + carry[...]
        carry[...] = rc[0:1, :]
        df_ref[...] = rc * (1.0 / (1.0 + jnp.exp(fl_ref[...])))

    rev = pl.BlockSpec((tb, LANES), lambda i: (nb - 1 - i, 0))
    return pl.pallas_call(
        body, name="gate_bwd", grid=(nb,), in_specs=[rev, rev, rev], out_specs=rev,
        out_shape=_sds((T, LANES), _F32), scratch_shapes=[pltpu.VMEM((1, LANES), _F32)],
    )(drs, dcs, fl)


def _fox_scores(q_ref, k_ref, hh, masked):
    sl = slice(LANES * hh, LANES * (hh + 1))
    s = _dot_nt(q_ref[:, sl], k_ref[:, sl])
    if masked:
        row = lax.broadcasted_iota(jnp.int32, s.shape, 0)
        col = lax.broadcasted_iota(jnp.int32, s.shape, 1)
        s = jnp.where(col <= row, s, -jnp.inf)
    return s


def _causal_steps(nq, query_major):
    if query_major:
        steps = [(i, j) for i in range(nq) for j in range(i + 1)]
    else:
        steps = [(i, j) for j in range(nq) for i in range(j, nq)]
    return (jnp.asarray(np.array([s[0] for s in steps], np.int32)),
            jnp.asarray(np.array([s[1] for s in steps], np.int32)))


def _fox_fwd(qa, ka, va):
    T = qa.shape[0]
    tq = min(_TQ, T)
    nq = T // tq
    ii, jj = _causal_steps(nq, True)

    def body(ii_ref, jj_ref, q_ref, k_ref, v_ref, o_ref, qb_ref, m_s, acc_s):
        t = pl.program_id(1)
        i = ii_ref[t]
        j = jj_ref[t]

        @pl.when(j == 0)
        def _():
            m_s[...] = jnp.full(m_s.shape, -jnp.inf, _F32)
            acc_s[...] = jnp.zeros_like(acc_s)

        def step(masked):
            for hh in range(2):
                s = _fox_scores(q_ref, k_ref, hh, masked)
                m_prev = m_s[hh]
                m_new = jnp.maximum(m_prev, jnp.max(s, axis=-1, keepdims=True))
                p = jnp.exp(s - m_new).astype(_MXU)
                acc_s[hh] = jnp.exp(m_prev - m_new) * acc_s[hh] + _dot(p, v_ref[:, LANES * hh:LANES * (hh + 1)])
                m_s[hh] = m_new

        @pl.when(j < i)
        def _():
            step(False)

        @pl.when(j == i)
        def _():
            step(True)
            outs = []
            for hh in range(2):
                sl = slice(LANES * hh, LANES * (hh + 1))
                acc = acc_s[hh]
                l = acc[:, FOX_HEAD_DIM:FOX_HEAD_DIM + 1]
                outs.append(acc[:, :FOX_HEAD_DIM] / l)
                qh = q_ref[:, sl]
                cq = (qh[:, FOX_HEAD_DIM:FOX_HEAD_DIM + 1].astype(_F32)
                      + qh[:, FOX_HEAD_DIM + 1:FOX_HEAD_DIM + 2].astype(_F32)
                      + qh[:, FOX_HEAD_DIM + 2:FOX_HEAD_DIM + 3].astype(_F32))
                qb_ref[:, sl] = _aug_head(qh[:, :FOX_HEAD_DIM], _split3(cq - (m_s[hh] + jnp.log(l))), _ONES3)
            o_ref[...] = jnp.concatenate(outs, axis=-1).astype(_MXU)

    qspec = pl.BlockSpec((tq, 2 * LANES), lambda p, t, ii, jj: (ii[t], p))
    kspec = pl.BlockSpec((tq, 2 * LANES), lambda p, t, ii, jj: (jj[t], p))
    return pl.pallas_call(
        body, name="fox_fwd",
        grid_spec=pltpu.PrefetchScalarGridSpec(
            num_scalar_prefetch=2, grid=(FOX_HEADS // 2, int(ii.shape[0])),
            in_specs=[qspec, kspec, kspec],
            out_specs=[pl.BlockSpec((tq, 2 * FOX_HEAD_DIM), lambda p, t, ii, jj: (ii[t], p)), qspec],
            scratch_shapes=[pltpu.VMEM((2, tq, 1), _F32), pltpu.VMEM((2, tq, LANES), _F32)]),
        out_shape=[_sds((T, FOX_WIDTH), _MXU), _sds((T, FOX_HEADS * LANES), _MXU)],
    )(ii, jj, qa, ka, va)


def _pool_window_sum(ext, w, forward):
    n = ext.shape[0]
    sm = ext
    k = 1
    while k < w:
        sm = sm + pltpu.roll(sm, (n - k) if forward else k, axis=0)
        k *= 2
    return sm


def _out_proj_fwd(h, o, pin, w_pool, pscale, w_out):
    T = h.shape[0]
    tm = min(_TM, T)
    hb = tm // HALO

    def body(h_ref, o_ref, pin_ref, halo_ref, wp_ref, ps_ref, wo_ref, h1_ref, mixed_ref, y_ref):
        i = pl.program_id(0)
        pin_t = pin_ref[...]
        halo = jnp.where(i == 0, 0.0, halo_ref[...])
        ext = jnp.concatenate([halo, pin_t], axis=0)
        t = (i * tm + lax.broadcasted_iota(jnp.int32, (tm, 1), 0) + 1).astype(_F32)
        mixed, ys = [], []
        for g, w in enumerate(POOL_WINDOWS):
            sl = slice(g * POOL_GROUP_DIM, (g + 1) * POOL_GROUP_DIM)
            win = _pool_window_sum(ext[:, sl], w, False)[HALO:, :]
            mg = (win / jnp.minimum(t, float(w)) - pin_t[:, sl]).astype(_MXU)
            mixed.append(mg)
            ys.append(_dot(mg, wp_ref[g]))
        mixed_ref[...] = jnp.concatenate(mixed, axis=-1)
        y = (jnp.concatenate(ys, axis=-1) * ps_ref[...]).astype(_MXU)
        y_ref[...] = y
        h1_ref[...] = h_ref[...] + _dot(o_ref[...], wo_ref[:FOX_WIDTH, :]) + _dot(y, wo_ref[FOX_WIDTH:, :])

    return pl.pallas_call(
        body, name="out_proj_fwd", grid=(T // tm,),
        in_specs=[_rows(tm, D_MODEL), _rows(tm, FOX_WIDTH), _rows(tm, POOL_WIDTH),
                  pl.BlockSpec((HALO, POOL_WIDTH), lambda i: (jnp.maximum(i * hb - 1, 0), 0)),
                  _resident(w_pool.shape), _resident((1, POOL_WIDTH)), _resident(w_out.shape)],
        out_specs=[_rows(tm, D_MODEL), _rows(tm, POOL_WIDTH), _rows(tm, POOL_WIDTH)],
        out_shape=[_sds((T, D_MODEL), _F32), _sds((T, POOL_WIDTH), _MXU), _sds((T, POOL_WIDTH), _MXU)],
    )(h, o, pin, pin, w_pool, pscale, w_out)


def _mem_kv_fwd(mem, g_kv, w_kv, gkm, bd128):
    M = mem.shape[0]

    def body(mem_ref, g_ref, w_ref, gk_ref, bd_ref, mn_ref, mkv_ref, mk_ref, mv_ref):
        x = mem_ref[...]
        mn = ((x * _rstd(x)) * g_ref[...]).astype(_MXU)
        mn_ref[...] = mn
        z = _dot(mn, w_ref[...])
        mkv_ref[...] = z
        k = z[:, :MEM_WIDTH]
        rk = lax.rsqrt(_group_sum(k * k, bd_ref[...]) * (1.0 / MEM_HEAD_DIM) + EPS)
        mk_ref[...] = ((k * rk) * gk_ref[...]).astype(_MXU)
        mv_ref[...] = z[:, MEM_WIDTH:].astype(_MXU)

    return pl.pallas_call(
        body, name="mem_kv_fwd",
        out_shape=[_sds((M, D_MODEL), _MXU), _sds((M, 2 * MEM_WIDTH), _F32), _sds((M, MEM_WIDTH), _MXU),
                   _sds((M, MEM_WIDTH), _MXU)],
    )(mem, g_kv, w_kv, gkm, bd128)


def _mem_softmax(qn, mk_ref, hd):
    sl = slice(hd * MEM_HEAD_DIM, (hd + 1) * MEM_HEAD_DIM)
    s = _dot_nt(qn[:, sl], mk_ref[:, sl]) * (MEM_HEAD_DIM ** -0.5)
    e = jnp.exp(s - jnp.max(s, axis=-1, keepdims=True))
    return e / jnp.sum(e, axis=-1, keepdims=True)


def _mem_attn_fwd(h1, g_q, w_q, gqm, bd128, mk, mv, w_mo):
    T = h1.shape[0]
    tm = min(_TM, T)

    def body(h_ref, g_ref, wq_ref, gq_ref, bd_ref, mk_ref, mv_ref, wo_ref, h2_ref, hn_ref, mo_ref):
        x = h_ref[...]
        hn = ((x * _rstd(x)) * g_ref[...]).astype(_MXU)
        hn_ref[...] = hn
        mq = _dot(hn, wq_ref[...])
        rq = lax.rsqrt(_group_sum(mq * mq, bd_ref[...]) * (1.0 / MEM_HEAD_DIM) + EPS)
        qn = ((mq * rq) * gq_ref[...]).astype(_MXU)
        outs = []
        for hd in range(MEM_HEADS):
            p = _mem_softmax(qn, mk_ref, hd).astype(_MXU)
            outs.append(_dot(p, mv_ref[:, hd * MEM_HEAD_DIM:(hd + 1) * MEM_HEAD_DIM]))
        mo = jnp.concatenate(outs, axis=-1).astype(_MXU)
        mo_ref[...] = mo
        h2_ref[...] = x + _dot(mo, wo_ref[...])

    return pl.pallas_call(
        body, name="mem_attn_fwd", grid=(T // tm,),
        in_specs=[_rows(tm, D_MODEL), _resident((1, D_MODEL)), _resident(w_q.shape), _resident((1, MEM_WIDTH)),
                  _resident(bd128.shape), _resident(mk.shape), _resident(mv.shape), _resident(w_mo.shape)],
        out_specs=[_rows(tm, D_MODEL), _rows(tm, D_MODEL), _rows(tm, MEM_WIDTH)],
        out_shape=[_sds((T, D_MODEL), _F32), _sds((T, D_MODEL), _MXU), _sds((T, MEM_WIDTH), _MXU)],
    )(h1, g_q, w_q, gqm, bd128, mk, mv, w_mo)


def _ffn_weight_specs(layer):
    nf = D_FF // _TF
    return [pl.BlockSpec((1, D_MODEL, _TF), lambda i, j: (j, layer, 0)),
            pl.BlockSpec((1, D_MODEL, _TF), lambda i, j: (j + nf, layer, 0)),
            pl.BlockSpec((2, _TF // 2, D_MODEL), lambda i, j: (j, layer, 0))]


def _ffn_fwd(h2, g_ffn, w_gu, w_d, layer):
    T = h2.shape[0]
    tm = min(_TMF, T)
    nf = D_FF // _TF

    def body(h_ref, g_ref, wg_ref, wu_ref, wd_ref, h3_ref, hn_ref, acc, xn_s):
        j = pl.program_id(1)

        @pl.when(j == 0)
        def _():
            x = h_ref[...]
            xn = ((x * _rstd(x)) * g_ref[...]).astype(_MXU)
            xn_s[...] = xn
            hn_ref[...] = xn
            acc[...] = jnp.zeros_like(acc)

        xn = xn_s[...]
        g = _dot(xn, wg_ref[0])
        u = _dot(xn, wu_ref[0])
        a = ((g * jax.nn.sigmoid(g)) * u).astype(_MXU)
        acc[...] += _dot(a, wd_ref[...].reshape(_TF, D_MODEL))

        @pl.when(j == nf - 1)
        def _():
            h3_ref[...] = h_ref[...] + acc[...]

    tok = pl.BlockSpec((tm, D_MODEL), lambda i, j: (i, 0))
    return pl.pallas_call(
        body, name="ffn_fwd", grid=(T // tm, nf),
        in_specs=[tok, pl.BlockSpec((1, D_MODEL), lambda i, j: (0, 0))] + _ffn_weight_specs(layer),
        out_specs=[tok, tok],
        out_shape=[_sds((T, D_MODEL), _F32), _sds((T, D_MODEL), _MXU)],
        scratch_shapes=[pltpu.VMEM((tm, D_MODEL), _F32), pltpu.VMEM((tm, D_MODEL), _MXU)],
    )(h2, g_ffn, w_gu, w_gu, w_d)


def _loss_grad(y, tgt):
    T = y.shape[0]
    tm = min(_TA, T)

    def body(y_ref, t_ref, dy_ref, loss_ref):
        @pl.when(pl.program_id(0) == 0)
        def _():
            loss_ref[...] = jnp.zeros_like(loss_ref)

        err = y_ref[...] - t_ref[...]
        dy_ref[...] = err * (1.0 / D_MODEL)
        part = jnp.sum(jnp.sum(err * err, axis=0, keepdims=True), axis=1, keepdims=True)
        loss_ref[...] += part * (0.5 / D_MODEL)

    return pl.pallas_call(
        body, name="loss_grad", grid=(T // tm,), in_specs=[_rows(tm, D_MODEL), _rows(tm, D_MODEL)],
        out_specs=[_rows(tm, D_MODEL), pl.BlockSpec((1, 1), lambda i: (0, 0))],
        out_shape=[_sds((T, D_MODEL), _F32), _sds((1, 1), _F32)],
    )(y, tgt)


def _pick_tile(n, candidates=(1408, 1024, 512, 256, 128)):
    for c in candidates:
        if n % c == 0:
            return c
    return n


def _matmul_tn(a, b, name, tk=None, tn=None, dst=None, dst_shape=None, dst_index=None, dst_chips=1):
    T, K = a.shape
    N = b.shape[1]
    tk, tn, tt = tk or _pick_tile(K), tn or _pick_tile(N), min(_TT, T)

    def body(*refs):
        a_ref, b_ref, o_ref = refs[0], refs[1], refs[-1]

        @pl.when(pl.program_id(2) == 0)
        def _():
            o_ref[...] = jnp.zeros_like(o_ref)

        o_ref[...] += _dot_tn(a_ref[...].astype(_MXU), b_ref[...].astype(_MXU)).reshape(o_ref.shape)

    in_specs = [pl.BlockSpec((tt, tk), lambda i, j, t: (t, i)), pl.BlockSpec((tt, tn), lambda i, j, t: (t, j))]
    if dst_shape is None:
        out_spec, out_shape, args, alias = pl.BlockSpec((tk, tn), lambda i, j, t: (i, j)), (K, N), (a, b), {}
    else:
        out_spec = pl.BlockSpec((dst_chips, tk // dst_chips, tn), lambda i, j, t: dst_index(i, j))
        out_shape, args, alias = dst_shape, (a, b), {}
        if dst is not None:
            in_specs, args, alias = in_specs + [_ANY], (a, b, dst), {2: 0}
    return pl.pallas_call(
        body, name=name, grid=(K // tk, N // tn, T // tt), in_specs=in_specs, out_specs=out_spec,
        out_shape=_sds(out_shape, _F32), input_output_aliases=alias,
    )(*args)


def _ffn_bwd(dh3, h2, hn, g_ffn, w_gu, w_d, layer):
    T = h2.shape[0]
    tm = min(_TMF, T)
    nf = D_FF // _TF

    def body(dh_ref, h_ref, hn_ref, g_ref, wg_ref, wu_ref, wd_ref, dh2_ref, act_ref, dg_ref, du_ref, dgain_ref,
             acc, dyb):
        i = pl.program_id(0)
        j = pl.program_id(1)

        @pl.when((i == 0) & (j == 0))
        def _():
            dgain_ref[...] = jnp.zeros_like(dgain_ref)

        @pl.when(j == 0)
        def _():
            dyb[...] = dh_ref[...].astype(_MXU)
            acc[...] = jnp.zeros_like(acc)

        xn = hn_ref[...]
        wg = wg_ref[0]
        wu = wu_ref[0]
        g = _dot(xn, wg)
        u = _dot(xn, wu)
        sg = jax.nn.sigmoid(g)
        sl = g * sg
        act_ref[...] = (sl * u).astype(_MXU)
        da = _dot_nt(dyb[...], wd_ref[...].reshape(_TF, D_MODEL))
        dgate = (da * u * (sg * (1.0 + g * (1.0 - sg)))).astype(_MXU)
        dup = (da * sl).astype(_MXU)
        dg_ref[...] = dgate
        du_ref[...] = dup
        acc[...] += _dot_nt(dgate, wg) + _dot_nt(dup, wu)

        @pl.when(j == nf - 1)
        def _():
            dhn = acc[...]
            dx, dgain = _norm_bwd(dhn, h_ref[...], g_ref[...])
            dh2_ref[...] = dh_ref[...] + dx
            dgain_ref[...] += dgain

    tok = pl.BlockSpec((tm, D_MODEL), lambda i, j: (i, 0))
    ffb = pl.BlockSpec((tm, _TF), lambda i, j: (i, j))
    row = pl.BlockSpec((1, D_MODEL), lambda i, j: (0, 0))
    return pl.pallas_call(
        body, name="ffn_bwd", grid=(T // tm, nf),
        in_specs=[tok, tok, tok, row] + _ffn_weight_specs(layer),
        out_specs=[tok, ffb, ffb, ffb, row],
        out_shape=[_sds((T, D_MODEL), _F32), _sds((T, D_FF), _MXU), _sds((T, D_FF), _MXU), _sds((T, D_FF), _MXU),
                   _sds((1, D_MODEL), _F32)],
        scratch_shapes=[pltpu.VMEM((tm, D_MODEL), _F32), pltpu.VMEM((tm, D_MODEL), _MXU)],
    )(dh3, h2, hn, g_ffn, w_gu, w_gu, w_d)


def _mem_attn_bwd(dh2, h1, g_q, w_q, gqm, bd128, mk, mv, w_mo):
    T = h1.shape[0]
    M = mk.shape[0]
    tm = min(_TM, T)
    nt = T // tm

    def body(dh_ref, h_ref, g_ref, wq_ref, gq_ref, bd_ref, mk_ref, mv_ref, wo_ref,
             dh1_ref, dmq_ref, dmk_ref, dmv_ref, dgq_ref, dgain_ref, dgq_acc):
        i = pl.program_id(0)

        @pl.when(i == 0)
        def _():
            dmk_ref[...] = jnp.zeros_like(dmk_ref)
            dmv_ref[...] = jnp.zeros_like(dmv_ref)
            dgain_ref[...] = jnp.zeros_like(dgain_ref)
            dgq_acc[...] = jnp.zeros_like(dgq_acc)

        x = h_ref[...]
        g = g_ref[...]
        bd = bd_ref[...]
        hn = ((x * _rstd(x)) * g).astype(_MXU)
        mq = _dot(hn, wq_ref[...])
        rq = lax.rsqrt(_group_sum(mq * mq, bd) * (1.0 / MEM_HEAD_DIM) + EPS)
        qn = ((mq * rq) * gq_ref[...]).astype(_MXU)
        dmo = _dot_nt(dh_ref[...].astype(_MXU), wo_ref[...])
        dqn = []
        for hd in range(MEM_HEADS):
            sl = slice(hd * MEM_HEAD_DIM, (hd + 1) * MEM_HEAD_DIM)
            p = _mem_softmax(qn, mk_ref, hd)
            dmo_h = dmo[:, sl].astype(_MXU)
            dp = _dot_nt(dmo_h, mv_ref[:, sl])
            ds = (p * (dp - jnp.sum(p * dp, axis=-1, keepdims=True)) * (MEM_HEAD_DIM ** -0.5)).astype(_MXU)
            dqn.append(_dot(ds, mk_ref[:, sl]))
            dmk_ref[:, sl] += _dot_tn(ds, qn[:, sl])
            dmv_ref[:, sl] += _dot_tn(p.astype(_MXU), dmo_h)
        dqn = jnp.concatenate(dqn, axis=-1)
        dmq, dgq = _headnorm_bwd(dqn, mq, gq_ref[...], bd, MEM_HEAD_DIM)
        dgq_acc[...] += dgq
        dmq = dmq.astype(_MXU)
        dmq_ref[...] = dmq
        dhn = _dot_nt(dmq, wq_ref[...])
        dx, dgain = _norm_bwd(dhn, x, g)
        dh1_ref[...] = dh_ref[...] + dx
        dgain_ref[...] += dgain

        @pl.when(i == nt - 1)
        def _():
            dgq_ref[...] = _fold_heads(dgq_acc[...], MEM_HEADS, MEM_HEAD_DIM)

    const2 = lambda shape: pl.BlockSpec(shape, lambda i: (0, 0))
    return pl.pallas_call(
        body, name="mem_attn_bwd", grid=(nt,),
        in_specs=[_rows(tm, D_MODEL), _rows(tm, D_MODEL), _resident((1, D_MODEL)), _resident(w_q.shape),
                  _resident((1, MEM_WIDTH)), _resident(bd128.shape), _resident(mk.shape), _resident(mv.shape),
                  _resident(w_mo.shape)],
        out_specs=[_rows(tm, D_MODEL), _rows(tm, MEM_WIDTH), const2((M, MEM_WIDTH)), const2((M, MEM_WIDTH)),
                   const2((1, MEM_HEAD_DIM)), const2((1, D_MODEL))],
        out_shape=[_sds((T, D_MODEL), _F32), _sds((T, MEM_WIDTH), _MXU), _sds((M, MEM_WIDTH), _F32),
                   _sds((M, MEM_WIDTH), _F32), _sds((1, MEM_HEAD_DIM), _F32), _sds((1, D_MODEL), _F32)],
        scratch_shapes=[pltpu.VMEM((1, MEM_WIDTH), _F32)],
    )(dh2, h1, g_q, w_q, gqm, bd128, mk, mv, w_mo)


def _mem_kv_bwd(dmk, dmv, mkv, mn, mem, g_kv, gkm, bd128, w_kv, dst, dst_shape, block):
    rows = D_MODEL // N_CHIPS

    def body(dmk_ref, dmv_ref, mkv_ref, mn_ref, mem_ref, g_ref, gk_ref, bd_ref, w_ref, *rest):
        dw_ref, dgk_ref, dgain_ref = rest[-3:]
        kraw = mkv_ref[:, :MEM_WIDTH]
        dk, dgk = _headnorm_bwd(dmk_ref[...], kraw, gk_ref[...], bd_ref[...], MEM_HEAD_DIM)
        dgk_ref[...] = _fold_heads(dgk, MEM_HEADS, MEM_HEAD_DIM)
        dmkv = jnp.concatenate([dk, dmv_ref[...]], axis=-1).astype(_MXU)
        dw_ref[...] = _dot_tn(mn_ref[...], dmkv).reshape(N_CHIPS, rows, 2 * MEM_WIDTH)
        dmn = _dot_nt(dmkv, w_ref[...])
        _, dgain = _norm_bwd(dmn, mem_ref[...], g_ref[...])
        dgain_ref[...] = dgain

    args = (dmk, dmv, mkv, mn, mem, g_kv, gkm, bd128, w_kv)
    whole = lambda a: pl.BlockSpec(a.shape, lambda i: (0,) * a.ndim)
    in_specs, alias = [whole(a) for a in args], {}
    if dst is not None:
        in_specs, args, alias = in_specs + [_ANY], args + (dst,), {len(args): 0}
    return pl.pallas_call(
        body, name="mem_kv_bwd", grid=(1,), in_specs=in_specs,
        out_specs=[pl.BlockSpec((N_CHIPS, rows, 2 * MEM_WIDTH), lambda i: (0, block, 0)),
                   pl.BlockSpec((1, MEM_HEAD_DIM), lambda i: (0, 0)), pl.BlockSpec((1, D_MODEL), lambda i: (0, 0))],
        out_shape=[_sds(dst_shape, _F32), _sds((1, MEM_HEAD_DIM), _F32), _sds((1, D_MODEL), _F32)],
        input_output_aliases=alias,
    )(*args)


def _out_proj_bwd(dh1, mixed, o, bd64, w_pool, pscale, w_out):
    T = dh1.shape[0]
    tm = min(_TM, T)
    hb = tm // HALO
    nt = T // tm

    def body(dh_ref, halo_ref, mx_ref, o_ref, bd_ref, wp_ref, ps_ref, wo_ref, doa_ref, dpin_ref, dwp_ref, dps_ref):
        i = pl.program_id(0)

        @pl.when(i == 0)
        def _():
            dwp_ref[...] = jnp.zeros_like(dwp_ref)
            dps_ref[...] = jnp.zeros_like(dps_ref)

        dcat = _dot_nt(dh_ref[...].astype(_MXU), wo_ref[...])
        do = dcat[:, :FOX_WIDTH].astype(_MXU)
        delta = _group_sum(do.astype(_F32) * o_ref[...].astype(_F32), bd_ref[...])
        for h in range(FOX_HEADS):
            sl = slice(h * FOX_HEAD_DIM, (h + 1) * FOX_HEAD_DIM)
            doa_ref[:, h * LANES:(h + 1) * LANES] = _aug_head(
                do[:, sl], _split3(-delta[:, h * FOX_HEAD_DIM:h * FOX_HEAD_DIM + 1]), _ZEROS3)
        dy = dcat[:, FOX_WIDTH:]
        dyh = _dot_nt(halo_ref[...].astype(_MXU), wo_ref[FOX_WIDTH:, :])
        dyh = jnp.where(i == nt - 1, 0.0, dyh)
        ps = ps_ref[...]
        t = (i * tm + lax.broadcasted_iota(jnp.int32, (tm + HALO, 1), 0) + 1).astype(_F32)
        mixed_t = mx_ref[...]
        dpin, dps = [], []
        for g, w in enumerate(POOL_WINDOWS):
            sl = slice(g * POOL_GROUP_DIM, (g + 1) * POOL_GROUP_DIM)
            mg = mixed_t[:, sl]
            wg = wp_ref[g]
            dps.append(jnp.sum(dy[:, sl] * _dot(mg, wg), axis=0, keepdims=True))
            dyl = (dy[:, sl] * ps[:, sl]).astype(_MXU)
            dylh = (dyh[:, sl] * ps[:, sl]).astype(_MXU)
            dwp_ref[g] += _dot_tn(mg, dyl)
            dmx = _dot_nt(dyl, wg)
            ext = jnp.concatenate([dmx, _dot_nt(dylh, wg)], axis=0) / jnp.minimum(t, float(w))
            dpin.append(_pool_window_sum(ext, w, True)[:tm, :] - dmx)
        dpin_ref[...] = jnp.concatenate(dpin, axis=-1)
        dps_ref[...] += jnp.concatenate(dps, axis=-1)

    return pl.pallas_call(
        body, name="out_proj_bwd", grid=(nt,),
        in_specs=[_rows(tm, D_MODEL),
                  pl.BlockSpec((HALO, D_MODEL), lambda i: (jnp.minimum((i + 1) * hb, T // HALO - 1), 0)),
                  _rows(tm, POOL_WIDTH), _rows(tm, FOX_WIDTH), _resident(bd64.shape), _resident(w_pool.shape),
                  _resident((1, POOL_WIDTH)), _resident(w_out.shape)],
        out_specs=[_rows(tm, FOX_HEADS * LANES), _rows(tm, POOL_WIDTH),
                   pl.BlockSpec(w_pool.shape, lambda i: (0, 0, 0)), pl.BlockSpec((1, POOL_WIDTH), lambda i: (0, 0))],
        out_shape=[_sds((T, FOX_HEADS * LANES), _MXU), _sds((T, POOL_WIDTH), _F32), _sds(w_pool.shape, _F32),
                   _sds((1, POOL_WIDTH), _F32)],
    )(dh1, dh1, mixed, o, bd64, w_pool, pscale, w_out)


def _fox_bwd(qb, ka, va, doa):
    T = qb.shape[0]
    tq = min(_TQ, T)
    nq = T // tq
    pair = 2 * FOX_HEAD_DIM
    ii, jj = _causal_steps(nq, False)
    row_lane, col_lane = FOX_HEAD_DIM, FOX_HEAD_DIM + 3

    def two_lanes(a, b):
        lane = lax.broadcasted_iota(jnp.int32, (tq, LANES), 1)
        return jnp.where(lane == 0, a, jnp.where(lane == 1, b, 0.0))

    def body(ii_ref, jj_ref, q_ref, k_ref, v_ref, do_ref, dq_ref, dk_ref, dv_ref, drs_ref, dcs_ref, dk_acc, dv_acc):
        t = pl.program_id(1)
        i = ii_ref[t]
        j = jj_ref[t]

        @pl.when(t == 0)
        def _():
            dq_ref[...] = jnp.zeros_like(dq_ref)
            drs_ref[...] = jnp.zeros_like(drs_ref)

        @pl.when(i == j)
        def _():
            dk_acc[...] = jnp.zeros_like(dk_acc)
            dv_acc[...] = jnp.zeros_like(dv_acc)

        def step(masked):
            dqs, rs = [], []
            for hh in range(2):
                sl = slice(LANES * hh, LANES * (hh + 1))
                p = jnp.exp(_fox_scores(q_ref, k_ref, hh, masked))
                dob = do_ref[:, sl]
                dv_acc[hh] += _dot_tn(p.astype(_MXU), dob)
                dsb = (p * _dot_nt(dob, v_ref[:, sl])).astype(_MXU)
                dk_acc[hh] += _dot_tn(dsb, q_ref[:, sl])
                dqa = _dot(dsb, k_ref[:, sl])
                dqs.append(dqa[:, :FOX_HEAD_DIM] * (FOX_HEAD_DIM ** -0.5))
                rs.append(dqa[:, row_lane:row_lane + 1])
            rows = pl.ds(pl.multiple_of(i * tq, tq), tq)
            dq_ref[rows, :] += jnp.concatenate(dqs, axis=-1)
            drs_ref[0, rows, :] += two_lanes(rs[0], rs[1])

        @pl.when(i > j)
        def _():
            step(False)

        @pl.when(i == j)
        def _():
            step(True)

        @pl.when(i == nq - 1)
        def _():
            dk_ref[...] = jnp.concatenate([dk_acc[0][:, :FOX_HEAD_DIM], dk_acc[1][:, :FOX_HEAD_DIM]], axis=-1)
            dv_ref[...] = jnp.concatenate([dv_acc[0][:, :FOX_HEAD_DIM], dv_acc[1][:, :FOX_HEAD_DIM]], axis=-1)
            dcs_ref[0] = two_lanes(dk_acc[0][:, col_lane:col_lane + 1], dk_acc[1][:, col_lane:col_lane + 1])

    qspec = pl.BlockSpec((tq, 2 * LANES), lambda p, t, ii, jj: (ii[t], p))
    kspec = pl.BlockSpec((tq, 2 * LANES), lambda p, t, ii, jj: (jj[t], p))
    kout = pl.BlockSpec((tq, pair), lambda p, t, ii, jj: (jj[t], p))
    return pl.pallas_call(
        body, name="fox_bwd",
        grid_spec=pltpu.PrefetchScalarGridSpec(
            num_scalar_prefetch=2, grid=(FOX_HEADS // 2, int(ii.shape[0])),
            in_specs=[qspec, kspec, kspec, qspec],
            out_specs=[pl.BlockSpec((T, pair), lambda p, t, ii, jj: (0, p)), kout, kout,
                       pl.BlockSpec((1, T, LANES), lambda p, t, ii, jj: (p, 0, 0)),
                       pl.BlockSpec((1, tq, LANES), lambda p, t, ii, jj: (p, jj[t], 0))],
            scratch_shapes=[pltpu.VMEM((2, tq, LANES), _F32), pltpu.VMEM((2, tq, LANES), _F32)]),
        out_shape=[_sds((T, FOX_WIDTH), _F32), _sds((T, FOX_WIDTH), _F32), _sds((T, FOX_WIDTH), _F32),
                   _sds((FOX_HEADS // 2, T, LANES), _F32), _sds((FOX_HEADS // 2, T, LANES), _F32)],
    )(ii, jj, qb, ka, va, doa)


def _mix_in_bwd(dh1, h, g_mix, zqk, dq, dk, dv, dpin, df, gq, gk, bd64, w_main, w_f):
    T = h.shape[0]
    tm = min(_TM, T)
    nt = T // tm

    def body(dh1_ref, h_ref, g_ref, zqk_ref, dq_ref, dk_ref, dv_ref, dpin_ref, df_ref, gq_ref, gk_ref, bd_ref,
             wm_ref, wf_ref, dh_ref, dz_ref, dzf_ref, dgq_ref, dgk_ref, dgain_ref, dbf_ref, dgq_acc, dgk_acc):
        i = pl.program_id(0)

        @pl.when(i == 0)
        def _():
            dgain_ref[...] = jnp.zeros_like(dgain_ref)
            dbf_ref[...] = jnp.zeros_like(dbf_ref)
            dgq_acc[...] = jnp.zeros_like(dgq_acc)
            dgk_acc[...] = jnp.zeros_like(dgk_acc)

        bd = bd_ref[...]
        dqr, dgq = _headnorm_bwd(dq_ref[...], zqk_ref[:, :FOX_WIDTH], gq_ref[...], bd, FOX_HEAD_DIM)
        dkr, dgk = _headnorm_bwd(dk_ref[...], zqk_ref[:, FOX_WIDTH:], gk_ref[...], bd, FOX_HEAD_DIM)
        dgq_acc[...] += dgq
        dgk_acc[...] += dgk
        dz = jnp.concatenate([dqr, dkr, dv_ref[...], dpin_ref[...]], axis=-1).astype(_MXU)
        dz_ref[...] = dz
        df = df_ref[...]
        dzf = df.astype(_MXU)
        dzf_ref[...] = dzf
        dbf_ref[...] += jnp.sum(df, axis=0, keepdims=True)
        dxn = _dot_nt(dz, wm_ref[...]) + _dot_nt(dzf, wf_ref[...])
        dx, dgain = _norm_bwd(dxn, h_ref[...], g_ref[...])
        dh_ref[...] = dh1_ref[...] + dx
        dgain_ref[...] += dgain

        @pl.when(i == nt - 1)
        def _():
            dgq_ref[...] = _fold_heads(dgq_acc[...], FOX_HEADS, FOX_HEAD_DIM)
            dgk_ref[...] = _fold_heads(dgk_acc[...], FOX_HEADS, FOX_HEAD_DIM)

    const2 = lambda shape: pl.BlockSpec(shape, lambda i: (0, 0))
    return pl.pallas_call(
        body, name="mix_in_bwd", grid=(nt,),
        in_specs=[_rows(tm, D_MODEL), _rows(tm, D_MODEL), _resident((1, D_MODEL)), _rows(tm, 2 * FOX_WIDTH),
                  _rows(tm, FOX_WIDTH), _rows(tm, FOX_WIDTH), _rows(tm, FOX_WIDTH), _rows(tm, POOL_WIDTH),
                  _rows(tm, LANES), _resident((1, FOX_WIDTH)), _resident((1, FOX_WIDTH)), _resident(bd64.shape),
                  _resident(w_main.shape), _resident(w_f.shape)],
        out_specs=[_rows(tm, D_MODEL), _rows(tm, 4 * FOX_WIDTH), _rows(tm, LANES), const2((1, FOX_HEAD_DIM)),
                   const2((1, FOX_HEAD_DIM)), const2((1, D_MODEL)), const2((1, LANES))],
        out_shape=[_sds((T, D_MODEL), _F32), _sds((T, 4 * FOX_WIDTH), _MXU), _sds((T, LANES), _MXU),
                   _sds((1, FOX_HEAD_DIM), _F32), _sds((1, FOX_HEAD_DIM), _F32), _sds((1, D_MODEL), _F32),
                   _sds((1, LANES), _F32)],
        scratch_shapes=[pltpu.VMEM((1, FOX_WIDTH), _F32), pltpu.VMEM((1, FOX_WIDTH), _F32)],
    )(dh1, h, g_mix, zqk, dq, dk, dv, dpin, df, gq, gk, bd64, w_main, w_f)


CLASSES = (("a", D_MODEL, ("w_out", "w_mem_kv")), ("d", D_MODEL, ("w_down",)), ("g", _TF, ("w_gate_up",)),
           ("i", 514, ("w_in",)), ("q", MEM_WIDTH, ("w_mem_q",)), ("o", 256, ("w_mem_out",)))
W_OUT_ROWS = D_MODEL // N_CHIPS
W_DOWN_ROWS = D_FF // N_CHIPS
W_MEM_OUT_COLS = D_MODEL // N_CHIPS


def _layer_params(WC, WS, l):
    rows = lambda buf, first, n: buf[:, first:first + n]
    w_in = rows(WC["i"], l * D_MODEL, D_MODEL).transpose(1, 0, 2).reshape(D_MODEL, -1)
    n_main = 3 * FOX_WIDTH + POOL_WIDTH
    row = lambda a: a.reshape(1, -1).astype(_F32)
    return dict(
        layer=l,
        g_mix=row(WS["g_mix"][l]),
        w_main=w_in[:, :n_main],
        w_f=jnp.pad(w_in[:, n_main:], ((0, 0), (0, LANES - FOX_HEADS))),
        b_f=jnp.pad(row(WS["b_forget"][l]), ((0, 0), (0, LANES - FOX_HEADS))),
        gq=jnp.tile(row(WS["g_q_fox"][l]), (1, FOX_HEADS)),
        gk=jnp.tile(row(WS["g_k_fox"][l]), (1, FOX_HEADS)),
        w_pool=WS["w_pool"][l].astype(_MXU),
        pscale=row(WS["pool_scale"][l]),
        w_out=rows(WC["a"], l * W_OUT_ROWS, W_OUT_ROWS).reshape(D_MODEL, D_MODEL),
        g_mem_q=row(WS["g_mem_q"][l]),
        g_mem_kv=row(WS["g_mem_kv"][l]),
        w_mem_q=rows(WC["q"], l * W_OUT_ROWS, W_OUT_ROWS).reshape(D_MODEL, MEM_WIDTH),
        w_mem_kv=rows(WC["a"], (DEPTH + l) * W_OUT_ROWS, W_OUT_ROWS).reshape(D_MODEL, 2 * MEM_WIDTH),
        gqm=jnp.tile(row(WS["g_q_mem"][l]), (1, MEM_HEADS)),
        gkm=jnp.tile(row(WS["g_k_mem"][l]), (1, MEM_HEADS)),
        w_mem_out=rows(WC["o"], l * MEM_WIDTH, MEM_WIDTH).transpose(1, 0, 2).reshape(MEM_WIDTH, D_MODEL),
        g_ffn=row(WS["g_ffn"][l]),
        w_gu=WC["g"],
        w_d=WC["d"],
    )


def _per_head_lanes(sums):
    T = sums.shape[1]
    return jnp.pad(sums[:, :, :2].transpose(1, 0, 2).reshape(T, FOX_HEADS), ((0, 0), (0, LANES - FOX_HEADS)))


def _layer_fwd(h, mem, P, bd64, bd128):
    s = dict(h=h)
    s["xn"], s["zqk"], qn, kn, v, pin, s["fl"] = _mix_in_fwd(
        h, P["g_mix"], P["w_main"], P["w_f"], P["b_f"], P["gq"], P["gk"], bd64)
    qa, s["ka"], s["va"] = _gate_fwd(s["fl"], qn, kn, v)
    s["o"], s["qb"] = _fox_fwd(qa, s["ka"], s["va"])
    s["h1"], s["mixed"], s["y"] = _out_proj_fwd(h, s["o"], pin, P["w_pool"], P["pscale"], P["w_out"])
    s["mn"], s["mkv"], s["mk"], s["mv"] = _mem_kv_fwd(mem, P["g_mem_kv"], P["w_mem_kv"], P["gkm"], bd128)
    s["h2"], s["hn_mem"], s["mo"] = _mem_attn_fwd(s["h1"], P["g_mem_q"], P["w_mem_q"], P["gqm"], bd128, s["mk"],
                                                  s["mv"], P["w_mem_out"])
    h3, s["hn_ffn"] = _ffn_fwd(s["h2"], P["g_ffn"], P["w_gu"], P["w_d"], P["layer"])
    return h3, s


def _layer_bwd(dh3, mem, P, s, bd64, bd128, G, shapes):
    l = P["layer"]
    g = {}

    def into(c, a, b, name, tk, tn, index, chips=1):
        G[c] = _matmul_tn(a, b, name, tk=tk, tn=tn, dst=G.get(c), dst_shape=shapes[c], dst_index=index,
                          dst_chips=chips)

    dh2, act, dgate, dup, g["g_ffn"] = _ffn_bwd(dh3, s["h2"], s["hn_ffn"], P["g_ffn"], P["w_gu"], P["w_d"], l)
    into("d", act, dh3, "dw_down", 2 * W_DOWN_ROWS, D_MODEL, lambda i, j: (i, l, 0), chips=2)
    into("g", s["hn_ffn"], dgate, "dw_gate", D_MODEL, _TF, lambda i, j: (j, l, 0))
    into("g", s["hn_ffn"], dup, "dw_up", D_MODEL, _TF, lambda i, j: (j + D_FF // _TF, l, 0))

    dh1, dmq, dmk, dmv, g["g_q_mem"], g["g_mem_q"] = _mem_attn_bwd(
        dh2, s["h1"], P["g_mem_q"], P["w_mem_q"], P["gqm"], bd128, s["mk"], s["mv"], P["w_mem_out"])
    into("o", s["mo"], dh2, "dw_mem_out", MEM_WIDTH, W_MEM_OUT_COLS, lambda i, j: (j, l, 0))
    into("q", s["hn_mem"], dmq, "dw_mem_q", W_OUT_ROWS, MEM_WIDTH, lambda i, j: (i, l, 0))
    G["a"], g["g_k_mem"], g["g_mem_kv"] = _mem_kv_bwd(dmk, dmv, s["mkv"], s["mn"], mem, P["g_mem_kv"], P["gkm"],
                                                      bd128, P["w_mem_kv"], G.get("a"), shapes["a"], DEPTH + l)

    doa, dpin, g["w_pool"], g["pool_scale"] = _out_proj_bwd(dh1, s["mixed"], s["o"], bd64, P["w_pool"], P["pscale"],
                                                            P["w_out"])
    half = FOX_WIDTH // W_OUT_ROWS
    into("a", s["o"], dh1, "dw_out_fox", W_OUT_ROWS, D_MODEL, lambda i, j: (i, l, 0))
    into("a", s["y"], dh1, "dw_out_pool", W_OUT_ROWS, D_MODEL, lambda i, j: (i + half, l, 0))
    dq, dk, dv, drs, dcs = _fox_bwd(s["qb"], s["ka"], s["va"], doa)
    df = _gate_bwd(_per_head_lanes(drs), _per_head_lanes(dcs), s["fl"])
    dh, dz, dzf, g["g_q_fox"], g["g_k_fox"], g["g_mix"], dbf = _mix_in_bwd(
        dh1, s["h"], P["g_mix"], s["zqk"], dq, dk, dv, dpin, df, P["gq"], P["gk"], bd64, P["w_main"], P["w_f"])
    g["b_forget"] = dbf[:, :FOX_HEADS]
    dw_in = jnp.concatenate([_matmul_tn(s["xn"], dz, "dw_in_main"),
                             _matmul_tn(s["xn"], dzf, "dw_in_gate")[:, :FOX_HEADS]], axis=1)
    return dh, G, g, _to_shards(dw_in, 1)


def _device_grads(x, mem, tgt, WC, WS):
    bd64 = _blockdiag_ones(FOX_HEADS, FOX_HEAD_DIM)
    bd128 = _blockdiag_ones(MEM_HEADS, MEM_HEAD_DIM)
    params = [_layer_params(WC, WS, l) for l in range(DEPTH)]
    shapes = {c: WC[c].shape for c in WC}
    h, saved = x, []
    for l in range(DEPTH):
        h, s = _layer_fwd(h, mem, params[l], bd64, bd128)
        saved.append(s)
    dh, loss = _loss_grad(h, tgt)
    G, small, dw_in = {}, [None] * DEPTH, [None] * DEPTH
    for l in reversed(range(DEPTH)):
        dh, G, small[l], dw_in[l] = _layer_bwd(dh, mem, params[l], saved[l], bd64, bd128, G, shapes)
    G["i"] = jnp.concatenate(dw_in, axis=1)
    gsmall = {n: jnp.stack([small[l][n].reshape(WS[n].shape[1:]) for l in range(DEPTH)]) for n in SMALL}
    return loss, dh, G, gsmall


def _class_slabs(shards):
    return {c: jnp.concatenate([shards[n].reshape(-1, width) for n in names]) for c, width, names in CLASSES}


def _class_rows(shards):
    where = {}
    for c, width, names in CLASSES:
        off = 0
        for n in names:
            rows = shards[n].shape[0] * shards[n].shape[1]
            where[n] = (c, off, rows)
            off += rows
    return where


def _to_shards(full, ax):
    shp = full.shape
    parts = full.reshape(shp[:ax] + (N_CHIPS, shp[ax] // N_CHIPS) + shp[ax + 1:])
    return jnp.moveaxis(parts, ax, 0)


def _from_shards(parts, ax):
    full = jnp.moveaxis(parts, 0, ax)
    shp = full.shape
    return full.reshape(shp[:ax] + (shp[ax] * shp[ax + 1],) + shp[ax + 2:])


def _pack_small(arrs):
    flat = jnp.concatenate([a.reshape(-1).astype(_F32) for a in arrs])
    rows = -(-flat.shape[0] // (8 * D_MODEL)) * 8
    return jnp.pad(flat, (0, rows * D_MODEL - flat.shape[0])).reshape(rows, D_MODEL)


def _unpack_small(flat, shapes):
    flat = flat.reshape(-1)
    out, off = [], 0
    for shp in shapes:
        n = 1
        for d in shp:
            n *= d
        out.append(flat[off:off + n].reshape(shp))
        off += n
    return out


def _mesh_pos():
    return lax.axis_index("x"), lax.axis_index("y"), lax.axis_index("c")


def _other_chips(x, y):
    return [(1 - x, y), (x, 1 - y), (1 - x, 1 - y)]


def _half_rows(ref_rows, which):
    half = ref_rows // 2
    return pl.ds(pl.multiple_of(which * half, 16), half)


def _remote(src_ref, dst_ref, send_sems, recv_sems, k, to):
    return pltpu.make_async_remote_copy(src_ref=src_ref, dst_ref=dst_ref, send_sem=send_sems.at[k],
                                        recv_sem=recv_sems.at[k], device_id=to, device_id_type=MESH)


def _allgather_weights(slabs):
    n = len(slabs)

    def body(*refs):
        srcs, outs, send_sems, recv_sems = refs[:n], refs[n:2 * n], refs[2 * n], refs[2 * n + 1]
        x, y, c = _mesh_pos()
        me = 2 * x + y
        sibling = (x, y, 1 - c)
        chips = _other_chips(x, y)
        first, passed = [], []
        for a, (src, out) in enumerate(zip(srcs, outs)):
            mine = _half_rows(src.shape[0], c)
            for j, chip in enumerate(chips):
                cp = _remote(src.at[mine], out.at[me, mine], send_sems, recv_sems, 6 * a + j, (*chip, c))
                cp.start()
                first.append(cp)
        for a, out in enumerate(outs):
            mine = _half_rows(out.shape[1], c)
            for j, (cx, cy) in enumerate(chips):
                slab = out.at[2 * cx + cy, mine]
                _remote(slab, slab, send_sems, recv_sems, 6 * a + j, (cx, cy, c)).wait_recv()
                fwd = _remote(slab, slab, send_sems, recv_sems, 6 * a + 3 + j, sibling)
                fwd.start()
                passed.append(fwd)
        for a, out in enumerate(outs):
            theirs = _half_rows(out.shape[1], 1 - c)
            for j, (cx, cy) in enumerate(chips):
                slab = out.at[2 * cx + cy, theirs]
                _remote(slab, slab, send_sems, recv_sems, 6 * a + 3 + j, sibling).wait_recv()
        for cp in first + passed:
            cp.wait_send()

    return pl.pallas_call(
        body, name="allgather_weights", in_specs=[_ANY] * n, out_specs=[_ANY] * n,
        out_shape=[_sds((N_CHIPS,) + s.shape, s.dtype) for s in slabs],
        scratch_shapes=[pltpu.SemaphoreType.DMA((6 * n,)), pltpu.SemaphoreType.DMA((6 * n,))],
    )(*slabs)


def _exchange_halves(grads):
    n = len(grads)

    def body(*refs):
        srcs, gots, send_sems, recv_sems = refs[:n], refs[n:2 * n], refs[2 * n], refs[2 * n + 1]
        x, y, c = _mesh_pos()
        copies = []
        for a, (src, got) in enumerate(zip(srcs, gots)):
            theirs = _half_rows(src.shape[1], 1 - c)
            for k in range(N_CHIPS):
                cp = _remote(src.at[k, theirs], got.at[k], send_sems, recv_sems, N_CHIPS * a + k, (x, y, 1 - c))
                cp.start()
                copies.append(cp)
        for cp in copies:
            cp.wait()

    return pl.pallas_call(
        body, name="grad_exchange_halves", in_specs=[_ANY] * n, out_specs=[_ANY] * n,
        out_shape=[_sds((N_CHIPS, g.shape[1] // 2, g.shape[2]), _F32) for g in grads],
        scratch_shapes=[pltpu.SemaphoreType.DMA((N_CHIPS * n,)), pltpu.SemaphoreType.DMA((N_CHIPS * n,))],
    )(*grads)


def _row_tile(rows):
    return _pick_tile(rows, (704, 512, 256))


def _add_halves(g, got, c_idx, name):
    _, half, width = got.shape
    ta = _row_tile(half)
    nb = half // ta

    def body(c_ref, a_ref, b_ref, o_ref):
        o_ref[...] = (a_ref[...] + b_ref[...]).astype(jnp.bfloat16)

    return pl.pallas_call(
        body, name=name,
        grid_spec=pltpu.PrefetchScalarGridSpec(
            num_scalar_prefetch=1, grid=(N_CHIPS, nb),
            in_specs=[pl.BlockSpec((1, ta, width), lambda k, i, c: (k, c[0] * nb + i, 0)),
                      pl.BlockSpec((1, ta, width), lambda k, i, c: (k, i, 0))],
            out_specs=pl.BlockSpec((1, ta, width), lambda k, i, c: (k, i, 0))),
        out_shape=_sds(got.shape, jnp.bfloat16),
    )(c_idx, g, got)


def _scatter_to_chips(parts):
    n = len(parts)

    def body(*refs):
        srcs, gots, send_sems, recv_sems = refs[:n], refs[n:2 * n], refs[2 * n], refs[2 * n + 1]
        x, y, c = _mesh_pos()
        me = 2 * x + y
        chips = _other_chips(x, y)
        copies = []
        for a, (src, got) in enumerate(zip(srcs, gots)):
            for j, (cx, cy) in enumerate(chips):
                cp = _remote(src.at[2 * cx + cy], got.at[me], send_sems, recv_sems, 3 * a + j, (cx, cy, c))
                cp.start()
                copies.append(cp)
        for a, got in enumerate(gots):
            for j, (cx, cy) in enumerate(chips):
                slab = got.at[2 * cx + cy]
                _remote(slab, slab, send_sems, recv_sems, 3 * a + j, (cx, cy, c)).wait_recv()
        for cp in copies:
            cp.wait_send()

    return pl.pallas_call(
        body, name="grad_scatter_chips", in_specs=[_ANY] * n, out_specs=[_ANY] * n,
        out_shape=[_sds(p.shape, p.dtype) for p in parts],
        scratch_shapes=[pltpu.SemaphoreType.DMA((3 * n,)), pltpu.SemaphoreType.DMA((3 * n,))],
    )(*parts)


def _sum_chips(got, c_idx, name):
    _, half, width = got.shape
    ta = _row_tile(half)
    nb = half // ta

    def body(c_ref, a_ref, o_ref):
        f = lambda k: a_ref[k].astype(_F32)
        o_ref[...] = ((f(0) + f(1)) + f(2)) + f(3)

    return pl.pallas_call(
        body, name=name,
        grid_spec=pltpu.PrefetchScalarGridSpec(
            num_scalar_prefetch=1, grid=(nb,),
            in_specs=[pl.BlockSpec((N_CHIPS, ta, width), lambda i, c: (0, i, 0))],
            out_specs=pl.BlockSpec((ta, width), lambda i, c: (c[0] * nb + i, 0))),
        out_shape=_sds((2 * half, width), _F32),
    )(c_idx, got)


def _share_with_sibling(bufs):
    n = len(bufs)

    def body(*refs):
        outs, send_sems, recv_sems = refs[n:2 * n], refs[2 * n], refs[2 * n + 1]
        x, y, c = _mesh_pos()
        copies = []
        for a, out in enumerate(outs):
            mine = out.at[_half_rows(out.shape[0], c)]
            cp = _remote(mine, mine, send_sems, recv_sems, a, (x, y, 1 - c))
            cp.start()
            copies.append(cp)
        for a, out in enumerate(outs):
            theirs = out.at[_half_rows(out.shape[0], 1 - c)]
            _remote(theirs, theirs, send_sems, recv_sems, a, (x, y, 1 - c)).wait_recv()
        for cp in copies:
            cp.wait_send()

    return pl.pallas_call(
        body, name="grad_share_sibling", in_specs=[_ANY] * n, out_specs=[_ANY] * n,
        out_shape=[_sds(b.shape, _F32) for b in bufs], input_output_aliases={a: a for a in range(n)},
        scratch_shapes=[pltpu.SemaphoreType.DMA((n,)), pltpu.SemaphoreType.DMA((n,))],
    )(*bufs)


def _allreduce_small(g):
    rows = g.shape[0]
    n_dev = 2 * N_CHIPS

    def body(g_ref, out_ref, gathered, local_sem, send_sems, recv_sems):
        x, y, c = _mesh_pos()
        me = 4 * x + 2 * y + c
        own = pltpu.make_async_copy(g_ref, gathered.at[me], local_sem)
        own.start()
        copies = []
        for k in range(1, n_dev):
            fx, fy, fc = (k >> 2) & 1, (k >> 1) & 1, k & 1
            cp = pltpu.make_async_remote_copy(
                src_ref=g_ref, dst_ref=gathered.at[me], send_sem=send_sems.at[k - 1], recv_sem=recv_sems.at[k - 1],
                device_id=(x ^ fx, y ^ fy, c ^ fc), device_id_type=MESH)
            cp.start()
            copies.append(cp)
        for k in range(1, n_dev):
            fx, fy, fc = (k >> 2) & 1, (k >> 1) & 1, k & 1
            px, py, pc = x ^ fx, y ^ fy, c ^ fc
            slab = gathered.at[4 * px + 2 * py + pc]
            pltpu.make_async_remote_copy(src_ref=slab, dst_ref=slab, send_sem=send_sems.at[k - 1],
                                         recv_sem=recv_sems.at[k - 1], device_id=(px, py, pc),
                                         device_id_type=MESH).wait_recv()
        for cp in copies:
            cp.wait_send()
        own.wait()
        acc = gathered[0]
        for d in range(1, n_dev):
            acc = acc + gathered[d]
        out_ref[...] = acc

    vmem = pl.BlockSpec(memory_space=pltpu.VMEM)
    return pl.pallas_call(
        body, name="allreduce_small", in_specs=[vmem], out_specs=vmem, out_shape=_sds((rows, D_MODEL), _F32),
        scratch_shapes=[pltpu.VMEM((n_dev, rows, D_MODEL), _F32), pltpu.SemaphoreType.DMA,
                        pltpu.SemaphoreType.DMA((n_dev - 1,)), pltpu.SemaphoreType.DMA((n_dev - 1,))],
    )(g)


def _adamw(w, g, m, v, name, g_first_row=0):
    shape = w.shape
    cols = shape[-1]
    rows = 1
    for d in shape[:-1]:
        rows *= d
    w2, m2, v2 = (a.reshape(rows, cols) for a in (w, m, v))
    tr = _pick_tile(rows, (256, 128, 64, 32, 16, 8))
    g0 = g_first_row // tr

    def body(w_ref, g_ref, m_ref, v_ref, go_ref, d_ref, nm_ref, nv_ref):
        gg = g_ref[...]
        go_ref[...] = gg
        nm = ADAM_B1 * m_ref[...] + (1.0 - ADAM_B1) * gg
        nv = ADAM_B2 * v_ref[...] + (1.0 - ADAM_B2) * (gg * gg)
        m_hat = nm / (1.0 - ADAM_B1 ** ADAM_STEP)
        v_hat = nv / (1.0 - ADAM_B2 ** ADAM_STEP)
        d_ref[...] = -ADAM_LR * (m_hat / (jnp.sqrt(v_hat) + ADAM_EPS) + ADAM_WD * w_ref[...])
        nm_ref[...] = nm
        nv_ref[...] = nv

    spec = pl.BlockSpec((tr, cols), lambda i: (i, 0))
    outs = pl.pallas_call(
        body, name=name, grid=(rows // tr,),
        in_specs=[spec, pl.BlockSpec((tr, cols), lambda i: (g0 + i, 0)), spec, spec], out_specs=[spec] * 4,
        out_shape=[_sds((rows, cols), _F32)] * 4,
    )(w2, g, m2, v2)
    return tuple(o.reshape(shape) for o in outs)


def kernel(x, mem, g_mix, w_in, b_forget, g_q_fox, g_k_fox, w_pool, pool_scale, w_out, g_mem_q, g_mem_kv, w_mem_q, w_mem_kv, g_q_mem, g_k_mem, w_mem_out, g_ffn, w_gate_up, w_down, loss_target, m_g_mix, m_w_in, m_b_forget, m_g_q_fox, m_g_k_fox, m_w_pool, m_pool_scale, m_w_out, m_g_mem_q, m_g_mem_kv, m_w_mem_q, m_w_mem_kv, m_g_q_mem, m_g_k_mem, m_w_mem_out, m_g_ffn, m_w_gate_up, m_w_down, v_g_mix, v_w_in, v_b_forget, v_g_q_fox, v_g_k_fox, v_w_pool, v_pool_scale, v_w_out, v_g_mem_q, v_g_mem_kv, v_w_mem_q, v_w_mem_kv, v_g_q_mem, v_g_k_mem, v_w_mem_out, v_g_ffn, v_w_gate_up, v_w_down):
    w = dict(g_mix=g_mix, w_in=w_in, b_forget=b_forget, g_q_fox=g_q_fox, g_k_fox=g_k_fox, w_pool=w_pool,
             pool_scale=pool_scale, w_out=w_out, g_mem_q=g_mem_q, g_mem_kv=g_mem_kv, w_mem_q=w_mem_q,
             w_mem_kv=w_mem_kv, g_q_mem=g_q_mem, g_k_mem=g_k_mem, w_mem_out=w_mem_out, g_ffn=g_ffn,
             w_gate_up=w_gate_up, w_down=w_down)
    m = dict(g_mix=m_g_mix, w_in=m_w_in, b_forget=m_b_forget, g_q_fox=m_g_q_fox, g_k_fox=m_g_k_fox, w_pool=m_w_pool,
             pool_scale=m_pool_scale, w_out=m_w_out, g_mem_q=m_g_mem_q, g_mem_kv=m_g_mem_kv, w_mem_q=m_w_mem_q,
             w_mem_kv=m_w_mem_kv, g_q_mem=m_g_q_mem, g_k_mem=m_g_k_mem, w_mem_out=m_w_mem_out, g_ffn=m_g_ffn,
             w_gate_up=m_w_gate_up, w_down=m_w_down)
    v = dict(g_mix=v_g_mix, w_in=v_w_in, b_forget=v_b_forget, g_q_fox=v_g_q_fox, g_k_fox=v_g_k_fox, w_pool=v_w_pool,
             pool_scale=v_pool_scale, w_out=v_w_out, g_mem_q=v_g_mem_q, g_mem_kv=v_g_mem_kv, w_mem_q=v_w_mem_q,
             w_mem_kv=v_w_mem_kv, g_q_mem=v_g_q_mem, g_k_mem=v_g_k_mem, w_mem_out=v_w_mem_out, g_ffn=v_g_ffn,
             w_gate_up=v_w_gate_up, w_down=v_w_down)

    classes = [c for c, _, _ in CLASSES]
    chip = 2 * lax.axis_index("x") + lax.axis_index("y")
    c_idx = lax.axis_index("c").astype(jnp.int32).reshape(1)
    own_slab = lambda bufs, own: lax.dynamic_update_slice(bufs, own[None], (chip, 0, 0))

    slabs = _class_slabs({n: w[n].astype(_MXU) for n in BIG})
    gathered = _allgather_weights([slabs[c] for c in classes])
    WC = {c: own_slab(buf, slabs[c]) for c, buf in zip(classes, gathered)}

    loss, grad_x, G, gsmall = _device_grads(x[0], mem[0], loss_target[0], WC, {n: w[n] for n in SMALL})
    loss = lax.psum(loss[0, 0], ("x", "y", "c"))

    grads = [G[c] for c in classes]
    partial = [_add_halves(g, got, c_idx, "grad_add_halves_" + c)
               for c, g, got in zip(classes, grads, _exchange_halves(grads))]
    landed = _scatter_to_chips(partial)
    partials = [own_slab(got, lax.dynamic_index_in_dim(p, chip, 0, keepdims=False)) for got, p in zip(landed, partial)]
    reduced = _share_with_sibling([_sum_chips(p, c_idx, "grad_sum_chips_" + c) for c, p in zip(classes, partials)])
    reduced = dict(zip(classes, reduced))

    small_shapes = [w[n].shape for n in SMALL]
    gsmall = _unpack_small(_allreduce_small(_pack_small([gsmall[n] for n in SMALL])), small_shapes)

    g_out, d_out, m_out, v_out = {}, {}, {}, {}
    for n, (c, first, _) in _class_rows({n: w[n] for n in BIG}).items():
        g_out[n], d_out[n], m_out[n], v_out[n] = _adamw(w[n], reduced[c], m[n], v[n], "adamw_" + n, first)
    packed = [_pack_small([t[n] for n in SMALL]) for t in (w, m, v)]
    _, ds, ms, vs = _adamw(packed[0], _pack_small(gsmall), packed[1], packed[2], "adamw_small")
    for n, gi, di, mi, vi in zip(SMALL, gsmall, _unpack_small(ds, small_shapes), _unpack_small(ms, small_shapes),
                                 _unpack_small(vs, small_shapes)):
        g_out[n], d_out[n], m_out[n], v_out[n] = gi, di, mi, vi

    return (loss, grad_x[None], *[g_out[n] for n in WEIGHTS], *[d_out[n] for n in WEIGHTS],
            *[m_out[n] for n in WEIGHTS], *[v_out[n] for n in WEIGHTS])
```

```python
import functools

import jax
import jax.numpy as jnp
import numpy as np
from jax import lax
from jax.experimental import pallas as pl
from jax.experimental.pallas import tpu as pltpu

_F32 = jnp.float32
_MXU = jnp.bfloat16

D_MODEL = 1024
DEPTH = 2
FOX_HEADS = 8
FOX_HEAD_DIM = 64
FOX_WIDTH = FOX_HEADS * FOX_HEAD_DIM
POOL_WINDOWS = (2, 4, 8, 16)
POOL_GROUP_DIM = 128
POOL_WIDTH = len(POOL_WINDOWS) * POOL_GROUP_DIM
MEM_HEADS = 4
MEM_HEAD_DIM = 128
MEM_WIDTH = MEM_HEADS * MEM_HEAD_DIM
D_FF = 2816
EPS = 1e-6
LANES = 128
HALO = 16

ADAM_LR = 0.001
ADAM_B1 = 0.9
ADAM_B2 = 0.999
ADAM_EPS = 1e-08
ADAM_WD = 0.01
ADAM_STEP = 10

N_CHIPS = 4
MESH = pl.DeviceIdType.MESH

_TM = 512
_TQ = 1024
_TMF = 256
_TF = 1408
_TT = 1024
_TB = 256
_TA = 512

BIG = ("w_in", "w_out", "w_mem_q", "w_mem_kv", "w_mem_out", "w_gate_up", "w_down")
SHARD_AXIS = {"w_in": 2, "w_out": 1, "w_mem_q": 1, "w_mem_kv": 1, "w_mem_out": 2, "w_gate_up": 2, "w_down": 1}
SMALL = ("g_mix", "b_forget", "g_q_fox", "g_k_fox", "w_pool", "pool_scale", "g_mem_q", "g_mem_kv", "g_q_mem",
         "g_k_mem", "g_ffn")
WEIGHTS = ("g_mix", "w_in", "b_forget", "g_q_fox", "g_k_fox", "w_pool", "pool_scale", "w_out", "g_mem_q", "g_mem_kv",
           "w_mem_q", "w_mem_kv", "g_q_mem", "g_k_mem", "w_mem_out", "g_ffn", "w_gate_up", "w_down")


def _dot(a, b):
    return jnp.dot(a, b, preferred_element_type=_F32)


def _dot_nt(a, b):
    return lax.dot_general(a, b, (((1,), (1,)), ((), ())), preferred_element_type=_F32)


def _dot_tn(a, b):
    return lax.dot_general(a, b, (((0,), (0,)), ((), ())), preferred_element_type=_F32)


def _group_sum(x, ones_blockdiag):
    hi = x.astype(_MXU)
    lo = (x - hi.astype(_F32)).astype(_MXU)
    return _dot(hi, ones_blockdiag) + _dot(lo, ones_blockdiag)


def _tri_dot(tri, x):
    h1 = x.astype(jnp.bfloat16)
    r1 = x - h1.astype(_F32)
    h2 = r1.astype(jnp.bfloat16)
    h3 = (r1 - h2.astype(_F32)).astype(jnp.bfloat16)
    return _dot(tri, h1) + _dot(tri, h2) + _dot(tri, h3)


def _rstd(x):
    return lax.rsqrt(jnp.mean(x * x, axis=-1, keepdims=True) + EPS)


def _norm_bwd(dy, x, g):
    r = _rstd(x)
    xhat = x * r
    u = dy * g
    dx = r * (u - xhat * jnp.mean(u * xhat, axis=-1, keepdims=True))
    return dx, jnp.sum(dy * xhat, axis=0, keepdims=True)


def _headnorm_bwd(dy, x, g, ones_blockdiag, width):
    r = lax.rsqrt(_group_sum(x * x, ones_blockdiag) * (1.0 / width) + EPS)
    xhat = x * r
    u = dy * g
    dx = r * (u - xhat * (_group_sum(u * xhat, ones_blockdiag) * (1.0 / width)))
    return dx, jnp.sum(dy * xhat, axis=0, keepdims=True)


def _fold_heads(row, heads, width):
    acc = row[:, 0:width]
    for h in range(1, heads):
        acc = acc + row[:, h * width:(h + 1) * width]
    return acc


_ANY = pl.BlockSpec(memory_space=pl.ANY)


def _resident(shape):
    return pl.BlockSpec(shape, lambda *_: (0,) * len(shape), pipeline_mode=pl.Buffered(1))


def _rows(tm, width):
    return pl.BlockSpec((tm, width), lambda i: (i, 0))


def _blockdiag_ones(groups, width):
    return jnp.kron(jnp.eye(groups, dtype=_F32), jnp.ones((width, width), _F32)).astype(_MXU)


def _sds(shape, dtype):
    return jax.ShapeDtypeStruct(shape, dtype)


def _mix_in_fwd(h, g_mix, w_main, w_f, b_f, gq, gk, bd64):
    T = h.shape[0]
    tm = min(_TM, T)

    def body(h_ref, g_ref, wm_ref, wf_ref, bf_ref, gq_ref, gk_ref, bd_ref,
             xn_ref, zqk_ref, qn_ref, kn_ref, v_ref, pin_ref, fl_ref):
        x = h_ref[...]
        xn = ((x * _rstd(x)) * g_ref[...]).astype(_MXU)
        xn_ref[...] = xn
        z = _dot(xn, wm_ref[...])
        q = z[:, :FOX_WIDTH]
        k = z[:, FOX_WIDTH:2 * FOX_WIDTH]
        zqk_ref[...] = z[:, :2 * FOX_WIDTH]
        bd = bd_ref[...]
        rq = lax.rsqrt(_group_sum(q * q, bd) * (1.0 / FOX_HEAD_DIM) + EPS)
        rk = lax.rsqrt(_group_sum(k * k, bd) * (1.0 / FOX_HEAD_DIM) + EPS)
        qn_ref[...] = ((q * rq) * gq_ref[...]).astype(_MXU)
        kn_ref[...] = ((k * rk) * gk_ref[...]).astype(_MXU)
        v_ref[...] = z[:, 2 * FOX_WIDTH:3 * FOX_WIDTH].astype(_MXU)
        pin_ref[...] = z[:, 3 * FOX_WIDTH:]
        fl_ref[...] = _dot(xn, wf_ref[...]) + bf_ref[...]

    return pl.pallas_call(
        body, name="mix_in_fwd", grid=(T // tm,),
        in_specs=[_rows(tm, D_MODEL), _resident((1, D_MODEL)), _resident(w_main.shape), _resident(w_f.shape),
                  _resident((1, LANES)), _resident((1, FOX_WIDTH)), _resident((1, FOX_WIDTH)), _resident(bd64.shape)],
        out_specs=[_rows(tm, D_MODEL), _rows(tm, 2 * FOX_WIDTH), _rows(tm, FOX_WIDTH), _rows(tm, FOX_WIDTH),
                   _rows(tm, FOX_WIDTH), _rows(tm, POOL_WIDTH), _rows(tm, LANES)],
        out_shape=[_sds((T, D_MODEL), _MXU), _sds((T, 2 * FOX_WIDTH), _F32), _sds((T, FOX_WIDTH), _MXU),
                   _sds((T, FOX_WIDTH), _MXU), _sds((T, FOX_WIDTH), _MXU), _sds((T, POOL_WIDTH), _F32),
                   _sds((T, LANES), _F32)],
    )(h, g_mix, w_main, w_f, b_f, gq, gk, bd64)


def _split3(x):
    h1 = x.astype(jnp.bfloat16).astype(_F32)
    r1 = x - h1
    h2 = r1.astype(jnp.bfloat16).astype(_F32)
    h3 = (r1 - h2).astype(jnp.bfloat16).astype(_F32)
    return h1, h2, h3


_ONES3 = (1.0, 1.0, 1.0)
_ZEROS3 = (0.0, 0.0, 0.0)


def _aug_head(feat, first, second):
    rows = feat.shape[0]
    lane = lax.broadcasted_iota(jnp.int32, (rows, LANES - FOX_HEAD_DIM), 1)
    aux = jnp.zeros((rows, LANES - FOX_HEAD_DIM), _F32)
    for k in range(3):
        aux = jnp.where(lane == k, first[k], aux)
        aux = jnp.where(lane == 3 + k, second[k], aux)
    return jnp.concatenate([feat.astype(_MXU), aux.astype(_MXU)], axis=-1)


def _gate_fwd(fl, qn, kn, v):
    T = fl.shape[0]
    tb = min(_TB, T)
    wide = FOX_HEADS * LANES

    def body(fl_ref, q_ref, k_ref, v_ref, qa_ref, ka_ref, va_ref, carry):
        @pl.when(pl.program_id(0) == 0)
        def _():
            carry[...] = jnp.zeros_like(carry)

        x = fl_ref[...]
        ls = jnp.minimum(x, 0.0) - jnp.log1p(jnp.exp(-jnp.abs(x)))
        row = lax.broadcasted_iota(jnp.int32, (tb, tb), 0)
        col = lax.broadcasted_iota(jnp.int32, (tb, tb), 1)
        tri = jnp.where(col <= row, 1.0, 0.0).astype(jnp.bfloat16)
        cs = _tri_dot(tri, ls) + carry[...]
        carry[...] = cs[tb - 1:tb, :]
        for h in range(FOX_HEADS):
            sl = slice(h * FOX_HEAD_DIM, (h + 1) * FOX_HEAD_DIM)
            out = slice(h * LANES, (h + 1) * LANES)
            c3 = _split3(cs[:, h:h + 1])
            qa_ref[:, out] = _aug_head(q_ref[:, sl].astype(_F32) * (FOX_HEAD_DIM ** -0.5), c3, _ONES3)
            ka_ref[:, out] = _aug_head(k_ref[:, sl], _ONES3, tuple(-t for t in c3))
            va_ref[:, out] = _aug_head(v_ref[:, sl], _ONES3, _ZEROS3)

    return pl.pallas_call(
        body, name="gate_fwd", grid=(T // tb,),
        in_specs=[_rows(tb, LANES), _rows(tb, FOX_WIDTH), _rows(tb, FOX_WIDTH), _rows(tb, FOX_WIDTH)],
        out_specs=[_rows(tb, wide)] * 3, out_shape=[_sds((T, wide), _MXU)] * 3,
        scratch_shapes=[pltpu.VMEM((1, LANES), _F32)],
    )(fl, qn, kn, v)


def _gate_bwd(drs, dcs, fl):
    T = fl.shape[0]
    tb = min(_TB, T)
    nb = T // tb

    def body(r_ref, d_ref, fl_ref, df_ref, carry):
        @pl.when(pl.program_id(0) == 0)
        def _():
            carry[...] = jnp.zeros_like(carry)

        row = lax.broadcasted_iota(jnp.int32, (tb, tb), 0)
        col = lax.broadcasted_iota(jnp.int32, (tb, tb), 1)
        tri = jnp.where(col >= row, 1.0, 0.0).astype(jnp.bfloat16)
        rc = _tri_dot(tri, r_ref[...] - d_ref[...]) + carry[...]
        carry[...] = rc[0:1, :]
        df_ref[...] = rc * (1.0 / (1.0 + jnp.exp(fl_ref[...])))

    rev = pl.BlockSpec((tb, LANES), lambda i: (nb - 1 - i, 0))
    return pl.pallas_call(
        body, name="gate_bwd", grid=(nb,), in_specs=[rev, rev, rev], out_specs=rev,
        out_shape=_sds((T, LANES), _F32), scratch_shapes=[pltpu.VMEM((1, LANES), _F32)],
    )(drs, dcs, fl)


def _fox_scores(q_ref, k_ref, hh, masked):
    sl = slice(LANES * hh, LANES * (hh + 1))
    s = _dot_nt(q_ref[:, sl], k_ref[:, sl])
    if masked:
        row = lax.broadcasted_iota(jnp.int32, s.shape, 0)
        col = lax.broadcasted_iota(jnp.int32, s.shape, 1)
        s = jnp.where(col <= row, s, -jnp.inf)
    return s


def _causal_steps(nq, query_major):
    if query_major:
        steps = [(i, j) for i in range(nq) for j in range(i + 1)]
    else:
        steps = [(i, j) for j in range(nq) for i in range(j, nq)]
    return (jnp.asarray(np.array([s[0] for s in steps], np.int32)),
            jnp.asarray(np.array([s[1] for s in steps], np.int32)))


def _two_lanes(a, b):
    lane = lax.broadcasted_iota(jnp.int32, (a.shape[0], LANES), 1)
    return jnp.where(lane == 0, a, jnp.where(lane == 1, b, 0.0))


def _fox_fwd(qa, ka, va):
    T = qa.shape[0]
    tq = min(_TQ, T)
    nq = T // tq
    ii, jj = _causal_steps(nq, True)
    steps = int(ii.shape[0])

    def body(ii_ref, jj_ref, q_ref, k_ref, v_ref, o_ref, p_ref, mb_ref, lse_ref, m_s, acc_s):
        t = pl.program_id(1)
        i = ii_ref[t]
        j = jj_ref[t]

        @pl.when(j == 0)
        def _():
            m_s[...] = jnp.full(m_s.shape, -jnp.inf, _F32)
            acc_s[...] = jnp.zeros_like(acc_s)

        def step(masked):
            for hh in range(2):
                s = _fox_scores(q_ref, k_ref, hh, masked)
                m_prev = m_s[hh]
                m_new = jnp.maximum(m_prev, jnp.max(s, axis=-1, keepdims=True))
                p_ref[0, 0, hh] = jnp.exp(s - m_new).astype(_MXU)
                acc_s[hh] = (jnp.exp(m_prev - m_new) * acc_s[hh]
                             + _dot(p_ref[0, 0, hh], v_ref[:, LANES * hh:LANES * (hh + 1)]))
                m_s[hh] = m_new
            mb_ref[0, 0] = _two_lanes(m_s[0], m_s[1])

        @pl.when(j < i)
        def _():
            step(False)

        @pl.when(j == i)
        def _():
            step(True)
            outs, lses = [], []
            for hh in range(2):
                acc = acc_s[hh]
                l = acc[:, FOX_HEAD_DIM:FOX_HEAD_DIM + 1]
                outs.append(acc[:, :FOX_HEAD_DIM] / l)
                lses.append(m_s[hh] + jnp.log(l))
            o_ref[...] = jnp.concatenate(outs, axis=-1).astype(_MXU)
            lse_ref[0] = _two_lanes(lses[0], lses[1])

    qspec = pl.BlockSpec((tq, 2 * LANES), lambda p, t, ii, jj: (ii[t], p))
    kspec = pl.BlockSpec((tq, 2 * LANES), lambda p, t, ii, jj: (jj[t], p))
    return pl.pallas_call(
        body, name="fox_fwd",
        grid_spec=pltpu.PrefetchScalarGridSpec(
            num_scalar_prefetch=2, grid=(FOX_HEADS // 2, steps),
            in_specs=[qspec, kspec, kspec],
            out_specs=[pl.BlockSpec((tq, 2 * FOX_HEAD_DIM), lambda p, t, ii, jj: (ii[t], p)),
                       pl.BlockSpec((1, 1, 2, tq, tq), lambda p, t, ii, jj: (p, t, 0, 0, 0)),
                       pl.BlockSpec((1, 1, tq, LANES), lambda p, t, ii, jj: (p, t, 0, 0)),
                       pl.BlockSpec((1, tq, LANES), lambda p, t, ii, jj: (p, ii[t], 0))],
            scratch_shapes=[pltpu.VMEM((2, tq, 1), _F32), pltpu.VMEM((2, tq, LANES), _F32)]),
        out_shape=[_sds((T, FOX_WIDTH), _MXU), _sds((FOX_HEADS // 2, steps, 2, tq, tq), _MXU),
                   _sds((FOX_HEADS // 2, steps, tq, LANES), _F32), _sds((FOX_HEADS // 2, T, LANES), _F32)],
    )(ii, jj, qa, ka, va)


def _pool_window_sum(ext, w, forward):
    n = ext.shape[0]
    sm = ext
    k = 1
    while k < w:
        sm = sm + pltpu.roll(sm, (n - k) if forward else k, axis=0)
        k *= 2
    return sm


def _out_proj_fwd(h, o, pin, w_pool, pscale, w_out):
    T = h.shape[0]
    tm = min(_TM, T)
    hb = tm // HALO

    def body(h_ref, o_ref, pin_ref, halo_ref, wp_ref, ps_ref, wo_ref, h1_ref, mixed_ref, y_ref):
        i = pl.program_id(0)
        pin_t = pin_ref[...]
        halo = jnp.where(i == 0, 0.0, halo_ref[...])
        ext = jnp.concatenate([halo, pin_t], axis=0)
        t = (i * tm + lax.broadcasted_iota(jnp.int32, (tm, 1), 0) + 1).astype(_F32)
        mixed, ys = [], []
        for g, w in enumerate(POOL_WINDOWS):
            sl = slice(g * POOL_GROUP_DIM, (g + 1) * POOL_GROUP_DIM)
            win = _pool_window_sum(ext[:, sl], w, False)[HALO:, :]
            mg = (win / jnp.minimum(t, float(w)) - pin_t[:, sl]).astype(_MXU)
            mixed.append(mg)
            ys.append(_dot(mg, wp_ref[g]))
        mixed_ref[...] = jnp.concatenate(mixed, axis=-1)
        y = (jnp.concatenate(ys, axis=-1) * ps_ref[...]).astype(_MXU)
        y_ref[...] = y
        h1_ref[...] = h_ref[...] + _dot(o_ref[...], wo_ref[:FOX_WIDTH, :]) + _dot(y, wo_ref[FOX_WIDTH:, :])

    return pl.pallas_call(
        body, name="out_proj_fwd", grid=(T // tm,),
        in_specs=[_rows(tm, D_MODEL), _rows(tm, FOX_WIDTH), _rows(tm, POOL_WIDTH),
                  pl.BlockSpec((HALO, POOL_WIDTH), lambda i: (jnp.maximum(i * hb - 1, 0), 0)),
                  _resident(w_pool.shape), _resident((1, POOL_WIDTH)), _resident(w_out.shape)],
        out_specs=[_rows(tm, D_MODEL), _rows(tm, POOL_WIDTH), _rows(tm, POOL_WIDTH)],
        out_shape=[_sds((T, D_MODEL), _F32), _sds((T, POOL_WIDTH), _MXU), _sds((T, POOL_WIDTH), _MXU)],
    )(h, o, pin, pin, w_pool, pscale, w_out)


def _mem_kv_fwd(mem, g_kv, w_kv, gkm, bd128):
    M = mem.shape[0]

    def body(mem_ref, g_ref, w_ref, gk_ref, bd_ref, mn_ref, mkv_ref, mk_ref, mv_ref):
        x = mem_ref[...]
        mn = ((x * _rstd(x)) * g_ref[...]).astype(_MXU)
        mn_ref[...] = mn
        z = _dot(mn, w_ref[...])
        mkv_ref[...] = z
        k = z[:, :MEM_WIDTH]
        rk = lax.rsqrt(_group_sum(k * k, bd_ref[...]) * (1.0 / MEM_HEAD_DIM) + EPS)
        mk_ref[...] = ((k * rk) * gk_ref[...]).astype(_MXU)
        mv_ref[...] = z[:, MEM_WIDTH:].astype(_MXU)

    return pl.pallas_call(
        body, name="mem_kv_fwd",
        out_shape=[_sds((M, D_MODEL), _MXU), _sds((M, 2 * MEM_WIDTH), _F32), _sds((M, MEM_WIDTH), _MXU),
                   _sds((M, MEM_WIDTH), _MXU)],
    )(mem, g_kv, w_kv, gkm, bd128)


def _mem_softmax(qn, mk_ref, hd):
    sl = slice(hd * MEM_HEAD_DIM, (hd + 1) * MEM_HEAD_DIM)
    s = _dot_nt(qn[:, sl], mk_ref[:, sl]) * (MEM_HEAD_DIM ** -0.5)
    e = jnp.exp(s - jnp.max(s, axis=-1, keepdims=True))
    return e / jnp.sum(e, axis=-1, keepdims=True)


def _mem_attn_fwd(h1, g_q, w_q, gqm, bd128, mk, mv, w_mo):
    T = h1.shape[0]
    tm = min(_TM, T)

    def body(h_ref, g_ref, wq_ref, gq_ref, bd_ref, mk_ref, mv_ref, wo_ref, h2_ref, hn_ref, mo_ref):
        x = h_ref[...]
        hn = ((x * _rstd(x)) * g_ref[...]).astype(_MXU)
        hn_ref[...] = hn
        mq = _dot(hn, wq_ref[...])
        rq = lax.rsqrt(_group_sum(mq * mq, bd_ref[...]) * (1.0 / MEM_HEAD_DIM) + EPS)
        qn = ((mq * rq) * gq_ref[...]).astype(_MXU)
        outs = []
        for hd in range(MEM_HEADS):
            p = _mem_softmax(qn, mk_ref, hd).astype(_MXU)
            outs.append(_dot(p, mv_ref[:, hd * MEM_HEAD_DIM:(hd + 1) * MEM_HEAD_DIM]))
        mo = jnp.concatenate(outs, axis=-1).astype(_MXU)
        mo_ref[...] = mo
        h2_ref[...] = x + _dot(mo, wo_ref[...])

    return pl.pallas_call(
        body, name="mem_attn_fwd", grid=(T // tm,),
        in_specs=[_rows(tm, D_MODEL), _resident((1, D_MODEL)), _resident(w_q.shape), _resident((1, MEM_WIDTH)),
                  _resident(bd128.shape), _resident(mk.shape), _resident(mv.shape), _resident(w_mo.shape)],
        out_specs=[_rows(tm, D_MODEL), _rows(tm, D_MODEL), _rows(tm, MEM_WIDTH)],
        out_shape=[_sds((T, D_MODEL), _F32), _sds((T, D_MODEL), _MXU), _sds((T, MEM_WIDTH), _MXU)],
    )(h1, g_q, w_q, gqm, bd128, mk, mv, w_mo)


def _ffn_weight_specs(layer):
    nf = D_FF // _TF
    return [pl.BlockSpec((1, D_MODEL, _TF), lambda i, j: (j, layer, 0)),
            pl.BlockSpec((1, D_MODEL, _TF), lambda i, j: (j + nf, layer, 0)),
            pl.BlockSpec((2, _TF // 2, D_MODEL), lambda i, j: (j, layer, 0))]


def _ffn_fwd(h2, g_ffn, w_gu, w_d, layer):
    T = h2.shape[0]
    tm = min(_TMF, T)
    nf = D_FF // _TF

    def body(h_ref, g_ref, wg_ref, wu_ref, wd_ref, h3_ref, hn_ref, acc, xn_s):
        j = pl.program_id(1)

        @pl.when(j == 0)
        def _():
            x = h_ref[...]
            xn = ((x * _rstd(x)) * g_ref[...]).astype(_MXU)
            xn_s[...] = xn
            hn_ref[...] = xn
            acc[...] = jnp.zeros_like(acc)

        xn = xn_s[...]
        g = _dot(xn, wg_ref[0])
        u = _dot(xn, wu_ref[0])
        a = ((g * jax.nn.sigmoid(g)) * u).astype(_MXU)
        acc[...] += _dot(a, wd_ref[...].reshape(_TF, D_MODEL))

        @pl.when(j == nf - 1)
        def _():
            h3_ref[...] = h_ref[...] + acc[...]

    tok = pl.BlockSpec((tm, D_MODEL), lambda i, j: (i, 0))
    return pl.pallas_call(
        body, name="ffn_fwd", grid=(T // tm, nf),
        in_specs=[tok, pl.BlockSpec((1, D_MODEL), lambda i, j: (0, 0))] + _ffn_weight_specs(layer),
        out_specs=[tok, tok],
        out_shape=[_sds((T, D_MODEL), _F32), _sds((T, D_MODEL), _MXU)],
        scratch_shapes=[pltpu.VMEM((tm, D_MODEL), _F32), pltpu.VMEM((tm, D_MODEL), _MXU)],
    )(h2, g_ffn, w_gu, w_gu, w_d)


def _loss_grad(y, tgt):
    T = y.shape[0]
    tm = min(_TA, T)

    def body(y_ref, t_ref, dy_ref, loss_ref):
        @pl.when(pl.program_id(0) == 0)
        def _():
            loss_ref[...] = jnp.zeros_like(loss_ref)

        err = y_ref[...] - t_ref[...]
        dy_ref[...] = err * (1.0 / D_MODEL)
        part = jnp.sum(jnp.sum(err * err, axis=0, keepdims=True), axis=1, keepdims=True)
        loss_ref[...] += part * (0.5 / D_MODEL)

    return pl.pallas_call(
        body, name="loss_grad", grid=(T // tm,), in_specs=[_rows(tm, D_MODEL), _rows(tm, D_MODEL)],
        out_specs=[_rows(tm, D_MODEL), pl.BlockSpec((1, 1), lambda i: (0, 0))],
        out_shape=[_sds((T, D_MODEL), _F32), _sds((1, 1), _F32)],
    )(y, tgt)


def _pick_tile(n, candidates=(1408, 1024, 512, 256, 128)):
    for c in candidates:
        if n % c == 0:
            return c
    return n


def _matmul_tn(a, b, name, tk=None, tn=None, dst=None, dst_shape=None, dst_index=None, dst_chips=1):
    T, K = a.shape
    N = b.shape[1]
    tk, tn, tt = tk or _pick_tile(K), tn or _pick_tile(N), min(_TT, T)

    def body(*refs):
        a_ref, b_ref, o_ref = refs[0], refs[1], refs[-1]

        @pl.when(pl.program_id(2) == 0)
        def _():
            o_ref[...] = jnp.zeros_like(o_ref)

        o_ref[...] += _dot_tn(a_ref[...].astype(_MXU), b_ref[...].astype(_MXU)).reshape(o_ref.shape)

    in_specs = [pl.BlockSpec((tt, tk), lambda i, j, t: (t, i)), pl.BlockSpec((tt, tn), lambda i, j, t: (t, j))]
    if dst_shape is None:
        out_spec, out_shape, args, alias = pl.BlockSpec((tk, tn), lambda i, j, t: (i, j)), (K, N), (a, b), {}
    else:
        out_spec = pl.BlockSpec((dst_chips, tk // dst_chips, tn), lambda i, j, t: dst_index(i, j))
        out_shape, args, alias = dst_shape, (a, b), {}
        if dst is not None:
            in_specs, args, alias = in_specs + [_ANY], (a, b, dst), {2: 0}
    return pl.pallas_call(
        body, name=name, grid=(K // tk, N // tn, T // tt), in_specs=in_specs, out_specs=out_spec,
        out_shape=_sds(out_shape, _F32), input_output_aliases=alias,
    )(*args)


def _ffn_bwd(dh3, h2, hn, g_ffn, w_gu, w_d, layer):
    T = h2.shape[0]
    tm = min(_TMF, T)
    nf = D_FF // _TF

    def body(dh_ref, h_ref, hn_ref, g_ref, wg_ref, wu_ref, wd_ref, dh2_ref, act_ref, dg_ref, du_ref, dgain_ref,
             acc, dyb):
        i = pl.program_id(0)
        j = pl.program_id(1)

        @pl.when((i == 0) & (j == 0))
        def _():
            dgain_ref[...] = jnp.zeros_like(dgain_ref)

        @pl.when(j == 0)
        def _():
            dyb[...] = dh_ref[...].astype(_MXU)
            acc[...] = jnp.zeros_like(acc)

        xn = hn_ref[...]
        wg = wg_ref[0]
        wu = wu_ref[0]
        g = _dot(xn, wg)
        u = _dot(xn, wu)
        sg = jax.nn.sigmoid(g)
        sl = g * sg
        act_ref[...] = (sl * u).astype(_MXU)
        da = _dot_nt(dyb[...], wd_ref[...].reshape(_TF, D_MODEL))
        dgate = (da * u * (sg * (1.0 + g * (1.0 - sg)))).astype(_MXU)
        dup = (da * sl).astype(_MXU)
        dg_ref[...] = dgate
        du_ref[...] = dup
        acc[...] += _dot_nt(dgate, wg) + _dot_nt(dup, wu)

        @pl.when(j == nf - 1)
        def _():
            dhn = acc[...]
            dx, dgain = _norm_bwd(dhn, h_ref[...], g_ref[...])
            dh2_ref[...] = dh_ref[...] + dx
            dgain_ref[...] += dgain

    tok = pl.BlockSpec((tm, D_MODEL), lambda i, j: (i, 0))
    ffb = pl.BlockSpec((tm, _TF), lambda i, j: (i, j))
    row = pl.BlockSpec((1, D_MODEL), lambda i, j: (0, 0))
    return pl.pallas_call(
        body, name="ffn_bwd", grid=(T // tm, nf),
        in_specs=[tok, tok, tok, row] + _ffn_weight_specs(layer),
        out_specs=[tok, ffb, ffb, ffb, row],
        out_shape=[_sds((T, D_MODEL), _F32), _sds((T, D_FF), _MXU), _sds((T, D_FF), _MXU), _sds((T, D_FF), _MXU),
                   _sds((1, D_MODEL), _F32)],
        scratch_shapes=[pltpu.VMEM((tm, D_MODEL), _F32), pltpu.VMEM((tm, D_MODEL), _MXU)],
    )(dh3, h2, hn, g_ffn, w_gu, w_gu, w_d)


def _mem_attn_bwd(dh2, h1, g_q, w_q, gqm, bd128, mk, mv, w_mo):
    T = h1.shape[0]
    M = mk.shape[0]
    tm = min(_TM, T)
    nt = T // tm

    def body(dh_ref, h_ref, g_ref, wq_ref, gq_ref, bd_ref, mk_ref, mv_ref, wo_ref,
             dh1_ref, dmq_ref, dmk_ref, dmv_ref, dgq_ref, dgain_ref, dgq_acc):
        i = pl.program_id(0)

        @pl.when(i == 0)
        def _():
            dmk_ref[...] = jnp.zeros_like(dmk_ref)
            dmv_ref[...] = jnp.zeros_like(dmv_ref)
            dgain_ref[...] = jnp.zeros_like(dgain_ref)
            dgq_acc[...] = jnp.zeros_like(dgq_acc)

        x = h_ref[...]
        g = g_ref[...]
        bd = bd_ref[...]
        hn = ((x * _rstd(x)) * g).astype(_MXU)
        mq = _dot(hn, wq_ref[...])
        rq = lax.rsqrt(_group_sum(mq * mq, bd) * (1.0 / MEM_HEAD_DIM) + EPS)
        qn = ((mq * rq) * gq_ref[...]).astype(_MXU)
        dmo = _dot_nt(dh_ref[...].astype(_MXU), wo_ref[...])
        dqn = []
        for hd in range(MEM_HEADS):
            sl = slice(hd * MEM_HEAD_DIM, (hd + 1) * MEM_HEAD_DIM)
            p = _mem_softmax(qn, mk_ref, hd)
            dmo_h = dmo[:, sl].astype(_MXU)
            dp = _dot_nt(dmo_h, mv_ref[:, sl])
            ds = (p * (dp - jnp.sum(p * dp, axis=-1, keepdims=True)) * (MEM_HEAD_DIM ** -0.5)).astype(_MXU)
            dqn.append(_dot(ds, mk_ref[:, sl]))
            dmk_ref[:, sl] += _dot_tn(ds, qn[:, sl])
            dmv_ref[:, sl] += _dot_tn(p.astype(_MXU), dmo_h)
        dqn = jnp.concatenate(dqn, axis=-1)
        dmq, dgq = _headnorm_bwd(dqn, mq, gq_ref[...], bd, MEM_HEAD_DIM)
        dgq_acc[...] += dgq
        dmq = dmq.astype(_MXU)
        dmq_ref[...] = dmq
        dhn = _dot_nt(dmq, wq_ref[...])
        dx, dgain = _norm_bwd(dhn, x, g)
        dh1_ref[...] = dh_ref[...] + dx
        dgain_ref[...] += dgain

        @pl.when(i == nt - 1)
        def _():
            dgq_ref[...] = _fold_heads(dgq_acc[...], MEM_HEADS, MEM_HEAD_DIM)

    const2 = lambda shape: pl.BlockSpec(shape, lambda i: (0, 0))
    return pl.pallas_call(
        body, name="mem_attn_bwd", grid=(nt,),
        in_specs=[_rows(tm, D_MODEL), _rows(tm, D_MODEL), _resident((1, D_MODEL)), _resident(w_q.shape),
                  _resident((1, MEM_WIDTH)), _resident(bd128.shape), _resident(mk.shape), _resident(mv.shape),
                  _resident(w_mo.shape)],
        out_specs=[_rows(tm, D_MODEL), _rows(tm, MEM_WIDTH), const2((M, MEM_WIDTH)), const2((M, MEM_WIDTH)),
                   const2((1, MEM_HEAD_DIM)), const2((1, D_MODEL))],
        out_shape=[_sds((T, D_MODEL), _F32), _sds((T, MEM_WIDTH), _MXU), _sds((M, MEM_WIDTH), _F32),
                   _sds((M, MEM_WIDTH), _F32), _sds((1, MEM_HEAD_DIM), _F32), _sds((1, D_MODEL), _F32)],
        scratch_shapes=[pltpu.VMEM((1, MEM_WIDTH), _F32)],
    )(dh2, h1, g_q, w_q, gqm, bd128, mk, mv, w_mo)


def _mem_kv_bwd(dmk, dmv, mkv, mn, mem, g_kv, gkm, bd128, w_kv, dst, dst_shape, block):
    rows = D_MODEL // N_CHIPS

    def body(dmk_ref, dmv_ref, mkv_ref, mn_ref, mem_ref, g_ref, gk_ref, bd_ref, w_ref, *rest):
        dw_ref, dgk_ref, dgain_ref = rest[-3:]
        kraw = mkv_ref[:, :MEM_WIDTH]
        dk, dgk = _headnorm_bwd(dmk_ref[...], kraw, gk_ref[...], bd_ref[...], MEM_HEAD_DIM)
        dgk_ref[...] = _fold_heads(dgk, MEM_HEADS, MEM_HEAD_DIM)
        dmkv = jnp.concatenate([dk, dmv_ref[...]], axis=-1).astype(_MXU)
        dw_ref[...] = _dot_tn(mn_ref[...], dmkv).reshape(N_CHIPS, rows, 2 * MEM_WIDTH)
        dmn = _dot_nt(dmkv, w_ref[...])
        _, dgain = _norm_bwd(dmn, mem_ref[...], g_ref[...])
        dgain_ref[...] = dgain

    args = (dmk, dmv, mkv, mn, mem, g_kv, gkm, bd128, w_kv)
    whole = lambda a: pl.BlockSpec(a.shape, lambda i: (0,) * a.ndim)
    in_specs, alias = [whole(a) for a in args], {}
    if dst is not None:
        in_specs, args, alias = in_specs + [_ANY], args + (dst,), {len(args): 0}
    return pl.pallas_call(
        body, name="mem_kv_bwd", grid=(1,), in_specs=in_specs,
        out_specs=[pl.BlockSpec((N_CHIPS, rows, 2 * MEM_WIDTH), lambda i: (0, block, 0)),
                   pl.BlockSpec((1, MEM_HEAD_DIM), lambda i: (0, 0)), pl.BlockSpec((1, D_MODEL), lambda i: (0, 0))],
        out_shape=[_sds(dst_shape, _F32), _sds((1, MEM_HEAD_DIM), _F32), _sds((1, D_MODEL), _F32)],
        input_output_aliases=alias,
    )(*args)


def _out_proj_bwd(dh1, mixed, o, bd64, w_pool, pscale, w_out):
    T = dh1.shape[0]
    tm = min(_TM, T)
    hb = tm // HALO
    nt = T // tm

    def body(dh_ref, halo_ref, mx_ref, o_ref, bd_ref, wp_ref, ps_ref, wo_ref, doa_ref, dpin_ref, dwp_ref, dps_ref):
        i = pl.program_id(0)

        @pl.when(i == 0)
        def _():
            dwp_ref[...] = jnp.zeros_like(dwp_ref)
            dps_ref[...] = jnp.zeros_like(dps_ref)

        dcat = _dot_nt(dh_ref[...].astype(_MXU), wo_ref[...])
        do = dcat[:, :FOX_WIDTH].astype(_MXU)
        delta = _group_sum(do.astype(_F32) * o_ref[...].astype(_F32), bd_ref[...])
        for h in range(FOX_HEADS):
            sl = slice(h * FOX_HEAD_DIM, (h + 1) * FOX_HEAD_DIM)
            doa_ref[:, h * LANES:(h + 1) * LANES] = _aug_head(
                do[:, sl], _split3(-delta[:, h * FOX_HEAD_DIM:h * FOX_HEAD_DIM + 1]), _ZEROS3)
        dy = dcat[:, FOX_WIDTH:]
        dyh = _dot_nt(halo_ref[...].astype(_MXU), wo_ref[FOX_WIDTH:, :])
        dyh = jnp.where(i == nt - 1, 0.0, dyh)
        ps = ps_ref[...]
        t = (i * tm + lax.broadcasted_iota(jnp.int32, (tm + HALO, 1), 0) + 1).astype(_F32)
        mixed_t = mx_ref[...]
        dpin, dps = [], []
        for g, w in enumerate(POOL_WINDOWS):
            sl = slice(g * POOL_GROUP_DIM, (g + 1) * POOL_GROUP_DIM)
            mg = mixed_t[:, sl]
            wg = wp_ref[g]
            dps.append(jnp.sum(dy[:, sl] * _dot(mg, wg), axis=0, keepdims=True))
            dyl = (dy[:, sl] * ps[:, sl]).astype(_MXU)
            dylh = (dyh[:, sl] * ps[:, sl]).astype(_MXU)
            dwp_ref[g] += _dot_tn(mg, dyl)
            dmx = _dot_nt(dyl, wg)
            ext = jnp.concatenate([dmx, _dot_nt(dylh, wg)], axis=0) / jnp.minimum(t, float(w))
            dpin.append(_pool_window_sum(ext, w, True)[:tm, :] - dmx)
        dpin_ref[...] = jnp.concatenate(dpin, axis=-1)
        dps_ref[...] += jnp.concatenate(dps, axis=-1)

    return pl.pallas_call(
        body, name="out_proj_bwd", grid=(nt,),
        in_specs=[_rows(tm, D_MODEL),
                  pl.BlockSpec((HALO, D_MODEL), lambda i: (jnp.minimum((i + 1) * hb, T // HALO - 1), 0)),
                  _rows(tm, POOL_WIDTH), _rows(tm, FOX_WIDTH), _resident(bd64.shape), _resident(w_pool.shape),
                  _resident((1, POOL_WIDTH)), _resident(w_out.shape)],
        out_specs=[_rows(tm, FOX_HEADS * LANES), _rows(tm, POOL_WIDTH),
                   pl.BlockSpec(w_pool.shape, lambda i: (0, 0, 0)), pl.BlockSpec((1, POOL_WIDTH), lambda i: (0, 0))],
        out_shape=[_sds((T, FOX_HEADS * LANES), _MXU), _sds((T, POOL_WIDTH), _F32), _sds(w_pool.shape, _F32),
                   _sds((1, POOL_WIDTH), _F32)],
    )(dh1, dh1, mixed, o, bd64, w_pool, pscale, w_out)


def _fox_bwd(qa, ka, va, doa, pt, mb, lse):
    T = qa.shape[0]
    tq = min(_TQ, T)
    nq = T // tq
    pair = 2 * FOX_HEAD_DIM
    ii, jj = _causal_steps(nq, False)
    fwd_step = ii * (ii + 1) // 2 + jj
    row_lane, col_lane = FOX_HEAD_DIM, FOX_HEAD_DIM + 3

    def body(ii_ref, jj_ref, fs_ref, q_ref, k_ref, v_ref, do_ref, p_ref, mb_ref, lse_ref,
             dq_ref, dk_ref, dv_ref, drs_ref, dcs_ref, dk_acc, dv_acc):
        t = pl.program_id(1)
        i = ii_ref[t]
        j = jj_ref[t]

        @pl.when(t == 0)
        def _():
            dq_ref[...] = jnp.zeros_like(dq_ref)
            drs_ref[...] = jnp.zeros_like(drs_ref)

        @pl.when(i == j)
        def _():
            dk_acc[...] = jnp.zeros_like(dk_acc)
            dv_acc[...] = jnp.zeros_like(dv_acc)

        r = jnp.exp(mb_ref[0, 0] - lse_ref[0])
        dqs, rs = [], []
        for hh in range(2):
            sl = slice(LANES * hh, LANES * (hh + 1))
            dof = do_ref[:, sl].astype(_F32) * r[:, hh:hh + 1]
            dob = dof.astype(_MXU)
            p = p_ref[0, 0, hh]
            dv_acc[hh] += _dot(dof.T.astype(_MXU), p)
            dsb = (p.astype(_F32) * _dot_nt(dob, v_ref[:, sl])).astype(_MXU)
            dk_acc[hh] += _dot(q_ref[:, sl].astype(_F32).T.astype(_MXU), dsb)
            dqa = _dot(dsb, k_ref[:, sl])
            dqs.append(dqa[:, :FOX_HEAD_DIM] * (FOX_HEAD_DIM ** -0.5))
            rs.append(dqa[:, row_lane:row_lane + 1])
        rows = pl.ds(pl.multiple_of(i * tq, tq), tq)
        dq_ref[rows, :] += jnp.concatenate(dqs, axis=-1)
        drs_ref[0, rows, :] += _two_lanes(rs[0], rs[1])

        @pl.when(i == nq - 1)
        def _():
            dk = [dk_acc[hh].T for hh in range(2)]
            dv = [dv_acc[hh].T for hh in range(2)]
            dk_ref[...] = jnp.concatenate([dk[0][:, :FOX_HEAD_DIM], dk[1][:, :FOX_HEAD_DIM]], axis=-1)
            dv_ref[...] = jnp.concatenate([dv[0][:, :FOX_HEAD_DIM], dv[1][:, :FOX_HEAD_DIM]], axis=-1)
            dcs_ref[0] = _two_lanes(dk[0][:, col_lane:col_lane + 1], dk[1][:, col_lane:col_lane + 1])

    qspec = pl.BlockSpec((tq, 2 * LANES), lambda p, t, ii, jj, fs: (ii[t], p))
    kspec = pl.BlockSpec((tq, 2 * LANES), lambda p, t, ii, jj, fs: (jj[t], p))
    kout = pl.BlockSpec((tq, pair), lambda p, t, ii, jj, fs: (jj[t], p))
    return pl.pallas_call(
        body, name="fox_bwd",
        grid_spec=pltpu.PrefetchScalarGridSpec(
            num_scalar_prefetch=3, grid=(FOX_HEADS // 2, int(ii.shape[0])),
            in_specs=[qspec, kspec, kspec, qspec,
                      pl.BlockSpec((1, 1, 2, tq, tq), lambda p, t, ii, jj, fs: (p, fs[t], 0, 0, 0)),
                      pl.BlockSpec((1, 1, tq, LANES), lambda p, t, ii, jj, fs: (p, fs[t], 0, 0)),
                      pl.BlockSpec((1, tq, LANES), lambda p, t, ii, jj, fs: (p, ii[t], 0))],
            out_specs=[pl.BlockSpec((T, pair), lambda p, t, ii, jj, fs: (0, p)), kout, kout,
                       pl.BlockSpec((1, T, LANES), lambda p, t, ii, jj, fs: (p, 0, 0)),
                       pl.BlockSpec((1, tq, LANES), lambda p, t, ii, jj, fs: (p, jj[t], 0))],
            scratch_shapes=[pltpu.VMEM((2, LANES, tq), _F32), pltpu.VMEM((2, LANES, tq), _F32)]),
        out_shape=[_sds((T, FOX_WIDTH), _F32), _sds((T, FOX_WIDTH), _F32), _sds((T, FOX_WIDTH), _F32),
                   _sds((FOX_HEADS // 2, T, LANES), _F32), _sds((FOX_HEADS // 2, T, LANES), _F32)],
    )(ii, jj, fwd_step, qa, ka, va, doa, pt, mb, lse)


def _mix_in_bwd(dh1, h, g_mix, zqk, dq, dk, dv, dpin, df, gq, gk, bd64, w_main, w_f):
    T = h.shape[0]
    tm = min(_TM, T)
    nt = T // tm

    def body(dh1_ref, h_ref, g_ref, zqk_ref, dq_ref, dk_ref, dv_ref, dpin_ref, df_ref, gq_ref, gk_ref, bd_ref,
             wm_ref, wf_ref, dh_ref, dz_ref, dzf_ref, dgq_ref, dgk_ref, dgain_ref, dbf_ref, dgq_acc, dgk_acc):
        i = pl.program_id(0)

        @pl.when(i == 0)
        def _():
            dgain_ref[...] = jnp.zeros_like(dgain_ref)
            dbf_ref[...] = jnp.zeros_like(dbf_ref)
            dgq_acc[...] = jnp.zeros_like(dgq_acc)
            dgk_acc[...] = jnp.zeros_like(dgk_acc)

        bd = bd_ref[...]
        dqr, dgq = _headnorm_bwd(dq_ref[...], zqk_ref[:, :FOX_WIDTH], gq_ref[...], bd, FOX_HEAD_DIM)
        dkr, dgk = _headnorm_bwd(dk_ref[...], zqk_ref[:, FOX_WIDTH:], gk_ref[...], bd, FOX_HEAD_DIM)
        dgq_acc[...] += dgq
        dgk_acc[...] += dgk
        dz = jnp.concatenate([dqr, dkr, dv_ref[...], dpin_ref[...]], axis=-1).astype(_MXU)
        dz_ref[...] = dz
        df = df_ref[...]
        dzf = df.astype(_MXU)
        dzf_ref[...] = dzf
        dbf_ref[...] += jnp.sum(df, axis=0, keepdims=True)
        dxn = _dot_nt(dz, wm_ref[...]) + _dot_nt(dzf, wf_ref[...])
        dx, dgain = _norm_bwd(dxn, h_ref[...], g_ref[...])
        dh_ref[...] = dh1_ref[...] + dx
        dgain_ref[...] += dgain

        @pl.when(i == nt - 1)
        def _():
            dgq_ref[...] = _fold_heads(dgq_acc[...], FOX_HEADS, FOX_HEAD_DIM)
            dgk_ref[...] = _fold_heads(dgk_acc[...], FOX_HEADS, FOX_HEAD_DIM)

    const2 = lambda shape: pl.BlockSpec(shape, lambda i: (0, 0))
    return pl.pallas_call(
        body, name="mix_in_bwd", grid=(nt,),
        in_specs=[_rows(tm, D_MODEL), _rows(tm, D_MODEL), _resident((1, D_MODEL)), _rows(tm, 2 * FOX_WIDTH),
                  _rows(tm, FOX_WIDTH), _rows(tm, FOX_WIDTH), _rows(tm, FOX_WIDTH), _rows(tm, POOL_WIDTH),
                  _rows(tm, LANES), _resident((1, FOX_WIDTH)), _resident((1, FOX_WIDTH)), _resident(bd64.shape),
                  _resident(w_main.shape), _resident(w_f.shape)],
        out_specs=[_rows(tm, D_MODEL), _rows(tm, 4 * FOX_WIDTH), _rows(tm, LANES), const2((1, FOX_HEAD_DIM)),
                   const2((1, FOX_HEAD_DIM)), const2((1, D_MODEL)), const2((1, LANES))],
        out_shape=[_sds((T, D_MODEL), _F32), _sds((T, 4 * FOX_WIDTH), _MXU), _sds((T, LANES), _MXU),
                   _sds((1, FOX_HEAD_DIM), _F32), _sds((1, FOX_HEAD_DIM), _F32), _sds((1, D_MODEL), _F32),
                   _sds((1, LANES), _F32)],
        scratch_shapes=[pltpu.VMEM((1, FOX_WIDTH), _F32), pltpu.VMEM((1, FOX_WIDTH), _F32)],
    )(dh1, h, g_mix, zqk, dq, dk, dv, dpin, df, gq, gk, bd64, w_main, w_f)


CLASSES = (("a", D_MODEL, ("w_out", "w_mem_kv")), ("d", D_MODEL, ("w_down",)), ("g", _TF, ("w_gate_up",)),
           ("i", 514, ("w_in",)), ("q", MEM_WIDTH, ("w_mem_q",)), ("o", 256, ("w_mem_out",)))
W_OUT_ROWS = D_MODEL // N_CHIPS
W_DOWN_ROWS = D_FF // N_CHIPS
W_MEM_OUT_COLS = D_MODEL // N_CHIPS


def _layer_params(WC, WS, l):
    rows = lambda buf, first, n: buf[:, first:first + n]
    w_in = rows(WC["i"], l * D_MODEL, D_MODEL).transpose(1, 0, 2).reshape(D_MODEL, -1)
    n_main = 3 * FOX_WIDTH + POOL_WIDTH
    row = lambda a: a.reshape(1, -1).astype(_F32)
    return dict(
        layer=l,
        g_mix=row(WS["g_mix"][l]),
        w_main=w_in[:, :n_main],
        w_f=jnp.pad(w_in[:, n_main:], ((0, 0), (0, LANES - FOX_HEADS))),
        b_f=jnp.pad(row(WS["b_forget"][l]), ((0, 0), (0, LANES - FOX_HEADS))),
        gq=jnp.tile(row(WS["g_q_fox"][l]), (1, FOX_HEADS)),
        gk=jnp.tile(row(WS["g_k_fox"][l]), (1, FOX_HEADS)),
        w_pool=WS["w_pool"][l].astype(_MXU),
        pscale=row(WS["pool_scale"][l]),
        w_out=rows(WC["a"], l * W_OUT_ROWS, W_OUT_ROWS).reshape(D_MODEL, D_MODEL),
        g_mem_q=row(WS["g_mem_q"][l]),
        g_mem_kv=row(WS["g_mem_kv"][l]),
        w_mem_q=rows(WC["q"], l * W_OUT_ROWS, W_OUT_ROWS).reshape(D_MODEL, MEM_WIDTH),
        w_mem_kv=rows(WC["a"], (DEPTH + l) * W_OUT_ROWS, W_OUT_ROWS).reshape(D_MODEL, 2 * MEM_WIDTH),
        gqm=jnp.tile(row(WS["g_q_mem"][l]), (1, MEM_HEADS)),
        gkm=jnp.tile(row(WS["g_k_mem"][l]), (1, MEM_HEADS)),
        w_mem_out=rows(WC["o"], l * MEM_WIDTH, MEM_WIDTH).transpose(1, 0, 2).reshape(MEM_WIDTH, D_MODEL),
        g_ffn=row(WS["g_ffn"][l]),
        w_gu=WC["g"],
        w_d=WC["d"],
    )


def _per_head_lanes(sums):
    T = sums.shape[1]
    return jnp.pad(sums[:, :, :2].transpose(1, 0, 2).reshape(T, FOX_HEADS), ((0, 0), (0, LANES - FOX_HEADS)))


def _layer_fwd(h, mem, P, bd64, bd128):
    s = dict(h=h)
    s["xn"], s["zqk"], qn, kn, v, pin, s["fl"] = _mix_in_fwd(
        h, P["g_mix"], P["w_main"], P["w_f"], P["b_f"], P["gq"], P["gk"], bd64)
    s["qa"], s["ka"], s["va"] = _gate_fwd(s["fl"], qn, kn, v)
    s["o"], s["pt"], s["mb"], s["lse"] = _fox_fwd(s["qa"], s["ka"], s["va"])
    s["h1"], s["mixed"], s["y"] = _out_proj_fwd(h, s["o"], pin, P["w_pool"], P["pscale"], P["w_out"])
    s["mn"], s["mkv"], s["mk"], s["mv"] = _mem_kv_fwd(mem, P["g_mem_kv"], P["w_mem_kv"], P["gkm"], bd128)
    s["h2"], s["hn_mem"], s["mo"] = _mem_attn_fwd(s["h1"], P["g_mem_q"], P["w_mem_q"], P["gqm"], bd128, s["mk"],
                                                  s["mv"], P["w_mem_out"])
    h3, s["hn_ffn"] = _ffn_fwd(s["h2"], P["g_ffn"], P["w_gu"], P["w_d"], P["layer"])
    return h3, s


def _layer_bwd(dh3, mem, P, s, bd64, bd128, G, shapes):
    l = P["layer"]
    g = {}

    def into(c, a, b, name, tk, tn, index, chips=1):
        G[c] = _matmul_tn(a, b, name, tk=tk, tn=tn, dst=G.get(c), dst_shape=shapes[c], dst_index=index,
                          dst_chips=chips)

    dh2, act, dgate, dup, g["g_ffn"] = _ffn_bwd(dh3, s["h2"], s["hn_ffn"], P["g_ffn"], P["w_gu"], P["w_d"], l)
    into("d", act, dh3, "dw_down", 2 * W_DOWN_ROWS, D_MODEL, lambda i, j: (i, l, 0), chips=2)
    into("g", s["hn_ffn"], dgate, "dw_gate", D_MODEL, _TF, lambda i, j: (j, l, 0))
    into("g", s["hn_ffn"], dup, "dw_up", D_MODEL, _TF, lambda i, j: (j + D_FF // _TF, l, 0))

    dh1, dmq, dmk, dmv, g["g_q_mem"], g["g_mem_q"] = _mem_attn_bwd(
        dh2, s["h1"], P["g_mem_q"], P["w_mem_q"], P["gqm"], bd128, s["mk"], s["mv"], P["w_mem_out"])
    into("o", s["mo"], dh2, "dw_mem_out", MEM_WIDTH, W_MEM_OUT_COLS, lambda i, j: (j, l, 0))
    into("q", s["hn_mem"], dmq, "dw_mem_q", W_OUT_ROWS, MEM_WIDTH, lambda i, j: (i, l, 0))
    G["a"], g["g_k_mem"], g["g_mem_kv"] = _mem_kv_bwd(dmk, dmv, s["mkv"], s["mn"], mem, P["g_mem_kv"], P["gkm"],
                                                      bd128, P["w_mem_kv"], G.get("a"), shapes["a"], DEPTH + l)

    doa, dpin, g["w_pool"], g["pool_scale"] = _out_proj_bwd(dh1, s["mixed"], s["o"], bd64, P["w_pool"], P["pscale"],
                                                            P["w_out"])
    half = FOX_WIDTH // W_OUT_ROWS
    into("a", s["o"], dh1, "dw_out_fox", W_OUT_ROWS, D_MODEL, lambda i, j: (i, l, 0))
    into("a", s["y"], dh1, "dw_out_pool", W_OUT_ROWS, D_MODEL, lambda i, j: (i + half, l, 0))
    dq, dk, dv, drs, dcs = _fox_bwd(s["qa"], s["ka"], s["va"], doa, s["pt"], s["mb"], s["lse"])
    df = _gate_bwd(_per_head_lanes(drs), _per_head_lanes(dcs), s["fl"])
    dh, dz, dzf, g["g_q_fox"], g["g_k_fox"], g["g_mix"], dbf = _mix_in_bwd(
        dh1, s["h"], P["g_mix"], s["zqk"], dq, dk, dv, dpin, df, P["gq"], P["gk"], bd64, P["w_main"], P["w_f"])
    g["b_forget"] = dbf[:, :FOX_HEADS]
    dw_in = jnp.concatenate([_matmul_tn(s["xn"], dz, "dw_in_main"),
                             _matmul_tn(s["xn"], dzf, "dw_in_gate")[:, :FOX_HEADS]], axis=1)
    return dh, G, g, _to_shards(dw_in, 1)


def _device_grads(x, mem, tgt, WC, WS):
    bd64 = _blockdiag_ones(FOX_HEADS, FOX_HEAD_DIM)
    bd128 = _blockdiag_ones(MEM_HEADS, MEM_HEAD_DIM)
    params = [_layer_params(WC, WS, l) for l in range(DEPTH)]
    shapes = {c: WC[c].shape for c in WC}
    h, saved = x, []
    for l in range(DEPTH):
        h, s = _layer_fwd(h, mem, params[l], bd64, bd128)
        saved.append(s)
    dh, loss = _loss_grad(h, tgt)
    G, small, dw_in = {}, [None] * DEPTH, [None] * DEPTH
    for l in reversed(range(DEPTH)):
        dh, G, small[l], dw_in[l] = _layer_bwd(dh, mem, params[l], saved[l], bd64, bd128, G, shapes)
    G["i"] = jnp.concatenate(dw_in, axis=1)
    gsmall = {n: jnp.stack([small[l][n].reshape(WS[n].shape[1:]) for l in range(DEPTH)]) for n in SMALL}
    return loss, dh, G, gsmall


def _class_slabs(shards):
    return {c: jnp.concatenate([shards[n].reshape(-1, width) for n in names]) for c, width, names in CLASSES}


def _class_rows(shards):
    where = {}
    for c, width, names in CLASSES:
        off = 0
        for n in names:
            rows = shards[n].shape[0] * shards[n].shape[1]
            where[n] = (c, off, rows)
            off += rows
    return where


def _to_shards(full, ax):
    shp = full.shape
    parts = full.reshape(shp[:ax] + (N_CHIPS, shp[ax] // N_CHIPS) + shp[ax + 1:])
    return jnp.moveaxis(parts, ax, 0)


def _from_shards(parts, ax):
    full = jnp.moveaxis(parts, 0, ax)
    shp = full.shape
    return full.reshape(shp[:ax] + (shp[ax] * shp[ax + 1],) + shp[ax + 2:])


def _pack_small(arrs):
    flat = jnp.concatenate([a.reshape(-1).astype(_F32) for a in arrs])
    rows = -(-flat.shape[0] // (8 * D_MODEL)) * 8
    return jnp.pad(flat, (0, rows * D_MODEL - flat.shape[0])).reshape(rows, D_MODEL)


def _unpack_small(flat, shapes):
    flat = flat.reshape(-1)
    out, off = [], 0
    for shp in shapes:
        n = 1
        for d in shp:
            n *= d
        out.append(flat[off:off + n].reshape(shp))
        off += n
    return out


def _mesh_pos():
    return lax.axis_index("x"), lax.axis_index("y"), lax.axis_index("c")


def _other_chips(x, y):
    return [(1 - x, y), (x, 1 - y), (1 - x, 1 - y)]


def _half_rows(ref_rows, which):
    half = ref_rows // 2
    return pl.ds(pl.multiple_of(which * half, 16), half)


def _remote(src_ref, dst_ref, send_sems, recv_sems, k, to):
    return pltpu.make_async_remote_copy(src_ref=src_ref, dst_ref=dst_ref, send_sem=send_sems.at[k],
                                        recv_sem=recv_sems.at[k], device_id=to, device_id_type=MESH)


def _allgather_weights(slabs):
    n = len(slabs)

    def body(*refs):
        srcs, outs, send_sems, recv_sems = refs[:n], refs[n:2 * n], refs[2 * n], refs[2 * n + 1]
        x, y, c = _mesh_pos()
        me = 2 * x + y
        sibling = (x, y, 1 - c)
        chips = _other_chips(x, y)
        first, passed = [], []
        for a, (src, out) in enumerate(zip(srcs, outs)):
            mine = _half_rows(src.shape[0], c)
            for j, chip in enumerate(chips):
                cp = _remote(src.at[mine], out.at[me, mine], send_sems, recv_sems, 6 * a + j, (*chip, c))
                cp.start()
                first.append(cp)
        for a, out in enumerate(outs):
            mine = _half_rows(out.shape[1], c)
            for j, (cx, cy) in enumerate(chips):
                slab = out.at[2 * cx + cy, mine]
                _remote(slab, slab, send_sems, recv_sems, 6 * a + j, (cx, cy, c)).wait_recv()
                fwd = _remote(slab, slab, send_sems, recv_sems, 6 * a + 3 + j, sibling)
                fwd.start()
                passed.append(fwd)
        for a, out in enumerate(outs):
            theirs = _half_rows(out.shape[1], 1 - c)
            for j, (cx, cy) in enumerate(chips):
                slab = out.at[2 * cx + cy, theirs]
                _remote(slab, slab, send_sems, recv_sems, 6 * a + 3 + j, sibling).wait_recv()
        for cp in first + passed:
            cp.wait_send()

    return pl.pallas_call(
        body, name="allgather_weights", in_specs=[_ANY] * n, out_specs=[_ANY] * n,
        out_shape=[_sds((N_CHIPS,) + s.shape, s.dtype) for s in slabs],
        scratch_shapes=[pltpu.SemaphoreType.DMA((6 * n,)), pltpu.SemaphoreType.DMA((6 * n,))],
    )(*slabs)


def _exchange_halves(grads):
    n = len(grads)

    def body(*refs):
        srcs, gots, send_sems, recv_sems = refs[:n], refs[n:2 * n], refs[2 * n], refs[2 * n + 1]
        x, y, c = _mesh_pos()
        copies = []
        for a, (src, got) in enumerate(zip(srcs, gots)):
            theirs = _half_rows(src.shape[1], 1 - c)
            for k in range(N_CHIPS):
                cp = _remote(src.at[k, theirs], got.at[k], send_sems, recv_sems, N_CHIPS * a + k, (x, y, 1 - c))
                cp.start()
                copies.append(cp)
        for cp in copies:
            cp.wait()

    return pl.pallas_call(
        body, name="grad_exchange_halves", in_specs=[_ANY] * n, out_specs=[_ANY] * n,
        out_shape=[_sds((N_CHIPS, g.shape[1] // 2, g.shape[2]), _F32) for g in grads],
        scratch_shapes=[pltpu.SemaphoreType.DMA((N_CHIPS * n,)), pltpu.SemaphoreType.DMA((N_CHIPS * n,))],
    )(*grads)


def _row_tile(rows):
    return _pick_tile(rows, (704, 512, 256))


def _add_halves(g, got, c_idx, name):
    _, half, width = got.shape
    ta = _row_tile(half)
    nb = half // ta

    def body(c_ref, a_ref, b_ref, o_ref):
        o_ref[...] = (a_ref[...] + b_ref[...]).astype(jnp.bfloat16)

    return pl.pallas_call(
        body, name=name,
        grid_spec=pltpu.PrefetchScalarGridSpec(
            num_scalar_prefetch=1, grid=(N_CHIPS, nb),
            in_specs=[pl.BlockSpec((1, ta, width), lambda k, i, c: (k, c[0] * nb + i, 0)),
                      pl.BlockSpec((1, ta, width), lambda k, i, c: (k, i, 0))],
            out_specs=pl.BlockSpec((1, ta, width), lambda k, i, c: (k, i, 0))),
        out_shape=_sds(got.shape, jnp.bfloat16),
    )(c_idx, g, got)


def _scatter_to_chips(parts):
    n = len(parts)

    def body(*refs):
        srcs, gots, send_sems, recv_sems = refs[:n], refs[n:2 * n], refs[2 * n], refs[2 * n + 1]
        x, y, c = _mesh_pos()
        me = 2 * x + y
        chips = _other_chips(x, y)
        copies = []
        for a, (src, got) in enumerate(zip(srcs, gots)):
            for j, (cx, cy) in enumerate(chips):
                cp = _remote(src.at[2 * cx + cy], got.at[me], send_sems, recv_sems, 3 * a + j, (cx, cy, c))
                cp.start()
                copies.append(cp)
        for a, got in enumerate(gots):
            for j, (cx, cy) in enumerate(chips):
                slab = got.at[2 * cx + cy]
                _remote(slab, slab, send_sems, recv_sems, 3 * a + j, (cx, cy, c)).wait_recv()
        for cp in copies:
            cp.wait_send()

    return pl.pallas_call(
        body, name="grad_scatter_chips", in_specs=[_ANY] * n, out_specs=[_ANY] * n,
        out_shape=[_sds(p.shape, p.dtype) for p in parts],
        scratch_shapes=[pltpu.SemaphoreType.DMA((3 * n,)), pltpu.SemaphoreType.DMA((3 * n,))],
    )(*parts)


def _sum_chips(got, c_idx, name):
    _, half, width = got.shape
    ta = _row_tile(half)
    nb = half // ta

    def body(c_ref, a_ref, o_ref):
        f = lambda k: a_ref[k].astype(_F32)
        o_ref[...] = ((f(0) + f(1)) + f(2)) + f(3)

    return pl.pallas_call(
        body, name=name,
        grid_spec=pltpu.PrefetchScalarGridSpec(
            num_scalar_prefetch=1, grid=(nb,),
            in_specs=[pl.BlockSpec((N_CHIPS, ta, width), lambda i, c: (0, i, 0))],
            out_specs=pl.BlockSpec((ta, width), lambda i, c: (c[0] * nb + i, 0))),
        out_shape=_sds((2 * half, width), _F32),
    )(c_idx, got)


def _share_with_sibling(bufs):
    n = len(bufs)

    def body(*refs):
        outs, send_sems, recv_sems = refs[n:2 * n], refs[2 * n], refs[2 * n + 1]
        x, y, c = _mesh_pos()
        copies = []
        for a, out in enumerate(outs):
            mine = out.at[_half_rows(out.shape[0], c)]
            cp = _remote(mine, mine, send_sems, recv_sems, a, (x, y, 1 - c))
            cp.start()
            copies.append(cp)
        for a, out in enumerate(outs):
            theirs = out.at[_half_rows(out.shape[0], 1 - c)]
            _remote(theirs, theirs, send_sems, recv_sems, a, (x, y, 1 - c)).wait_recv()
        for cp in copies:
            cp.wait_send()

    return pl.pallas_call(
        body, name="grad_share_sibling", in_specs=[_ANY] * n, out_specs=[_ANY] * n,
        out_shape=[_sds(b.shape, _F32) for b in bufs], input_output_aliases={a: a for a in range(n)},
        scratch_shapes=[pltpu.SemaphoreType.DMA((n,)), pltpu.SemaphoreType.DMA((n,))],
    )(*bufs)


def _allreduce_small(g):
    rows = g.shape[0]
    n_dev = 2 * N_CHIPS

    def body(g_ref, out_ref, gathered, local_sem, send_sems, recv_sems):
        x, y, c = _mesh_pos()
        me = 4 * x + 2 * y + c
        own = pltpu.make_async_copy(g_ref, gathered.at[me], local_sem)
        own.start()
        copies = []
        for k in range(1, n_dev):
            fx, fy, fc = (k >> 2) & 1, (k >> 1) & 1, k & 1
            cp = pltpu.make_async_remote_copy(
                src_ref=g_ref, dst_ref=gathered.at[me], send_sem=send_sems.at[k - 1], recv_sem=recv_sems.at[k - 1],
                device_id=(x ^ fx, y ^ fy, c ^ fc), device_id_type=MESH)
            cp.start()
            copies.append(cp)
        for k in range(1, n_dev):
            fx, fy, fc = (k >> 2) & 1, (k >> 1) & 1, k & 1
            px, py, pc = x ^ fx, y ^ fy, c ^ fc
            slab = gathered.at[4 * px + 2 * py + pc]
            pltpu.make_async_remote_copy(src_ref=slab, dst_ref=slab, send_sem=send_sems.at[k - 1],
                                         recv_sem=recv_sems.at[k - 1], device_id=(px, py, pc),
                                         device_id_type=MESH).wait_recv()
        for cp in copies:
            cp.wait_send()
        own.wait()
        acc = gathered[0]
        for d in range(1, n_dev):
            acc = acc + gathered[d]
        out_ref[...] = acc

    vmem = pl.BlockSpec(memory_space=pltpu.VMEM)
    return pl.pallas_call(
        body, name="allreduce_small", in_specs=[vmem], out_specs=vmem, out_shape=_sds((rows, D_MODEL), _F32),
        scratch_shapes=[pltpu.VMEM((n_dev, rows, D_MODEL), _F32), pltpu.SemaphoreType.DMA,
                        pltpu.SemaphoreType.DMA((n_dev - 1,)), pltpu.SemaphoreType.DMA((n_dev - 1,))],
    )(g)


def _adamw(w, g, m, v, name, g_first_row=0):
    shape = w.shape
    cols = shape[-1]
    rows = 1
    for d in shape[:-1]:
        rows *= d
    w2, m2, v2 = (a.reshape(rows, cols) for a in (w, m, v))
    tr = _pick_tile(rows, (256, 128, 64, 32, 16, 8))
    g0 = g_first_row // tr

    def body(w_ref, g_ref, m_ref, v_ref, go_ref, d_ref, nm_ref, nv_ref):
        gg = g_ref[...]
        go_ref[...] = gg
        nm = ADAM_B1 * m_ref[...] + (1.0 - ADAM_B1) * gg
        nv = ADAM_B2 * v_ref[...] + (1.0 - ADAM_B2) * (gg * gg)
        m_hat = nm / (1.0 - ADAM_B1 ** ADAM_STEP)
        v_hat = nv / (1.0 - ADAM_B2 ** ADAM_STEP)
        d_ref[...] = -ADAM_LR * (m_hat / (jnp.sqrt(v_hat) + ADAM_EPS) + ADAM_WD * w_ref[...])
        nm_ref[...] = nm
        nv_ref[...] = nv

    spec = pl.BlockSpec((tr, cols), lambda i: (i, 0))
    outs = pl.pallas_call(
        body, name=name, grid=(rows // tr,),
        in_specs=[spec, pl.BlockSpec((tr, cols), lambda i: (g0 + i, 0)), spec, spec], out_specs=[spec] * 4,
        out_shape=[_sds((rows, cols), _F32)] * 4,
    )(w2, g, m2, v2)
    return tuple(o.reshape(shape) for o in outs)


def kernel(x, mem, g_mix, w_in, b_forget, g_q_fox, g_k_fox, w_pool, pool_scale, w_out, g_mem_q, g_mem_kv, w_mem_q, w_mem_kv, g_q_mem, g_k_mem, w_mem_out, g_ffn, w_gate_up, w_down, loss_target, m_g_mix, m_w_in, m_b_forget, m_g_q_fox, m_g_k_fox, m_w_pool, m_pool_scale, m_w_out, m_g_mem_q, m_g_mem_kv, m_w_mem_q, m_w_mem_kv, m_g_q_mem, m_g_k_mem, m_w_mem_out, m_g_ffn, m_w_gate_up, m_w_down, v_g_mix, v_w_in, v_b_forget, v_g_q_fox, v_g_k_fox, v_w_pool, v_pool_scale, v_w_out, v_g_mem_q, v_g_mem_kv, v_w_mem_q, v_w_mem_kv, v_g_q_mem, v_g_k_mem, v_w_mem_out, v_g_ffn, v_w_gate_up, v_w_down):
    w = dict(g_mix=g_mix, w_in=w_in, b_forget=b_forget, g_q_fox=g_q_fox, g_k_fox=g_k_fox, w_pool=w_pool,
             pool_scale=pool_scale, w_out=w_out, g_mem_q=g_mem_q, g_mem_kv=g_mem_kv, w_mem_q=w_mem_q,
             w_mem_kv=w_mem_kv, g_q_mem=g_q_mem, g_k_mem=g_k_mem, w_mem_out=w_mem_out, g_ffn=g_ffn,
             w_gate_up=w_gate_up, w_down=w_down)
    m = dict(g_mix=m_g_mix, w_in=m_w_in, b_forget=m_b_forget, g_q_fox=m_g_q_fox, g_k_fox=m_g_k_fox, w_pool=m_w_pool,
             pool_scale=m_pool_scale, w_out=m_w_out, g_mem_q=m_g_mem_q, g_mem_kv=m_g_mem_kv, w_mem_q=m_w_mem_q,
             w_mem_kv=m_w_mem_kv, g_q_mem=m_g_q_mem, g_k_mem=m_g_k_mem, w_mem_out=m_w_mem_out, g_ffn=m_g_ffn,
             w_gate_up=m_w_gate_up, w_down=m_w_down)
    v = dict(g_mix=v_g_mix, w_in=v_w_in, b_forget=v_b_forget, g_q_fox=v_g_q_fox, g_k_fox=v_g_k_fox, w_pool=v_w_pool,
             pool_scale=v_pool_scale, w_out=v_w_out, g_mem_q=v_g_mem_q, g_mem_kv=v_g_mem_kv, w_mem_q=v_w_mem_q,
             w_mem_kv=v_w_mem_kv, g_q_mem=v_g_q_mem, g_k_mem=v_g_k_mem, w_mem_out=v_w_mem_out, g_ffn=v_g_ffn,
             w_gate_up=v_w_gate_up, w_down=v_w_down)

    classes = [c for c, _, _ in CLASSES]
    chip = 2 * lax.axis_index("x") + lax.axis_index("y")
    c_idx = lax.axis_index("c").astype(jnp.int32).reshape(1)
    own_slab = lambda bufs, own: lax.dynamic_update_slice(bufs, own[None], (chip, 0, 0))

    slabs = _class_slabs({n: w[n].astype(_MXU) for n in BIG})
    gathered = _allgather_weights([slabs[c] for c in classes])
    WC = {c: own_slab(buf, slabs[c]) for c, buf in zip(classes, gathered)}

    loss, grad_x, G, gsmall = _device_grads(x[0], mem[0], loss_target[0], WC, {n: w[n] for n in SMALL})
    loss = lax.psum(loss[0, 0], ("x", "y", "c"))

    grads = [G[c] for c in classes]
    partial = [_add_halves(g, got, c_idx, "grad_add_halves_" + c)
               for c, g, got in zip(classes, grads, _exchange_halves(grads))]
    landed = _scatter_to_chips(partial)
    partials = [own_slab(got, lax.dynamic_index_in_dim(p, chip, 0, keepdims=False)) for got, p in zip(landed, partial)]
    reduced = _share_with_sibling([_sum_chips(p, c_idx, "grad_sum_chips_" + c) for c, p in zip(classes, partials)])
    reduced = dict(zip(classes, reduced))

    small_shapes = [w[n].shape for n in SMALL]
    gsmall = _unpack_small(_allreduce_small(_pack_small([gsmall[n] for n in SMALL])), small_shapes)

    g_out, d_out, m_out, v_out = {}, {}, {}, {}
    for n, (c, first, _) in _class_rows({n: w[n] for n in BIG}).items():
        g_out[n], d_out[n], m_out[n], v_out[n] = _adamw(w[n], reduced[c], m[n], v[n], "adamw_" + n, first)
    packed = [_pack_small([t[n] for n in SMALL]) for t in (w, m, v)]
    _, ds, ms, vs = _adamw(packed[0], _pack_small(gsmall), packed[1], packed[2], "adamw_small")
    for n, gi, di, mi, vi in zip(SMALL, gsmall, _unpack_small(ds, small_shapes), _unpack_small(ms, small_shapes),
                                 _unpack_small(vs, small_shapes)):
        g_out[n], d_out[n], m_out[n], v_out[n] = gi, di, mi, vi

    return (loss, grad_x[None], *[g_out[n] for n in WEIGHTS], *[d_out[n] for n in WEIGHTS],
            *[m_out[n] for n in WEIGHTS], *[v_out[n] for n in WEIGHTS])
```

```python
import functools

import jax
import jax.numpy as jnp
import numpy as np
from jax import lax
from jax.experimental import pallas as pl
from jax.experimental.pallas import tpu as pltpu

_F32 = jnp.float32
_MXU = jnp.bfloat16

D_MODEL = 1024
DEPTH = 2
FOX_HEADS = 8
FOX_HEAD_DIM = 64
FOX_WIDTH = FOX_HEADS * FOX_HEAD_DIM
POOL_WINDOWS = (2, 4, 8, 16)
POOL_GROUP_DIM = 128
POOL_WIDTH = len(POOL_WINDOWS) * POOL_GROUP_DIM
MEM_HEADS = 4
MEM_HEAD_DIM = 128
MEM_WIDTH = MEM_HEADS * MEM_HEAD_DIM
D_FF = 2816
EPS = 1e-6
LANES = 128
HALO = 16

ADAM_LR = 0.001
ADAM_B1 = 0.9
ADAM_B2 = 0.999
ADAM_EPS = 1e-08
ADAM_WD = 0.01
ADAM_STEP = 10

N_CHIPS = 4
MESH = pl.DeviceIdType.MESH

_TM = 512
_TQ = 1024
_TMF = 256
_TF = 1408
_TT = 1024
_TB = 256
_TA = 512

BIG = ("w_in", "w_out", "w_mem_q", "w_mem_kv", "w_mem_out", "w_gate_up", "w_down")
SHARD_AXIS = {"w_in": 2, "w_out": 1, "w_mem_q": 1, "w_mem_kv": 1, "w_mem_out": 2, "w_gate_up": 2, "w_down": 1}
SMALL = ("g_mix", "b_forget", "g_q_fox", "g_k_fox", "w_pool", "pool_scale", "g_mem_q", "g_mem_kv", "g_q_mem",
         "g_k_mem", "g_ffn")
WEIGHTS = ("g_mix", "w_in", "b_forget", "g_q_fox", "g_k_fox", "w_pool", "pool_scale", "w_out", "g_mem_q", "g_mem_kv",
           "w_mem_q", "w_mem_kv", "g_q_mem", "g_k_mem", "w_mem_out", "g_ffn", "w_gate_up", "w_down")


def _dot(a, b):
    return jnp.dot(a, b, preferred_element_type=_F32)


def _dot_nt(a, b):
    return lax.dot_general(a, b, (((1,), (1,)), ((), ())), preferred_element_type=_F32)


def _dot_tn(a, b):
    return lax.dot_general(a, b, (((0,), (0,)), ((), ())), preferred_element_type=_F32)


def _group_sum(x, ones_blockdiag):
    hi = x.astype(_MXU)
    lo = (x - hi.astype(_F32)).astype(_MXU)
    return _dot(hi, ones_blockdiag) + _dot(lo, ones_blockdiag)


def _tri_dot(tri, x):
    h1 = x.astype(jnp.bfloat16)
    r1 = x - h1.astype(_F32)
    h2 = r1.astype(jnp.bfloat16)
    h3 = (r1 - h2.astype(_F32)).astype(jnp.bfloat16)
    return _dot(tri, h1) + _dot(tri, h2) + _dot(tri, h3)


def _rstd(x):
    return lax.rsqrt(jnp.mean(x * x, axis=-1, keepdims=True) + EPS)


def _norm_bwd(dy, x, g):
    r = _rstd(x)
    xhat = x * r
    u = dy * g
    dx = r * (u - xhat * jnp.mean(u * xhat, axis=-1, keepdims=True))
    return dx, jnp.sum(dy * xhat, axis=0, keepdims=True)


def _headnorm_bwd(dy, x, g, ones_blockdiag, width):
    r = lax.rsqrt(_group_sum(x * x, ones_blockdiag) * (1.0 / width) + EPS)
    xhat = x * r
    u = dy * g
    dx = r * (u - xhat * (_group_sum(u * xhat, ones_blockdiag) * (1.0 / width)))
    return dx, jnp.sum(dy * xhat, axis=0, keepdims=True)


def _fold_heads(row, heads, width):
    acc = row[:, 0:width]
    for h in range(1, heads):
        acc = acc + row[:, h * width:(h + 1) * width]
    return acc


_ANY = pl.BlockSpec(memory_space=pl.ANY)


def _resident(shape):
    return pl.BlockSpec(shape, lambda *_: (0,) * len(shape), pipeline_mode=pl.Buffered(1))


def _rows(tm, width):
    return pl.BlockSpec((tm, width), lambda i: (i, 0))


def _blockdiag_ones(groups, width):
    return jnp.kron(jnp.eye(groups, dtype=_F32), jnp.ones((width, width), _F32)).astype(_MXU)


def _sds(shape, dtype):
    return jax.ShapeDtypeStruct(shape, dtype)


def _mix_in_fwd(h, g_mix, w_main, w_f, b_f, gq, gk, bd64):
    T = h.shape[0]
    tm = min(_TM, T)

    def body(h_ref, g_ref, wm_ref, wf_ref, bf_ref, gq_ref, gk_ref, bd_ref,
             xn_ref, zqk_ref, qn_ref, kn_ref, v_ref, pin_ref, fl_ref):
        x = h_ref[...]
        xn = ((x * _rstd(x)) * g_ref[...]).astype(_MXU)
        xn_ref[...] = xn
        z = _dot(xn, wm_ref[...])
        q = z[:, :FOX_WIDTH]
        k = z[:, FOX_WIDTH:2 * FOX_WIDTH]
        zqk_ref[...] = z[:, :2 * FOX_WIDTH]
        bd = bd_ref[...]
        rq = lax.rsqrt(_group_sum(q * q, bd) * (1.0 / FOX_HEAD_DIM) + EPS)
        rk = lax.rsqrt(_group_sum(k * k, bd) * (1.0 / FOX_HEAD_DIM) + EPS)
        qn_ref[...] = ((q * rq) * gq_ref[...]).astype(_MXU)
        kn_ref[...] = ((k * rk) * gk_ref[...]).astype(_MXU)
        v_ref[...] = z[:, 2 * FOX_WIDTH:3 * FOX_WIDTH].astype(_MXU)
        pin_ref[...] = z[:, 3 * FOX_WIDTH:]
        fl_ref[...] = _dot(xn, wf_ref[...]) + bf_ref[...]

    return pl.pallas_call(
        body, name="mix_in_fwd", grid=(T // tm,),
        in_specs=[_rows(tm, D_MODEL), _resident((1, D_MODEL)), _resident(w_main.shape), _resident(w_f.shape),
                  _resident((1, LANES)), _resident((1, FOX_WIDTH)), _resident((1, FOX_WIDTH)), _resident(bd64.shape)],
        out_specs=[_rows(tm, D_MODEL), _rows(tm, 2 * FOX_WIDTH), _rows(tm, FOX_WIDTH), _rows(tm, FOX_WIDTH),
                   _rows(tm, FOX_WIDTH), _rows(tm, POOL_WIDTH), _rows(tm, LANES)],
        out_shape=[_sds((T, D_MODEL), _MXU), _sds((T, 2 * FOX_WIDTH), _F32), _sds((T, FOX_WIDTH), _MXU),
                   _sds((T, FOX_WIDTH), _MXU), _sds((T, FOX_WIDTH), _MXU), _sds((T, POOL_WIDTH), _F32),
                   _sds((T, LANES), _F32)],
    )(h, g_mix, w_main, w_f, b_f, gq, gk, bd64)


def _split3(x):
    h1 = x.astype(jnp.bfloat16).astype(_F32)
    r1 = x - h1
    h2 = r1.astype(jnp.bfloat16).astype(_F32)
    h3 = (r1 - h2).astype(jnp.bfloat16).astype(_F32)
    return h1, h2, h3


_ONES3 = (1.0, 1.0, 1.0)
_ZEROS3 = (0.0, 0.0, 0.0)


def _aug_head(feat, first, second):
    rows = feat.shape[0]
    lane = lax.broadcasted_iota(jnp.int32, (rows, LANES - FOX_HEAD_DIM), 1)
    aux = jnp.zeros((rows, LANES - FOX_HEAD_DIM), _F32)
    for k in range(3):
        aux = jnp.where(lane == k, first[k], aux)
        aux = jnp.where(lane == 3 + k, second[k], aux)
    return jnp.concatenate([feat.astype(_MXU), aux.astype(_MXU)], axis=-1)


def _gate_fwd(fl, qn, kn, v):
    T = fl.shape[0]
    tb = min(_TB, T)
    wide = FOX_HEADS * LANES

    def body(fl_ref, q_ref, k_ref, v_ref, qa_ref, ka_ref, va_ref, carry):
        @pl.when(pl.program_id(0) == 0)
        def _():
            carry[...] = jnp.zeros_like(carry)

        x = fl_ref[...]
        ls = jnp.minimum(x, 0.0) - jnp.log1p(jnp.exp(-jnp.abs(x)))
        row = lax.broadcasted_iota(jnp.int32, (tb, tb), 0)
        col = lax.broadcasted_iota(jnp.int32, (tb, tb), 1)
        tri = jnp.where(col <= row, 1.0, 0.0).astype(jnp.bfloat16)
        cs = _tri_dot(tri, ls) + carry[...]
        carry[...] = cs[tb - 1:tb, :]
        for h in range(FOX_HEADS):
            sl = slice(h * FOX_HEAD_DIM, (h + 1) * FOX_HEAD_DIM)
            out = slice(h * LANES, (h + 1) * LANES)
            c3 = _split3(cs[:, h:h + 1])
            qa_ref[:, out] = _aug_head(q_ref[:, sl].astype(_F32) * (FOX_HEAD_DIM ** -0.5), c3, _ONES3)
            ka_ref[:, out] = _aug_head(k_ref[:, sl], _ONES3, tuple(-t for t in c3))
            va_ref[:, out] = _aug_head(v_ref[:, sl], _ONES3, _ZEROS3)

    return pl.pallas_call(
        body, name="gate_fwd", grid=(T // tb,),
        in_specs=[_rows(tb, LANES), _rows(tb, FOX_WIDTH), _rows(tb, FOX_WIDTH), _rows(tb, FOX_WIDTH)],
        out_specs=[_rows(tb, wide)] * 3, out_shape=[_sds((T, wide), _MXU)] * 3,
        scratch_shapes=[pltpu.VMEM((1, LANES), _F32)],
    )(fl, qn, kn, v)


def _gate_bwd(drs, dcs, fl):
    T = fl.shape[0]
    tb = min(_TB, T)
    nb = T // tb

    def body(r_ref, d_ref, fl_ref, df_ref, carry):
        @pl.when(pl.program_id(0) == 0)
        def _():
            carry[...] = jnp.zeros_like(carry)

        row = lax.broadcasted_iota(jnp.int32, (tb, tb), 0)
        col = lax.broadcasted_iota(jnp.int32, (tb, tb), 1)
        tri = jnp.where(col >= row, 1.0, 0.0).astype(jnp.bfloat16)
        rc = _tri_dot(tri, r_ref[...] - d_ref[...]) + carry[...]
        carry[...] = rc[0:1, :]
        df_ref[...] = rc * (1.0 / (1.0 + jnp.exp(fl_ref[...])))

    rev = pl.BlockSpec((tb, LANES), lambda i: (nb - 1 - i, 0))
    return pl.pallas_call(
        body, name="gate_bwd", grid=(nb,), in_specs=[rev, rev, rev], out_specs=rev,
        out_shape=_sds((T, LANES), _F32), scratch_shapes=[pltpu.VMEM((1, LANES), _F32)],
    )(drs, dcs, fl)


def _fox_scores(q_ref, k_ref, hh, masked):
    sl = slice(LANES * hh, LANES * (hh + 1))
    s = _dot_nt(q_ref[:, sl], k_ref[:, sl])
    if masked:
        row = lax.broadcasted_iota(jnp.int32, s.shape, 0)
        col = lax.broadcasted_iota(jnp.int32, s.shape, 1)
        s = jnp.where(col <= row, s, -jnp.inf)
    return s


def _causal_steps(nq, query_major):
    if query_major:
        steps = [(i, j) for i in range(nq) for j in range(i + 1)]
    else:
        steps = [(i, j) for j in range(nq) for i in range(j, nq)]
    return (jnp.asarray(np.array([s[0] for s in steps], np.int32)),
            jnp.asarray(np.array([s[1] for s in steps], np.int32)))


def _two_lanes(a, b):
    lane = lax.broadcasted_iota(jnp.int32, (a.shape[0], LANES), 1)
    return jnp.where(lane == 0, a, jnp.where(lane == 1, b, 0.0))


def _fox_fwd(qa, ka, va):
    T = qa.shape[0]
    tq = min(_TQ, T)
    nq = T // tq
    ii, jj = _causal_steps(nq, True)
    steps = int(ii.shape[0])

    def body(ii_ref, jj_ref, q_ref, k_ref, v_ref, o_ref, p_ref, mb_ref, lse_ref, m_s, acc_s):
        t = pl.program_id(1)
        i = ii_ref[t]
        j = jj_ref[t]

        @pl.when(j == 0)
        def _():
            m_s[...] = jnp.full(m_s.shape, -jnp.inf, _F32)
            acc_s[...] = jnp.zeros_like(acc_s)

        def step(masked):
            for hh in range(2):
                sl = slice(LANES * hh, LANES * (hh + 1))
                st = _dot_nt(k_ref[:, sl], q_ref[:, sl])
                if masked:
                    key = lax.broadcasted_iota(jnp.int32, st.shape, 0)
                    qry = lax.broadcasted_iota(jnp.int32, st.shape, 1)
                    st = jnp.where(key <= qry, st, -jnp.inf)
                top = jnp.broadcast_to(jnp.max(st, axis=0, keepdims=True), (LANES, tq)).T[:, :1]
                s = _fox_scores(q_ref, k_ref, hh, masked)
                m_prev = m_s[hh]
                m_new = jnp.maximum(m_prev, top)
                p_ref[0, 0, hh] = jnp.exp(s - m_new).astype(_MXU)
                acc_s[hh] = (jnp.exp(m_prev - m_new) * acc_s[hh]
                             + _dot(p_ref[0, 0, hh], v_ref[:, LANES * hh:LANES * (hh + 1)]))
                m_s[hh] = m_new
            mb_ref[0, 0] = _two_lanes(m_s[0], m_s[1])

        @pl.when(j < i)
        def _():
            step(False)

        @pl.when(j == i)
        def _():
            step(True)
            outs, lses = [], []
            for hh in range(2):
                acc = acc_s[hh]
                l = acc[:, FOX_HEAD_DIM:FOX_HEAD_DIM + 1]
                outs.append(acc[:, :FOX_HEAD_DIM] / l)
                lses.append(m_s[hh] + jnp.log(l))
            o_ref[...] = jnp.concatenate(outs, axis=-1).astype(_MXU)
            lse_ref[0] = _two_lanes(lses[0], lses[1])

    qspec = pl.BlockSpec((tq, 2 * LANES), lambda p, t, ii, jj: (ii[t], p))
    kspec = pl.BlockSpec((tq, 2 * LANES), lambda p, t, ii, jj: (jj[t], p))
    return pl.pallas_call(
        body, name="fox_fwd",
        grid_spec=pltpu.PrefetchScalarGridSpec(
            num_scalar_prefetch=2, grid=(FOX_HEADS // 2, steps),
            in_specs=[qspec, kspec, kspec],
            out_specs=[pl.BlockSpec((tq, 2 * FOX_HEAD_DIM), lambda p, t, ii, jj: (ii[t], p)),
                       pl.BlockSpec((1, 1, 2, tq, tq), lambda p, t, ii, jj: (p, t, 0, 0, 0)),
                       pl.BlockSpec((1, 1, tq, LANES), lambda p, t, ii, jj: (p, t, 0, 0)),
                       pl.BlockSpec((1, tq, LANES), lambda p, t, ii, jj: (p, ii[t], 0))],
            scratch_shapes=[pltpu.VMEM((2, tq, 1), _F32), pltpu.VMEM((2, tq, LANES), _F32)]),
        out_shape=[_sds((T, FOX_WIDTH), _MXU), _sds((FOX_HEADS // 2, steps, 2, tq, tq), _MXU),
                   _sds((FOX_HEADS // 2, steps, tq, LANES), _F32), _sds((FOX_HEADS // 2, T, LANES), _F32)],
    )(ii, jj, qa, ka, va)


def _pool_window_sum(ext, w, forward):
    n = ext.shape[0]
    sm = ext
    k = 1
    while k < w:
        sm = sm + pltpu.roll(sm, (n - k) if forward else k, axis=0)
        k *= 2
    return sm


def _out_proj_fwd(h, o, pin, w_pool, pscale, w_out):
    T = h.shape[0]
    tm = min(_TM, T)
    hb = tm // HALO

    def body(h_ref, o_ref, pin_ref, halo_ref, wp_ref, ps_ref, wo_ref, h1_ref, mixed_ref, y_ref):
        i = pl.program_id(0)
        pin_t = pin_ref[...]
        halo = jnp.where(i == 0, 0.0, halo_ref[...])
        ext = jnp.concatenate([halo, pin_t], axis=0)
        t = (i * tm + lax.broadcasted_iota(jnp.int32, (tm, 1), 0) + 1).astype(_F32)
        mixed, ys = [], []
        for g, w in enumerate(POOL_WINDOWS):
            sl = slice(g * POOL_GROUP_DIM, (g + 1) * POOL_GROUP_DIM)
            win = _pool_window_sum(ext[:, sl], w, False)[HALO:, :]
            mg = (win / jnp.minimum(t, float(w)) - pin_t[:, sl]).astype(_MXU)
            mixed.append(mg)
            ys.append(_dot(mg, wp_ref[g]))
        mixed_ref[...] = jnp.concatenate(mixed, axis=-1)
        y = (jnp.concatenate(ys, axis=-1) * ps_ref[...]).astype(_MXU)
        y_ref[...] = y
        h1_ref[...] = h_ref[...] + _dot(o_ref[...], wo_ref[:FOX_WIDTH, :]) + _dot(y, wo_ref[FOX_WIDTH:, :])

    return pl.pallas_call(
        body, name="out_proj_fwd", grid=(T // tm,),
        in_specs=[_rows(tm, D_MODEL), _rows(tm, FOX_WIDTH), _rows(tm, POOL_WIDTH),
                  pl.BlockSpec((HALO, POOL_WIDTH), lambda i: (jnp.maximum(i * hb - 1, 0), 0)),
                  _resident(w_pool.shape), _resident((1, POOL_WIDTH)), _resident(w_out.shape)],
        out_specs=[_rows(tm, D_MODEL), _rows(tm, POOL_WIDTH), _rows(tm, POOL_WIDTH)],
        out_shape=[_sds((T, D_MODEL), _F32), _sds((T, POOL_WIDTH), _MXU), _sds((T, POOL_WIDTH), _MXU)],
    )(h, o, pin, pin, w_pool, pscale, w_out)


def _mem_kv_fwd(mem, g_kv, w_kv, gkm, bd128):
    M = mem.shape[0]

    def body(mem_ref, g_ref, w_ref, gk_ref, bd_ref, mn_ref, mkv_ref, mk_ref, mv_ref):
        x = mem_ref[...]
        mn = ((x * _rstd(x)) * g_ref[...]).astype(_MXU)
        mn_ref[...] = mn
        z = _dot(mn, w_ref[...])
        mkv_ref[...] = z
        k = z[:, :MEM_WIDTH]
        rk = lax.rsqrt(_group_sum(k * k, bd_ref[...]) * (1.0 / MEM_HEAD_DIM) + EPS)
        mk_ref[...] = ((k * rk) * gk_ref[...]).astype(_MXU)
        mv_ref[...] = z[:, MEM_WIDTH:].astype(_MXU)

    return pl.pallas_call(
        body, name="mem_kv_fwd",
        out_shape=[_sds((M, D_MODEL), _MXU), _sds((M, 2 * MEM_WIDTH), _F32), _sds((M, MEM_WIDTH), _MXU),
                   _sds((M, MEM_WIDTH), _MXU)],
    )(mem, g_kv, w_kv, gkm, bd128)


def _mem_softmax(qn, mk_ref, hd):
    sl = slice(hd * MEM_HEAD_DIM, (hd + 1) * MEM_HEAD_DIM)
    s = _dot_nt(qn[:, sl], mk_ref[:, sl]) * (MEM_HEAD_DIM ** -0.5)
    e = jnp.exp(s - jnp.max(s, axis=-1, keepdims=True))
    return e / jnp.sum(e, axis=-1, keepdims=True)


def _mem_attn_fwd(h1, g_q, w_q, gqm, bd128, mk, mv, w_mo):
    T = h1.shape[0]
    tm = min(_TM, T)

    def body(h_ref, g_ref, wq_ref, gq_ref, bd_ref, mk_ref, mv_ref, wo_ref, h2_ref, hn_ref, mo_ref):
        x = h_ref[...]
        hn = ((x * _rstd(x)) * g_ref[...]).astype(_MXU)
        hn_ref[...] = hn
        mq = _dot(hn, wq_ref[...])
        rq = lax.rsqrt(_group_sum(mq * mq, bd_ref[...]) * (1.0 / MEM_HEAD_DIM) + EPS)
        qn = ((mq * rq) * gq_ref[...]).astype(_MXU)
        outs = []
        for hd in range(MEM_HEADS):
            p = _mem_softmax(qn, mk_ref, hd).astype(_MXU)
            outs.append(_dot(p, mv_ref[:, hd * MEM_HEAD_DIM:(hd + 1) * MEM_HEAD_DIM]))
        mo = jnp.concatenate(outs, axis=-1).astype(_MXU)
        mo_ref[...] = mo
        h2_ref[...] = x + _dot(mo, wo_ref[...])

    return pl.pallas_call(
        body, name="mem_attn_fwd", grid=(T // tm,),
        in_specs=[_rows(tm, D_MODEL), _resident((1, D_MODEL)), _resident(w_q.shape), _resident((1, MEM_WIDTH)),
                  _resident(bd128.shape), _resident(mk.shape), _resident(mv.shape), _resident(w_mo.shape)],
        out_specs=[_rows(tm, D_MODEL), _rows(tm, D_MODEL), _rows(tm, MEM_WIDTH)],
        out_shape=[_sds((T, D_MODEL), _F32), _sds((T, D_MODEL), _MXU), _sds((T, MEM_WIDTH), _MXU)],
    )(h1, g_q, w_q, gqm, bd128, mk, mv, w_mo)


def _ffn_weight_specs(layer):
    nf = D_FF // _TF
    return [pl.BlockSpec((1, D_MODEL, _TF), lambda i, j: (j, layer, 0)),
            pl.BlockSpec((1, D_MODEL, _TF), lambda i, j: (j + nf, layer, 0)),
            pl.BlockSpec((2, _TF // 2, D_MODEL), lambda i, j: (j, layer, 0))]


def _ffn_fwd(h2, g_ffn, w_gu, w_d, layer):
    T = h2.shape[0]
    tm = min(_TMF, T)
    nf = D_FF // _TF

    def body(h_ref, g_ref, wg_ref, wu_ref, wd_ref, h3_ref, hn_ref, acc, xn_s):
        j = pl.program_id(1)

        @pl.when(j == 0)
        def _():
            x = h_ref[...]
            xn = ((x * _rstd(x)) * g_ref[...]).astype(_MXU)
            xn_s[...] = xn
            hn_ref[...] = xn
            acc[...] = jnp.zeros_like(acc)

        xn = xn_s[...]
        g = _dot(xn, wg_ref[0])
        u = _dot(xn, wu_ref[0])
        a = ((g * jax.nn.sigmoid(g)) * u).astype(_MXU)
        acc[...] += _dot(a, wd_ref[...].reshape(_TF, D_MODEL))

        @pl.when(j == nf - 1)
        def _():
            h3_ref[...] = h_ref[...] + acc[...]

    tok = pl.BlockSpec((tm, D_MODEL), lambda i, j: (i, 0))
    return pl.pallas_call(
        body, name="ffn_fwd", grid=(T // tm, nf),
        in_specs=[tok, pl.BlockSpec((1, D_MODEL), lambda i, j: (0, 0))] + _ffn_weight_specs(layer),
        out_specs=[tok, tok],
        out_shape=[_sds((T, D_MODEL), _F32), _sds((T, D_MODEL), _MXU)],
        scratch_shapes=[pltpu.VMEM((tm, D_MODEL), _F32), pltpu.VMEM((tm, D_MODEL), _MXU)],
    )(h2, g_ffn, w_gu, w_gu, w_d)


def _loss_grad(y, tgt):
    T = y.shape[0]
    tm = min(_TA, T)

    def body(y_ref, t_ref, dy_ref, loss_ref):
        @pl.when(pl.program_id(0) == 0)
        def _():
            loss_ref[...] = jnp.zeros_like(loss_ref)

        err = y_ref[...] - t_ref[...]
        dy_ref[...] = err * (1.0 / D_MODEL)
        part = jnp.sum(jnp.sum(err * err, axis=0, keepdims=True), axis=1, keepdims=True)
        loss_ref[...] += part * (0.5 / D_MODEL)

    return pl.pallas_call(
        body, name="loss_grad", grid=(T // tm,), in_specs=[_rows(tm, D_MODEL), _rows(tm, D_MODEL)],
        out_specs=[_rows(tm, D_MODEL), pl.BlockSpec((1, 1), lambda i: (0, 0))],
        out_shape=[_sds((T, D_MODEL), _F32), _sds((1, 1), _F32)],
    )(y, tgt)


def _pick_tile(n, candidates=(1408, 1024, 512, 256, 128)):
    for c in candidates:
        if n % c == 0:
            return c
    return n


def _matmul_tn(a, b, name, tk=None, tn=None, dst=None, dst_shape=None, dst_index=None, dst_chips=1):
    T, K = a.shape
    N = b.shape[1]
    tk, tn, tt = tk or _pick_tile(K), tn or _pick_tile(N), min(_TT, T)

    def body(*refs):
        a_ref, b_ref, o_ref = refs[0], refs[1], refs[-1]

        @pl.when(pl.program_id(2) == 0)
        def _():
            o_ref[...] = jnp.zeros_like(o_ref)

        o_ref[...] += _dot_tn(a_ref[...].astype(_MXU), b_ref[...].astype(_MXU)).reshape(o_ref.shape)

    in_specs = [pl.BlockSpec((tt, tk), lambda i, j, t: (t, i)), pl.BlockSpec((tt, tn), lambda i, j, t: (t, j))]
    if dst_shape is None:
        out_spec, out_shape, args, alias = pl.BlockSpec((tk, tn), lambda i, j, t: (i, j)), (K, N), (a, b), {}
    else:
        out_spec = pl.BlockSpec((dst_chips, tk // dst_chips, tn), lambda i, j, t: dst_index(i, j))
        out_shape, args, alias = dst_shape, (a, b), {}
        if dst is not None:
            in_specs, args, alias = in_specs + [_ANY], (a, b, dst), {2: 0}
    return pl.pallas_call(
        body, name=name, grid=(K // tk, N // tn, T // tt), in_specs=in_specs, out_specs=out_spec,
        out_shape=_sds(out_shape, _F32), input_output_aliases=alias,
    )(*args)


def _ffn_bwd(dh3, h2, hn, g_ffn, w_gu, w_d, layer):
    T = h2.shape[0]
    tm = min(_TMF, T)
    nf = D_FF // _TF

    def body(dh_ref, h_ref, hn_ref, g_ref, wg_ref, wu_ref, wd_ref, dh2_ref, act_ref, dg_ref, du_ref, dgain_ref,
             acc, dyb):
        i = pl.program_id(0)
        j = pl.program_id(1)

        @pl.when((i == 0) & (j == 0))
        def _():
            dgain_ref[...] = jnp.zeros_like(dgain_ref)

        @pl.when(j == 0)
        def _():
            dyb[...] = dh_ref[...].astype(_MXU)
            acc[...] = jnp.zeros_like(acc)

        xn = hn_ref[...]
        wg = wg_ref[0]
        wu = wu_ref[0]
        g = _dot(xn, wg)
        u = _dot(xn, wu)
        sg = jax.nn.sigmoid(g)
        sl = g * sg
        act_ref[...] = (sl * u).astype(_MXU)
        da = _dot_nt(dyb[...], wd_ref[...].reshape(_TF, D_MODEL))
        dgate = (da * u * (sg * (1.0 + g * (1.0 - sg)))).astype(_MXU)
        dup = (da * sl).astype(_MXU)
        dg_ref[...] = dgate
        du_ref[...] = dup
        acc[...] += _dot_nt(dgate, wg) + _dot_nt(dup, wu)

        @pl.when(j == nf - 1)
        def _():
            dhn = acc[...]
            dx, dgain = _norm_bwd(dhn, h_ref[...], g_ref[...])
            dh2_ref[...] = dh_ref[...] + dx
            dgain_ref[...] += dgain

    tok = pl.BlockSpec((tm, D_MODEL), lambda i, j: (i, 0))
    ffb = pl.BlockSpec((tm, _TF), lambda i, j: (i, j))
    row = pl.BlockSpec((1, D_MODEL), lambda i, j: (0, 0))
    return pl.pallas_call(
        body, name="ffn_bwd", grid=(T // tm, nf),
        in_specs=[tok, tok, tok, row] + _ffn_weight_specs(layer),
        out_specs=[tok, ffb, ffb, ffb, row],
        out_shape=[_sds((T, D_MODEL), _F32), _sds((T, D_FF), _MXU), _sds((T, D_FF), _MXU), _sds((T, D_FF), _MXU),
                   _sds((1, D_MODEL), _F32)],
        scratch_shapes=[pltpu.VMEM((tm, D_MODEL), _F32), pltpu.VMEM((tm, D_MODEL), _MXU)],
    )(dh3, h2, hn, g_ffn, w_gu, w_gu, w_d)


def _mem_attn_bwd(dh2, h1, g_q, w_q, gqm, bd128, mk, mv, w_mo):
    T = h1.shape[0]
    M = mk.shape[0]
    tm = min(_TM, T)
    nt = T // tm

    def body(dh_ref, h_ref, g_ref, wq_ref, gq_ref, bd_ref, mk_ref, mv_ref, wo_ref,
             dh1_ref, dmq_ref, dmk_ref, dmv_ref, dgq_ref, dgain_ref, dgq_acc):
        i = pl.program_id(0)

        @pl.when(i == 0)
        def _():
            dmk_ref[...] = jnp.zeros_like(dmk_ref)
            dmv_ref[...] = jnp.zeros_like(dmv_ref)
            dgain_ref[...] = jnp.zeros_like(dgain_ref)
            dgq_acc[...] = jnp.zeros_like(dgq_acc)

        x = h_ref[...]
        g = g_ref[...]
        bd = bd_ref[...]
        hn = ((x * _rstd(x)) * g).astype(_MXU)
        mq = _dot(hn, wq_ref[...])
        rq = lax.rsqrt(_group_sum(mq * mq, bd) * (1.0 / MEM_HEAD_DIM) + EPS)
        qn = ((mq * rq) * gq_ref[...]).astype(_MXU)
        dmo = _dot_nt(dh_ref[...].astype(_MXU), wo_ref[...])
        dqn = []
        for hd in range(MEM_HEADS):
            sl = slice(hd * MEM_HEAD_DIM, (hd + 1) * MEM_HEAD_DIM)
            p = _mem_softmax(qn, mk_ref, hd)
            dmo_h = dmo[:, sl].astype(_MXU)
            dp = _dot_nt(dmo_h, mv_ref[:, sl])
            ds = (p * (dp - jnp.sum(p * dp, axis=-1, keepdims=True)) * (MEM_HEAD_DIM ** -0.5)).astype(_MXU)
            dqn.append(_dot(ds, mk_ref[:, sl]))
            dmk_ref[:, sl] += _dot_tn(ds, qn[:, sl])
            dmv_ref[:, sl] += _dot_tn(p.astype(_MXU), dmo_h)
        dqn = jnp.concatenate(dqn, axis=-1)
        dmq, dgq = _headnorm_bwd(dqn, mq, gq_ref[...], bd, MEM_HEAD_DIM)
        dgq_acc[...] += dgq
        dmq = dmq.astype(_MXU)
        dmq_ref[...] = dmq
        dhn = _dot_nt(dmq, wq_ref[...])
        dx, dgain = _norm_bwd(dhn, x, g)
        dh1_ref[...] = dh_ref[...] + dx
        dgain_ref[...] += dgain

        @pl.when(i == nt - 1)
        def _():
            dgq_ref[...] = _fold_heads(dgq_acc[...], MEM_HEADS, MEM_HEAD_DIM)

    const2 = lambda shape: pl.BlockSpec(shape, lambda i: (0, 0))
    return pl.pallas_call(
        body, name="mem_attn_bwd", grid=(nt,),
        in_specs=[_rows(tm, D_MODEL), _rows(tm, D_MODEL), _resident((1, D_MODEL)), _resident(w_q.shape),
                  _resident((1, MEM_WIDTH)), _resident(bd128.shape), _resident(mk.shape), _resident(mv.shape),
                  _resident(w_mo.shape)],
        out_specs=[_rows(tm, D_MODEL), _rows(tm, MEM_WIDTH), const2((M, MEM_WIDTH)), const2((M, MEM_WIDTH)),
                   const2((1, MEM_HEAD_DIM)), const2((1, D_MODEL))],
        out_shape=[_sds((T, D_MODEL), _F32), _sds((T, MEM_WIDTH), _MXU), _sds((M, MEM_WIDTH), _F32),
                   _sds((M, MEM_WIDTH), _F32), _sds((1, MEM_HEAD_DIM), _F32), _sds((1, D_MODEL), _F32)],
        scratch_shapes=[pltpu.VMEM((1, MEM_WIDTH), _F32)],
    )(dh2, h1, g_q, w_q, gqm, bd128, mk, mv, w_mo)


def _mem_kv_bwd(dmk, dmv, mkv, mn, mem, g_kv, gkm, bd128, w_kv, dst, dst_shape, block):
    rows = D_MODEL // N_CHIPS

    def body(dmk_ref, dmv_ref, mkv_ref, mn_ref, mem_ref, g_ref, gk_ref, bd_ref, w_ref, *rest):
        dw_ref, dgk_ref, dgain_ref = rest[-3:]
        kraw = mkv_ref[:, :MEM_WIDTH]
        dk, dgk = _headnorm_bwd(dmk_ref[...], kraw, gk_ref[...], bd_ref[...], MEM_HEAD_DIM)
        dgk_ref[...] = _fold_heads(dgk, MEM_HEADS, MEM_HEAD_DIM)
        dmkv = jnp.concatenate([dk, dmv_ref[...]], axis=-1).astype(_MXU)
        dw_ref[...] = _dot_tn(mn_ref[...], dmkv).reshape(N_CHIPS, rows, 2 * MEM_WIDTH)
        dmn = _dot_nt(dmkv, w_ref[...])
        _, dgain = _norm_bwd(dmn, mem_ref[...], g_ref[...])
        dgain_ref[...] = dgain

    args = (dmk, dmv, mkv, mn, mem, g_kv, gkm, bd128, w_kv)
    whole = lambda a: pl.BlockSpec(a.shape, lambda i: (0,) * a.ndim)
    in_specs, alias = [whole(a) for a in args], {}
    if dst is not None:
        in_specs, args, alias = in_specs + [_ANY], args + (dst,), {len(args): 0}
    return pl.pallas_call(
        body, name="mem_kv_bwd", grid=(1,), in_specs=in_specs,
        out_specs=[pl.BlockSpec((N_CHIPS, rows, 2 * MEM_WIDTH), lambda i: (0, block, 0)),
                   pl.BlockSpec((1, MEM_HEAD_DIM), lambda i: (0, 0)), pl.BlockSpec((1, D_MODEL), lambda i: (0, 0))],
        out_shape=[_sds(dst_shape, _F32), _sds((1, MEM_HEAD_DIM), _F32), _sds((1, D_MODEL), _F32)],
        input_output_aliases=alias,
    )(*args)


def _out_proj_bwd(dh1, mixed, o, bd64, w_pool, pscale, w_out):
    T = dh1.shape[0]
    tm = min(_TM, T)
    hb = tm // HALO
    nt = T // tm

    def body(dh_ref, halo_ref, mx_ref, o_ref, bd_ref, wp_ref, ps_ref, wo_ref, doa_ref, dpin_ref, dwp_ref, dps_ref):
        i = pl.program_id(0)

        @pl.when(i == 0)
        def _():
            dwp_ref[...] = jnp.zeros_like(dwp_ref)
            dps_ref[...] = jnp.zeros_like(dps_ref)

        dcat = _dot_nt(dh_ref[...].astype(_MXU), wo_ref[...])
        do = dcat[:, :FOX_WIDTH].astype(_MXU)
        delta = _group_sum(do.astype(_F32) * o_ref[...].astype(_F32), bd_ref[...])
        for h in range(FOX_HEADS):
            sl = slice(h * FOX_HEAD_DIM, (h + 1) * FOX_HEAD_DIM)
            doa_ref[:, h * LANES:(h + 1) * LANES] = _aug_head(
                do[:, sl], _split3(-delta[:, h * FOX_HEAD_DIM:h * FOX_HEAD_DIM + 1]), _ZEROS3)
        dy = dcat[:, FOX_WIDTH:]
        dyh = _dot_nt(halo_ref[...].astype(_MXU), wo_ref[FOX_WIDTH:, :])
        dyh = jnp.where(i == nt - 1, 0.0, dyh)
        ps = ps_ref[...]
        t = (i * tm + lax.broadcasted_iota(jnp.int32, (tm + HALO, 1), 0) + 1).astype(_F32)
        mixed_t = mx_ref[...]
        dpin, dps = [], []
        for g, w in enumerate(POOL_WINDOWS):
            sl = slice(g * POOL_GROUP_DIM, (g + 1) * POOL_GROUP_DIM)
            mg = mixed_t[:, sl]
            wg = wp_ref[g]
            dps.append(jnp.sum(dy[:, sl] * _dot(mg, wg), axis=0, keepdims=True))
            dyl = (dy[:, sl] * ps[:, sl]).astype(_MXU)
            dylh = (dyh[:, sl] * ps[:, sl]).astype(_MXU)
            dwp_ref[g] += _dot_tn(mg, dyl)
            dmx = _dot_nt(dyl, wg)
            ext = jnp.concatenate([dmx, _dot_nt(dylh, wg)], axis=0) / jnp.minimum(t, float(w))
            dpin.append(_pool_window_sum(ext, w, True)[:tm, :] - dmx)
        dpin_ref[...] = jnp.concatenate(dpin, axis=-1)
        dps_ref[...] += jnp.concatenate(dps, axis=-1)

    return pl.pallas_call(
        body, name="out_proj_bwd", grid=(nt,),
        in_specs=[_rows(tm, D_MODEL),
                  pl.BlockSpec((HALO, D_MODEL), lambda i: (jnp.minimum((i + 1) * hb, T // HALO - 1), 0)),
                  _rows(tm, POOL_WIDTH), _rows(tm, FOX_WIDTH), _resident(bd64.shape), _resident(w_pool.shape),
                  _resident((1, POOL_WIDTH)), _resident(w_out.shape)],
        out_specs=[_rows(tm, FOX_HEADS * LANES), _rows(tm, POOL_WIDTH),
                   pl.BlockSpec(w_pool.shape, lambda i: (0, 0, 0)), pl.BlockSpec((1, POOL_WIDTH), lambda i: (0, 0))],
        out_shape=[_sds((T, FOX_HEADS * LANES), _MXU), _sds((T, POOL_WIDTH), _F32), _sds(w_pool.shape, _F32),
                   _sds((1, POOL_WIDTH), _F32)],
    )(dh1, dh1, mixed, o, bd64, w_pool, pscale, w_out)


def _fox_bwd(qa, ka, va, doa, pt, mb, lse):
    T = qa.shape[0]
    tq = min(_TQ, T)
    nq = T // tq
    pair = 2 * FOX_HEAD_DIM
    ii, jj = _causal_steps(nq, False)
    fwd_step = ii * (ii + 1) // 2 + jj
    row_lane, col_lane = FOX_HEAD_DIM, FOX_HEAD_DIM + 3

    def body(ii_ref, jj_ref, fs_ref, q_ref, k_ref, v_ref, do_ref, p_ref, mb_ref, lse_ref,
             dq_ref, dk_ref, dv_ref, drs_ref, dcs_ref, dk_acc, dv_acc):
        t = pl.program_id(1)
        i = ii_ref[t]
        j = jj_ref[t]

        @pl.when(t == 0)
        def _():
            dq_ref[...] = jnp.zeros_like(dq_ref)
            drs_ref[...] = jnp.zeros_like(drs_ref)

        @pl.when(i == j)
        def _():
            dk_acc[...] = jnp.zeros_like(dk_acc)
            dv_acc[...] = jnp.zeros_like(dv_acc)

        r = jnp.exp(mb_ref[0, 0] - lse_ref[0])
        dqs, rs = [], []
        for hh in range(2):
            sl = slice(LANES * hh, LANES * (hh + 1))
            dof = do_ref[:, sl].astype(_F32) * r[:, hh:hh + 1]
            dob = dof.astype(_MXU)
            p = p_ref[0, 0, hh]
            dv_acc[hh] += _dot(dof.T.astype(_MXU), p)
            dsb = (p.astype(_F32) * _dot_nt(dob, v_ref[:, sl])).astype(_MXU)
            dk_acc[hh] += _dot(q_ref[:, sl].astype(_F32).T.astype(_MXU), dsb)
            dqa = _dot(dsb, k_ref[:, sl])
            dqs.append(dqa[:, :FOX_HEAD_DIM] * (FOX_HEAD_DIM ** -0.5))
            rs.append(dqa[:, row_lane:row_lane + 1])
        rows = pl.ds(pl.multiple_of(i * tq, tq), tq)
        dq_ref[rows, :] += jnp.concatenate(dqs, axis=-1)
        drs_ref[0, rows, :] += _two_lanes(rs[0], rs[1])

        @pl.when(i == nq - 1)
        def _():
            dk = [dk_acc[hh].T for hh in range(2)]
            dv = [dv_acc[hh].T for hh in range(2)]
            dk_ref[...] = jnp.concatenate([dk[0][:, :FOX_HEAD_DIM], dk[1][:, :FOX_HEAD_DIM]], axis=-1)
            dv_ref[...] = jnp.concatenate([dv[0][:, :FOX_HEAD_DIM], dv[1][:, :FOX_HEAD_DIM]], axis=-1)
            dcs_ref[0] = _two_lanes(dk[0][:, col_lane:col_lane + 1], dk[1][:, col_lane:col_lane + 1])

    qspec = pl.BlockSpec((tq, 2 * LANES), lambda p, t, ii, jj, fs: (ii[t], p))
    kspec = pl.BlockSpec((tq, 2 * LANES), lambda p, t, ii, jj, fs: (jj[t], p))
    kout = pl.BlockSpec((tq, pair), lambda p, t, ii, jj, fs: (jj[t], p))
    return pl.pallas_call(
        body, name="fox_bwd",
        grid_spec=pltpu.PrefetchScalarGridSpec(
            num_scalar_prefetch=3, grid=(FOX_HEADS // 2, int(ii.shape[0])),
            in_specs=[qspec, kspec, kspec, qspec,
                      pl.BlockSpec((1, 1, 2, tq, tq), lambda p, t, ii, jj, fs: (p, fs[t], 0, 0, 0)),
                      pl.BlockSpec((1, 1, tq, LANES), lambda p, t, ii, jj, fs: (p, fs[t], 0, 0)),
                      pl.BlockSpec((1, tq, LANES), lambda p, t, ii, jj, fs: (p, ii[t], 0))],
            out_specs=[pl.BlockSpec((T, pair), lambda p, t, ii, jj, fs: (0, p)), kout, kout,
                       pl.BlockSpec((1, T, LANES), lambda p, t, ii, jj, fs: (p, 0, 0)),
                       pl.BlockSpec((1, tq, LANES), lambda p, t, ii, jj, fs: (p, jj[t], 0))],
            scratch_shapes=[pltpu.VMEM((2, LANES, tq), _F32), pltpu.VMEM((2, LANES, tq), _F32)]),
        out_shape=[_sds((T, FOX_WIDTH), _F32), _sds((T, FOX_WIDTH), _F32), _sds((T, FOX_WIDTH), _F32),
                   _sds((FOX_HEADS // 2, T, LANES), _F32), _sds((FOX_HEADS // 2, T, LANES), _F32)],
    )(ii, jj, fwd_step, qa, ka, va, doa, pt, mb, lse)


def _mix_in_bwd(dh1, h, g_mix, zqk, dq, dk, dv, dpin, df, gq, gk, bd64, w_main, w_f):
    T = h.shape[0]
    tm = min(_TM, T)
    nt = T // tm

    def body(dh1_ref, h_ref, g_ref, zqk_ref, dq_ref, dk_ref, dv_ref, dpin_ref, df_ref, gq_ref, gk_ref, bd_ref,
             wm_ref, wf_ref, dh_ref, dz_ref, dzf_ref, dgq_ref, dgk_ref, dgain_ref, dbf_ref, dgq_acc, dgk_acc):
        i = pl.program_id(0)

        @pl.when(i == 0)
        def _():
            dgain_ref[...] = jnp.zeros_like(dgain_ref)
            dbf_ref[...] = jnp.zeros_like(dbf_ref)
            dgq_acc[...] = jnp.zeros_like(dgq_acc)
            dgk_acc[...] = jnp.zeros_like(dgk_acc)

        bd = bd_ref[...]
        dqr, dgq = _headnorm_bwd(dq_ref[...], zqk_ref[:, :FOX_WIDTH], gq_ref[...], bd, FOX_HEAD_DIM)
        dkr, dgk = _headnorm_bwd(dk_ref[...], zqk_ref[:, FOX_WIDTH:], gk_ref[...], bd, FOX_HEAD_DIM)
        dgq_acc[...] += dgq
        dgk_acc[...] += dgk
        dz = jnp.concatenate([dqr, dkr, dv_ref[...], dpin_ref[...]], axis=-1).astype(_MXU)
        dz_ref[...] = dz
        df = df_ref[...]
        dzf = df.astype(_MXU)
        dzf_ref[...] = dzf
        dbf_ref[...] += jnp.sum(df, axis=0, keepdims=True)
        dxn = _dot_nt(dz, wm_ref[...]) + _dot_nt(dzf, wf_ref[...])
        dx, dgain = _norm_bwd(dxn, h_ref[...], g_ref[...])
        dh_ref[...] = dh1_ref[...] + dx
        dgain_ref[...] += dgain

        @pl.when(i == nt - 1)
        def _():
            dgq_ref[...] = _fold_heads(dgq_acc[...], FOX_HEADS, FOX_HEAD_DIM)
            dgk_ref[...] = _fold_heads(dgk_acc[...], FOX_HEADS, FOX_HEAD_DIM)

    const2 = lambda shape: pl.BlockSpec(shape, lambda i: (0, 0))
    return pl.pallas_call(
        body, name="mix_in_bwd", grid=(nt,),
        in_specs=[_rows(tm, D_MODEL), _rows(tm, D_MODEL), _resident((1, D_MODEL)), _rows(tm, 2 * FOX_WIDTH),
                  _rows(tm, FOX_WIDTH), _rows(tm, FOX_WIDTH), _rows(tm, FOX_WIDTH), _rows(tm, POOL_WIDTH),
                  _rows(tm, LANES), _resident((1, FOX_WIDTH)), _resident((1, FOX_WIDTH)), _resident(bd64.shape),
                  _resident(w_main.shape), _resident(w_f.shape)],
        out_specs=[_rows(tm, D_MODEL), _rows(tm, 4 * FOX_WIDTH), _rows(tm, LANES), const2((1, FOX_HEAD_DIM)),
                   const2((1, FOX_HEAD_DIM)), const2((1, D_MODEL)), const2((1, LANES))],
        out_shape=[_sds((T, D_MODEL), _F32), _sds((T, 4 * FOX_WIDTH), _MXU), _sds((T, LANES), _MXU),
                   _sds((1, FOX_HEAD_DIM), _F32), _sds((1, FOX_HEAD_DIM), _F32), _sds((1, D_MODEL), _F32),
                   _sds((1, LANES), _F32)],
        scratch_shapes=[pltpu.VMEM((1, FOX_WIDTH), _F32), pltpu.VMEM((1, FOX_WIDTH), _F32)],
    )(dh1, h, g_mix, zqk, dq, dk, dv, dpin, df, gq, gk, bd64, w_main, w_f)


CLASSES = (("a", D_MODEL, ("w_out", "w_mem_kv")), ("d", D_MODEL, ("w_down",)), ("g", _TF, ("w_gate_up",)),
           ("i", 514, ("w_in",)), ("q", MEM_WIDTH, ("w_mem_q",)), ("o", 256, ("w_mem_out",)))
W_OUT_ROWS = D_MODEL // N_CHIPS
W_DOWN_ROWS = D_FF // N_CHIPS
W_MEM_OUT_COLS = D_MODEL // N_CHIPS


def _layer_params(WC, WS, l):
    rows = lambda buf, first, n: buf[:, first:first + n]
    w_in = rows(WC["i"], l * D_MODEL, D_MODEL).transpose(1, 0, 2).reshape(D_MODEL, -1)
    n_main = 3 * FOX_WIDTH + POOL_WIDTH
    row = lambda a: a.reshape(1, -1).astype(_F32)
    return dict(
        layer=l,
        g_mix=row(WS["g_mix"][l]),
        w_main=w_in[:, :n_main],
        w_f=jnp.pad(w_in[:, n_main:], ((0, 0), (0, LANES - FOX_HEADS))),
        b_f=jnp.pad(row(WS["b_forget"][l]), ((0, 0), (0, LANES - FOX_HEADS))),
        gq=jnp.tile(row(WS["g_q_fox"][l]), (1, FOX_HEADS)),
        gk=jnp.tile(row(WS["g_k_fox"][l]), (1, FOX_HEADS)),
        w_pool=WS["w_pool"][l].astype(_MXU),
        pscale=row(WS["pool_scale"][l]),
        w_out=rows(WC["a"], l * W_OUT_ROWS, W_OUT_ROWS).reshape(D_MODEL, D_MODEL),
        g_mem_q=row(WS["g_mem_q"][l]),
        g_mem_kv=row(WS["g_mem_kv"][l]),
        w_mem_q=rows(WC["q"], l * W_OUT_ROWS, W_OUT_ROWS).reshape(D_MODEL, MEM_WIDTH),
        w_mem_kv=rows(WC["a"], (DEPTH + l) * W_OUT_ROWS, W_OUT_ROWS).reshape(D_MODEL, 2 * MEM_WIDTH),
        gqm=jnp.tile(row(WS["g_q_mem"][l]), (1, MEM_HEADS)),
        gkm=jnp.tile(row(WS["g_k_mem"][l]), (1, MEM_HEADS)),
        w_mem_out=rows(WC["o"], l * MEM_WIDTH, MEM_WIDTH).transpose(1, 0, 2).reshape(MEM_WIDTH, D_MODEL),
        g_ffn=row(WS["g_ffn"][l]),
        w_gu=WC["g"],
        w_d=WC["d"],
    )


def _per_head_lanes(sums):
    T = sums.shape[1]
    return jnp.pad(sums[:, :, :2].transpose(1, 0, 2).reshape(T, FOX_HEADS), ((0, 0), (0, LANES - FOX_HEADS)))


def _layer_fwd(h, mem, P, bd64, bd128):
    s = dict(h=h)
    s["xn"], s["zqk"], qn, kn, v, pin, s["fl"] = _mix_in_fwd(
        h, P["g_mix"], P["w_main"], P["w_f"], P["b_f"], P["gq"], P["gk"], bd64)
    s["qa"], s["ka"], s["va"] = _gate_fwd(s["fl"], qn, kn, v)
    s["o"], s["pt"], s["mb"], s["lse"] = _fox_fwd(s["qa"], s["ka"], s["va"])
    s["h1"], s["mixed"], s["y"] = _out_proj_fwd(h, s["o"], pin, P["w_pool"], P["pscale"], P["w_out"])
    s["mn"], s["mkv"], s["mk"], s["mv"] = _mem_kv_fwd(mem, P["g_mem_kv"], P["w_mem_kv"], P["gkm"], bd128)
    s["h2"], s["hn_mem"], s["mo"] = _mem_attn_fwd(s["h1"], P["g_mem_q"], P["w_mem_q"], P["gqm"], bd128, s["mk"],
                                                  s["mv"], P["w_mem_out"])
    h3, s["hn_ffn"] = _ffn_fwd(s["h2"], P["g_ffn"], P["w_gu"], P["w_d"], P["layer"])
    return h3, s


def _layer_bwd(dh3, mem, P, s, bd64, bd128, G, shapes):
    l = P["layer"]
    g = {}

    def into(c, a, b, name, tk, tn, index, chips=1):
        G[c] = _matmul_tn(a, b, name, tk=tk, tn=tn, dst=G.get(c), dst_shape=shapes[c], dst_index=index,
                          dst_chips=chips)

    dh2, act, dgate, dup, g["g_ffn"] = _ffn_bwd(dh3, s["h2"], s["hn_ffn"], P["g_ffn"], P["w_gu"], P["w_d"], l)
    into("d", act, dh3, "dw_down", 2 * W_DOWN_ROWS, D_MODEL, lambda i, j: (i, l, 0), chips=2)
    into("g", s["hn_ffn"], dgate, "dw_gate", D_MODEL, _TF, lambda i, j: (j, l, 0))
    into("g", s["hn_ffn"], dup, "dw_up", D_MODEL, _TF, lambda i, j: (j + D_FF // _TF, l, 0))

    dh1, dmq, dmk, dmv, g["g_q_mem"], g["g_mem_q"] = _mem_attn_bwd(
        dh2, s["h1"], P["g_mem_q"], P["w_mem_q"], P["gqm"], bd128, s["mk"], s["mv"], P["w_mem_out"])
    into("o", s["mo"], dh2, "dw_mem_out", MEM_WIDTH, W_MEM_OUT_COLS, lambda i, j: (j, l, 0))
    into("q", s["hn_mem"], dmq, "dw_mem_q", W_OUT_ROWS, MEM_WIDTH, lambda i, j: (i, l, 0))
    G["a"], g["g_k_mem"], g["g_mem_kv"] = _mem_kv_bwd(dmk, dmv, s["mkv"], s["mn"], mem, P["g_mem_kv"], P["gkm"],
                                                      bd128, P["w_mem_kv"], G.get("a"), shapes["a"], DEPTH + l)

    doa, dpin, g["w_pool"], g["pool_scale"] = _out_proj_bwd(dh1, s["mixed"], s["o"], bd64, P["w_pool"], P["pscale"],
                                                            P["w_out"])
    half = FOX_WIDTH // W_OUT_ROWS
    into("a", s["o"], dh1, "dw_out_fox", W_OUT_ROWS, D_MODEL, lambda i, j: (i, l, 0))
    into("a", s["y"], dh1, "dw_out_pool", W_OUT_ROWS, D_MODEL, lambda i, j: (i + half, l, 0))
    dq, dk, dv, drs, dcs = _fox_bwd(s["qa"], s["ka"], s["va"], doa, s["pt"], s["mb"], s["lse"])
    df = _gate_bwd(_per_head_lanes(drs), _per_head_lanes(dcs), s["fl"])
    dh, dz, dzf, g["g_q_fox"], g["g_k_fox"], g["g_mix"], dbf = _mix_in_bwd(
        dh1, s["h"], P["g_mix"], s["zqk"], dq, dk, dv, dpin, df, P["gq"], P["gk"], bd64, P["w_main"], P["w_f"])
    g["b_forget"] = dbf[:, :FOX_HEADS]
    dw_in = jnp.concatenate([_matmul_tn(s["xn"], dz, "dw_in_main"),
                             _matmul_tn(s["xn"], dzf, "dw_in_gate")[:, :FOX_HEADS]], axis=1)
    return dh, G, g, _to_shards(dw_in, 1)


def _device_grads(x, mem, tgt, WC, WS):
    bd64 = _blockdiag_ones(FOX_HEADS, FOX_HEAD_DIM)
    bd128 = _blockdiag_ones(MEM_HEADS, MEM_HEAD_DIM)
    params = [_layer_params(WC, WS, l) for l in range(DEPTH)]
    shapes = {c: WC[c].shape for c in WC}
    h, saved = x, []
    for l in range(DEPTH):
        h, s = _layer_fwd(h, mem, params[l], bd64, bd128)
        saved.append(s)
    dh, loss = _loss_grad(h, tgt)
    G, small, dw_in = {}, [None] * DEPTH, [None] * DEPTH
    for l in reversed(range(DEPTH)):
        dh, G, small[l], dw_in[l] = _layer_bwd(dh, mem, params[l], saved[l], bd64, bd128, G, shapes)
    G["i"] = jnp.concatenate(dw_in, axis=1)
    gsmall = {n: jnp.stack([small[l][n].reshape(WS[n].shape[1:]) for l in range(DEPTH)]) for n in SMALL}
    return loss, dh, G, gsmall


def _class_slabs(shards):
    return {c: jnp.concatenate([shards[n].reshape(-1, width) for n in names]) for c, width, names in CLASSES}


def _class_rows(shards):
    where = {}
    for c, width, names in CLASSES:
        off = 0
        for n in names:
            rows = shards[n].shape[0] * shards[n].shape[1]
            where[n] = (c, off, rows)
            off += rows
    return where


def _to_shards(full, ax):
    shp = full.shape
    parts = full.reshape(shp[:ax] + (N_CHIPS, shp[ax] // N_CHIPS) + shp[ax + 1:])
    return jnp.moveaxis(parts, ax, 0)


def _from_shards(parts, ax):
    full = jnp.moveaxis(parts, 0, ax)
    shp = full.shape
    return full.reshape(shp[:ax] + (shp[ax] * shp[ax + 1],) + shp[ax + 2:])


def _pack_small(arrs):
    flat = jnp.concatenate([a.reshape(-1).astype(_F32) for a in arrs])
    rows = -(-flat.shape[0] // (8 * D_MODEL)) * 8
    return jnp.pad(flat, (0, rows * D_MODEL - flat.shape[0])).reshape(rows, D_MODEL)


def _unpack_small(flat, shapes):
    flat = flat.reshape(-1)
    out, off = [], 0
    for shp in shapes:
        n = 1
        for d in shp:
            n *= d
        out.append(flat[off:off + n].reshape(shp))
        off += n
    return out


def _mesh_pos():
    return lax.axis_index("x"), lax.axis_index("y"), lax.axis_index("c")


def _other_chips(x, y):
    return [(1 - x, y), (x, 1 - y), (1 - x, 1 - y)]


def _half_rows(ref_rows, which):
    half = ref_rows // 2
    return pl.ds(pl.multiple_of(which * half, 16), half)


def _remote(src_ref, dst_ref, send_sems, recv_sems, k, to):
    return pltpu.make_async_remote_copy(src_ref=src_ref, dst_ref=dst_ref, send_sem=send_sems.at[k],
                                        recv_sem=recv_sems.at[k], device_id=to, device_id_type=MESH)


def _allgather_weights(slabs):
    n = len(slabs)

    def body(*refs):
        srcs, outs, send_sems, recv_sems = refs[:n], refs[n:2 * n], refs[2 * n], refs[2 * n + 1]
        x, y, c = _mesh_pos()
        me = 2 * x + y
        sibling = (x, y, 1 - c)
        chips = _other_chips(x, y)
        first, passed = [], []
        for a, (src, out) in enumerate(zip(srcs, outs)):
            mine = _half_rows(src.shape[0], c)
            for j, chip in enumerate(chips):
                cp = _remote(src.at[mine], out.at[me, mine], send_sems, recv_sems, 6 * a + j, (*chip, c))
                cp.start()
                first.append(cp)
        for a, out in enumerate(outs):
            mine = _half_rows(out.shape[1], c)
            for j, (cx, cy) in enumerate(chips):
                slab = out.at[2 * cx + cy, mine]
                _remote(slab, slab, send_sems, recv_sems, 6 * a + j, (cx, cy, c)).wait_recv()
                fwd = _remote(slab, slab, send_sems, recv_sems, 6 * a + 3 + j, sibling)
                fwd.start()
                passed.append(fwd)
        for a, out in enumerate(outs):
            theirs = _half_rows(out.shape[1], 1 - c)
            for j, (cx, cy) in enumerate(chips):
                slab = out.at[2 * cx + cy, theirs]
                _remote(slab, slab, send_sems, recv_sems, 6 * a + 3 + j, sibling).wait_recv()
        for cp in first + passed:
            cp.wait_send()

    return pl.pallas_call(
        body, name="allgather_weights", in_specs=[_ANY] * n, out_specs=[_ANY] * n,
        out_shape=[_sds((N_CHIPS,) + s.shape, s.dtype) for s in slabs],
        scratch_shapes=[pltpu.SemaphoreType.DMA((6 * n,)), pltpu.SemaphoreType.DMA((6 * n,))],
    )(*slabs)


def _exchange_halves(grads):
    n = len(grads)

    def body(*refs):
        srcs, gots, send_sems, recv_sems = refs[:n], refs[n:2 * n], refs[2 * n], refs[2 * n + 1]
        x, y, c = _mesh_pos()
        copies = []
        for a, (src, got) in enumerate(zip(srcs, gots)):
            theirs = _half_rows(src.shape[1], 1 - c)
            for k in range(N_CHIPS):
                cp = _remote(src.at[k, theirs], got.at[k], send_sems, recv_sems, N_CHIPS * a + k, (x, y, 1 - c))
                cp.start()
                copies.append(cp)
        for cp in copies:
            cp.wait()

    return pl.pallas_call(
        body, name="grad_exchange_halves", in_specs=[_ANY] * n, out_specs=[_ANY] * n,
        out_shape=[_sds((N_CHIPS, g.shape[1] // 2, g.shape[2]), _F32) for g in grads],
        scratch_shapes=[pltpu.SemaphoreType.DMA((N_CHIPS * n,)), pltpu.SemaphoreType.DMA((N_CHIPS * n,))],
    )(*grads)


def _row_tile(rows):
    return _pick_tile(rows, (704, 512, 256))


def _add_halves(g, got, c_idx, name):
    _, half, width = got.shape
    ta = _row_tile(half)
    nb = half // ta

    def body(c_ref, a_ref, b_ref, o_ref):
        o_ref[...] = (a_ref[...] + b_ref[...]).astype(jnp.bfloat16)

    return pl.pallas_call(
        body, name=name,
        grid_spec=pltpu.PrefetchScalarGridSpec(
            num_scalar_prefetch=1, grid=(N_CHIPS, nb),
            in_specs=[pl.BlockSpec((1, ta, width), lambda k, i, c: (k, c[0] * nb + i, 0)),
                      pl.BlockSpec((1, ta, width), lambda k, i, c: (k, i, 0))],
            out_specs=pl.BlockSpec((1, ta, width), lambda k, i, c: (k, i, 0))),
        out_shape=_sds(got.shape, jnp.bfloat16),
    )(c_idx, g, got)


def _scatter_to_chips(parts):
    n = len(parts)

    def body(*refs):
        srcs, gots, send_sems, recv_sems = refs[:n], refs[n:2 * n], refs[2 * n], refs[2 * n + 1]
        x, y, c = _mesh_pos()
        me = 2 * x + y
        chips = _other_chips(x, y)
        copies = []
        for a, (src, got) in enumerate(zip(srcs, gots)):
            for j, (cx, cy) in enumerate(chips):
                cp = _remote(src.at[2 * cx + cy], got.at[me], send_sems, recv_sems, 3 * a + j, (cx, cy, c))
                cp.start()
                copies.append(cp)
        for a, got in enumerate(gots):
            for j, (cx, cy) in enumerate(chips):
                slab = got.at[2 * cx + cy]
                _remote(slab, slab, send_sems, recv_sems, 3 * a + j, (cx, cy, c)).wait_recv()
        for cp in copies:
            cp.wait_send()

    return pl.pallas_call(
        body, name="grad_scatter_chips", in_specs=[_ANY] * n, out_specs=[_ANY] * n,
        out_shape=[_sds(p.shape, p.dtype) for p in parts],
        scratch_shapes=[pltpu.SemaphoreType.DMA((3 * n,)), pltpu.SemaphoreType.DMA((3 * n,))],
    )(*parts)


def _sum_chips(got, c_idx, name):
    _, half, width = got.shape
    ta = _row_tile(half)
    nb = half // ta

    def body(c_ref, a_ref, o_ref):
        f = lambda k: a_ref[k].astype(_F32)
        o_ref[...] = ((f(0) + f(1)) + f(2)) + f(3)

    return pl.pallas_call(
        body, name=name,
        grid_spec=pltpu.PrefetchScalarGridSpec(
            num_scalar_prefetch=1, grid=(nb,),
            in_specs=[pl.BlockSpec((N_CHIPS, ta, width), lambda i, c: (0, i, 0))],
            out_specs=pl.BlockSpec((ta, width), lambda i, c: (c[0] * nb + i, 0))),
        out_shape=_sds((2 * half, width), _F32),
    )(c_idx, got)


def _share_with_sibling(bufs):
    n = len(bufs)

    def body(*refs):
        outs, send_sems, recv_sems = refs[n:2 * n], refs[2 * n], refs[2 * n + 1]
        x, y, c = _mesh_pos()
        copies = []
        for a, out in enumerate(outs):
            mine = out.at[_half_rows(out.shape[0], c)]
            cp = _remote(mine, mine, send_sems, recv_sems, a, (x, y, 1 - c))
            cp.start()
            copies.append(cp)
        for a, out in enumerate(outs):
            theirs = out.at[_half_rows(out.shape[0], 1 - c)]
            _remote(theirs, theirs, send_sems, recv_sems, a, (x, y, 1 - c)).wait_recv()
        for cp in copies:
            cp.wait_send()

    return pl.pallas_call(
        body, name="grad_share_sibling", in_specs=[_ANY] * n, out_specs=[_ANY] * n,
        out_shape=[_sds(b.shape, _F32) for b in bufs], input_output_aliases={a: a for a in range(n)},
        scratch_shapes=[pltpu.SemaphoreType.DMA((n,)), pltpu.SemaphoreType.DMA((n,))],
    )(*bufs)


def _allreduce_small(g):
    rows = g.shape[0]
    n_dev = 2 * N_CHIPS

    def body(g_ref, out_ref, gathered, local_sem, send_sems, recv_sems):
        x, y, c = _mesh_pos()
        me = 4 * x + 2 * y + c
        own = pltpu.make_async_copy(g_ref, gathered.at[me], local_sem)
        own.start()
        copies = []
        for k in range(1, n_dev):
            fx, fy, fc = (k >> 2) & 1, (k >> 1) & 1, k & 1
            cp = pltpu.make_async_remote_copy(
                src_ref=g_ref, dst_ref=gathered.at[me], send_sem=send_sems.at[k - 1], recv_sem=recv_sems.at[k - 1],
                device_id=(x ^ fx, y ^ fy, c ^ fc), device_id_type=MESH)
            cp.start()
            copies.append(cp)
        for k in range(1, n_dev):
            fx, fy, fc = (k >> 2) & 1, (k >> 1) & 1, k & 1
            px, py, pc = x ^ fx, y ^ fy, c ^ fc
            slab = gathered.at[4 * px + 2 * py + pc]
            pltpu.make_async_remote_copy(src_ref=slab, dst_ref=slab, send_sem=send_sems.at[k - 1],
                                         recv_sem=recv_sems.at[k - 1], device_id=(px, py, pc),
                                         device_id_type=MESH).wait_recv()
        for cp in copies:
            cp.wait_send()
        own.wait()
        acc = gathered[0]
        for d in range(1, n_dev):
            acc = acc + gathered[d]
        out_ref[...] = acc

    vmem = pl.BlockSpec(memory_space=pltpu.VMEM)
    return pl.pallas_call(
        body, name="allreduce_small", in_specs=[vmem], out_specs=vmem, out_shape=_sds((rows, D_MODEL), _F32),
        scratch_shapes=[pltpu.VMEM((n_dev, rows, D_MODEL), _F32), pltpu.SemaphoreType.DMA,
                        pltpu.SemaphoreType.DMA((n_dev - 1,)), pltpu.SemaphoreType.DMA((n_dev - 1,))],
    )(g)


def _adamw(w, g, m, v, name, g_first_row=0):
    shape = w.shape
    cols = shape[-1]
    rows = 1
    for d in shape[:-1]:
        rows *= d
    w2, m2, v2 = (a.reshape(rows, cols) for a in (w, m, v))
    tr = _pick_tile(rows, (256, 128, 64, 32, 16, 8))
    g0 = g_first_row // tr

    def body(w_ref, g_ref, m_ref, v_ref, go_ref, d_ref, nm_ref, nv_ref):
        gg = g_ref[...]
        go_ref[...] = gg
        nm = ADAM_B1 * m_ref[...] + (1.0 - ADAM_B1) * gg
        nv = ADAM_B2 * v_ref[...] + (1.0 - ADAM_B2) * (gg * gg)
        m_hat = nm / (1.0 - ADAM_B1 ** ADAM_STEP)
        v_hat = nv / (1.0 - ADAM_B2 ** ADAM_STEP)
        d_ref[...] = -ADAM_LR * (m_hat / (jnp.sqrt(v_hat) + ADAM_EPS) + ADAM_WD * w_ref[...])
        nm_ref[...] = nm
        nv_ref[...] = nv

    spec = pl.BlockSpec((tr, cols), lambda i: (i, 0))
    outs = pl.pallas_call(
        body, name=name, grid=(rows // tr,),
        in_specs=[spec, pl.BlockSpec((tr, cols), lambda i: (g0 + i, 0)), spec, spec], out_specs=[spec] * 4,
        out_shape=[_sds((rows, cols), _F32)] * 4,
    )(w2, g, m2, v2)
    return tuple(o.reshape(shape) for o in outs)


def kernel(x, mem, g_mix, w_in, b_forget, g_q_fox, g_k_fox, w_pool, pool_scale, w_out, g_mem_q, g_mem_kv, w_mem_q, w_mem_kv, g_q_mem, g_k_mem, w_mem_out, g_ffn, w_gate_up, w_down, loss_target, m_g_mix, m_w_in, m_b_forget, m_g_q_fox, m_g_k_fox, m_w_pool, m_pool_scale, m_w_out, m_g_mem_q, m_g_mem_kv, m_w_mem_q, m_w_mem_kv, m_g_q_mem, m_g_k_mem, m_w_mem_out, m_g_ffn, m_w_gate_up, m_w_down, v_g_mix, v_w_in, v_b_forget, v_g_q_fox, v_g_k_fox, v_w_pool, v_pool_scale, v_w_out, v_g_mem_q, v_g_mem_kv, v_w_mem_q, v_w_mem_kv, v_g_q_mem, v_g_k_mem, v_w_mem_out, v_g_ffn, v_w_gate_up, v_w_down):
    w = dict(g_mix=g_mix, w_in=w_in, b_forget=b_forget, g_q_fox=g_q_fox, g_k_fox=g_k_fox, w_pool=w_pool,
             pool_scale=pool_scale, w_out=w_out, g_mem_q=g_mem_q, g_mem_kv=g_mem_kv, w_mem_q=w_mem_q,
             w_mem_kv=w_mem_kv, g_q_mem=g_q_mem, g_k_mem=g_k_mem, w_mem_out=w_mem_out, g_ffn=g_ffn,
             w_gate_up=w_gate_up, w_down=w_down)
    m = dict(g_mix=m_g_mix, w_in=m_w_in, b_forget=m_b_forget, g_q_fox=m_g_q_fox, g_k_fox=m_g_k_fox, w_pool=m_w_pool,
             pool_scale=m_pool_scale, w_out=m_w_out, g_mem_q=m_g_mem_q, g_mem_kv=m_g_mem_kv, w_mem_q=m_w_mem_q,
             w_mem_kv=m_w_mem_kv, g_q_mem=m_g_q_mem, g_k_mem=m_g_k_mem, w_mem_out=m_w_mem_out, g_ffn=m_g_ffn,
             w_gate_up=m_w_gate_up, w_down=m_w_down)
    v = dict(g_mix=v_g_mix, w_in=v_w_in, b_forget=v_b_forget, g_q_fox=v_g_q_fox, g_k_fox=v_g_k_fox, w_pool=v_w_pool,
             pool_scale=v_pool_scale, w_out=v_w_out, g_mem_q=v_g_mem_q, g_mem_kv=v_g_mem_kv, w_mem_q=v_w_mem_q,
             w_mem_kv=v_w_mem_kv, g_q_mem=v_g_q_mem, g_k_mem=v_g_k_mem, w_mem_out=v_w_mem_out, g_ffn=v_g_ffn,
             w_gate_up=v_w_gate_up, w_down=v_w_down)

    classes = [c for c, _, _ in CLASSES]
    chip = 2 * lax.axis_index("x") + lax.axis_index("y")
    c_idx = lax.axis_index("c").astype(jnp.int32).reshape(1)
    own_slab = lambda bufs, own: lax.dynamic_update_slice(bufs, own[None], (chip, 0, 0))

    slabs = _class_slabs({n: w[n].astype(_MXU) for n in BIG})
    gathered = _allgather_weights([slabs[c] for c in classes])
    WC = {c: own_slab(buf, slabs[c]) for c, buf in zip(classes, gathered)}

    loss, grad_x, G, gsmall = _device_grads(x[0], mem[0], loss_target[0], WC, {n: w[n] for n in SMALL})
    loss = lax.psum(loss[0, 0], ("x", "y", "c"))

    grads = [G[c] for c in classes]
    partial = [_add_halves(g, got, c_idx, "grad_add_halves_" + c)
               for c, g, got in zip(classes, grads, _exchange_halves(grads))]
    landed = _scatter_to_chips(partial)
    partials = [own_slab(got, lax.dynamic_index_in_dim(p, chip, 0, keepdims=False)) for got, p in zip(landed, partial)]
    reduced = _share_with_sibling([_sum_chips(p, c_idx, "grad_sum_chips_" + c) for c, p in zip(classes, partials)])
    reduced = dict(zip(classes, reduced))

    small_shapes = [w[n].shape for n in SMALL]
    gsmall = _unpack_small(_allreduce_small(_pack_small([gsmall[n] for n in SMALL])), small_shapes)

    g_out, d_out, m_out, v_out = {}, {}, {}, {}
    for n, (c, first, _) in _class_rows({n: w[n] for n in BIG}).items():
        g_out[n], d_out[n], m_out[n], v_out[n] = _adamw(w[n], reduced[c], m[n], v[n], "adamw_" + n, first)
    packed = [_pack_small([t[n] for n in SMALL]) for t in (w, m, v)]
    _, ds, ms, vs = _adamw(packed[0], _pack_small(gsmall), packed[1], packed[2], "adamw_small")
    for n, gi, di, mi, vi in zip(SMALL, gsmall, _unpack_small(ds, small_shapes), _unpack_small(ms, small_shapes),
                                 _unpack_small(vs, small_shapes)):
        g_out[n], d_out[n], m_out[n], v_out[n] = gi, di, mi, vi

    return (loss, grad_x[None], *[g_out[n] for n in WEIGHTS], *[d_out[n] for n in WEIGHTS],
            *[m_out[n] for n in WEIGHTS], *[v_out[n] for n in WEIGHTS])
```

```python
import functools

import jax
import jax.numpy as jnp
import numpy as np
from jax import lax
from jax.experimental import pallas as pl
from jax.experimental.pallas import tpu as pltpu

_F32 = jnp.float32
_MXU = jnp.bfloat16

D_MODEL = 1024
DEPTH = 2
FOX_HEADS = 8
FOX_HEAD_DIM = 64
FOX_WIDTH = FOX_HEADS * FOX_HEAD_DIM
POOL_WINDOWS = (2, 4, 8, 16)
POOL_GROUP_DIM = 128
POOL_WIDTH = len(POOL_WINDOWS) * POOL_GROUP_DIM
MEM_HEADS = 4
MEM_HEAD_DIM = 128
MEM_WIDTH = MEM_HEADS * MEM_HEAD_DIM
D_FF = 2816
EPS = 1e-6
LANES = 128
HALO = 16

ADAM_LR = 0.001
ADAM_B1 = 0.9
ADAM_B2 = 0.999
ADAM_EPS = 1e-08
ADAM_WD = 0.01
ADAM_STEP = 10

N_CHIPS = 4
MESH = pl.DeviceIdType.MESH

_TM = 512
_TQ = 1024
_TMF = 256
_TF = 1408
_TT = 1024
_TB = 256
_TA = 512

BIG = ("w_in", "w_out", "w_mem_q", "w_mem_kv", "w_mem_out", "w_gate_up", "w_down")
SHARD_AXIS = {"w_in": 2, "w_out": 1, "w_mem_q": 1, "w_mem_kv": 1, "w_mem_out": 2, "w_gate_up": 2, "w_down": 1}
SMALL = ("g_mix", "b_forget", "g_q_fox", "g_k_fox", "w_pool", "pool_scale", "g_mem_q", "g_mem_kv", "g_q_mem",
         "g_k_mem", "g_ffn")
WEIGHTS = ("g_mix", "w_in", "b_forget", "g_q_fox", "g_k_fox", "w_pool", "pool_scale", "w_out", "g_mem_q", "g_mem_kv",
           "w_mem_q", "w_mem_kv", "g_q_mem", "g_k_mem", "w_mem_out", "g_ffn", "w_gate_up", "w_down")


def _dot(a, b):
    return jnp.dot(a, b, preferred_element_type=_F32)


def _dot_nt(a, b):
    return lax.dot_general(a, b, (((1,), (1,)), ((), ())), preferred_element_type=_F32)


def _dot_tn(a, b):
    return lax.dot_general(a, b, (((0,), (0,)), ((), ())), preferred_element_type=_F32)


def _group_sum(x, ones_blockdiag):
    hi = x.astype(_MXU)
    lo = (x - hi.astype(_F32)).astype(_MXU)
    return _dot(hi, ones_blockdiag) + _dot(lo, ones_blockdiag)


def _tri_dot(tri, x):
    h1 = x.astype(jnp.bfloat16)
    r1 = x - h1.astype(_F32)
    h2 = r1.astype(jnp.bfloat16)
    h3 = (r1 - h2.astype(_F32)).astype(jnp.bfloat16)
    return _dot(tri, h1) + _dot(tri, h2) + _dot(tri, h3)


def _rstd(x):
    return lax.rsqrt(jnp.mean(x * x, axis=-1, keepdims=True) + EPS)


def _norm_bwd(dy, x, g):
    r = _rstd(x)
    xhat = x * r
    u = dy * g
    dx = r * (u - xhat * jnp.mean(u * xhat, axis=-1, keepdims=True))
    return dx, jnp.sum(dy * xhat, axis=0, keepdims=True)


def _headnorm_bwd(dy, x, g, ones_blockdiag, width):
    r = lax.rsqrt(_group_sum(x * x, ones_blockdiag) * (1.0 / width) + EPS)
    xhat = x * r
    u = dy * g
    dx = r * (u - xhat * (_group_sum(u * xhat, ones_blockdiag) * (1.0 / width)))
    return dx, jnp.sum(dy * xhat, axis=0, keepdims=True)


def _fold_heads(row, heads, width):
    acc = row[:, 0:width]
    for h in range(1, heads):
        acc = acc + row[:, h * width:(h + 1) * width]
    return acc


_ANY = pl.BlockSpec(memory_space=pl.ANY)


def _resident(shape):
    return pl.BlockSpec(shape, lambda *_: (0,) * len(shape), pipeline_mode=pl.Buffered(1))


def _rows(tm, width):
    return pl.BlockSpec((tm, width), lambda i: (i, 0))


def _blockdiag_ones(groups, width):
    return jnp.kron(jnp.eye(groups, dtype=_F32), jnp.ones((width, width), _F32)).astype(_MXU)


def _sds(shape, dtype):
    return jax.ShapeDtypeStruct(shape, dtype)


def _mix_in_fwd(h, g_mix, w_main, w_f, b_f, gq, gk, bd64):
    T = h.shape[0]
    tm = min(_TM, T)

    def body(h_ref, g_ref, wm_ref, wf_ref, bf_ref, gq_ref, gk_ref, bd_ref,
             xn_ref, zqk_ref, qn_ref, kn_ref, v_ref, pin_ref, fl_ref):
        x = h_ref[...]
        xn = ((x * _rstd(x)) * g_ref[...]).astype(_MXU)
        xn_ref[...] = xn
        z = _dot(xn, wm_ref[...])
        q = z[:, :FOX_WIDTH]
        k = z[:, FOX_WIDTH:2 * FOX_WIDTH]
        zqk_ref[...] = z[:, :2 * FOX_WIDTH]
        bd = bd_ref[...]
        rq = lax.rsqrt(_group_sum(q * q, bd) * (1.0 / FOX_HEAD_DIM) + EPS)
        rk = lax.rsqrt(_group_sum(k * k, bd) * (1.0 / FOX_HEAD_DIM) + EPS)
        qn_ref[...] = ((q * rq) * gq_ref[...]).astype(_MXU)
        kn_ref[...] = ((k * rk) * gk_ref[...]).astype(_MXU)
        v_ref[...] = z[:, 2 * FOX_WIDTH:3 * FOX_WIDTH].astype(_MXU)
        pin_ref[...] = z[:, 3 * FOX_WIDTH:]
        fl_ref[...] = _dot(xn, wf_ref[...]) + bf_ref[...]

    return pl.pallas_call(
        body, name="mix_in_fwd", grid=(T // tm,),
        in_specs=[_rows(tm, D_MODEL), _resident((1, D_MODEL)), _resident(w_main.shape), _resident(w_f.shape),
                  _resident((1, LANES)), _resident((1, FOX_WIDTH)), _resident((1, FOX_WIDTH)), _resident(bd64.shape)],
        out_specs=[_rows(tm, D_MODEL), _rows(tm, 2 * FOX_WIDTH), _rows(tm, FOX_WIDTH), _rows(tm, FOX_WIDTH),
                   _rows(tm, FOX_WIDTH), _rows(tm, POOL_WIDTH), _rows(tm, LANES)],
        out_shape=[_sds((T, D_MODEL), _MXU), _sds((T, 2 * FOX_WIDTH), _F32), _sds((T, FOX_WIDTH), _MXU),
                   _sds((T, FOX_WIDTH), _MXU), _sds((T, FOX_WIDTH), _MXU), _sds((T, POOL_WIDTH), _F32),
                   _sds((T, LANES), _F32)],
    )(h, g_mix, w_main, w_f, b_f, gq, gk, bd64)


def _split3(x):
    h1 = x.astype(jnp.bfloat16).astype(_F32)
    r1 = x - h1
    h2 = r1.astype(jnp.bfloat16).astype(_F32)
    h3 = (r1 - h2).astype(jnp.bfloat16).astype(_F32)
    return h1, h2, h3


_ONES3 = (1.0, 1.0, 1.0)
_ZEROS3 = (0.0, 0.0, 0.0)


def _aug_head(feat, first, second):
    rows = feat.shape[0]
    lane = lax.broadcasted_iota(jnp.int32, (rows, LANES - FOX_HEAD_DIM), 1)
    aux = jnp.zeros((rows, LANES - FOX_HEAD_DIM), _F32)
    for k in range(3):
        aux = jnp.where(lane == k, first[k], aux)
        aux = jnp.where(lane == 3 + k, second[k], aux)
    return jnp.concatenate([feat.astype(_MXU), aux.astype(_MXU)], axis=-1)


def _gate_fwd(fl, qn, kn, v):
    T = fl.shape[0]
    tb = min(_TB, T)
    wide = FOX_HEADS * LANES

    def body(fl_ref, q_ref, k_ref, v_ref, qa_ref, ka_ref, va_ref, carry):
        @pl.when(pl.program_id(0) == 0)
        def _():
            carry[...] = jnp.zeros_like(carry)

        x = fl_ref[...]
        ls = jnp.minimum(x, 0.0) - jnp.log1p(jnp.exp(-jnp.abs(x)))
        row = lax.broadcasted_iota(jnp.int32, (tb, tb), 0)
        col = lax.broadcasted_iota(jnp.int32, (tb, tb), 1)
        tri = jnp.where(col <= row, 1.0, 0.0).astype(jnp.bfloat16)
        cs = _tri_dot(tri, ls) + carry[...]
        carry[...] = cs[tb - 1:tb, :]
        for h in range(FOX_HEADS):
            sl = slice(h * FOX_HEAD_DIM, (h + 1) * FOX_HEAD_DIM)
            out = slice(h * LANES, (h + 1) * LANES)
            c3 = _split3(cs[:, h:h + 1])
            qa_ref[:, out] = _aug_head(q_ref[:, sl].astype(_F32) * (FOX_HEAD_DIM ** -0.5), c3, _ONES3)
            ka_ref[:, out] = _aug_head(k_ref[:, sl], _ONES3, tuple(-t for t in c3))
            va_ref[:, out] = _aug_head(v_ref[:, sl], _ONES3, _ZEROS3)

    return pl.pallas_call(
        body, name="gate_fwd", grid=(T // tb,),
        in_specs=[_rows(tb, LANES), _rows(tb, FOX_WIDTH), _rows(tb, FOX_WIDTH), _rows(tb, FOX_WIDTH)],
        out_specs=[_rows(tb, wide)] * 3, out_shape=[_sds((T, wide), _MXU)] * 3,
        scratch_shapes=[pltpu.VMEM((1, LANES), _F32)],
    )(fl, qn, kn, v)


def _gate_bwd(drs, dcs, fl):
    T = fl.shape[0]
    tb = min(_TB, T)
    nb = T // tb

    def body(r_ref, d_ref, fl_ref, df_ref, carry):
        @pl.when(pl.program_id(0) == 0)
        def _():
            carry[...] = jnp.zeros_like(carry)

        lane = lax.broadcasted_iota(jnp.int32, (tb, LANES), 1)
        dc = jnp.zeros((tb, LANES), _F32)
        for p in range(FOX_HEADS // 2):
            pair = r_ref[p] - d_ref[p]
            for hh in range(2):
                dc = jnp.where(lane == 2 * p + hh, pair[:, hh:hh + 1], dc)
        row = lax.broadcasted_iota(jnp.int32, (tb, tb), 0)
        col = lax.broadcasted_iota(jnp.int32, (tb, tb), 1)
        tri = jnp.where(col >= row, 1.0, 0.0).astype(jnp.bfloat16)
        rc = _tri_dot(tri, dc) + carry[...]
        carry[...] = rc[0:1, :]
        df_ref[...] = rc * (1.0 / (1.0 + jnp.exp(fl_ref[...])))

    rev = pl.BlockSpec((tb, LANES), lambda i: (nb - 1 - i, 0))
    rev4 = pl.BlockSpec((FOX_HEADS // 2, tb, LANES), lambda i: (0, nb - 1 - i, 0))
    return pl.pallas_call(
        body, name="gate_bwd", grid=(nb,), in_specs=[rev4, rev4, rev], out_specs=rev,
        out_shape=_sds((T, LANES), _F32), scratch_shapes=[pltpu.VMEM((1, LANES), _F32)],
    )(drs, dcs, fl)


def _fox_scores(q_ref, k_ref, hh, masked):
    sl = slice(LANES * hh, LANES * (hh + 1))
    s = _dot_nt(q_ref[:, sl], k_ref[:, sl])
    if masked:
        row = lax.broadcasted_iota(jnp.int32, s.shape, 0)
        col = lax.broadcasted_iota(jnp.int32, s.shape, 1)
        s = jnp.where(col <= row, s, -jnp.inf)
    return s


def _causal_steps(nq, query_major):
    if query_major:
        steps = [(i, j) for i in range(nq) for j in range(i + 1)]
    else:
        steps = [(i, j) for j in range(nq) for i in range(j, nq)]
    return (jnp.asarray(np.array([s[0] for s in steps], np.int32)),
            jnp.asarray(np.array([s[1] for s in steps], np.int32)))


def _two_lanes(a, b):
    lane = lax.broadcasted_iota(jnp.int32, (a.shape[0], LANES), 1)
    return jnp.where(lane == 0, a, jnp.where(lane == 1, b, 0.0))


def _gather_behind(pid, t, n_pairs, steps, srcs, outs, send_sems, recv_sems):
    x, y, c = _mesh_pos()
    me = 2 * x + y
    sibling = (x, y, 1 - c)
    chips = _other_chips(x, y)

    def first(a, j):
        src, out = srcs[a], outs[a]
        mine = _half_rows(src.shape[0], c)
        return _remote(src.at[mine], out.at[me, mine], send_sems, recv_sems, 6 * a + j, (*chips[j], c))

    def landed(a, j):
        cx, cy = chips[j]
        slab = outs[a].at[2 * cx + cy, _half_rows(outs[a].shape[1], c)]
        return _remote(slab, slab, send_sems, recv_sems, 6 * a + j, (cx, cy, c))

    def passed(a, j):
        cx, cy = chips[j]
        slab = outs[a].at[2 * cx + cy, _half_rows(outs[a].shape[1], c)]
        return _remote(slab, slab, send_sems, recv_sems, 6 * a + 3 + j, sibling)

    def from_sibling(a, j):
        cx, cy = chips[j]
        slab = outs[a].at[2 * cx + cy, _half_rows(outs[a].shape[1], 1 - c)]
        return _remote(slab, slab, send_sems, recv_sems, 6 * a + 3 + j, sibling)

    every = [(a, j) for a in range(len(srcs)) for j in range(3)]

    @pl.when((pid == 0) & (t == 0))
    def _():
        for a, j in every:
            first(a, j).start()

    @pl.when((pid == n_pairs // 2) & (t == 0))
    def _():
        for a, j in every:
            landed(a, j).wait_recv()
            passed(a, j).start()

    @pl.when((pid == n_pairs - 1) & (t == steps - 1))
    def _():
        for a, j in every:
            from_sibling(a, j).wait_recv()
        for a, j in every:
            first(a, j).wait_send()
            passed(a, j).wait_send()


def _fox_fwd(qa, ka, va, gather=()):
    T = qa.shape[0]
    tq = min(_TQ, T)
    nq = T // tq
    ii, jj = _causal_steps(nq, True)
    steps = int(ii.shape[0])
    n = len(gather)
    n_pairs = FOX_HEADS // 2

    def body(ii_ref, jj_ref, q_ref, k_ref, v_ref, *refs):
        srcs, (o_ref, p_ref, mb_ref, lse_ref), outs = refs[:n], refs[n:n + 4], refs[n + 4:2 * n + 4]
        m_s, acc_s = refs[2 * n + 4:2 * n + 6]
        t = pl.program_id(1)
        i = ii_ref[t]
        j = jj_ref[t]
        if n:
            _gather_behind(pl.program_id(0), t, n_pairs, steps, srcs, outs, *refs[2 * n + 6:])

        @pl.when(j == 0)
        def _():
            m_s[...] = jnp.full(m_s.shape, -jnp.inf, _F32)
            acc_s[...] = jnp.zeros_like(acc_s)

        def step(masked):
            for hh in range(2):
                sl = slice(LANES * hh, LANES * (hh + 1))
                st = _dot_nt(k_ref[:, sl], q_ref[:, sl])
                if masked:
                    key = lax.broadcasted_iota(jnp.int32, st.shape, 0)
                    qry = lax.broadcasted_iota(jnp.int32, st.shape, 1)
                    st = jnp.where(key <= qry, st, -jnp.inf)
                top = jnp.broadcast_to(jnp.max(st, axis=0, keepdims=True), (LANES, tq)).T[:, :1]
                s = _fox_scores(q_ref, k_ref, hh, masked)
                m_prev = m_s[hh]
                m_new = jnp.maximum(m_prev, top)
                p_ref[0, 0, hh] = jnp.exp(s - m_new).astype(_MXU)
                acc_s[hh] = (jnp.exp(m_prev - m_new) * acc_s[hh]
                             + _dot(p_ref[0, 0, hh], v_ref[:, LANES * hh:LANES * (hh + 1)]))
                m_s[hh] = m_new
            mb_ref[0, 0] = _two_lanes(m_s[0], m_s[1])

        @pl.when(j < i)
        def _():
            step(False)

        @pl.when(j == i)
        def _():
            step(True)
            outs, lses = [], []
            for hh in range(2):
                acc = acc_s[hh]
                l = acc[:, FOX_HEAD_DIM:FOX_HEAD_DIM + 1]
                outs.append(acc[:, :FOX_HEAD_DIM] / l)
                lses.append(m_s[hh] + jnp.log(l))
            o_ref[...] = jnp.concatenate(outs, axis=-1).astype(_MXU)
            lse_ref[0] = _two_lanes(lses[0], lses[1])

    qspec = pl.BlockSpec((tq, 2 * LANES), lambda p, t, ii, jj: (ii[t], p))
    kspec = pl.BlockSpec((tq, 2 * LANES), lambda p, t, ii, jj: (jj[t], p))
    sems = [pltpu.SemaphoreType.DMA((6 * n,)), pltpu.SemaphoreType.DMA((6 * n,))] if n else []
    res = pl.pallas_call(
        body, name="fox_fwd_gather" if n else "fox_fwd",
        grid_spec=pltpu.PrefetchScalarGridSpec(
            num_scalar_prefetch=2, grid=(n_pairs, steps),
            in_specs=[qspec, kspec, kspec] + [_ANY] * n,
            out_specs=[pl.BlockSpec((tq, 2 * FOX_HEAD_DIM), lambda p, t, ii, jj: (ii[t], p)),
                       pl.BlockSpec((1, 1, 2, tq, tq), lambda p, t, ii, jj: (p, t, 0, 0, 0)),
                       pl.BlockSpec((1, 1, tq, LANES), lambda p, t, ii, jj: (p, t, 0, 0)),
                       pl.BlockSpec((1, tq, LANES), lambda p, t, ii, jj: (p, ii[t], 0))] + [_ANY] * n,
            scratch_shapes=[pltpu.VMEM((2, tq, 1), _F32), pltpu.VMEM((2, tq, LANES), _F32)] + sems),
        out_shape=[_sds((T, FOX_WIDTH), _MXU), _sds((n_pairs, steps, 2, tq, tq), _MXU),
                   _sds((n_pairs, steps, tq, LANES), _F32), _sds((n_pairs, T, LANES), _F32)]
        + [_sds((N_CHIPS,) + s.shape, s.dtype) for s in gather],
    )(ii, jj, qa, ka, va, *gather)
    return res[0], res[1], res[2], res[3], list(res[4:])


def _pool_window_sum(ext, w, forward):
    n = ext.shape[0]
    sm = ext
    k = 1
    while k < w:
        sm = sm + pltpu.roll(sm, (n - k) if forward else k, axis=0)
        k *= 2
    return sm


def _out_proj_fwd(h, o, pin, w_pool, pscale, w_out):
    T = h.shape[0]
    tm = min(_TM, T)
    hb = tm // HALO

    def body(h_ref, o_ref, pin_ref, halo_ref, wp_ref, ps_ref, wo_ref, h1_ref, mixed_ref, y_ref):
        i = pl.program_id(0)
        pin_t = pin_ref[...]
        halo = jnp.where(i == 0, 0.0, halo_ref[...])
        ext = jnp.concatenate([halo, pin_t], axis=0)
        t = (i * tm + lax.broadcasted_iota(jnp.int32, (tm, 1), 0) + 1).astype(_F32)
        mixed, ys = [], []
        for g, w in enumerate(POOL_WINDOWS):
            sl = slice(g * POOL_GROUP_DIM, (g + 1) * POOL_GROUP_DIM)
            win = _pool_window_sum(ext[:, sl], w, False)[HALO:, :]
            mg = (win / jnp.minimum(t, float(w)) - pin_t[:, sl]).astype(_MXU)
            mixed.append(mg)
            ys.append(_dot(mg, wp_ref[g]))
        mixed_ref[...] = jnp.concatenate(mixed, axis=-1)
        y = (jnp.concatenate(ys, axis=-1) * ps_ref[...]).astype(_MXU)
        y_ref[...] = y
        h1_ref[...] = h_ref[...] + _dot(o_ref[...], wo_ref[:FOX_WIDTH, :]) + _dot(y, wo_ref[FOX_WIDTH:, :])

    return pl.pallas_call(
        body, name="out_proj_fwd", grid=(T // tm,),
        in_specs=[_rows(tm, D_MODEL), _rows(tm, FOX_WIDTH), _rows(tm, POOL_WIDTH),
                  pl.BlockSpec((HALO, POOL_WIDTH), lambda i: (jnp.maximum(i * hb - 1, 0), 0)),
                  _resident(w_pool.shape), _resident((1, POOL_WIDTH)), _resident(w_out.shape)],
        out_specs=[_rows(tm, D_MODEL), _rows(tm, POOL_WIDTH), _rows(tm, POOL_WIDTH)],
        out_shape=[_sds((T, D_MODEL), _F32), _sds((T, POOL_WIDTH), _MXU), _sds((T, POOL_WIDTH), _MXU)],
    )(h, o, pin, pin, w_pool, pscale, w_out)


def _mem_kv_fwd(mem, g_kv, w_kv, gkm, bd128):
    M = mem.shape[0]

    def body(mem_ref, g_ref, w_ref, gk_ref, bd_ref, mn_ref, mkv_ref, mk_ref, mv_ref):
        x = mem_ref[...]
        mn = ((x * _rstd(x)) * g_ref[...]).astype(_MXU)
        mn_ref[...] = mn
        z = _dot(mn, w_ref[...])
        mkv_ref[...] = z
        k = z[:, :MEM_WIDTH]
        rk = lax.rsqrt(_group_sum(k * k, bd_ref[...]) * (1.0 / MEM_HEAD_DIM) + EPS)
        mk_ref[...] = ((k * rk) * gk_ref[...]).astype(_MXU)
        mv_ref[...] = z[:, MEM_WIDTH:].astype(_MXU)

    return pl.pallas_call(
        body, name="mem_kv_fwd",
        out_shape=[_sds((M, D_MODEL), _MXU), _sds((M, 2 * MEM_WIDTH), _F32), _sds((M, MEM_WIDTH), _MXU),
                   _sds((M, MEM_WIDTH), _MXU)],
    )(mem, g_kv, w_kv, gkm, bd128)


def _mem_softmax(qn, mk_ref, hd):
    sl = slice(hd * MEM_HEAD_DIM, (hd + 1) * MEM_HEAD_DIM)
    s = _dot_nt(qn[:, sl], mk_ref[:, sl]) * (MEM_HEAD_DIM ** -0.5)
    e = jnp.exp(s - jnp.max(s, axis=-1, keepdims=True))
    return e / jnp.sum(e, axis=-1, keepdims=True)


def _mem_attn_fwd(h1, g_q, w_q, gqm, bd128, mk, mv, w_mo):
    T = h1.shape[0]
    tm = min(_TM, T)

    def body(h_ref, g_ref, wq_ref, gq_ref, bd_ref, mk_ref, mv_ref, wo_ref, h2_ref, hn_ref, mo_ref):
        x = h_ref[...]
        hn = ((x * _rstd(x)) * g_ref[...]).astype(_MXU)
        hn_ref[...] = hn
        mq = _dot(hn, wq_ref[...])
        rq = lax.rsqrt(_group_sum(mq * mq, bd_ref[...]) * (1.0 / MEM_HEAD_DIM) + EPS)
        qn = ((mq * rq) * gq_ref[...]).astype(_MXU)
        outs = []
        for hd in range(MEM_HEADS):
            p = _mem_softmax(qn, mk_ref, hd).astype(_MXU)
            outs.append(_dot(p, mv_ref[:, hd * MEM_HEAD_DIM:(hd + 1) * MEM_HEAD_DIM]))
        mo = jnp.concatenate(outs, axis=-1).astype(_MXU)
        mo_ref[...] = mo
        h2_ref[...] = x + _dot(mo, wo_ref[...])

    return pl.pallas_call(
        body, name="mem_attn_fwd", grid=(T // tm,),
        in_specs=[_rows(tm, D_MODEL), _resident((1, D_MODEL)), _resident(w_q.shape), _resident((1, MEM_WIDTH)),
                  _resident(bd128.shape), _resident(mk.shape), _resident(mv.shape), _resident(w_mo.shape)],
        out_specs=[_rows(tm, D_MODEL), _rows(tm, D_MODEL), _rows(tm, MEM_WIDTH)],
        out_shape=[_sds((T, D_MODEL), _F32), _sds((T, D_MODEL), _MXU), _sds((T, MEM_WIDTH), _MXU)],
    )(h1, g_q, w_q, gqm, bd128, mk, mv, w_mo)


def _ffn_weight_specs(layer):
    nf = D_FF // _TF
    return [pl.BlockSpec((1, D_MODEL, _TF), lambda i, j: (j, layer, 0)),
            pl.BlockSpec((1, D_MODEL, _TF), lambda i, j: (j + nf, layer, 0)),
            pl.BlockSpec((2, _TF // 2, D_MODEL), lambda i, j: (j, layer, 0))]


def _ffn_fwd(h2, g_ffn, w_gu, w_d, layer):
    T = h2.shape[0]
    tm = min(_TMF, T)
    nf = D_FF // _TF

    def body(h_ref, g_ref, wg_ref, wu_ref, wd_ref, h3_ref, hn_ref, acc, xn_s):
        j = pl.program_id(1)

        @pl.when(j == 0)
        def _():
            x = h_ref[...]
            xn = ((x * _rstd(x)) * g_ref[...]).astype(_MXU)
            xn_s[...] = xn
            hn_ref[...] = xn
            acc[...] = jnp.zeros_like(acc)

        xn = xn_s[...]
        g = _dot(xn, wg_ref[0])
        u = _dot(xn, wu_ref[0])
        a = ((g * jax.nn.sigmoid(g)) * u).astype(_MXU)
        acc[...] += _dot(a, wd_ref[...].reshape(_TF, D_MODEL))

        @pl.when(j == nf - 1)
        def _():
            h3_ref[...] = h_ref[...] + acc[...]

    tok = pl.BlockSpec((tm, D_MODEL), lambda i, j: (i, 0))
    return pl.pallas_call(
        body, name="ffn_fwd", grid=(T // tm, nf),
        in_specs=[tok, pl.BlockSpec((1, D_MODEL), lambda i, j: (0, 0))] + _ffn_weight_specs(layer),
        out_specs=[tok, tok],
        out_shape=[_sds((T, D_MODEL), _F32), _sds((T, D_MODEL), _MXU)],
        scratch_shapes=[pltpu.VMEM((tm, D_MODEL), _F32), pltpu.VMEM((tm, D_MODEL), _MXU)],
    )(h2, g_ffn, w_gu, w_gu, w_d)


def _loss_grad(y, tgt):
    T = y.shape[0]
    tm = min(_TA, T)

    def body(y_ref, t_ref, dy_ref, loss_ref):
        @pl.when(pl.program_id(0) == 0)
        def _():
            loss_ref[...] = jnp.zeros_like(loss_ref)

        err = y_ref[...] - t_ref[...]
        dy_ref[...] = err * (1.0 / D_MODEL)
        part = jnp.sum(jnp.sum(err * err, axis=0, keepdims=True), axis=1, keepdims=True)
        loss_ref[...] += part * (0.5 / D_MODEL)

    return pl.pallas_call(
        body, name="loss_grad", grid=(T // tm,), in_specs=[_rows(tm, D_MODEL), _rows(tm, D_MODEL)],
        out_specs=[_rows(tm, D_MODEL), pl.BlockSpec((1, 1), lambda i: (0, 0))],
        out_shape=[_sds((T, D_MODEL), _F32), _sds((1, 1), _F32)],
    )(y, tgt)


def _pick_tile(n, candidates=(1408, 1024, 512, 256, 128)):
    for c in candidates:
        if n % c == 0:
            return c
    return n


def _matmul_tn(a, b, name, tk=None, tn=None, dst=None, dst_shape=None, dst_index=None, dst_chips=1):
    T, K = a.shape
    N = b.shape[1]
    tk, tn, tt = tk or _pick_tile(K), tn or _pick_tile(N), min(_TT, T)

    def body(*refs):
        a_ref, b_ref, o_ref = refs[0], refs[1], refs[-1]

        @pl.when(pl.program_id(2) == 0)
        def _():
            o_ref[...] = jnp.zeros_like(o_ref)

        o_ref[...] += _dot_tn(a_ref[...].astype(_MXU), b_ref[...].astype(_MXU)).reshape(o_ref.shape)

    in_specs = [pl.BlockSpec((tt, tk), lambda i, j, t: (t, i)), pl.BlockSpec((tt, tn), lambda i, j, t: (t, j))]
    if dst_shape is None:
        out_spec, out_shape, args, alias = pl.BlockSpec((tk, tn), lambda i, j, t: (i, j)), (K, N), (a, b), {}
    else:
        out_spec = pl.BlockSpec((dst_chips, tk // dst_chips, tn), lambda i, j, t: dst_index(i, j))
        out_shape, args, alias = dst_shape, (a, b), {}
        if dst is not None:
            in_specs, args, alias = in_specs + [_ANY], (a, b, dst), {2: 0}
    return pl.pallas_call(
        body, name=name, grid=(K // tk, N // tn, T // tt), in_specs=in_specs, out_specs=out_spec,
        out_shape=_sds(out_shape, _F32), input_output_aliases=alias,
    )(*args)


def _ffn_bwd(dh3, h2, hn, g_ffn, w_gu, w_d, layer):
    T = h2.shape[0]
    tm = min(_TMF, T)
    nf = D_FF // _TF

    def body(dh_ref, h_ref, hn_ref, g_ref, wg_ref, wu_ref, wd_ref, dh2_ref, act_ref, dg_ref, du_ref, dgain_ref,
             acc, dyb):
        i = pl.program_id(0)
        j = pl.program_id(1)

        @pl.when((i == 0) & (j == 0))
        def _():
            dgain_ref[...] = jnp.zeros_like(dgain_ref)

        @pl.when(j == 0)
        def _():
            dyb[...] = dh_ref[...].astype(_MXU)
            acc[...] = jnp.zeros_like(acc)

        xn = hn_ref[...]
        wg = wg_ref[0]
        wu = wu_ref[0]
        g = _dot(xn, wg)
        u = _dot(xn, wu)
        sg = jax.nn.sigmoid(g)
        sl = g * sg
        act_ref[...] = (sl * u).astype(_MXU)
        da = _dot_nt(dyb[...], wd_ref[...].reshape(_TF, D_MODEL))
        dgate = (da * u * (sg * (1.0 + g * (1.0 - sg)))).astype(_MXU)
        dup = (da * sl).astype(_MXU)
        dg_ref[...] = dgate
        du_ref[...] = dup
        acc[...] += _dot_nt(dgate, wg) + _dot_nt(dup, wu)

        @pl.when(j == nf - 1)
        def _():
            dhn = acc[...]
            dx, dgain = _norm_bwd(dhn, h_ref[...], g_ref[...])
            dh2_ref[...] = dh_ref[...] + dx
            dgain_ref[...] += dgain

    tok = pl.BlockSpec((tm, D_MODEL), lambda i, j: (i, 0))
    ffb = pl.BlockSpec((tm, _TF), lambda i, j: (i, j))
    row = pl.BlockSpec((1, D_MODEL), lambda i, j: (0, 0))
    return pl.pallas_call(
        body, name="ffn_bwd", grid=(T // tm, nf),
        in_specs=[tok, tok, tok, row] + _ffn_weight_specs(layer),
        out_specs=[tok, ffb, ffb, ffb, row],
        out_shape=[_sds((T, D_MODEL), _F32), _sds((T, D_FF), _MXU), _sds((T, D_FF), _MXU), _sds((T, D_FF), _MXU),
                   _sds((1, D_MODEL), _F32)],
        scratch_shapes=[pltpu.VMEM((tm, D_MODEL), _F32), pltpu.VMEM((tm, D_MODEL), _MXU)],
    )(dh3, h2, hn, g_ffn, w_gu, w_gu, w_d)


def _mem_attn_bwd(dh2, h1, g_q, w_q, gqm, bd128, mk, mv, w_mo):
    T = h1.shape[0]
    M = mk.shape[0]
    tm = min(_TM, T)
    nt = T // tm

    def body(dh_ref, h_ref, g_ref, wq_ref, gq_ref, bd_ref, mk_ref, mv_ref, wo_ref,
             dh1_ref, dmq_ref, dmk_ref, dmv_ref, dgq_ref, dgain_ref, dgq_acc):
        i = pl.program_id(0)

        @pl.when(i == 0)
        def _():
            dmk_ref[...] = jnp.zeros_like(dmk_ref)
            dmv_ref[...] = jnp.zeros_like(dmv_ref)
            dgain_ref[...] = jnp.zeros_like(dgain_ref)
            dgq_acc[...] = jnp.zeros_like(dgq_acc)

        x = h_ref[...]
        g = g_ref[...]
        bd = bd_ref[...]
        hn = ((x * _rstd(x)) * g).astype(_MXU)
        mq = _dot(hn, wq_ref[...])
        rq = lax.rsqrt(_group_sum(mq * mq, bd) * (1.0 / MEM_HEAD_DIM) + EPS)
        qn = ((mq * rq) * gq_ref[...]).astype(_MXU)
        dmo = _dot_nt(dh_ref[...].astype(_MXU), wo_ref[...])
        dqn = []
        for hd in range(MEM_HEADS):
            sl = slice(hd * MEM_HEAD_DIM, (hd + 1) * MEM_HEAD_DIM)
            p = _mem_softmax(qn, mk_ref, hd)
            dmo_h = dmo[:, sl].astype(_MXU)
            dp = _dot_nt(dmo_h, mv_ref[:, sl])
            ds = (p * (dp - jnp.sum(p * dp, axis=-1, keepdims=True)) * (MEM_HEAD_DIM ** -0.5)).astype(_MXU)
            dqn.append(_dot(ds, mk_ref[:, sl]))
            dmk_ref[:, sl] += _dot_tn(ds, qn[:, sl])
            dmv_ref[:, sl] += _dot_tn(p.astype(_MXU), dmo_h)
        dqn = jnp.concatenate(dqn, axis=-1)
        dmq, dgq = _headnorm_bwd(dqn, mq, gq_ref[...], bd, MEM_HEAD_DIM)
        dgq_acc[...] += dgq
        dmq = dmq.astype(_MXU)
        dmq_ref[...] = dmq
        dhn = _dot_nt(dmq, wq_ref[...])
        dx, dgain = _norm_bwd(dhn, x, g)
        dh1_ref[...] = dh_ref[...] + dx
        dgain_ref[...] += dgain

        @pl.when(i == nt - 1)
        def _():
            dgq_ref[...] = _fold_heads(dgq_acc[...], MEM_HEADS, MEM_HEAD_DIM)

    const2 = lambda shape: pl.BlockSpec(shape, lambda i: (0, 0))
    return pl.pallas_call(
        body, name="mem_attn_bwd", grid=(nt,),
        in_specs=[_rows(tm, D_MODEL), _rows(tm, D_MODEL), _resident((1, D_MODEL)), _resident(w_q.shape),
                  _resident((1, MEM_WIDTH)), _resident(bd128.shape), _resident(mk.shape), _resident(mv.shape),
                  _resident(w_mo.shape)],
        out_specs=[_rows(tm, D_MODEL), _rows(tm, MEM_WIDTH), const2((M, MEM_WIDTH)), const2((M, MEM_WIDTH)),
                   const2((1, MEM_HEAD_DIM)), const2((1, D_MODEL))],
        out_shape=[_sds((T, D_MODEL), _F32), _sds((T, MEM_WIDTH), _MXU), _sds((M, MEM_WIDTH), _F32),
                   _sds((M, MEM_WIDTH), _F32), _sds((1, MEM_HEAD_DIM), _F32), _sds((1, D_MODEL), _F32)],
        scratch_shapes=[pltpu.VMEM((1, MEM_WIDTH), _F32)],
    )(dh2, h1, g_q, w_q, gqm, bd128, mk, mv, w_mo)


def _mem_kv_bwd(dmk, dmv, mkv, mn, mem, g_kv, gkm, bd128, w_kv, dst, dst_shape, block):
    rows = D_MODEL // N_CHIPS

    def body(dmk_ref, dmv_ref, mkv_ref, mn_ref, mem_ref, g_ref, gk_ref, bd_ref, w_ref, *rest):
        dw_ref, dgk_ref, dgain_ref = rest[-3:]
        kraw = mkv_ref[:, :MEM_WIDTH]
        dk, dgk = _headnorm_bwd(dmk_ref[...], kraw, gk_ref[...], bd_ref[...], MEM_HEAD_DIM)
        dgk_ref[...] = _fold_heads(dgk, MEM_HEADS, MEM_HEAD_DIM)
        dmkv = jnp.concatenate([dk, dmv_ref[...]], axis=-1).astype(_MXU)
        dw_ref[...] = _dot_tn(mn_ref[...], dmkv).reshape(N_CHIPS, rows, 2 * MEM_WIDTH)
        dmn = _dot_nt(dmkv, w_ref[...])
        _, dgain = _norm_bwd(dmn, mem_ref[...], g_ref[...])
        dgain_ref[...] = dgain

    args = (dmk, dmv, mkv, mn, mem, g_kv, gkm, bd128, w_kv)
    whole = lambda a: pl.BlockSpec(a.shape, lambda i: (0,) * a.ndim)
    in_specs, alias = [whole(a) for a in args], {}
    if dst is not None:
        in_specs, args, alias = in_specs + [_ANY], args + (dst,), {len(args): 0}
    return pl.pallas_call(
        body, name="mem_kv_bwd", grid=(1,), in_specs=in_specs,
        out_specs=[pl.BlockSpec((N_CHIPS, rows, 2 * MEM_WIDTH), lambda i: (0, block, 0)),
                   pl.BlockSpec((1, MEM_HEAD_DIM), lambda i: (0, 0)), pl.BlockSpec((1, D_MODEL), lambda i: (0, 0))],
        out_shape=[_sds(dst_shape, _F32), _sds((1, MEM_HEAD_DIM), _F32), _sds((1, D_MODEL), _F32)],
        input_output_aliases=alias,
    )(*args)


def _out_proj_bwd(dh1, mixed, o, bd64, w_pool, pscale, w_out):
    T = dh1.shape[0]
    tm = min(_TM, T)
    hb = tm // HALO
    nt = T // tm

    def body(dh_ref, halo_ref, mx_ref, o_ref, bd_ref, wp_ref, ps_ref, wo_ref, doa_ref, dpin_ref, dwp_ref, dps_ref):
        i = pl.program_id(0)

        @pl.when(i == 0)
        def _():
            dwp_ref[...] = jnp.zeros_like(dwp_ref)
            dps_ref[...] = jnp.zeros_like(dps_ref)

        dcat = _dot_nt(dh_ref[...].astype(_MXU), wo_ref[...])
        do = dcat[:, :FOX_WIDTH].astype(_MXU)
        delta = _group_sum(do.astype(_F32) * o_ref[...].astype(_F32), bd_ref[...])
        for h in range(FOX_HEADS):
            sl = slice(h * FOX_HEAD_DIM, (h + 1) * FOX_HEAD_DIM)
            doa_ref[:, h * LANES:(h + 1) * LANES] = _aug_head(
                do[:, sl], _split3(-delta[:, h * FOX_HEAD_DIM:h * FOX_HEAD_DIM + 1]), _ZEROS3)
        dy = dcat[:, FOX_WIDTH:]
        dyh = _dot_nt(halo_ref[...].astype(_MXU), wo_ref[FOX_WIDTH:, :])
        dyh = jnp.where(i == nt - 1, 0.0, dyh)
        ps = ps_ref[...]
        t = (i * tm + lax.broadcasted_iota(jnp.int32, (tm + HALO, 1), 0) + 1).astype(_F32)
        mixed_t = mx_ref[...]
        dpin, dps = [], []
        for g, w in enumerate(POOL_WINDOWS):
            sl = slice(g * POOL_GROUP_DIM, (g + 1) * POOL_GROUP_DIM)
            mg = mixed_t[:, sl]
            wg = wp_ref[g]
            dps.append(jnp.sum(dy[:, sl] * _dot(mg, wg), axis=0, keepdims=True))
            dyl = (dy[:, sl] * ps[:, sl]).astype(_MXU)
            dylh = (dyh[:, sl] * ps[:, sl]).astype(_MXU)
            dwp_ref[g] += _dot_tn(mg, dyl)
            dmx = _dot_nt(dyl, wg)
            ext = jnp.concatenate([dmx, _dot_nt(dylh, wg)], axis=0) / jnp.minimum(t, float(w))
            dpin.append(_pool_window_sum(ext, w, True)[:tm, :] - dmx)
        dpin_ref[...] = jnp.concatenate(dpin, axis=-1)
        dps_ref[...] += jnp.concatenate(dps, axis=-1)

    return pl.pallas_call(
        body, name="out_proj_bwd", grid=(nt,),
        in_specs=[_rows(tm, D_MODEL),
                  pl.BlockSpec((HALO, D_MODEL), lambda i: (jnp.minimum((i + 1) * hb, T // HALO - 1), 0)),
                  _rows(tm, POOL_WIDTH), _rows(tm, FOX_WIDTH), _resident(bd64.shape), _resident(w_pool.shape),
                  _resident((1, POOL_WIDTH)), _resident(w_out.shape)],
        out_specs=[_rows(tm, FOX_HEADS * LANES), _rows(tm, POOL_WIDTH),
                   pl.BlockSpec(w_pool.shape, lambda i: (0, 0, 0)), pl.BlockSpec((1, POOL_WIDTH), lambda i: (0, 0))],
        out_shape=[_sds((T, FOX_HEADS * LANES), _MXU), _sds((T, POOL_WIDTH), _F32), _sds(w_pool.shape, _F32),
                   _sds((1, POOL_WIDTH), _F32)],
    )(dh1, dh1, mixed, o, bd64, w_pool, pscale, w_out)


def _fox_bwd(qa, ka, va, doa, pt, mb, lse):
    T = qa.shape[0]
    tq = min(_TQ, T)
    nq = T // tq
    pair = 2 * FOX_HEAD_DIM
    ii, jj = _causal_steps(nq, False)
    fwd_step = ii * (ii + 1) // 2 + jj
    row_lane, col_lane = FOX_HEAD_DIM, FOX_HEAD_DIM + 3

    def body(ii_ref, jj_ref, fs_ref, q_ref, k_ref, v_ref, do_ref, p_ref, mb_ref, lse_ref,
             dq_ref, dk_ref, dv_ref, drs_ref, dcs_ref, dk_acc, dv_acc):
        t = pl.program_id(1)
        i = ii_ref[t]
        j = jj_ref[t]

        @pl.when(t == 0)
        def _():
            dq_ref[...] = jnp.zeros_like(dq_ref)
            drs_ref[...] = jnp.zeros_like(drs_ref)

        @pl.when(i == j)
        def _():
            dk_acc[...] = jnp.zeros_like(dk_acc)
            dv_acc[...] = jnp.zeros_like(dv_acc)

        r = jnp.exp(mb_ref[0, 0] - lse_ref[0])
        dqs, rs = [], []
        for hh in range(2):
            sl = slice(LANES * hh, LANES * (hh + 1))
            dof = do_ref[:, sl].astype(_F32) * r[:, hh:hh + 1]
            dob = dof.astype(_MXU)
            p = p_ref[0, 0, hh]
            dv_acc[hh] += _dot(dof.T.astype(_MXU), p)
            dsb = (p.astype(_F32) * _dot_nt(dob, v_ref[:, sl])).astype(_MXU)
            dk_acc[hh] += _dot(q_ref[:, sl].astype(_F32).T.astype(_MXU), dsb)
            dqa = _dot(dsb, k_ref[:, sl])
            dqs.append(dqa[:, :FOX_HEAD_DIM] * (FOX_HEAD_DIM ** -0.5))
            rs.append(dqa[:, row_lane:row_lane + 1])
        rows = pl.ds(pl.multiple_of(i * tq, tq), tq)
        dq_ref[rows, :] += jnp.concatenate(dqs, axis=-1)
        drs_ref[0, rows, :] += _two_lanes(rs[0], rs[1])

        @pl.when(i == nq - 1)
        def _():
            dk = [dk_acc[hh].T for hh in range(2)]
            dv = [dv_acc[hh].T for hh in range(2)]
            dk_ref[...] = jnp.concatenate([dk[0][:, :FOX_HEAD_DIM], dk[1][:, :FOX_HEAD_DIM]], axis=-1)
            dv_ref[...] = jnp.concatenate([dv[0][:, :FOX_HEAD_DIM], dv[1][:, :FOX_HEAD_DIM]], axis=-1)
            dcs_ref[0] = _two_lanes(dk[0][:, col_lane:col_lane + 1], dk[1][:, col_lane:col_lane + 1])

    qspec = pl.BlockSpec((tq, 2 * LANES), lambda p, t, ii, jj, fs: (ii[t], p))
    kspec = pl.BlockSpec((tq, 2 * LANES), lambda p, t, ii, jj, fs: (jj[t], p))
    kout = pl.BlockSpec((tq, pair), lambda p, t, ii, jj, fs: (jj[t], p))
    return pl.pallas_call(
        body, name="fox_bwd",
        grid_spec=pltpu.PrefetchScalarGridSpec(
            num_scalar_prefetch=3, grid=(FOX_HEADS // 2, int(ii.shape[0])),
            in_specs=[qspec, kspec, kspec, qspec,
                      pl.BlockSpec((1, 1, 2, tq, tq), lambda p, t, ii, jj, fs: (p, fs[t], 0, 0, 0)),
                      pl.BlockSpec((1, 1, tq, LANES), lambda p, t, ii, jj, fs: (p, fs[t], 0, 0)),
                      pl.BlockSpec((1, tq, LANES), lambda p, t, ii, jj, fs: (p, ii[t], 0))],
            out_specs=[pl.BlockSpec((T, pair), lambda p, t, ii, jj, fs: (0, p)), kout, kout,
                       pl.BlockSpec((1, T, LANES), lambda p, t, ii, jj, fs: (p, 0, 0)),
                       pl.BlockSpec((1, tq, LANES), lambda p, t, ii, jj, fs: (p, jj[t], 0))],
            scratch_shapes=[pltpu.VMEM((2, LANES, tq), _F32), pltpu.VMEM((2, LANES, tq), _F32)]),
        out_shape=[_sds((T, FOX_WIDTH), _F32), _sds((T, FOX_WIDTH), _F32), _sds((T, FOX_WIDTH), _F32),
                   _sds((FOX_HEADS // 2, T, LANES), _F32), _sds((FOX_HEADS // 2, T, LANES), _F32)],
    )(ii, jj, fwd_step, qa, ka, va, doa, pt, mb, lse)


def _mix_in_bwd(dh1, h, g_mix, zqk, dq, dk, dv, dpin, df, gq, gk, bd64, w_main, w_f):
    T = h.shape[0]
    tm = min(_TM, T)
    nt = T // tm

    def body(dh1_ref, h_ref, g_ref, zqk_ref, dq_ref, dk_ref, dv_ref, dpin_ref, df_ref, gq_ref, gk_ref, bd_ref,
             wm_ref, wf_ref, dh_ref, dz_ref, dzf_ref, dgq_ref, dgk_ref, dgain_ref, dbf_ref, dgq_acc, dgk_acc):
        i = pl.program_id(0)

        @pl.when(i == 0)
        def _():
            dgain_ref[...] = jnp.zeros_like(dgain_ref)
            dbf_ref[...] = jnp.zeros_like(dbf_ref)
            dgq_acc[...] = jnp.zeros_like(dgq_acc)
            dgk_acc[...] = jnp.zeros_like(dgk_acc)

        bd = bd_ref[...]
        dqr, dgq = _headnorm_bwd(dq_ref[...], zqk_ref[:, :FOX_WIDTH], gq_ref[...], bd, FOX_HEAD_DIM)
        dkr, dgk = _headnorm_bwd(dk_ref[...], zqk_ref[:, FOX_WIDTH:], gk_ref[...], bd, FOX_HEAD_DIM)
        dgq_acc[...] += dgq
        dgk_acc[...] += dgk
        dz = jnp.concatenate([dqr, dkr, dv_ref[...], dpin_ref[...]], axis=-1).astype(_MXU)
        dz_ref[...] = dz
        df = df_ref[...]
        dzf = df.astype(_MXU)
        dzf_ref[...] = dzf
        dbf_ref[...] += jnp.sum(df, axis=0, keepdims=True)
        dxn = _dot_nt(dz, wm_ref[...]) + _dot_nt(dzf, wf_ref[...])
        dx, dgain = _norm_bwd(dxn, h_ref[...], g_ref[...])
        dh_ref[...] = dh1_ref[...] + dx
        dgain_ref[...] += dgain

        @pl.when(i == nt - 1)
        def _():
            dgq_ref[...] = _fold_heads(dgq_acc[...], FOX_HEADS, FOX_HEAD_DIM)
            dgk_ref[...] = _fold_heads(dgk_acc[...], FOX_HEADS, FOX_HEAD_DIM)

    const2 = lambda shape: pl.BlockSpec(shape, lambda i: (0, 0))
    return pl.pallas_call(
        body, name="mix_in_bwd", grid=(nt,),
        in_specs=[_rows(tm, D_MODEL), _rows(tm, D_MODEL), _resident((1, D_MODEL)), _rows(tm, 2 * FOX_WIDTH),
                  _rows(tm, FOX_WIDTH), _rows(tm, FOX_WIDTH), _rows(tm, FOX_WIDTH), _rows(tm, POOL_WIDTH),
                  _rows(tm, LANES), _resident((1, FOX_WIDTH)), _resident((1, FOX_WIDTH)), _resident(bd64.shape),
                  _resident(w_main.shape), _resident(w_f.shape)],
        out_specs=[_rows(tm, D_MODEL), _rows(tm, 4 * FOX_WIDTH), _rows(tm, LANES), const2((1, FOX_HEAD_DIM)),
                   const2((1, FOX_HEAD_DIM)), const2((1, D_MODEL)), const2((1, LANES))],
        out_shape=[_sds((T, D_MODEL), _F32), _sds((T, 4 * FOX_WIDTH), _MXU), _sds((T, LANES), _MXU),
                   _sds((1, FOX_HEAD_DIM), _F32), _sds((1, FOX_HEAD_DIM), _F32), _sds((1, D_MODEL), _F32),
                   _sds((1, LANES), _F32)],
        scratch_shapes=[pltpu.VMEM((1, FOX_WIDTH), _F32), pltpu.VMEM((1, FOX_WIDTH), _F32)],
    )(dh1, h, g_mix, zqk, dq, dk, dv, dpin, df, gq, gk, bd64, w_main, w_f)


CLASSES = (("a", D_MODEL, ("w_out", "w_mem_kv")), ("d", D_MODEL, ("w_down",)), ("g", _TF, ("w_gate_up",)),
           ("i", 514, ("w_in",)), ("q", MEM_WIDTH, ("w_mem_q",)), ("o", 256, ("w_mem_out",)))
W_OUT_ROWS = D_MODEL // N_CHIPS
W_DOWN_ROWS = D_FF // N_CHIPS
W_MEM_OUT_COLS = D_MODEL // N_CHIPS


def _late_params(WC, l):
    rows = lambda buf, first, n: buf[:, first:first + n]
    return dict(
        w_out=rows(WC["a"], l * W_OUT_ROWS, W_OUT_ROWS).reshape(D_MODEL, D_MODEL),
        w_mem_q=rows(WC["q"], l * W_OUT_ROWS, W_OUT_ROWS).reshape(D_MODEL, MEM_WIDTH),
        w_mem_kv=rows(WC["a"], (DEPTH + l) * W_OUT_ROWS, W_OUT_ROWS).reshape(D_MODEL, 2 * MEM_WIDTH),
        w_mem_out=rows(WC["o"], l * MEM_WIDTH, MEM_WIDTH).transpose(1, 0, 2).reshape(MEM_WIDTH, D_MODEL),
        w_gu=WC["g"],
        w_d=WC["d"],
    )


def _layer_params(WC, WS, l):
    rows = lambda buf, first, n: buf[:, first:first + n]
    w_in = rows(WC["i"], l * D_MODEL, D_MODEL).transpose(1, 0, 2).reshape(D_MODEL, -1)
    n_main = 3 * FOX_WIDTH + POOL_WIDTH
    row = lambda a: a.reshape(1, -1).astype(_F32)
    return dict(
        layer=l,
        g_mix=row(WS["g_mix"][l]),
        w_main=w_in[:, :n_main],
        w_f=jnp.pad(w_in[:, n_main:], ((0, 0), (0, LANES - FOX_HEADS))),
        b_f=jnp.pad(row(WS["b_forget"][l]), ((0, 0), (0, LANES - FOX_HEADS))),
        gq=jnp.tile(row(WS["g_q_fox"][l]), (1, FOX_HEADS)),
        gk=jnp.tile(row(WS["g_k_fox"][l]), (1, FOX_HEADS)),
        w_pool=WS["w_pool"][l].astype(_MXU),
        pscale=row(WS["pool_scale"][l]),
        g_mem_q=row(WS["g_mem_q"][l]),
        g_mem_kv=row(WS["g_mem_kv"][l]),
        gqm=jnp.tile(row(WS["g_q_mem"][l]), (1, MEM_HEADS)),
        gkm=jnp.tile(row(WS["g_k_mem"][l]), (1, MEM_HEADS)),
        g_ffn=row(WS["g_ffn"][l]),
    )


def _layer_fwd(h, mem, P, bd64, bd128, pending=None):
    s = dict(h=h)
    s["xn"], s["zqk"], qn, kn, v, pin, s["fl"] = _mix_in_fwd(
        h, P["g_mix"], P["w_main"], P["w_f"], P["b_f"], P["gq"], P["gk"], bd64)
    s["qa"], s["ka"], s["va"] = _gate_fwd(s["fl"], qn, kn, v)
    s["o"], s["pt"], s["mb"], s["lse"], landed = _fox_fwd(s["qa"], s["ka"], s["va"], pending[0] if pending else ())
    if pending:
        pending[1](landed)
    s["h1"], s["mixed"], s["y"] = _out_proj_fwd(h, s["o"], pin, P["w_pool"], P["pscale"], P["w_out"])
    s["mn"], s["mkv"], s["mk"], s["mv"] = _mem_kv_fwd(mem, P["g_mem_kv"], P["w_mem_kv"], P["gkm"], bd128)
    s["h2"], s["hn_mem"], s["mo"] = _mem_attn_fwd(s["h1"], P["g_mem_q"], P["w_mem_q"], P["gqm"], bd128, s["mk"],
                                                  s["mv"], P["w_mem_out"])
    h3, s["hn_ffn"] = _ffn_fwd(s["h2"], P["g_ffn"], P["w_gu"], P["w_d"], P["layer"])
    return h3, s


def _layer_bwd(dh3, mem, P, s, bd64, bd128, G, shapes):
    l = P["layer"]
    g = {}

    def into(c, a, b, name, tk, tn, index, chips=1):
        G[c] = _matmul_tn(a, b, name, tk=tk, tn=tn, dst=G.get(c), dst_shape=shapes[c], dst_index=index,
                          dst_chips=chips)

    dh2, act, dgate, dup, g["g_ffn"] = _ffn_bwd(dh3, s["h2"], s["hn_ffn"], P["g_ffn"], P["w_gu"], P["w_d"], l)
    into("d", act, dh3, "dw_down", 2 * W_DOWN_ROWS, D_MODEL, lambda i, j: (i, l, 0), chips=2)
    into("g", s["hn_ffn"], dgate, "dw_gate", D_MODEL, _TF, lambda i, j: (j, l, 0))
    into("g", s["hn_ffn"], dup, "dw_up", D_MODEL, _TF, lambda i, j: (j + D_FF // _TF, l, 0))

    dh1, dmq, dmk, dmv, g["g_q_mem"], g["g_mem_q"] = _mem_attn_bwd(
        dh2, s["h1"], P["g_mem_q"], P["w_mem_q"], P["gqm"], bd128, s["mk"], s["mv"], P["w_mem_out"])
    into("o", s["mo"], dh2, "dw_mem_out", MEM_WIDTH, W_MEM_OUT_COLS, lambda i, j: (j, l, 0))
    into("q", s["hn_mem"], dmq, "dw_mem_q", W_OUT_ROWS, MEM_WIDTH, lambda i, j: (i, l, 0))
    G["a"], g["g_k_mem"], g["g_mem_kv"] = _mem_kv_bwd(dmk, dmv, s["mkv"], s["mn"], mem, P["g_mem_kv"], P["gkm"],
                                                      bd128, P["w_mem_kv"], G.get("a"), shapes["a"], DEPTH + l)

    doa, dpin, g["w_pool"], g["pool_scale"] = _out_proj_bwd(dh1, s["mixed"], s["o"], bd64, P["w_pool"], P["pscale"],
                                                            P["w_out"])
    half = FOX_WIDTH // W_OUT_ROWS
    into("a", s["o"], dh1, "dw_out_fox", W_OUT_ROWS, D_MODEL, lambda i, j: (i, l, 0))
    into("a", s["y"], dh1, "dw_out_pool", W_OUT_ROWS, D_MODEL, lambda i, j: (i + half, l, 0))
    dq, dk, dv, drs, dcs = _fox_bwd(s["qa"], s["ka"], s["va"], doa, s["pt"], s["mb"], s["lse"])
    df = _gate_bwd(drs, dcs, s["fl"])
    dh, dz, dzf, g["g_q_fox"], g["g_k_fox"], g["g_mix"], dbf = _mix_in_bwd(
        dh1, s["h"], P["g_mix"], s["zqk"], dq, dk, dv, dpin, df, P["gq"], P["gk"], bd64, P["w_main"], P["w_f"])
    g["b_forget"] = dbf[:, :FOX_HEADS]
    dw_in = jnp.concatenate([_matmul_tn(s["xn"], dz, "dw_in_main"),
                             _matmul_tn(s["xn"], dzf, "dw_in_gate")[:, :FOX_HEADS]], axis=1)
    return dh, G, g, _to_shards(dw_in, 1)


def _device_grads(x, mem, tgt, WC, WS, pending=None):
    bd64 = _blockdiag_ones(FOX_HEADS, FOX_HEAD_DIM)
    bd128 = _blockdiag_ones(MEM_HEADS, MEM_HEAD_DIM)
    params = [_layer_params(WC, WS, l) for l in range(DEPTH)]
    shapes = {c: WC[c].shape for c in WC}

    def complete(buffers):
        for l in range(DEPTH):
            params[l].update(_late_params(buffers, l))

    first = None
    if pending is None:
        complete(WC)
    else:
        slabs, place = pending
        late = list(slabs)
        shapes.update({c: (N_CHIPS,) + slabs[c].shape for c in late})
        first = ([slabs[c] for c in late],
                 lambda landed: complete({c: place(buf, slabs[c]) for c, buf in zip(late, landed)}))
    h, saved = x, []
    for l in range(DEPTH):
        h, s = _layer_fwd(h, mem, params[l], bd64, bd128, first if l == 0 else None)
        saved.append(s)
    dh, loss = _loss_grad(h, tgt)
    G, small, dw_in = {}, [None] * DEPTH, [None] * DEPTH
    for l in reversed(range(DEPTH)):
        dh, G, small[l], dw_in[l] = _layer_bwd(dh, mem, params[l], saved[l], bd64, bd128, G, shapes)
    G["i"] = jnp.concatenate(dw_in, axis=1)
    gsmall = {n: jnp.stack([small[l][n].reshape(WS[n].shape[1:]) for l in range(DEPTH)]) for n in SMALL}
    return loss, dh, G, gsmall


def _class_slabs(shards):
    return {c: jnp.concatenate([shards[n].reshape(-1, width) for n in names]) for c, width, names in CLASSES}


def _class_rows(shards):
    where = {}
    for c, width, names in CLASSES:
        off = 0
        for n in names:
            rows = shards[n].shape[0] * shards[n].shape[1]
            where[n] = (c, off, rows)
            off += rows
    return where


def _to_shards(full, ax):
    shp = full.shape
    parts = full.reshape(shp[:ax] + (N_CHIPS, shp[ax] // N_CHIPS) + shp[ax + 1:])
    return jnp.moveaxis(parts, ax, 0)


def _from_shards(parts, ax):
    full = jnp.moveaxis(parts, 0, ax)
    shp = full.shape
    return full.reshape(shp[:ax] + (shp[ax] * shp[ax + 1],) + shp[ax + 2:])


def _pack_small(arrs):
    flat = jnp.concatenate([a.reshape(-1).astype(_F32) for a in arrs])
    rows = -(-flat.shape[0] // (8 * D_MODEL)) * 8
    return jnp.pad(flat, (0, rows * D_MODEL - flat.shape[0])).reshape(rows, D_MODEL)


def _unpack_small(flat, shapes):
    flat = flat.reshape(-1)
    out, off = [], 0
    for shp in shapes:
        n = 1
        for d in shp:
            n *= d
        out.append(flat[off:off + n].reshape(shp))
        off += n
    return out


def _mesh_pos():
    return lax.axis_index("x"), lax.axis_index("y"), lax.axis_index("c")


def _other_chips(x, y):
    return [(1 - x, y), (x, 1 - y), (1 - x, 1 - y)]


def _half_rows(ref_rows, which):
    half = ref_rows // 2
    return pl.ds(pl.multiple_of(which * half, 16), half)


def _remote(src_ref, dst_ref, send_sems, recv_sems, k, to):
    return pltpu.make_async_remote_copy(src_ref=src_ref, dst_ref=dst_ref, send_sem=send_sems.at[k],
                                        recv_sem=recv_sems.at[k], device_id=to, device_id_type=MESH)


def _allgather_weights(slabs):
    n = len(slabs)

    def body(*refs):
        srcs, outs, send_sems, recv_sems = refs[:n], refs[n:2 * n], refs[2 * n], refs[2 * n + 1]
        x, y, c = _mesh_pos()
        me = 2 * x + y
        sibling = (x, y, 1 - c)
        chips = _other_chips(x, y)
        first, passed = [], []
        for a, (src, out) in enumerate(zip(srcs, outs)):
            mine = _half_rows(src.shape[0], c)
            for j, chip in enumerate(chips):
                cp = _remote(src.at[mine], out.at[me, mine], send_sems, recv_sems, 6 * a + j, (*chip, c))
                cp.start()
                first.append(cp)
        for a, out in enumerate(outs):
            mine = _half_rows(out.shape[1], c)
            for j, (cx, cy) in enumerate(chips):
                slab = out.at[2 * cx + cy, mine]
                _remote(slab, slab, send_sems, recv_sems, 6 * a + j, (cx, cy, c)).wait_recv()
                fwd = _remote(slab, slab, send_sems, recv_sems, 6 * a + 3 + j, sibling)
                fwd.start()
                passed.append(fwd)
        for a, out in enumerate(outs):
            theirs = _half_rows(out.shape[1], 1 - c)
            for j, (cx, cy) in enumerate(chips):
                slab = out.at[2 * cx + cy, theirs]
                _remote(slab, slab, send_sems, recv_sems, 6 * a + 3 + j, sibling).wait_recv()
        for cp in first + passed:
            cp.wait_send()

    return pl.pallas_call(
        body, name="allgather_weights", in_specs=[_ANY] * n, out_specs=[_ANY] * n,
        out_shape=[_sds((N_CHIPS,) + s.shape, s.dtype) for s in slabs],
        scratch_shapes=[pltpu.SemaphoreType.DMA((6 * n,)), pltpu.SemaphoreType.DMA((6 * n,))],
    )(*slabs)


def _exchange_halves(grads):
    n = len(grads)

    def body(*refs):
        srcs, gots, send_sems, recv_sems = refs[:n], refs[n:2 * n], refs[2 * n], refs[2 * n + 1]
        x, y, c = _mesh_pos()
        copies = []
        for a, (src, got) in enumerate(zip(srcs, gots)):
            theirs = _half_rows(src.shape[1], 1 - c)
            for k in range(N_CHIPS):
                cp = _remote(src.at[k, theirs], got.at[k], send_sems, recv_sems, N_CHIPS * a + k, (x, y, 1 - c))
                cp.start()
                copies.append(cp)
        for cp in copies:
            cp.wait()

    return pl.pallas_call(
        body, name="grad_exchange_halves", in_specs=[_ANY] * n, out_specs=[_ANY] * n,
        out_shape=[_sds((N_CHIPS, g.shape[1] // 2, g.shape[2]), _F32) for g in grads],
        scratch_shapes=[pltpu.SemaphoreType.DMA((N_CHIPS * n,)), pltpu.SemaphoreType.DMA((N_CHIPS * n,))],
    )(*grads)


def _row_tile(rows):
    return _pick_tile(rows, (704, 512, 256))


def _add_halves(g, got, c_idx, name):
    _, half, width = got.shape
    ta = _row_tile(half)
    nb = half // ta

    def body(c_ref, a_ref, b_ref, o_ref):
        o_ref[...] = (a_ref[...] + b_ref[...]).astype(jnp.bfloat16)

    return pl.pallas_call(
        body, name=name,
        grid_spec=pltpu.PrefetchScalarGridSpec(
            num_scalar_prefetch=1, grid=(N_CHIPS, nb),
            in_specs=[pl.BlockSpec((1, ta, width), lambda k, i, c: (k, c[0] * nb + i, 0)),
                      pl.BlockSpec((1, ta, width), lambda k, i, c: (k, i, 0))],
            out_specs=pl.BlockSpec((1, ta, width), lambda k, i, c: (k, i, 0))),
        out_shape=_sds(got.shape, jnp.bfloat16),
    )(c_idx, g, got)


def _scatter_to_chips(parts):
    n = len(parts)

    def body(*refs):
        srcs, gots, send_sems, recv_sems = refs[:n], refs[n:2 * n], refs[2 * n], refs[2 * n + 1]
        x, y, c = _mesh_pos()
        me = 2 * x + y
        chips = _other_chips(x, y)
        copies = []
        for a, (src, got) in enumerate(zip(srcs, gots)):
            for j, (cx, cy) in enumerate(chips):
                cp = _remote(src.at[2 * cx + cy], got.at[me], send_sems, recv_sems, 3 * a + j, (cx, cy, c))
                cp.start()
                copies.append(cp)
        for a, got in enumerate(gots):
            for j, (cx, cy) in enumerate(chips):
                slab = got.at[2 * cx + cy]
                _remote(slab, slab, send_sems, recv_sems, 3 * a + j, (cx, cy, c)).wait_recv()
        for cp in copies:
            cp.wait_send()

    return pl.pallas_call(
        body, name="grad_scatter_chips", in_specs=[_ANY] * n, out_specs=[_ANY] * n,
        out_shape=[_sds(p.shape, p.dtype) for p in parts],
        scratch_shapes=[pltpu.SemaphoreType.DMA((3 * n,)), pltpu.SemaphoreType.DMA((3 * n,))],
    )(*parts)


def _sum_chips(got, c_idx, name):
    _, half, width = got.shape
    ta = _row_tile(half)
    nb = half // ta

    def body(c_ref, a_ref, o_ref):
        f = lambda k: a_ref[k].astype(_F32)
        o_ref[...] = ((f(0) + f(1)) + f(2)) + f(3)

    return pl.pallas_call(
        body, name=name,
        grid_spec=pltpu.PrefetchScalarGridSpec(
            num_scalar_prefetch=1, grid=(nb,),
            in_specs=[pl.BlockSpec((N_CHIPS, ta, width), lambda i, c: (0, i, 0))],
            out_specs=pl.BlockSpec((ta, width), lambda i, c: (c[0] * nb + i, 0))),
        out_shape=_sds((2 * half, width), _F32),
    )(c_idx, got)


def _share_with_sibling(bufs):
    n = len(bufs)

    def body(*refs):
        outs, send_sems, recv_sems = refs[n:2 * n], refs[2 * n], refs[2 * n + 1]
        x, y, c = _mesh_pos()
        copies = []
        for a, out in enumerate(outs):
            mine = out.at[_half_rows(out.shape[0], c)]
            cp = _remote(mine, mine, send_sems, recv_sems, a, (x, y, 1 - c))
            cp.start()
            copies.append(cp)
        for a, out in enumerate(outs):
            theirs = out.at[_half_rows(out.shape[0], 1 - c)]
            _remote(theirs, theirs, send_sems, recv_sems, a, (x, y, 1 - c)).wait_recv()
        for cp in copies:
            cp.wait_send()

    return pl.pallas_call(
        body, name="grad_share_sibling", in_specs=[_ANY] * n, out_specs=[_ANY] * n,
        out_shape=[_sds(b.shape, _F32) for b in bufs], input_output_aliases={a: a for a in range(n)},
        scratch_shapes=[pltpu.SemaphoreType.DMA((n,)), pltpu.SemaphoreType.DMA((n,))],
    )(*bufs)


def _allreduce_small(g):
    rows = g.shape[0]
    n_dev = 2 * N_CHIPS

    def body(g_ref, out_ref, gathered, local_sem, send_sems, recv_sems):
        x, y, c = _mesh_pos()
        me = 4 * x + 2 * y + c
        own = pltpu.make_async_copy(g_ref, gathered.at[me], local_sem)
        own.start()
        copies = []
        for k in range(1, n_dev):
            fx, fy, fc = (k >> 2) & 1, (k >> 1) & 1, k & 1
            cp = pltpu.make_async_remote_copy(
                src_ref=g_ref, dst_ref=gathered.at[me], send_sem=send_sems.at[k - 1], recv_sem=recv_sems.at[k - 1],
                device_id=(x ^ fx, y ^ fy, c ^ fc), device_id_type=MESH)
            cp.start()
            copies.append(cp)
        for k in range(1, n_dev):
            fx, fy, fc = (k >> 2) & 1, (k >> 1) & 1, k & 1
            px, py, pc = x ^ fx, y ^ fy, c ^ fc
            slab = gathered.at[4 * px + 2 * py + pc]
            pltpu.make_async_remote_copy(src_ref=slab, dst_ref=slab, send_sem=send_sems.at[k - 1],
                                         recv_sem=recv_sems.at[k - 1], device_id=(px, py, pc),
                                         device_id_type=MESH).wait_recv()
        for cp in copies:
            cp.wait_send()
        own.wait()
        acc = gathered[0]
        for d in range(1, n_dev):
            acc = acc + gathered[d]
        out_ref[...] = acc

    vmem = pl.BlockSpec(memory_space=pltpu.VMEM)
    return pl.pallas_call(
        body, name="allreduce_small", in_specs=[vmem], out_specs=vmem, out_shape=_sds((rows, D_MODEL), _F32),
        scratch_shapes=[pltpu.VMEM((n_dev, rows, D_MODEL), _F32), pltpu.SemaphoreType.DMA,
                        pltpu.SemaphoreType.DMA((n_dev - 1,)), pltpu.SemaphoreType.DMA((n_dev - 1,))],
    )(g)


def _adamw(w, g, m, v, name, g_first_row=0):
    shape = w.shape
    cols = shape[-1]
    rows = 1
    for d in shape[:-1]:
        rows *= d
    w2, m2, v2 = (a.reshape(rows, cols) for a in (w, m, v))
    tr = _pick_tile(rows, (256, 128, 64, 32, 16, 8))
    g0 = g_first_row // tr

    def body(w_ref, g_ref, m_ref, v_ref, go_ref, d_ref, nm_ref, nv_ref):
        gg = g_ref[...]
        go_ref[...] = gg
        nm = ADAM_B1 * m_ref[...] + (1.0 - ADAM_B1) * gg
        nv = ADAM_B2 * v_ref[...] + (1.0 - ADAM_B2) * (gg * gg)
        m_hat = nm / (1.0 - ADAM_B1 ** ADAM_STEP)
        v_hat = nv / (1.0 - ADAM_B2 ** ADAM_STEP)
        d_ref[...] = -ADAM_LR * (m_hat / (jnp.sqrt(v_hat) + ADAM_EPS) + ADAM_WD * w_ref[...])
        nm_ref[...] = nm
        nv_ref[...] = nv

    spec = pl.BlockSpec((tr, cols), lambda i: (i, 0))
    outs = pl.pallas_call(
        body, name=name, grid=(rows // tr,),
        in_specs=[spec, pl.BlockSpec((tr, cols), lambda i: (g0 + i, 0)), spec, spec], out_specs=[spec] * 4,
        out_shape=[_sds((rows, cols), _F32)] * 4,
    )(w2, g, m2, v2)
    return tuple(o.reshape(shape) for o in outs)


def kernel(x, mem, g_mix, w_in, b_forget, g_q_fox, g_k_fox, w_pool, pool_scale, w_out, g_mem_q, g_mem_kv, w_mem_q, w_mem_kv, g_q_mem, g_k_mem, w_mem_out, g_ffn, w_gate_up, w_down, loss_target, m_g_mix, m_w_in, m_b_forget, m_g_q_fox, m_g_k_fox, m_w_pool, m_pool_scale, m_w_out, m_g_mem_q, m_g_mem_kv, m_w_mem_q, m_w_mem_kv, m_g_q_mem, m_g_k_mem, m_w_mem_out, m_g_ffn, m_w_gate_up, m_w_down, v_g_mix, v_w_in, v_b_forget, v_g_q_fox, v_g_k_fox, v_w_pool, v_pool_scale, v_w_out, v_g_mem_q, v_g_mem_kv, v_w_mem_q, v_w_mem_kv, v_g_q_mem, v_g_k_mem, v_w_mem_out, v_g_ffn, v_w_gate_up, v_w_down):
    w = dict(g_mix=g_mix, w_in=w_in, b_forget=b_forget, g_q_fox=g_q_fox, g_k_fox=g_k_fox, w_pool=w_pool,
             pool_scale=pool_scale, w_out=w_out, g_mem_q=g_mem_q, g_mem_kv=g_mem_kv, w_mem_q=w_mem_q,
             w_mem_kv=w_mem_kv, g_q_mem=g_q_mem, g_k_mem=g_k_mem, w_mem_out=w_mem_out, g_ffn=g_ffn,
             w_gate_up=w_gate_up, w_down=w_down)
    m = dict(g_mix=m_g_mix, w_in=m_w_in, b_forget=m_b_forget, g_q_fox=m_g_q_fox, g_k_fox=m_g_k_fox, w_pool=m_w_pool,
             pool_scale=m_pool_scale, w_out=m_w_out, g_mem_q=m_g_mem_q, g_mem_kv=m_g_mem_kv, w_mem_q=m_w_mem_q,
             w_mem_kv=m_w_mem_kv, g_q_mem=m_g_q_mem, g_k_mem=m_g_k_mem, w_mem_out=m_w_mem_out, g_ffn=m_g_ffn,
             w_gate_up=m_w_gate_up, w_down=m_w_down)
    v = dict(g_mix=v_g_mix, w_in=v_w_in, b_forget=v_b_forget, g_q_fox=v_g_q_fox, g_k_fox=v_g_k_fox, w_pool=v_w_pool,
             pool_scale=v_pool_scale, w_out=v_w_out, g_mem_q=v_g_mem_q, g_mem_kv=v_g_mem_kv, w_mem_q=v_w_mem_q,
             w_mem_kv=v_w_mem_kv, g_q_mem=v_g_q_mem, g_k_mem=v_g_k_mem, w_mem_out=v_w_mem_out, g_ffn=v_g_ffn,
             w_gate_up=v_w_gate_up, w_down=v_w_down)

    classes = [c for c, _, _ in CLASSES]
    chip = 2 * lax.axis_index("x") + lax.axis_index("y")
    c_idx = lax.axis_index("c").astype(jnp.int32).reshape(1)
    own_slab = lambda bufs, own: lax.dynamic_update_slice(bufs, own[None], (chip, 0, 0))

    slabs = _class_slabs({n: w[n].astype(_MXU) for n in BIG})
    WC = {"i": own_slab(_allgather_weights([slabs["i"]])[0], slabs["i"])}
    later = ({c: slabs[c] for c in classes if c != "i"}, own_slab)

    loss, grad_x, G, gsmall = _device_grads(x[0], mem[0], loss_target[0], WC, {n: w[n] for n in SMALL}, later)
    loss = lax.psum(loss[0, 0], ("x", "y", "c"))

    grads = [G[c] for c in classes]
    partial = [_add_halves(g, got, c_idx, "grad_add_halves_" + c)
               for c, g, got in zip(classes, grads, _exchange_halves(grads))]
    landed = _scatter_to_chips(partial)
    partials = [own_slab(got, lax.dynamic_index_in_dim(p, chip, 0, keepdims=False)) for got, p in zip(landed, partial)]
    reduced = _share_with_sibling([_sum_chips(p, c_idx, "grad_sum_chips_" + c) for c, p in zip(classes, partials)])
    reduced = dict(zip(classes, reduced))

    small_shapes = [w[n].shape for n in SMALL]
    gsmall = _unpack_small(_allreduce_small(_pack_small([gsmall[n] for n in SMALL])), small_shapes)

    g_out, d_out, m_out, v_out = {}, {}, {}, {}
    for n, (c, first, _) in _class_rows({n: w[n] for n in BIG}).items():
        g_out[n], d_out[n], m_out[n], v_out[n] = _adamw(w[n], reduced[c], m[n], v[n], "adamw_" + n, first)
    packed = [_pack_small([t[n] for n in SMALL]) for t in (w, m, v)]
    _, ds, ms, vs = _adamw(packed[0], _pack_small(gsmall), packed[1], packed[2], "adamw_small")
    for n, gi, di, mi, vi in zip(SMALL, gsmall, _unpack_small(ds, small_shapes), _unpack_small(ms, small_shapes),
                                 _unpack_small(vs, small_shapes)):
        g_out[n], d_out[n], m_out[n], v_out[n] = gi, di, mi, vi

    return (loss, grad_x[None], *[g_out[n] for n in WEIGHTS], *[d_out[n] for n in WEIGHTS],
            *[m_out[n] for n in WEIGHTS], *[v_out[n] for n in WEIGHTS])
```

```python
import functools

import jax
import jax.numpy as jnp
import numpy as np
from jax import lax
from jax.experimental import pallas as pl
from jax.experimental.pallas import tpu as pltpu

_F32 = jnp.float32
_MXU = jnp.bfloat16

D_MODEL = 1024
DEPTH = 2
FOX_HEADS = 8
FOX_HEAD_DIM = 64
FOX_WIDTH = FOX_HEADS * FOX_HEAD_DIM
POOL_WINDOWS = (2, 4, 8, 16)
POOL_GROUP_DIM = 128
POOL_WIDTH = len(POOL_WINDOWS) * POOL_GROUP_DIM
MEM_HEADS = 4
MEM_HEAD_DIM = 128
MEM_WIDTH = MEM_HEADS * MEM_HEAD_DIM
D_FF = 2816
EPS = 1e-6
LANES = 128
HALO = 16

ADAM_LR = 0.001
ADAM_B1 = 0.9
ADAM_B2 = 0.999
ADAM_EPS = 1e-08
ADAM_WD = 0.01
ADAM_STEP = 10

N_CHIPS = 4
MESH = pl.DeviceIdType.MESH

_TM = 512
_TQ = 1024
_TMF = 256
_TF = 1408
_TT = 1024
_TB = 256
_TA = 512

BIG = ("w_in", "w_out", "w_mem_q", "w_mem_kv", "w_mem_out", "w_gate_up", "w_down")
SHARD_AXIS = {"w_in": 2, "w_out": 1, "w_mem_q": 1, "w_mem_kv": 1, "w_mem_out": 2, "w_gate_up": 2, "w_down": 1}
SMALL = ("g_mix", "b_forget", "g_q_fox", "g_k_fox", "w_pool", "pool_scale", "g_mem_q", "g_mem_kv", "g_q_mem",
         "g_k_mem", "g_ffn")
WEIGHTS = ("g_mix", "w_in", "b_forget", "g_q_fox", "g_k_fox", "w_pool", "pool_scale", "w_out", "g_mem_q", "g_mem_kv",
           "w_mem_q", "w_mem_kv", "g_q_mem", "g_k_mem", "w_mem_out", "g_ffn", "w_gate_up", "w_down")


def _dot(a, b):
    return jnp.dot(a, b, preferred_element_type=_F32)


def _dot_nt(a, b):
    return lax.dot_general(a, b, (((1,), (1,)), ((), ())), preferred_element_type=_F32)


def _dot_tn(a, b):
    return lax.dot_general(a, b, (((0,), (0,)), ((), ())), preferred_element_type=_F32)


def _group_sum(x, ones_blockdiag):
    hi = x.astype(_MXU)
    lo = (x - hi.astype(_F32)).astype(_MXU)
    return _dot(hi, ones_blockdiag) + _dot(lo, ones_blockdiag)


def _tri_dot(tri, x):
    h1 = x.astype(jnp.bfloat16)
    r1 = x - h1.astype(_F32)
    h2 = r1.astype(jnp.bfloat16)
    h3 = (r1 - h2.astype(_F32)).astype(jnp.bfloat16)
    return _dot(tri, h1) + _dot(tri, h2) + _dot(tri, h3)


def _rstd(x):
    return lax.rsqrt(jnp.mean(x * x, axis=-1, keepdims=True) + EPS)


def _norm_bwd(dy, x, g):
    r = _rstd(x)
    xhat = x * r
    u = dy * g
    dx = r * (u - xhat * jnp.mean(u * xhat, axis=-1, keepdims=True))
    return dx, jnp.sum(dy * xhat, axis=0, keepdims=True)


def _headnorm_bwd(dy, x, g, ones_blockdiag, width):
    r = lax.rsqrt(_group_sum(x * x, ones_blockdiag) * (1.0 / width) + EPS)
    xhat = x * r
    u = dy * g
    dx = r * (u - xhat * (_group_sum(u * xhat, ones_blockdiag) * (1.0 / width)))
    return dx, jnp.sum(dy * xhat, axis=0, keepdims=True)


def _fold_heads(row, heads, width):
    acc = row[:, 0:width]
    for h in range(1, heads):
        acc = acc + row[:, h * width:(h + 1) * width]
    return acc


_ANY = pl.BlockSpec(memory_space=pl.ANY)


def _resident(shape):
    return pl.BlockSpec(shape, lambda *_: (0,) * len(shape), pipeline_mode=pl.Buffered(1))


def _rows(tm, width):
    return pl.BlockSpec((tm, width), lambda i: (i, 0))


def _blockdiag_ones(groups, width):
    return jnp.kron(jnp.eye(groups, dtype=_F32), jnp.ones((width, width), _F32)).astype(_MXU)


def _sds(shape, dtype):
    return jax.ShapeDtypeStruct(shape, dtype)


def _mix_in_fwd(h, g_mix, w_main, w_f, b_f, gq, gk, bd64):
    T = h.shape[0]
    tm = min(_TM, T)

    def body(h_ref, g_ref, wm_ref, wf_ref, bf_ref, gq_ref, gk_ref, bd_ref,
             xn_ref, zqk_ref, qn_ref, kn_ref, v_ref, pin_ref, fl_ref):
        x = h_ref[...]
        xn = ((x * _rstd(x)) * g_ref[...]).astype(_MXU)
        xn_ref[...] = xn
        z = _dot(xn, wm_ref[...])
        q = z[:, :FOX_WIDTH]
        k = z[:, FOX_WIDTH:2 * FOX_WIDTH]
        zqk_ref[...] = z[:, :2 * FOX_WIDTH]
        bd = bd_ref[...]
        rq = lax.rsqrt(_group_sum(q * q, bd) * (1.0 / FOX_HEAD_DIM) + EPS)
        rk = lax.rsqrt(_group_sum(k * k, bd) * (1.0 / FOX_HEAD_DIM) + EPS)
        qn_ref[...] = ((q * rq) * gq_ref[...]).astype(_MXU)
        kn_ref[...] = ((k * rk) * gk_ref[...]).astype(_MXU)
        v_ref[...] = z[:, 2 * FOX_WIDTH:3 * FOX_WIDTH].astype(_MXU)
        pin_ref[...] = z[:, 3 * FOX_WIDTH:]
        fl_ref[...] = _dot(xn, wf_ref[...]) + bf_ref[...]

    return pl.pallas_call(
        body, name="mix_in_fwd", grid=(T // tm,),
        in_specs=[_rows(tm, D_MODEL), _resident((1, D_MODEL)), _resident(w_main.shape), _resident(w_f.shape),
                  _resident((1, LANES)), _resident((1, FOX_WIDTH)), _resident((1, FOX_WIDTH)), _resident(bd64.shape)],
        out_specs=[_rows(tm, D_MODEL), _rows(tm, 2 * FOX_WIDTH), _rows(tm, FOX_WIDTH), _rows(tm, FOX_WIDTH),
                   _rows(tm, FOX_WIDTH), _rows(tm, POOL_WIDTH), _rows(tm, LANES)],
        out_shape=[_sds((T, D_MODEL), _MXU), _sds((T, 2 * FOX_WIDTH), _F32), _sds((T, FOX_WIDTH), _MXU),
                   _sds((T, FOX_WIDTH), _MXU), _sds((T, FOX_WIDTH), _MXU), _sds((T, POOL_WIDTH), _F32),
                   _sds((T, LANES), _F32)],
    )(h, g_mix, w_main, w_f, b_f, gq, gk, bd64)


def _split3(x):
    h1 = x.astype(jnp.bfloat16).astype(_F32)
    r1 = x - h1
    h2 = r1.astype(jnp.bfloat16).astype(_F32)
    h3 = (r1 - h2).astype(jnp.bfloat16).astype(_F32)
    return h1, h2, h3


_ONES3 = (1.0, 1.0, 1.0)
_ZEROS3 = (0.0, 0.0, 0.0)


def _aug_head(feat, first, second):
    rows = feat.shape[0]
    lane = lax.broadcasted_iota(jnp.int32, (rows, LANES - FOX_HEAD_DIM), 1)
    aux = jnp.zeros((rows, LANES - FOX_HEAD_DIM), _F32)
    for k in range(3):
        aux = jnp.where(lane == k, first[k], aux)
        aux = jnp.where(lane == 3 + k, second[k], aux)
    return jnp.concatenate([feat.astype(_MXU), aux.astype(_MXU)], axis=-1)


def _gate_fwd(fl, qn, kn, v):
    T = fl.shape[0]
    tb = min(_TB, T)
    wide = FOX_HEADS * LANES

    def body(fl_ref, q_ref, k_ref, v_ref, qa_ref, ka_ref, va_ref, carry):
        @pl.when(pl.program_id(0) == 0)
        def _():
            carry[...] = jnp.zeros_like(carry)

        x = fl_ref[...]
        ls = jnp.minimum(x, 0.0) - jnp.log1p(jnp.exp(-jnp.abs(x)))
        row = lax.broadcasted_iota(jnp.int32, (tb, tb), 0)
        col = lax.broadcasted_iota(jnp.int32, (tb, tb), 1)
        tri = jnp.where(col <= row, 1.0, 0.0).astype(jnp.bfloat16)
        cs = _tri_dot(tri, ls) + carry[...]
        carry[...] = cs[tb - 1:tb, :]
        for h in range(FOX_HEADS):
            sl = slice(h * FOX_HEAD_DIM, (h + 1) * FOX_HEAD_DIM)
            out = slice(h * LANES, (h + 1) * LANES)
            c3 = _split3(cs[:, h:h + 1])
            qa_ref[:, out] = _aug_head(q_ref[:, sl].astype(_F32) * (FOX_HEAD_DIM ** -0.5), c3, _ONES3)
            ka_ref[:, out] = _aug_head(k_ref[:, sl], _ONES3, tuple(-t for t in c3))
            va_ref[:, out] = _aug_head(v_ref[:, sl], _ONES3, _ZEROS3)

    return pl.pallas_call(
        body, name="gate_fwd", grid=(T // tb,),
        in_specs=[_rows(tb, LANES), _rows(tb, FOX_WIDTH), _rows(tb, FOX_WIDTH), _rows(tb, FOX_WIDTH)],
        out_specs=[_rows(tb, wide)] * 3, out_shape=[_sds((T, wide), _MXU)] * 3,
        scratch_shapes=[pltpu.VMEM((1, LANES), _F32)],
    )(fl, qn, kn, v)


def _gate_bwd(drs, dcs, fl):
    T = fl.shape[0]
    tb = min(_TB, T)
    nb = T // tb

    def body(r_ref, d_ref, fl_ref, df_ref, carry):
        @pl.when(pl.program_id(0) == 0)
        def _():
            carry[...] = jnp.zeros_like(carry)

        lane = lax.broadcasted_iota(jnp.int32, (tb, LANES), 1)
        dc = jnp.zeros((tb, LANES), _F32)
        for p in range(FOX_HEADS // 2):
            pair = r_ref[p] - d_ref[p]
            for hh in range(2):
                dc = jnp.where(lane == 2 * p + hh, pair[:, hh:hh + 1], dc)
        row = lax.broadcasted_iota(jnp.int32, (tb, tb), 0)
        col = lax.broadcasted_iota(jnp.int32, (tb, tb), 1)
        tri = jnp.where(col >= row, 1.0, 0.0).astype(jnp.bfloat16)
        rc = _tri_dot(tri, dc) + carry[...]
        carry[...] = rc[0:1, :]
        df_ref[...] = rc * (1.0 / (1.0 + jnp.exp(fl_ref[...])))

    rev = pl.BlockSpec((tb, LANES), lambda i: (nb - 1 - i, 0))
    rev4 = pl.BlockSpec((FOX_HEADS // 2, tb, LANES), lambda i: (0, nb - 1 - i, 0))
    return pl.pallas_call(
        body, name="gate_bwd", grid=(nb,), in_specs=[rev4, rev4, rev], out_specs=rev,
        out_shape=_sds((T, LANES), _F32), scratch_shapes=[pltpu.VMEM((1, LANES), _F32)],
    )(drs, dcs, fl)


def _fox_scores(q_ref, k_ref, hh, masked):
    sl = slice(LANES * hh, LANES * (hh + 1))
    s = _dot_nt(q_ref[:, sl], k_ref[:, sl])
    if masked:
        row = lax.broadcasted_iota(jnp.int32, s.shape, 0)
        col = lax.broadcasted_iota(jnp.int32, s.shape, 1)
        s = jnp.where(col <= row, s, -jnp.inf)
    return s


def _causal_steps(nq, query_major):
    if query_major:
        steps = [(i, j) for i in range(nq) for j in range(i + 1)]
    else:
        steps = [(i, j) for j in range(nq) for i in range(j, nq)]
    return (jnp.asarray(np.array([s[0] for s in steps], np.int32)),
            jnp.asarray(np.array([s[1] for s in steps], np.int32)))


def _two_lanes(a, b):
    lane = lax.broadcasted_iota(jnp.int32, (a.shape[0], LANES), 1)
    return jnp.where(lane == 0, a, jnp.where(lane == 1, b, 0.0))


def _gather_behind(pid, t, n_pairs, steps, srcs, outs, send_sems, recv_sems):
    x, y, c = _mesh_pos()
    me = 2 * x + y
    sibling = (x, y, 1 - c)
    chips = _other_chips(x, y)

    def first(a, j):
        src, out = srcs[a], outs[a]
        mine = _half_rows(src.shape[0], c)
        return _remote(src.at[mine], out.at[me, mine], send_sems, recv_sems, 6 * a + j, (*chips[j], c))

    def landed(a, j):
        cx, cy = chips[j]
        slab = outs[a].at[2 * cx + cy, _half_rows(outs[a].shape[1], c)]
        return _remote(slab, slab, send_sems, recv_sems, 6 * a + j, (cx, cy, c))

    def passed(a, j):
        cx, cy = chips[j]
        slab = outs[a].at[2 * cx + cy, _half_rows(outs[a].shape[1], c)]
        return _remote(slab, slab, send_sems, recv_sems, 6 * a + 3 + j, sibling)

    def from_sibling(a, j):
        cx, cy = chips[j]
        slab = outs[a].at[2 * cx + cy, _half_rows(outs[a].shape[1], 1 - c)]
        return _remote(slab, slab, send_sems, recv_sems, 6 * a + 3 + j, sibling)

    every = [(a, j) for a in range(len(srcs)) for j in range(3)]

    @pl.when((pid == 0) & (t == 0))
    def _():
        for a, j in every:
            first(a, j).start()

    @pl.when((pid == n_pairs // 2) & (t == 0))
    def _():
        for a, j in every:
            landed(a, j).wait_recv()
            passed(a, j).start()

    @pl.when((pid == n_pairs - 1) & (t == steps - 1))
    def _():
        for a, j in every:
            from_sibling(a, j).wait_recv()
        for a, j in every:
            first(a, j).wait_send()
            passed(a, j).wait_send()


def _fox_fwd(qa, ka, va, gather=()):
    T = qa.shape[0]
    tq = min(_TQ, T)
    nq = T // tq
    ii, jj = _causal_steps(nq, True)
    steps = int(ii.shape[0])
    n = len(gather)
    n_pairs = FOX_HEADS // 2

    def body(ii_ref, jj_ref, q_ref, k_ref, v_ref, *refs):
        srcs, (o_ref, p_ref, mb_ref, lse_ref), outs = refs[:n], refs[n:n + 4], refs[n + 4:2 * n + 4]
        m_s, acc_s = refs[2 * n + 4:2 * n + 6]
        t = pl.program_id(1)
        i = ii_ref[t]
        j = jj_ref[t]
        if n:
            _gather_behind(pl.program_id(0), t, n_pairs, steps, srcs, outs, *refs[2 * n + 6:])

        @pl.when(j == 0)
        def _():
            m_s[...] = jnp.full(m_s.shape, -jnp.inf, _F32)
            acc_s[...] = jnp.zeros_like(acc_s)

        def step(masked):
            for hh in range(2):
                sl = slice(LANES * hh, LANES * (hh + 1))
                st = _dot_nt(k_ref[:, sl], q_ref[:, sl])
                if masked:
                    key = lax.broadcasted_iota(jnp.int32, st.shape, 0)
                    qry = lax.broadcasted_iota(jnp.int32, st.shape, 1)
                    st = jnp.where(key <= qry, st, -jnp.inf)
                top = jnp.broadcast_to(jnp.max(st, axis=0, keepdims=True), (LANES, tq)).T[:, :1]
                s = _fox_scores(q_ref, k_ref, hh, masked)
                m_prev = m_s[hh]
                m_new = jnp.maximum(m_prev, top)
                p_ref[0, 0, hh] = jnp.exp(s - m_new).astype(_MXU)
                acc_s[hh] = (jnp.exp(m_prev - m_new) * acc_s[hh]
                             + _dot(p_ref[0, 0, hh], v_ref[:, LANES * hh:LANES * (hh + 1)]))
                m_s[hh] = m_new
            mb_ref[0, 0] = _two_lanes(m_s[0], m_s[1])

        @pl.when(j < i)
        def _():
            step(False)

        @pl.when(j == i)
        def _():
            step(True)
            outs, lses = [], []
            for hh in range(2):
                acc = acc_s[hh]
                l = acc[:, FOX_HEAD_DIM:FOX_HEAD_DIM + 1]
                outs.append(acc[:, :FOX_HEAD_DIM] / l)
                lses.append(m_s[hh] + jnp.log(l))
            o_ref[...] = jnp.concatenate(outs, axis=-1).astype(_MXU)
            lse_ref[0] = _two_lanes(lses[0], lses[1])

    qspec = pl.BlockSpec((tq, 2 * LANES), lambda p, t, ii, jj: (ii[t], p))
    kspec = pl.BlockSpec((tq, 2 * LANES), lambda p, t, ii, jj: (jj[t], p))
    sems = [pltpu.SemaphoreType.DMA((6 * n,)), pltpu.SemaphoreType.DMA((6 * n,))] if n else []
    res = pl.pallas_call(
        body, name="fox_fwd_gather" if n else "fox_fwd",
        grid_spec=pltpu.PrefetchScalarGridSpec(
            num_scalar_prefetch=2, grid=(n_pairs, steps),
            in_specs=[qspec, kspec, kspec] + [_ANY] * n,
            out_specs=[pl.BlockSpec((tq, 2 * FOX_HEAD_DIM), lambda p, t, ii, jj: (ii[t], p)),
                       pl.BlockSpec((1, 1, 2, tq, tq), lambda p, t, ii, jj: (p, t, 0, 0, 0)),
                       pl.BlockSpec((1, 1, tq, LANES), lambda p, t, ii, jj: (p, t, 0, 0)),
                       pl.BlockSpec((1, tq, LANES), lambda p, t, ii, jj: (p, ii[t], 0))] + [_ANY] * n,
            scratch_shapes=[pltpu.VMEM((2, tq, 1), _F32), pltpu.VMEM((2, tq, LANES), _F32)] + sems),
        out_shape=[_sds((T, FOX_WIDTH), _MXU), _sds((n_pairs, steps, 2, tq, tq), _MXU),
                   _sds((n_pairs, steps, tq, LANES), _F32), _sds((n_pairs, T, LANES), _F32)]
        + [_sds((N_CHIPS,) + s.shape, s.dtype) for s in gather],
    )(ii, jj, qa, ka, va, *gather)
    return res[0], res[1], res[2], res[3], list(res[4:])


def _pool_window_sum(ext, w, forward):
    n = ext.shape[0]
    sm = ext
    k = 1
    while k < w:
        sm = sm + pltpu.roll(sm, (n - k) if forward else k, axis=0)
        k *= 2
    return sm


def _out_proj_fwd(h, o, pin, w_pool, pscale, w_out):
    T = h.shape[0]
    tm = min(_TM, T)
    hb = tm // HALO

    def body(h_ref, o_ref, pin_ref, halo_ref, wp_ref, ps_ref, wo_ref, h1_ref, mixed_ref, y_ref):
        i = pl.program_id(0)
        pin_t = pin_ref[...]
        halo = jnp.where(i == 0, 0.0, halo_ref[...])
        ext = jnp.concatenate([halo, pin_t], axis=0)
        t = (i * tm + lax.broadcasted_iota(jnp.int32, (tm, 1), 0) + 1).astype(_F32)
        mixed, ys = [], []
        for g, w in enumerate(POOL_WINDOWS):
            sl = slice(g * POOL_GROUP_DIM, (g + 1) * POOL_GROUP_DIM)
            win = _pool_window_sum(ext[:, sl], w, False)[HALO:, :]
            mg = (win / jnp.minimum(t, float(w)) - pin_t[:, sl]).astype(_MXU)
            mixed.append(mg)
            ys.append(_dot(mg, wp_ref[g]))
        mixed_ref[...] = jnp.concatenate(mixed, axis=-1)
        y = (jnp.concatenate(ys, axis=-1) * ps_ref[...]).astype(_MXU)
        y_ref[...] = y
        h1_ref[...] = h_ref[...] + _dot(o_ref[...], wo_ref[:FOX_WIDTH, :]) + _dot(y, wo_ref[FOX_WIDTH:, :])

    return pl.pallas_call(
        body, name="out_proj_fwd", grid=(T // tm,),
        in_specs=[_rows(tm, D_MODEL), _rows(tm, FOX_WIDTH), _rows(tm, POOL_WIDTH),
                  pl.BlockSpec((HALO, POOL_WIDTH), lambda i: (jnp.maximum(i * hb - 1, 0), 0)),
                  _resident(w_pool.shape), _resident((1, POOL_WIDTH)), _resident(w_out.shape)],
        out_specs=[_rows(tm, D_MODEL), _rows(tm, POOL_WIDTH), _rows(tm, POOL_WIDTH)],
        out_shape=[_sds((T, D_MODEL), _F32), _sds((T, POOL_WIDTH), _MXU), _sds((T, POOL_WIDTH), _MXU)],
    )(h, o, pin, pin, w_pool, pscale, w_out)


def _mem_kv_fwd(mem, g_kv, w_kv, gkm, bd128):
    M = mem.shape[0]

    def body(mem_ref, g_ref, w_ref, gk_ref, bd_ref, mn_ref, mkv_ref, mk_ref, mv_ref):
        x = mem_ref[...]
        mn = ((x * _rstd(x)) * g_ref[...]).astype(_MXU)
        mn_ref[...] = mn
        z = _dot(mn, w_ref[...])
        mkv_ref[...] = z
        k = z[:, :MEM_WIDTH]
        rk = lax.rsqrt(_group_sum(k * k, bd_ref[...]) * (1.0 / MEM_HEAD_DIM) + EPS)
        mk_ref[...] = ((k * rk) * gk_ref[...]).astype(_MXU)
        mv_ref[...] = z[:, MEM_WIDTH:].astype(_MXU)

    return pl.pallas_call(
        body, name="mem_kv_fwd",
        out_shape=[_sds((M, D_MODEL), _MXU), _sds((M, 2 * MEM_WIDTH), _F32), _sds((M, MEM_WIDTH), _MXU),
                   _sds((M, MEM_WIDTH), _MXU)],
    )(mem, g_kv, w_kv, gkm, bd128)


def _mem_softmax(qn, mk_ref, hd):
    sl = slice(hd * MEM_HEAD_DIM, (hd + 1) * MEM_HEAD_DIM)
    s = _dot_nt(qn[:, sl], mk_ref[:, sl]) * (MEM_HEAD_DIM ** -0.5)
    e = jnp.exp(s - jnp.max(s, axis=-1, keepdims=True))
    return e / jnp.sum(e, axis=-1, keepdims=True)


def _mem_attn_fwd(h1, g_q, w_q, gqm, bd128, mk, mv, w_mo):
    T = h1.shape[0]
    tm = min(_TM, T)

    def body(h_ref, g_ref, wq_ref, gq_ref, bd_ref, mk_ref, mv_ref, wo_ref, h2_ref, hn_ref, mo_ref):
        x = h_ref[...]
        hn = ((x * _rstd(x)) * g_ref[...]).astype(_MXU)
        hn_ref[...] = hn
        mq = _dot(hn, wq_ref[...])
        rq = lax.rsqrt(_group_sum(mq * mq, bd_ref[...]) * (1.0 / MEM_HEAD_DIM) + EPS)
        qn = ((mq * rq) * gq_ref[...]).astype(_MXU)
        outs = []
        for hd in range(MEM_HEADS):
            p = _mem_softmax(qn, mk_ref, hd).astype(_MXU)
            outs.append(_dot(p, mv_ref[:, hd * MEM_HEAD_DIM:(hd + 1) * MEM_HEAD_DIM]))
        mo = jnp.concatenate(outs, axis=-1).astype(_MXU)
        mo_ref[...] = mo
        h2_ref[...] = x + _dot(mo, wo_ref[...])

    return pl.pallas_call(
        body, name="mem_attn_fwd", grid=(T // tm,),
        in_specs=[_rows(tm, D_MODEL), _resident((1, D_MODEL)), _resident(w_q.shape), _resident((1, MEM_WIDTH)),
                  _resident(bd128.shape), _resident(mk.shape), _resident(mv.shape), _resident(w_mo.shape)],
        out_specs=[_rows(tm, D_MODEL), _rows(tm, D_MODEL), _rows(tm, MEM_WIDTH)],
        out_shape=[_sds((T, D_MODEL), _F32), _sds((T, D_MODEL), _MXU), _sds((T, MEM_WIDTH), _MXU)],
    )(h1, g_q, w_q, gqm, bd128, mk, mv, w_mo)


def _ffn_weight_specs(layer):
    nf = D_FF // _TF
    return [pl.BlockSpec((1, D_MODEL, _TF), lambda i, j: (j, layer, 0)),
            pl.BlockSpec((1, D_MODEL, _TF), lambda i, j: (j + nf, layer, 0)),
            pl.BlockSpec((2, _TF // 2, D_MODEL), lambda i, j: (j, layer, 0))]


def _ffn_fwd(h2, g_ffn, w_gu, w_d, layer):
    T = h2.shape[0]
    tm = min(_TMF, T)
    nf = D_FF // _TF

    def body(h_ref, g_ref, wg_ref, wu_ref, wd_ref, h3_ref, hn_ref, acc, xn_s):
        j = pl.program_id(1)

        @pl.when(j == 0)
        def _():
            x = h_ref[...]
            xn = ((x * _rstd(x)) * g_ref[...]).astype(_MXU)
            xn_s[...] = xn
            hn_ref[...] = xn
            acc[...] = jnp.zeros_like(acc)

        xn = xn_s[...]
        g = _dot(xn, wg_ref[0])
        u = _dot(xn, wu_ref[0])
        a = ((g * jax.nn.sigmoid(g)) * u).astype(_MXU)
        acc[...] += _dot(a, wd_ref[...].reshape(_TF, D_MODEL))

        @pl.when(j == nf - 1)
        def _():
            h3_ref[...] = h_ref[...] + acc[...]

    tok = pl.BlockSpec((tm, D_MODEL), lambda i, j: (i, 0))
    return pl.pallas_call(
        body, name="ffn_fwd", grid=(T // tm, nf),
        in_specs=[tok, pl.BlockSpec((1, D_MODEL), lambda i, j: (0, 0))] + _ffn_weight_specs(layer),
        out_specs=[tok, tok],
        out_shape=[_sds((T, D_MODEL), _F32), _sds((T, D_MODEL), _MXU)],
        scratch_shapes=[pltpu.VMEM((tm, D_MODEL), _F32), pltpu.VMEM((tm, D_MODEL), _MXU)],
    )(h2, g_ffn, w_gu, w_gu, w_d)


def _loss_grad(y, tgt):
    T = y.shape[0]
    tm = min(_TA, T)

    def body(y_ref, t_ref, dy_ref, loss_ref):
        @pl.when(pl.program_id(0) == 0)
        def _():
            loss_ref[...] = jnp.zeros_like(loss_ref)

        err = y_ref[...] - t_ref[...]
        dy_ref[...] = err * (1.0 / D_MODEL)
        part = jnp.sum(jnp.sum(err * err, axis=0, keepdims=True), axis=1, keepdims=True)
        loss_ref[...] += part * (0.5 / D_MODEL)

    return pl.pallas_call(
        body, name="loss_grad", grid=(T // tm,), in_specs=[_rows(tm, D_MODEL), _rows(tm, D_MODEL)],
        out_specs=[_rows(tm, D_MODEL), pl.BlockSpec((1, 1), lambda i: (0, 0))],
        out_shape=[_sds((T, D_MODEL), _F32), _sds((1, 1), _F32)],
    )(y, tgt)


def _pick_tile(n, candidates=(1408, 1024, 512, 256, 128)):
    for c in candidates:
        if n % c == 0:
            return c
    return n


def _matmul_tn(a, b, name, tk=None, tn=None, dst=None, dst_shape=None, dst_index=None, dst_chips=1):
    T, K = a.shape
    N = b.shape[1]
    tk, tn, tt = tk or _pick_tile(K), tn or _pick_tile(N), min(_TT, T)

    def body(*refs):
        a_ref, b_ref, o_ref = refs[0], refs[1], refs[-1]

        @pl.when(pl.program_id(2) == 0)
        def _():
            o_ref[...] = jnp.zeros_like(o_ref)

        o_ref[...] += _dot_tn(a_ref[...].astype(_MXU), b_ref[...].astype(_MXU)).reshape(o_ref.shape)

    in_specs = [pl.BlockSpec((tt, tk), lambda i, j, t: (t, i)), pl.BlockSpec((tt, tn), lambda i, j, t: (t, j))]
    if dst_shape is None:
        out_spec, out_shape, args, alias = pl.BlockSpec((tk, tn), lambda i, j, t: (i, j)), (K, N), (a, b), {}
    else:
        out_spec = pl.BlockSpec((dst_chips, tk // dst_chips, tn), lambda i, j, t: dst_index(i, j))
        out_shape, args, alias = dst_shape, (a, b), {}
        if dst is not None:
            in_specs, args, alias = in_specs + [_ANY], (a, b, dst), {2: 0}
    return pl.pallas_call(
        body, name=name, grid=(K // tk, N // tn, T // tt), in_specs=in_specs, out_specs=out_spec,
        out_shape=_sds(out_shape, _F32), input_output_aliases=alias,
    )(*args)


def _ffn_bwd(dh3, h2, hn, g_ffn, w_gu, w_d, layer):
    T = h2.shape[0]
    tm = min(_TMF, T)
    nf = D_FF // _TF

    def body(dh_ref, h_ref, hn_ref, g_ref, wg_ref, wu_ref, wd_ref, dh2_ref, act_ref, dg_ref, du_ref, dgain_ref,
             acc, dyb):
        i = pl.program_id(0)
        j = pl.program_id(1)

        @pl.when((i == 0) & (j == 0))
        def _():
            dgain_ref[...] = jnp.zeros_like(dgain_ref)

        @pl.when(j == 0)
        def _():
            dyb[...] = dh_ref[...].astype(_MXU)
            acc[...] = jnp.zeros_like(acc)

        xn = hn_ref[...]
        wg = wg_ref[0]
        wu = wu_ref[0]
        g = _dot(xn, wg)
        u = _dot(xn, wu)
        sg = jax.nn.sigmoid(g)
        sl = g * sg
        act_ref[...] = (sl * u).astype(_MXU)
        da = _dot_nt(dyb[...], wd_ref[...].reshape(_TF, D_MODEL))
        dgate = (da * u * (sg * (1.0 + g * (1.0 - sg)))).astype(_MXU)
        dup = (da * sl).astype(_MXU)
        dg_ref[...] = dgate
        du_ref[...] = dup
        acc[...] += _dot_nt(dgate, wg) + _dot_nt(dup, wu)

        @pl.when(j == nf - 1)
        def _():
            dhn = acc[...]
            dx, dgain = _norm_bwd(dhn, h_ref[...], g_ref[...])
            dh2_ref[...] = dh_ref[...] + dx
            dgain_ref[...] += dgain

    tok = pl.BlockSpec((tm, D_MODEL), lambda i, j: (i, 0))
    ffb = pl.BlockSpec((tm, _TF), lambda i, j: (i, j))
    row = pl.BlockSpec((1, D_MODEL), lambda i, j: (0, 0))
    return pl.pallas_call(
        body, name="ffn_bwd", grid=(T // tm, nf),
        in_specs=[tok, tok, tok, row] + _ffn_weight_specs(layer),
        out_specs=[tok, ffb, ffb, ffb, row],
        out_shape=[_sds((T, D_MODEL), _F32), _sds((T, D_FF), _MXU), _sds((T, D_FF), _MXU), _sds((T, D_FF), _MXU),
                   _sds((1, D_MODEL), _F32)],
        scratch_shapes=[pltpu.VMEM((tm, D_MODEL), _F32), pltpu.VMEM((tm, D_MODEL), _MXU)],
    )(dh3, h2, hn, g_ffn, w_gu, w_gu, w_d)


def _mem_attn_bwd(dh2, h1, g_q, w_q, gqm, bd128, mk, mv, w_mo):
    T = h1.shape[0]
    M = mk.shape[0]
    tm = min(_TM, T)
    nt = T // tm

    def body(dh_ref, h_ref, g_ref, wq_ref, gq_ref, bd_ref, mk_ref, mv_ref, wo_ref,
             dh1_ref, dmq_ref, dmk_ref, dmv_ref, dgq_ref, dgain_ref, dgq_acc):
        i = pl.program_id(0)

        @pl.when(i == 0)
        def _():
            dmk_ref[...] = jnp.zeros_like(dmk_ref)
            dmv_ref[...] = jnp.zeros_like(dmv_ref)
            dgain_ref[...] = jnp.zeros_like(dgain_ref)
            dgq_acc[...] = jnp.zeros_like(dgq_acc)

        x = h_ref[...]
        g = g_ref[...]
        bd = bd_ref[...]
        hn = ((x * _rstd(x)) * g).astype(_MXU)
        mq = _dot(hn, wq_ref[...])
        rq = lax.rsqrt(_group_sum(mq * mq, bd) * (1.0 / MEM_HEAD_DIM) + EPS)
        qn = ((mq * rq) * gq_ref[...]).astype(_MXU)
        dmo = _dot_nt(dh_ref[...].astype(_MXU), wo_ref[...])
        dqn = []
        for hd in range(MEM_HEADS):
            sl = slice(hd * MEM_HEAD_DIM, (hd + 1) * MEM_HEAD_DIM)
            p = _mem_softmax(qn, mk_ref, hd)
            dmo_h = dmo[:, sl].astype(_MXU)
            dp = _dot_nt(dmo_h, mv_ref[:, sl])
            ds = (p * (dp - jnp.sum(p * dp, axis=-1, keepdims=True)) * (MEM_HEAD_DIM ** -0.5)).astype(_MXU)
            dqn.append(_dot(ds, mk_ref[:, sl]))
            dmk_ref[:, sl] += _dot_tn(ds, qn[:, sl])
            dmv_ref[:, sl] += _dot_tn(p.astype(_MXU), dmo_h)
        dqn = jnp.concatenate(dqn, axis=-1)
        dmq, dgq = _headnorm_bwd(dqn, mq, gq_ref[...], bd, MEM_HEAD_DIM)
        dgq_acc[...] += dgq
        dmq = dmq.astype(_MXU)
        dmq_ref[...] = dmq
        dhn = _dot_nt(dmq, wq_ref[...])
        dx, dgain = _norm_bwd(dhn, x, g)
        dh1_ref[...] = dh_ref[...] + dx
        dgain_ref[...] += dgain

        @pl.when(i == nt - 1)
        def _():
            dgq_ref[...] = _fold_heads(dgq_acc[...], MEM_HEADS, MEM_HEAD_DIM)

    const2 = lambda shape: pl.BlockSpec(shape, lambda i: (0, 0))
    return pl.pallas_call(
        body, name="mem_attn_bwd", grid=(nt,),
        in_specs=[_rows(tm, D_MODEL), _rows(tm, D_MODEL), _resident((1, D_MODEL)), _resident(w_q.shape),
                  _resident((1, MEM_WIDTH)), _resident(bd128.shape), _resident(mk.shape), _resident(mv.shape),
                  _resident(w_mo.shape)],
        out_specs=[_rows(tm, D_MODEL), _rows(tm, MEM_WIDTH), const2((M, MEM_WIDTH)), const2((M, MEM_WIDTH)),
                   const2((1, MEM_HEAD_DIM)), const2((1, D_MODEL))],
        out_shape=[_sds((T, D_MODEL), _F32), _sds((T, MEM_WIDTH), _MXU), _sds((M, MEM_WIDTH), _F32),
                   _sds((M, MEM_WIDTH), _F32), _sds((1, MEM_HEAD_DIM), _F32), _sds((1, D_MODEL), _F32)],
        scratch_shapes=[pltpu.VMEM((1, MEM_WIDTH), _F32)],
    )(dh2, h1, g_q, w_q, gqm, bd128, mk, mv, w_mo)


def _mem_kv_bwd(dmk, dmv, mkv, mn, mem, g_kv, gkm, bd128, w_kv, dst, dst_shape, block):
    rows = D_MODEL // N_CHIPS

    def body(dmk_ref, dmv_ref, mkv_ref, mn_ref, mem_ref, g_ref, gk_ref, bd_ref, w_ref, *rest):
        dw_ref, dgk_ref, dgain_ref = rest[-3:]
        kraw = mkv_ref[:, :MEM_WIDTH]
        dk, dgk = _headnorm_bwd(dmk_ref[...], kraw, gk_ref[...], bd_ref[...], MEM_HEAD_DIM)
        dgk_ref[...] = _fold_heads(dgk, MEM_HEADS, MEM_HEAD_DIM)
        dmkv = jnp.concatenate([dk, dmv_ref[...]], axis=-1).astype(_MXU)
        dw_ref[...] = _dot_tn(mn_ref[...], dmkv).reshape(N_CHIPS, rows, 2 * MEM_WIDTH)
        dmn = _dot_nt(dmkv, w_ref[...])
        _, dgain = _norm_bwd(dmn, mem_ref[...], g_ref[...])
        dgain_ref[...] = dgain

    args = (dmk, dmv, mkv, mn, mem, g_kv, gkm, bd128, w_kv)
    whole = lambda a: pl.BlockSpec(a.shape, lambda i: (0,) * a.ndim)
    in_specs, alias = [whole(a) for a in args], {}
    if dst is not None:
        in_specs, args, alias = in_specs + [_ANY], args + (dst,), {len(args): 0}
    return pl.pallas_call(
        body, name="mem_kv_bwd", grid=(1,), in_specs=in_specs,
        out_specs=[pl.BlockSpec((N_CHIPS, rows, 2 * MEM_WIDTH), lambda i: (0, block, 0)),
                   pl.BlockSpec((1, MEM_HEAD_DIM), lambda i: (0, 0)), pl.BlockSpec((1, D_MODEL), lambda i: (0, 0))],
        out_shape=[_sds(dst_shape, _F32), _sds((1, MEM_HEAD_DIM), _F32), _sds((1, D_MODEL), _F32)],
        input_output_aliases=alias,
    )(*args)


def _out_proj_bwd(dh1, mixed, o, bd64, w_pool, pscale, w_out):
    T = dh1.shape[0]
    tm = min(_TM, T)
    hb = tm // HALO
    nt = T // tm

    def body(dh_ref, halo_ref, mx_ref, o_ref, bd_ref, wp_ref, ps_ref, wo_ref, doa_ref, dpin_ref, dwp_ref, dps_ref):
        i = pl.program_id(0)

        @pl.when(i == 0)
        def _():
            dwp_ref[...] = jnp.zeros_like(dwp_ref)
            dps_ref[...] = jnp.zeros_like(dps_ref)

        dcat = _dot_nt(dh_ref[...].astype(_MXU), wo_ref[...])
        do = dcat[:, :FOX_WIDTH].astype(_MXU)
        delta = _group_sum(do.astype(_F32) * o_ref[...].astype(_F32), bd_ref[...])
        for h in range(FOX_HEADS):
            sl = slice(h * FOX_HEAD_DIM, (h + 1) * FOX_HEAD_DIM)
            doa_ref[:, h * LANES:(h + 1) * LANES] = _aug_head(
                do[:, sl], _split3(-delta[:, h * FOX_HEAD_DIM:h * FOX_HEAD_DIM + 1]), _ZEROS3)
        dy = dcat[:, FOX_WIDTH:]
        dyh = _dot_nt(halo_ref[...].astype(_MXU), wo_ref[FOX_WIDTH:, :])
        dyh = jnp.where(i == nt - 1, 0.0, dyh)
        ps = ps_ref[...]
        t = (i * tm + lax.broadcasted_iota(jnp.int32, (tm + HALO, 1), 0) + 1).astype(_F32)
        mixed_t = mx_ref[...]
        dpin, dps = [], []
        for g, w in enumerate(POOL_WINDOWS):
            sl = slice(g * POOL_GROUP_DIM, (g + 1) * POOL_GROUP_DIM)
            mg = mixed_t[:, sl]
            wg = wp_ref[g]
            dps.append(jnp.sum(dy[:, sl] * _dot(mg, wg), axis=0, keepdims=True))
            dyl = (dy[:, sl] * ps[:, sl]).astype(_MXU)
            dylh = (dyh[:, sl] * ps[:, sl]).astype(_MXU)
            dwp_ref[g] += _dot_tn(mg, dyl)
            dmx = _dot_nt(dyl, wg)
            ext = jnp.concatenate([dmx, _dot_nt(dylh, wg)], axis=0) / jnp.minimum(t, float(w))
            dpin.append(_pool_window_sum(ext, w, True)[:tm, :] - dmx)
        dpin_ref[...] = jnp.concatenate(dpin, axis=-1)
        dps_ref[...] += jnp.concatenate(dps, axis=-1)

    return pl.pallas_call(
        body, name="out_proj_bwd", grid=(nt,),
        in_specs=[_rows(tm, D_MODEL),
                  pl.BlockSpec((HALO, D_MODEL), lambda i: (jnp.minimum((i + 1) * hb, T // HALO - 1), 0)),
                  _rows(tm, POOL_WIDTH), _rows(tm, FOX_WIDTH), _resident(bd64.shape), _resident(w_pool.shape),
                  _resident((1, POOL_WIDTH)), _resident(w_out.shape)],
        out_specs=[_rows(tm, FOX_HEADS * LANES), _rows(tm, POOL_WIDTH),
                   pl.BlockSpec(w_pool.shape, lambda i: (0, 0, 0)), pl.BlockSpec((1, POOL_WIDTH), lambda i: (0, 0))],
        out_shape=[_sds((T, FOX_HEADS * LANES), _MXU), _sds((T, POOL_WIDTH), _F32), _sds(w_pool.shape, _F32),
                   _sds((1, POOL_WIDTH), _F32)],
    )(dh1, dh1, mixed, o, bd64, w_pool, pscale, w_out)


def _scatter_behind(pid, t, n_pairs, steps, srcs, gots, send_sems, recv_sems):
    x, y, c = _mesh_pos()
    me = 2 * x + y
    chips = _other_chips(x, y)
    every = [(a, j) for a in range(len(srcs)) for j in range(3)]

    def send(a, j):
        cx, cy = chips[j]
        return _remote(srcs[a].at[2 * cx + cy], gots[a].at[me], send_sems, recv_sems, 3 * a + j, (cx, cy, c))

    def arrival(a, j):
        cx, cy = chips[j]
        slab = gots[a].at[2 * cx + cy]
        return _remote(slab, slab, send_sems, recv_sems, 3 * a + j, (cx, cy, c))

    @pl.when((pid == 0) & (t == 0))
    def _():
        for a, j in every:
            send(a, j).start()

    @pl.when((pid == n_pairs - 1) & (t == steps - 1))
    def _():
        for a, j in every:
            arrival(a, j).wait_recv()
        for a, j in every:
            send(a, j).wait_send()


def _fox_bwd(qa, ka, va, doa, pt, mb, lse, scatter=()):
    T = qa.shape[0]
    tq = min(_TQ, T)
    nq = T // tq
    pair = 2 * FOX_HEAD_DIM
    ii, jj = _causal_steps(nq, False)
    fwd_step = ii * (ii + 1) // 2 + jj
    row_lane, col_lane = FOX_HEAD_DIM, FOX_HEAD_DIM + 3
    n = len(scatter)
    n_pairs = FOX_HEADS // 2
    steps = int(ii.shape[0])

    def body(ii_ref, jj_ref, fs_ref, q_ref, k_ref, v_ref, do_ref, p_ref, mb_ref, lse_ref, *refs):
        srcs, (dq_ref, dk_ref, dv_ref, drs_ref, dcs_ref), gots = refs[:n], refs[n:n + 5], refs[n + 5:2 * n + 5]
        dk_acc, dv_acc = refs[2 * n + 5:2 * n + 7]
        t = pl.program_id(1)
        i = ii_ref[t]
        j = jj_ref[t]
        if n:
            _scatter_behind(pl.program_id(0), t, n_pairs, steps, srcs, gots, *refs[2 * n + 7:])

        @pl.when(t == 0)
        def _():
            dq_ref[...] = jnp.zeros_like(dq_ref)
            drs_ref[...] = jnp.zeros_like(drs_ref)

        @pl.when(i == j)
        def _():
            dk_acc[...] = jnp.zeros_like(dk_acc)
            dv_acc[...] = jnp.zeros_like(dv_acc)

        r = jnp.exp(mb_ref[0, 0] - lse_ref[0])
        dqs, rs = [], []
        for hh in range(2):
            sl = slice(LANES * hh, LANES * (hh + 1))
            dof = do_ref[:, sl].astype(_F32) * r[:, hh:hh + 1]
            dob = dof.astype(_MXU)
            p = p_ref[0, 0, hh]
            dv_acc[hh] += _dot(dof.T.astype(_MXU), p)
            dsb = (p.astype(_F32) * _dot_nt(dob, v_ref[:, sl])).astype(_MXU)
            dk_acc[hh] += _dot(q_ref[:, sl].astype(_F32).T.astype(_MXU), dsb)
            dqa = _dot(dsb, k_ref[:, sl])
            dqs.append(dqa[:, :FOX_HEAD_DIM] * (FOX_HEAD_DIM ** -0.5))
            rs.append(dqa[:, row_lane:row_lane + 1])
        rows = pl.ds(pl.multiple_of(i * tq, tq), tq)
        dq_ref[rows, :] += jnp.concatenate(dqs, axis=-1)
        drs_ref[0, rows, :] += _two_lanes(rs[0], rs[1])

        @pl.when(i == nq - 1)
        def _():
            dk = [dk_acc[hh].T for hh in range(2)]
            dv = [dv_acc[hh].T for hh in range(2)]
            dk_ref[...] = jnp.concatenate([dk[0][:, :FOX_HEAD_DIM], dk[1][:, :FOX_HEAD_DIM]], axis=-1)
            dv_ref[...] = jnp.concatenate([dv[0][:, :FOX_HEAD_DIM], dv[1][:, :FOX_HEAD_DIM]], axis=-1)
            dcs_ref[0] = _two_lanes(dk[0][:, col_lane:col_lane + 1], dk[1][:, col_lane:col_lane + 1])

    qspec = pl.BlockSpec((tq, 2 * LANES), lambda p, t, ii, jj, fs: (ii[t], p))
    kspec = pl.BlockSpec((tq, 2 * LANES), lambda p, t, ii, jj, fs: (jj[t], p))
    kout = pl.BlockSpec((tq, pair), lambda p, t, ii, jj, fs: (jj[t], p))
    sems = [pltpu.SemaphoreType.DMA((3 * n,)), pltpu.SemaphoreType.DMA((3 * n,))] if n else []
    res = pl.pallas_call(
        body, name="fox_bwd_scatter" if n else "fox_bwd",
        grid_spec=pltpu.PrefetchScalarGridSpec(
            num_scalar_prefetch=3, grid=(n_pairs, steps),
            in_specs=[qspec, kspec, kspec, qspec,
                      pl.BlockSpec((1, 1, 2, tq, tq), lambda p, t, ii, jj, fs: (p, fs[t], 0, 0, 0)),
                      pl.BlockSpec((1, 1, tq, LANES), lambda p, t, ii, jj, fs: (p, fs[t], 0, 0)),
                      pl.BlockSpec((1, tq, LANES), lambda p, t, ii, jj, fs: (p, ii[t], 0))] + [_ANY] * n,
            out_specs=[pl.BlockSpec((T, pair), lambda p, t, ii, jj, fs: (0, p)), kout, kout,
                       pl.BlockSpec((1, T, LANES), lambda p, t, ii, jj, fs: (p, 0, 0)),
                       pl.BlockSpec((1, tq, LANES), lambda p, t, ii, jj, fs: (p, jj[t], 0))] + [_ANY] * n,
            scratch_shapes=[pltpu.VMEM((2, LANES, tq), _F32), pltpu.VMEM((2, LANES, tq), _F32)] + sems),
        out_shape=[_sds((T, FOX_WIDTH), _F32), _sds((T, FOX_WIDTH), _F32), _sds((T, FOX_WIDTH), _F32),
                   _sds((n_pairs, T, LANES), _F32), _sds((n_pairs, T, LANES), _F32)]
        + [_sds(p.shape, p.dtype) for p in scatter],
    )(ii, jj, fwd_step, qa, ka, va, doa, pt, mb, lse, *scatter)
    return res[0], res[1], res[2], res[3], res[4], list(res[5:])


def _mix_in_bwd(dh1, h, g_mix, zqk, dq, dk, dv, dpin, df, gq, gk, bd64, w_main, w_f):
    T = h.shape[0]
    tm = min(_TM, T)
    nt = T // tm

    def body(dh1_ref, h_ref, g_ref, zqk_ref, dq_ref, dk_ref, dv_ref, dpin_ref, df_ref, gq_ref, gk_ref, bd_ref,
             wm_ref, wf_ref, dh_ref, dz_ref, dzf_ref, dgq_ref, dgk_ref, dgain_ref, dbf_ref, dgq_acc, dgk_acc):
        i = pl.program_id(0)

        @pl.when(i == 0)
        def _():
            dgain_ref[...] = jnp.zeros_like(dgain_ref)
            dbf_ref[...] = jnp.zeros_like(dbf_ref)
            dgq_acc[...] = jnp.zeros_like(dgq_acc)
            dgk_acc[...] = jnp.zeros_like(dgk_acc)

        bd = bd_ref[...]
        dqr, dgq = _headnorm_bwd(dq_ref[...], zqk_ref[:, :FOX_WIDTH], gq_ref[...], bd, FOX_HEAD_DIM)
        dkr, dgk = _headnorm_bwd(dk_ref[...], zqk_ref[:, FOX_WIDTH:], gk_ref[...], bd, FOX_HEAD_DIM)
        dgq_acc[...] += dgq
        dgk_acc[...] += dgk
        dz = jnp.concatenate([dqr, dkr, dv_ref[...], dpin_ref[...]], axis=-1).astype(_MXU)
        dz_ref[...] = dz
        df = df_ref[...]
        dzf = df.astype(_MXU)
        dzf_ref[...] = dzf
        dbf_ref[...] += jnp.sum(df, axis=0, keepdims=True)
        dxn = _dot_nt(dz, wm_ref[...]) + _dot_nt(dzf, wf_ref[...])
        dx, dgain = _norm_bwd(dxn, h_ref[...], g_ref[...])
        dh_ref[...] = dh1_ref[...] + dx
        dgain_ref[...] += dgain

        @pl.when(i == nt - 1)
        def _():
            dgq_ref[...] = _fold_heads(dgq_acc[...], FOX_HEADS, FOX_HEAD_DIM)
            dgk_ref[...] = _fold_heads(dgk_acc[...], FOX_HEADS, FOX_HEAD_DIM)

    const2 = lambda shape: pl.BlockSpec(shape, lambda i: (0, 0))
    return pl.pallas_call(
        body, name="mix_in_bwd", grid=(nt,),
        in_specs=[_rows(tm, D_MODEL), _rows(tm, D_MODEL), _resident((1, D_MODEL)), _rows(tm, 2 * FOX_WIDTH),
                  _rows(tm, FOX_WIDTH), _rows(tm, FOX_WIDTH), _rows(tm, FOX_WIDTH), _rows(tm, POOL_WIDTH),
                  _rows(tm, LANES), _resident((1, FOX_WIDTH)), _resident((1, FOX_WIDTH)), _resident(bd64.shape),
                  _resident(w_main.shape), _resident(w_f.shape)],
        out_specs=[_rows(tm, D_MODEL), _rows(tm, 4 * FOX_WIDTH), _rows(tm, LANES), const2((1, FOX_HEAD_DIM)),
                   const2((1, FOX_HEAD_DIM)), const2((1, D_MODEL)), const2((1, LANES))],
        out_shape=[_sds((T, D_MODEL), _F32), _sds((T, 4 * FOX_WIDTH), _MXU), _sds((T, LANES), _MXU),
                   _sds((1, FOX_HEAD_DIM), _F32), _sds((1, FOX_HEAD_DIM), _F32), _sds((1, D_MODEL), _F32),
                   _sds((1, LANES), _F32)],
        scratch_shapes=[pltpu.VMEM((1, FOX_WIDTH), _F32), pltpu.VMEM((1, FOX_WIDTH), _F32)],
    )(dh1, h, g_mix, zqk, dq, dk, dv, dpin, df, gq, gk, bd64, w_main, w_f)


CLASSES = (("a", D_MODEL, ("w_out", "w_mem_kv")), ("d", D_MODEL, ("w_down",)), ("g", _TF, ("w_gate_up",)),
           ("i", 514, ("w_in",)), ("q", MEM_WIDTH, ("w_mem_q",)), ("o", 256, ("w_mem_out",)))
W_OUT_ROWS = D_MODEL // N_CHIPS
W_DOWN_ROWS = D_FF // N_CHIPS
W_MEM_OUT_COLS = D_MODEL // N_CHIPS


def _late_params(WC, l):
    rows = lambda buf, first, n: buf[:, first:first + n]
    return dict(
        w_out=rows(WC["a"], l * W_OUT_ROWS, W_OUT_ROWS).reshape(D_MODEL, D_MODEL),
        w_mem_q=rows(WC["q"], l * W_OUT_ROWS, W_OUT_ROWS).reshape(D_MODEL, MEM_WIDTH),
        w_mem_kv=rows(WC["a"], (DEPTH + l) * W_OUT_ROWS, W_OUT_ROWS).reshape(D_MODEL, 2 * MEM_WIDTH),
        w_mem_out=rows(WC["o"], l * MEM_WIDTH, MEM_WIDTH).transpose(1, 0, 2).reshape(MEM_WIDTH, D_MODEL),
        w_gu=WC["g"],
        w_d=WC["d"],
    )


def _layer_params(WC, WS, l):
    rows = lambda buf, first, n: buf[:, first:first + n]
    w_in = rows(WC["i"], l * D_MODEL, D_MODEL).transpose(1, 0, 2).reshape(D_MODEL, -1)
    n_main = 3 * FOX_WIDTH + POOL_WIDTH
    row = lambda a: a.reshape(1, -1).astype(_F32)
    return dict(
        layer=l,
        g_mix=row(WS["g_mix"][l]),
        w_main=w_in[:, :n_main],
        w_f=jnp.pad(w_in[:, n_main:], ((0, 0), (0, LANES - FOX_HEADS))),
        b_f=jnp.pad(row(WS["b_forget"][l]), ((0, 0), (0, LANES - FOX_HEADS))),
        gq=jnp.tile(row(WS["g_q_fox"][l]), (1, FOX_HEADS)),
        gk=jnp.tile(row(WS["g_k_fox"][l]), (1, FOX_HEADS)),
        w_pool=WS["w_pool"][l].astype(_MXU),
        pscale=row(WS["pool_scale"][l]),
        g_mem_q=row(WS["g_mem_q"][l]),
        g_mem_kv=row(WS["g_mem_kv"][l]),
        gqm=jnp.tile(row(WS["g_q_mem"][l]), (1, MEM_HEADS)),
        gkm=jnp.tile(row(WS["g_k_mem"][l]), (1, MEM_HEADS)),
        g_ffn=row(WS["g_ffn"][l]),
    )


def _layer_fwd(h, mem, P, bd64, bd128, pending=None):
    s = dict(h=h)
    s["xn"], s["zqk"], qn, kn, v, pin, s["fl"] = _mix_in_fwd(
        h, P["g_mix"], P["w_main"], P["w_f"], P["b_f"], P["gq"], P["gk"], bd64)
    s["qa"], s["ka"], s["va"] = _gate_fwd(s["fl"], qn, kn, v)
    s["o"], s["pt"], s["mb"], s["lse"], landed = _fox_fwd(s["qa"], s["ka"], s["va"], pending[0] if pending else ())
    if pending:
        pending[1](landed)
    s["h1"], s["mixed"], s["y"] = _out_proj_fwd(h, s["o"], pin, P["w_pool"], P["pscale"], P["w_out"])
    s["mn"], s["mkv"], s["mk"], s["mv"] = _mem_kv_fwd(mem, P["g_mem_kv"], P["w_mem_kv"], P["gkm"], bd128)
    s["h2"], s["hn_mem"], s["mo"] = _mem_attn_fwd(s["h1"], P["g_mem_q"], P["w_mem_q"], P["gqm"], bd128, s["mk"],
                                                  s["mv"], P["w_mem_out"])
    h3, s["hn_ffn"] = _ffn_fwd(s["h2"], P["g_ffn"], P["w_gu"], P["w_d"], P["layer"])
    return h3, s


def _layer_bwd(dh3, mem, P, s, bd64, bd128, G, shapes, early=None):
    l = P["layer"]
    g = {}

    def into(c, a, b, name, tk, tn, index, chips=1):
        G[c] = _matmul_tn(a, b, name, tk=tk, tn=tn, dst=G.get(c), dst_shape=shapes[c], dst_index=index,
                          dst_chips=chips)

    dh2, act, dgate, dup, g["g_ffn"] = _ffn_bwd(dh3, s["h2"], s["hn_ffn"], P["g_ffn"], P["w_gu"], P["w_d"], l)
    into("d", act, dh3, "dw_down", 2 * W_DOWN_ROWS, D_MODEL, lambda i, j: (i, l, 0), chips=2)
    into("g", s["hn_ffn"], dgate, "dw_gate", D_MODEL, _TF, lambda i, j: (j, l, 0))
    into("g", s["hn_ffn"], dup, "dw_up", D_MODEL, _TF, lambda i, j: (j + D_FF // _TF, l, 0))

    dh1, dmq, dmk, dmv, g["g_q_mem"], g["g_mem_q"] = _mem_attn_bwd(
        dh2, s["h1"], P["g_mem_q"], P["w_mem_q"], P["gqm"], bd128, s["mk"], s["mv"], P["w_mem_out"])
    into("o", s["mo"], dh2, "dw_mem_out", MEM_WIDTH, W_MEM_OUT_COLS, lambda i, j: (j, l, 0))
    into("q", s["hn_mem"], dmq, "dw_mem_q", W_OUT_ROWS, MEM_WIDTH, lambda i, j: (i, l, 0))
    G["a"], g["g_k_mem"], g["g_mem_kv"] = _mem_kv_bwd(dmk, dmv, s["mkv"], s["mn"], mem, P["g_mem_kv"], P["gkm"],
                                                      bd128, P["w_mem_kv"], G.get("a"), shapes["a"], DEPTH + l)

    doa, dpin, g["w_pool"], g["pool_scale"] = _out_proj_bwd(dh1, s["mixed"], s["o"], bd64, P["w_pool"], P["pscale"],
                                                            P["w_out"])
    half = FOX_WIDTH // W_OUT_ROWS
    into("a", s["o"], dh1, "dw_out_fox", W_OUT_ROWS, D_MODEL, lambda i, j: (i, l, 0))
    into("a", s["y"], dh1, "dw_out_pool", W_OUT_ROWS, D_MODEL, lambda i, j: (i + half, l, 0))
    sent = early(G) if early else ()
    dq, dk, dv, drs, dcs, landed = _fox_bwd(s["qa"], s["ka"], s["va"], doa, s["pt"], s["mb"], s["lse"], sent)
    df = _gate_bwd(drs, dcs, s["fl"])
    dh, dz, dzf, g["g_q_fox"], g["g_k_fox"], g["g_mix"], dbf = _mix_in_bwd(
        dh1, s["h"], P["g_mix"], s["zqk"], dq, dk, dv, dpin, df, P["gq"], P["gk"], bd64, P["w_main"], P["w_f"])
    g["b_forget"] = dbf[:, :FOX_HEADS]
    dw_in = jnp.concatenate([_matmul_tn(s["xn"], dz, "dw_in_main"),
                             _matmul_tn(s["xn"], dzf, "dw_in_gate")[:, :FOX_HEADS]], axis=1)
    return dh, G, g, _to_shards(dw_in, 1), (sent, landed)


def _device_grads(x, mem, tgt, WC, WS, pending=None, early=None):
    bd64 = _blockdiag_ones(FOX_HEADS, FOX_HEAD_DIM)
    bd128 = _blockdiag_ones(MEM_HEADS, MEM_HEAD_DIM)
    params = [_layer_params(WC, WS, l) for l in range(DEPTH)]
    shapes = {c: WC[c].shape for c in WC}

    def complete(buffers):
        for l in range(DEPTH):
            params[l].update(_late_params(buffers, l))

    first = None
    if pending is None:
        complete(WC)
    else:
        slabs, place = pending
        late = list(slabs)
        shapes.update({c: (N_CHIPS,) + slabs[c].shape for c in late})
        first = ([slabs[c] for c in late],
                 lambda landed: complete({c: place(buf, slabs[c]) for c, buf in zip(late, landed)}))
    h, saved = x, []
    for l in range(DEPTH):
        h, s = _layer_fwd(h, mem, params[l], bd64, bd128, first if l == 0 else None)
        saved.append(s)
    dh, loss = _loss_grad(h, tgt)
    G, small, dw_in = {}, [None] * DEPTH, [None] * DEPTH
    for l in reversed(range(DEPTH)):
        dh, G, small[l], dw_in[l], scattered = _layer_bwd(dh, mem, params[l], saved[l], bd64, bd128, G, shapes,
                                                          early if l == 0 else None)
    G["i"] = jnp.concatenate(dw_in, axis=1)
    gsmall = {n: jnp.stack([small[l][n].reshape(WS[n].shape[1:]) for l in range(DEPTH)]) for n in SMALL}
    return loss, dh, G, gsmall, scattered


def _class_slabs(shards):
    return {c: jnp.concatenate([shards[n].reshape(-1, width) for n in names]) for c, width, names in CLASSES}


def _class_rows(shards):
    where = {}
    for c, width, names in CLASSES:
        off = 0
        for n in names:
            rows = shards[n].shape[0] * shards[n].shape[1]
            where[n] = (c, off, rows)
            off += rows
    return where


def _to_shards(full, ax):
    shp = full.shape
    parts = full.reshape(shp[:ax] + (N_CHIPS, shp[ax] // N_CHIPS) + shp[ax + 1:])
    return jnp.moveaxis(parts, ax, 0)


def _from_shards(parts, ax):
    full = jnp.moveaxis(parts, 0, ax)
    shp = full.shape
    return full.reshape(shp[:ax] + (shp[ax] * shp[ax + 1],) + shp[ax + 2:])


def _pack_small(arrs):
    flat = jnp.concatenate([a.reshape(-1).astype(_F32) for a in arrs])
    rows = -(-flat.shape[0] // (8 * D_MODEL)) * 8
    return jnp.pad(flat, (0, rows * D_MODEL - flat.shape[0])).reshape(rows, D_MODEL)


def _unpack_small(flat, shapes):
    flat = flat.reshape(-1)
    out, off = [], 0
    for shp in shapes:
        n = 1
        for d in shp:
            n *= d
        out.append(flat[off:off + n].reshape(shp))
        off += n
    return out


def _mesh_pos():
    return lax.axis_index("x"), lax.axis_index("y"), lax.axis_index("c")


def _other_chips(x, y):
    return [(1 - x, y), (x, 1 - y), (1 - x, 1 - y)]


def _half_rows(ref_rows, which):
    half = ref_rows // 2
    return pl.ds(pl.multiple_of(which * half, 16), half)


def _remote(src_ref, dst_ref, send_sems, recv_sems, k, to):
    return pltpu.make_async_remote_copy(src_ref=src_ref, dst_ref=dst_ref, send_sem=send_sems.at[k],
                                        recv_sem=recv_sems.at[k], device_id=to, device_id_type=MESH)


def _allgather_weights(slabs):
    n = len(slabs)

    def body(*refs):
        srcs, outs, send_sems, recv_sems = refs[:n], refs[n:2 * n], refs[2 * n], refs[2 * n + 1]
        x, y, c = _mesh_pos()
        me = 2 * x + y
        sibling = (x, y, 1 - c)
        chips = _other_chips(x, y)
        first, passed = [], []
        for a, (src, out) in enumerate(zip(srcs, outs)):
            mine = _half_rows(src.shape[0], c)
            for j, chip in enumerate(chips):
                cp = _remote(src.at[mine], out.at[me, mine], send_sems, recv_sems, 6 * a + j, (*chip, c))
                cp.start()
                first.append(cp)
        for a, out in enumerate(outs):
            mine = _half_rows(out.shape[1], c)
            for j, (cx, cy) in enumerate(chips):
                slab = out.at[2 * cx + cy, mine]
                _remote(slab, slab, send_sems, recv_sems, 6 * a + j, (cx, cy, c)).wait_recv()
                fwd = _remote(slab, slab, send_sems, recv_sems, 6 * a + 3 + j, sibling)
                fwd.start()
                passed.append(fwd)
        for a, out in enumerate(outs):
            theirs = _half_rows(out.shape[1], 1 - c)
            for j, (cx, cy) in enumerate(chips):
                slab = out.at[2 * cx + cy, theirs]
                _remote(slab, slab, send_sems, recv_sems, 6 * a + 3 + j, sibling).wait_recv()
        for cp in first + passed:
            cp.wait_send()

    return pl.pallas_call(
        body, name="allgather_weights", in_specs=[_ANY] * n, out_specs=[_ANY] * n,
        out_shape=[_sds((N_CHIPS,) + s.shape, s.dtype) for s in slabs],
        scratch_shapes=[pltpu.SemaphoreType.DMA((6 * n,)), pltpu.SemaphoreType.DMA((6 * n,))],
    )(*slabs)


def _exchange_halves(grads, name):
    n = len(grads)

    def body(*refs):
        srcs, gots, send_sems, recv_sems = refs[:n], refs[n:2 * n], refs[2 * n], refs[2 * n + 1]
        x, y, c = _mesh_pos()
        copies = []
        for a, (src, got) in enumerate(zip(srcs, gots)):
            theirs = _half_rows(src.shape[1], 1 - c)
            for k in range(N_CHIPS):
                cp = _remote(src.at[k, theirs], got.at[k], send_sems, recv_sems, N_CHIPS * a + k, (x, y, 1 - c))
                cp.start()
                copies.append(cp)
        for cp in copies:
            cp.wait()

    return pl.pallas_call(
        body, name=name, in_specs=[_ANY] * n, out_specs=[_ANY] * n,
        out_shape=[_sds((N_CHIPS, g.shape[1] // 2, g.shape[2]), _F32) for g in grads],
        scratch_shapes=[pltpu.SemaphoreType.DMA((N_CHIPS * n,)), pltpu.SemaphoreType.DMA((N_CHIPS * n,))],
    )(*grads)


def _row_tile(rows):
    return _pick_tile(rows, (704, 512, 256))


def _add_halves(g, got, c_idx, name):
    _, half, width = got.shape
    ta = _row_tile(half)
    nb = half // ta

    def body(c_ref, a_ref, b_ref, o_ref):
        o_ref[...] = (a_ref[...] + b_ref[...]).astype(jnp.bfloat16)

    return pl.pallas_call(
        body, name=name,
        grid_spec=pltpu.PrefetchScalarGridSpec(
            num_scalar_prefetch=1, grid=(N_CHIPS, nb),
            in_specs=[pl.BlockSpec((1, ta, width), lambda k, i, c: (k, c[0] * nb + i, 0)),
                      pl.BlockSpec((1, ta, width), lambda k, i, c: (k, i, 0))],
            out_specs=pl.BlockSpec((1, ta, width), lambda k, i, c: (k, i, 0))),
        out_shape=_sds(got.shape, jnp.bfloat16),
    )(c_idx, g, got)


def _scatter_to_chips(parts):
    n = len(parts)

    def body(*refs):
        srcs, gots, send_sems, recv_sems = refs[:n], refs[n:2 * n], refs[2 * n], refs[2 * n + 1]
        x, y, c = _mesh_pos()
        me = 2 * x + y
        chips = _other_chips(x, y)
        copies = []
        for a, (src, got) in enumerate(zip(srcs, gots)):
            for j, (cx, cy) in enumerate(chips):
                cp = _remote(src.at[2 * cx + cy], got.at[me], send_sems, recv_sems, 3 * a + j, (cx, cy, c))
                cp.start()
                copies.append(cp)
        for a, got in enumerate(gots):
            for j, (cx, cy) in enumerate(chips):
                slab = got.at[2 * cx + cy]
                _remote(slab, slab, send_sems, recv_sems, 3 * a + j, (cx, cy, c)).wait_recv()
        for cp in copies:
            cp.wait_send()

    return pl.pallas_call(
        body, name="grad_scatter_chips", in_specs=[_ANY] * n, out_specs=[_ANY] * n,
        out_shape=[_sds(p.shape, p.dtype) for p in parts],
        scratch_shapes=[pltpu.SemaphoreType.DMA((3 * n,)), pltpu.SemaphoreType.DMA((3 * n,))],
    )(*parts)


def _sum_chips(got, c_idx, name):
    _, half, width = got.shape
    ta = _row_tile(half)
    nb = half // ta

    def body(c_ref, a_ref, o_ref):
        f = lambda k: a_ref[k].astype(_F32)
        o_ref[...] = ((f(0) + f(1)) + f(2)) + f(3)

    return pl.pallas_call(
        body, name=name,
        grid_spec=pltpu.PrefetchScalarGridSpec(
            num_scalar_prefetch=1, grid=(nb,),
            in_specs=[pl.BlockSpec((N_CHIPS, ta, width), lambda i, c: (0, i, 0))],
            out_specs=pl.BlockSpec((ta, width), lambda i, c: (c[0] * nb + i, 0))),
        out_shape=_sds((2 * half, width), _F32),
    )(c_idx, got)


def _share_with_sibling(bufs):
    n = len(bufs)

    def body(*refs):
        outs, send_sems, recv_sems = refs[n:2 * n], refs[2 * n], refs[2 * n + 1]
        x, y, c = _mesh_pos()
        copies = []
        for a, out in enumerate(outs):
            mine = out.at[_half_rows(out.shape[0], c)]
            cp = _remote(mine, mine, send_sems, recv_sems, a, (x, y, 1 - c))
            cp.start()
            copies.append(cp)
        for a, out in enumerate(outs):
            theirs = out.at[_half_rows(out.shape[0], 1 - c)]
            _remote(theirs, theirs, send_sems, recv_sems, a, (x, y, 1 - c)).wait_recv()
        for cp in copies:
            cp.wait_send()

    return pl.pallas_call(
        body, name="grad_share_sibling", in_specs=[_ANY] * n, out_specs=[_ANY] * n,
        out_shape=[_sds(b.shape, _F32) for b in bufs], input_output_aliases={a: a for a in range(n)},
        scratch_shapes=[pltpu.SemaphoreType.DMA((n,)), pltpu.SemaphoreType.DMA((n,))],
    )(*bufs)


def _allreduce_small(g):
    rows = g.shape[0]
    n_dev = 2 * N_CHIPS

    def body(g_ref, out_ref, gathered, local_sem, send_sems, recv_sems):
        x, y, c = _mesh_pos()
        me = 4 * x + 2 * y + c
        own = pltpu.make_async_copy(g_ref, gathered.at[me], local_sem)
        own.start()
        copies = []
        for k in range(1, n_dev):
            fx, fy, fc = (k >> 2) & 1, (k >> 1) & 1, k & 1
            cp = pltpu.make_async_remote_copy(
                src_ref=g_ref, dst_ref=gathered.at[me], send_sem=send_sems.at[k - 1], recv_sem=recv_sems.at[k - 1],
                device_id=(x ^ fx, y ^ fy, c ^ fc), device_id_type=MESH)
            cp.start()
            copies.append(cp)
        for k in range(1, n_dev):
            fx, fy, fc = (k >> 2) & 1, (k >> 1) & 1, k & 1
            px, py, pc = x ^ fx, y ^ fy, c ^ fc
            slab = gathered.at[4 * px + 2 * py + pc]
            pltpu.make_async_remote_copy(src_ref=slab, dst_ref=slab, send_sem=send_sems.at[k - 1],
                                         recv_sem=recv_sems.at[k - 1], device_id=(px, py, pc),
                                         device_id_type=MESH).wait_recv()
        for cp in copies:
            cp.wait_send()
        own.wait()
        acc = gathered[0]
        for d in range(1, n_dev):
            acc = acc + gathered[d]
        out_ref[...] = acc

    vmem = pl.BlockSpec(memory_space=pltpu.VMEM)
    return pl.pallas_call(
        body, name="allreduce_small", in_specs=[vmem], out_specs=vmem, out_shape=_sds((rows, D_MODEL), _F32),
        scratch_shapes=[pltpu.VMEM((n_dev, rows, D_MODEL), _F32), pltpu.SemaphoreType.DMA,
                        pltpu.SemaphoreType.DMA((n_dev - 1,)), pltpu.SemaphoreType.DMA((n_dev - 1,))],
    )(g)


def _adamw(w, g, m, v, name, g_first_row=0):
    shape = w.shape
    cols = shape[-1]
    rows = 1
    for d in shape[:-1]:
        rows *= d
    w2, m2, v2 = (a.reshape(rows, cols) for a in (w, m, v))
    tr = _pick_tile(rows, (256, 128, 64, 32, 16, 8))
    g0 = g_first_row // tr

    def body(w_ref, g_ref, m_ref, v_ref, go_ref, d_ref, nm_ref, nv_ref):
        gg = g_ref[...]
        go_ref[...] = gg
        nm = ADAM_B1 * m_ref[...] + (1.0 - ADAM_B1) * gg
        nv = ADAM_B2 * v_ref[...] + (1.0 - ADAM_B2) * (gg * gg)
        m_hat = nm / (1.0 - ADAM_B1 ** ADAM_STEP)
        v_hat = nv / (1.0 - ADAM_B2 ** ADAM_STEP)
        d_ref[...] = -ADAM_LR * (m_hat / (jnp.sqrt(v_hat) + ADAM_EPS) + ADAM_WD * w_ref[...])
        nm_ref[...] = nm
        nv_ref[...] = nv

    spec = pl.BlockSpec((tr, cols), lambda i: (i, 0))
    outs = pl.pallas_call(
        body, name=name, grid=(rows // tr,),
        in_specs=[spec, pl.BlockSpec((tr, cols), lambda i: (g0 + i, 0)), spec, spec], out_specs=[spec] * 4,
        out_shape=[_sds((rows, cols), _F32)] * 4,
    )(w2, g, m2, v2)
    return tuple(o.reshape(shape) for o in outs)


def kernel(x, mem, g_mix, w_in, b_forget, g_q_fox, g_k_fox, w_pool, pool_scale, w_out, g_mem_q, g_mem_kv, w_mem_q, w_mem_kv, g_q_mem, g_k_mem, w_mem_out, g_ffn, w_gate_up, w_down, loss_target, m_g_mix, m_w_in, m_b_forget, m_g_q_fox, m_g_k_fox, m_w_pool, m_pool_scale, m_w_out, m_g_mem_q, m_g_mem_kv, m_w_mem_q, m_w_mem_kv, m_g_q_mem, m_g_k_mem, m_w_mem_out, m_g_ffn, m_w_gate_up, m_w_down, v_g_mix, v_w_in, v_b_forget, v_g_q_fox, v_g_k_fox, v_w_pool, v_pool_scale, v_w_out, v_g_mem_q, v_g_mem_kv, v_w_mem_q, v_w_mem_kv, v_g_q_mem, v_g_k_mem, v_w_mem_out, v_g_ffn, v_w_gate_up, v_w_down):
    w = dict(g_mix=g_mix, w_in=w_in, b_forget=b_forget, g_q_fox=g_q_fox, g_k_fox=g_k_fox, w_pool=w_pool,
             pool_scale=pool_scale, w_out=w_out, g_mem_q=g_mem_q, g_mem_kv=g_mem_kv, w_mem_q=w_mem_q,
             w_mem_kv=w_mem_kv, g_q_mem=g_q_mem, g_k_mem=g_k_mem, w_mem_out=w_mem_out, g_ffn=g_ffn,
             w_gate_up=w_gate_up, w_down=w_down)
    m = dict(g_mix=m_g_mix, w_in=m_w_in, b_forget=m_b_forget, g_q_fox=m_g_q_fox, g_k_fox=m_g_k_fox, w_pool=m_w_pool,
             pool_scale=m_pool_scale, w_out=m_w_out, g_mem_q=m_g_mem_q, g_mem_kv=m_g_mem_kv, w_mem_q=m_w_mem_q,
             w_mem_kv=m_w_mem_kv, g_q_mem=m_g_q_mem, g_k_mem=m_g_k_mem, w_mem_out=m_w_mem_out, g_ffn=m_g_ffn,
             w_gate_up=m_w_gate_up, w_down=m_w_down)
    v = dict(g_mix=v_g_mix, w_in=v_w_in, b_forget=v_b_forget, g_q_fox=v_g_q_fox, g_k_fox=v_g_k_fox, w_pool=v_w_pool,
             pool_scale=v_pool_scale, w_out=v_w_out, g_mem_q=v_g_mem_q, g_mem_kv=v_g_mem_kv, w_mem_q=v_w_mem_q,
             w_mem_kv=v_w_mem_kv, g_q_mem=v_g_q_mem, g_k_mem=v_g_k_mem, w_mem_out=v_w_mem_out, g_ffn=v_g_ffn,
             w_gate_up=v_w_gate_up, w_down=v_w_down)

    classes = [c for c, _, _ in CLASSES]
    chip = 2 * lax.axis_index("x") + lax.axis_index("y")
    c_idx = lax.axis_index("c").astype(jnp.int32).reshape(1)
    own_slab = lambda bufs, own: lax.dynamic_update_slice(bufs, own[None], (chip, 0, 0))

    slabs = _class_slabs({n: w[n].astype(_MXU) for n in BIG})
    WC = {"i": own_slab(_allgather_weights([slabs["i"]])[0], slabs["i"])}
    later = ({c: slabs[c] for c in classes if c != "i"}, own_slab)

    def chip_sums(names, tag):
        grads = [G[c] for c in names]
        return [_add_halves(g, got, c_idx, "grad_add_halves_" + c)
                for c, g, got in zip(names, grads, _exchange_halves(grads, "grad_exchange_halves" + tag))]

    early = [c for c in classes if c != "i"]
    G = {}

    def send_early(buffers):
        G.update(buffers)
        return chip_sums(early, "")

    loss, grad_x, grads, gsmall, (sent, landed) = _device_grads(
        x[0], mem[0], loss_target[0], WC, {n: w[n] for n in SMALL}, later, send_early)
    loss = lax.psum(loss[0, 0], ("x", "y", "c"))
    G.update(grads)
    sent_i = chip_sums(["i"], "_w_in")
    partial = dict(zip(early + ["i"], list(sent) + sent_i))
    arrived = dict(zip(early + ["i"], list(landed) + list(_scatter_to_chips(sent_i))))
    partials = [own_slab(arrived[c], lax.dynamic_index_in_dim(partial[c], chip, 0, keepdims=False)) for c in classes]
    reduced = _share_with_sibling([_sum_chips(p, c_idx, "grad_sum_chips_" + c) for c, p in zip(classes, partials)])
    reduced = dict(zip(classes, reduced))

    small_shapes = [w[n].shape for n in SMALL]
    gsmall = _unpack_small(_allreduce_small(_pack_small([gsmall[n] for n in SMALL])), small_shapes)

    g_out, d_out, m_out, v_out = {}, {}, {}, {}
    for n, (c, first, _) in _class_rows({n: w[n] for n in BIG}).items():
        g_out[n], d_out[n], m_out[n], v_out[n] = _adamw(w[n], reduced[c], m[n], v[n], "adamw_" + n, first)
    packed = [_pack_small([t[n] for n in SMALL]) for t in (w, m, v)]
    _, ds, ms, vs = _adamw(packed[0], _pack_small(gsmall), packed[1], packed[2], "adamw_small")
    for n, gi, di, mi, vi in zip(SMALL, gsmall, _unpack_small(ds, small_shapes), _unpack_small(ms, small_shapes),
                                 _unpack_small(vs, small_shapes)):
        g_out[n], d_out[n], m_out[n], v_out[n] = gi, di, mi, vi

    return (loss, grad_x[None], *[g_out[n] for n in WEIGHTS], *[d_out[n] for n in WEIGHTS],
            *[m_out[n] for n in WEIGHTS], *[v_out[n] for n in WEIGHTS])
```

```python
import functools

import jax
import jax.numpy as jnp
import numpy as np
from jax import lax
from jax.experimental import pallas as pl
from jax.experimental.pallas import tpu as pltpu

_F32 = jnp.float32
_MXU = jnp.bfloat16

D_MODEL = 1024
DEPTH = 2
FOX_HEADS = 8
FOX_HEAD_DIM = 64
FOX_WIDTH = FOX_HEADS * FOX_HEAD_DIM
POOL_WINDOWS = (2, 4, 8, 16)
POOL_GROUP_DIM = 128
POOL_WIDTH = len(POOL_WINDOWS) * POOL_GROUP_DIM
MEM_HEADS = 4
MEM_HEAD_DIM = 128
MEM_WIDTH = MEM_HEADS * MEM_HEAD_DIM
D_FF = 2816
EPS = 1e-6
LANES = 128
HALO = 16

ADAM_LR = 0.001
ADAM_B1 = 0.9
ADAM_B2 = 0.999
ADAM_EPS = 1e-08
ADAM_WD = 0.01
ADAM_STEP = 10

N_CHIPS = 4
MESH = pl.DeviceIdType.MESH

_TM = 512
_TQ = 1024
_TMF = 256
_TF = 1408
_TT = 1024
_TB = 256
_TA = 512

BIG = ("w_in", "w_out", "w_mem_q", "w_mem_kv", "w_mem_out", "w_gate_up", "w_down")
SHARD_AXIS = {"w_in": 2, "w_out": 1, "w_mem_q": 1, "w_mem_kv": 1, "w_mem_out": 2, "w_gate_up": 2, "w_down": 1}
SMALL = ("g_mix", "b_forget", "g_q_fox", "g_k_fox", "w_pool", "pool_scale", "g_mem_q", "g_mem_kv", "g_q_mem",
         "g_k_mem", "g_ffn")
WEIGHTS = ("g_mix", "w_in", "b_forget", "g_q_fox", "g_k_fox", "w_pool", "pool_scale", "w_out", "g_mem_q", "g_mem_kv",
           "w_mem_q", "w_mem_kv", "g_q_mem", "g_k_mem", "w_mem_out", "g_ffn", "w_gate_up", "w_down")


def _dot(a, b):
    return jnp.dot(a, b, preferred_element_type=_F32)


def _dot_nt(a, b):
    return lax.dot_general(a, b, (((1,), (1,)), ((), ())), preferred_element_type=_F32)


def _dot_tn(a, b):
    return lax.dot_general(a, b, (((0,), (0,)), ((), ())), preferred_element_type=_F32)


def _group_sum(x, ones_blockdiag):
    hi = x.astype(_MXU)
    lo = (x - hi.astype(_F32)).astype(_MXU)
    return _dot(hi, ones_blockdiag) + _dot(lo, ones_blockdiag)


def _tri_dot(tri, x):
    h1 = x.astype(jnp.bfloat16)
    r1 = x - h1.astype(_F32)
    h2 = r1.astype(jnp.bfloat16)
    h3 = (r1 - h2.astype(_F32)).astype(jnp.bfloat16)
    return _dot(tri, h1) + _dot(tri, h2) + _dot(tri, h3)


def _rstd(x):
    return lax.rsqrt(jnp.mean(x * x, axis=-1, keepdims=True) + EPS)


def _norm_bwd(dy, x, g):
    r = _rstd(x)
    xhat = x * r
    u = dy * g
    dx = r * (u - xhat * jnp.mean(u * xhat, axis=-1, keepdims=True))
    return dx, jnp.sum(dy * xhat, axis=0, keepdims=True)


def _headnorm_bwd(dy, x, g, ones_blockdiag, width):
    r = lax.rsqrt(_group_sum(x * x, ones_blockdiag) * (1.0 / width) + EPS)
    xhat = x * r
    u = dy * g
    dx = r * (u - xhat * (_group_sum(u * xhat, ones_blockdiag) * (1.0 / width)))
    return dx, jnp.sum(dy * xhat, axis=0, keepdims=True)


def _fold_heads(row, heads, width):
    acc = row[:, 0:width]
    for h in range(1, heads):
        acc = acc + row[:, h * width:(h + 1) * width]
    return acc


_ANY = pl.BlockSpec(memory_space=pl.ANY)


def _resident(shape):
    return pl.BlockSpec(shape, lambda *_: (0,) * len(shape), pipeline_mode=pl.Buffered(1))


def _rows(tm, width):
    return pl.BlockSpec((tm, width), lambda i: (i, 0))


def _blockdiag_ones(groups, width):
    return jnp.kron(jnp.eye(groups, dtype=_F32), jnp.ones((width, width), _F32)).astype(_MXU)


def _sds(shape, dtype):
    return jax.ShapeDtypeStruct(shape, dtype)


def _mix_in_fwd(h, g_mix, w_main, w_f, b_f, gq, gk, bd64):
    T = h.shape[0]
    tm = min(_TM, T)

    def body(h_ref, g_ref, wm_ref, wf_ref, bf_ref, gq_ref, gk_ref, bd_ref,
             xn_ref, zqk_ref, qn_ref, kn_ref, v_ref, pin_ref, fl_ref):
        x = h_ref[...]
        xn = ((x * _rstd(x)) * g_ref[...]).astype(_MXU)
        xn_ref[...] = xn
        z = _dot(xn, wm_ref[...])
        q = z[:, :FOX_WIDTH]
        k = z[:, FOX_WIDTH:2 * FOX_WIDTH]
        zqk_ref[...] = z[:, :2 * FOX_WIDTH]
        bd = bd_ref[...]
        rq = lax.rsqrt(_group_sum(q * q, bd) * (1.0 / FOX_HEAD_DIM) + EPS)
        rk = lax.rsqrt(_group_sum(k * k, bd) * (1.0 / FOX_HEAD_DIM) + EPS)
        qn_ref[...] = ((q * rq) * gq_ref[...]).astype(_MXU)
        kn_ref[...] = ((k * rk) * gk_ref[...]).astype(_MXU)
        v_ref[...] = z[:, 2 * FOX_WIDTH:3 * FOX_WIDTH].astype(_MXU)
        pin_ref[...] = z[:, 3 * FOX_WIDTH:]
        fl_ref[...] = _dot(xn, wf_ref[...]) + bf_ref[...]

    return pl.pallas_call(
        body, name="mix_in_fwd", grid=(T // tm,),
        in_specs=[_rows(tm, D_MODEL), _resident((1, D_MODEL)), _resident(w_main.shape), _resident(w_f.shape),
                  _resident((1, LANES)), _resident((1, FOX_WIDTH)), _resident((1, FOX_WIDTH)), _resident(bd64.shape)],
        out_specs=[_rows(tm, D_MODEL), _rows(tm, 2 * FOX_WIDTH), _rows(tm, FOX_WIDTH), _rows(tm, FOX_WIDTH),
                   _rows(tm, FOX_WIDTH), _rows(tm, POOL_WIDTH), _rows(tm, LANES)],
        out_shape=[_sds((T, D_MODEL), _MXU), _sds((T, 2 * FOX_WIDTH), _F32), _sds((T, FOX_WIDTH), _MXU),
                   _sds((T, FOX_WIDTH), _MXU), _sds((T, FOX_WIDTH), _MXU), _sds((T, POOL_WIDTH), _F32),
                   _sds((T, LANES), _F32)],
    )(h, g_mix, w_main, w_f, b_f, gq, gk, bd64)


def _split3(x):
    h1 = x.astype(jnp.bfloat16).astype(_F32)
    r1 = x - h1
    h2 = r1.astype(jnp.bfloat16).astype(_F32)
    h3 = (r1 - h2).astype(jnp.bfloat16).astype(_F32)
    return h1, h2, h3


_ONES3 = (1.0, 1.0, 1.0)
_ZEROS3 = (0.0, 0.0, 0.0)


def _aug_head(feat, first, second):
    rows = feat.shape[0]
    lane = lax.broadcasted_iota(jnp.int32, (rows, LANES - FOX_HEAD_DIM), 1)
    aux = jnp.zeros((rows, LANES - FOX_HEAD_DIM), _F32)
    for k in range(3):
        aux = jnp.where(lane == k, first[k], aux)
        aux = jnp.where(lane == 3 + k, second[k], aux)
    return jnp.concatenate([feat.astype(_MXU), aux.astype(_MXU)], axis=-1)


def _gate_fwd(fl, qn, kn, v):
    T = fl.shape[0]
    tb = min(_TB, T)
    wide = FOX_HEADS * LANES

    def body(fl_ref, q_ref, k_ref, v_ref, qa_ref, ka_ref, va_ref, carry):
        @pl.when(pl.program_id(0) == 0)
        def _():
            carry[...] = jnp.zeros_like(carry)

        x = fl_ref[...]
        ls = jnp.minimum(x, 0.0) - jnp.log1p(jnp.exp(-jnp.abs(x)))
        row = lax.broadcasted_iota(jnp.int32, (tb, tb), 0)
        col = lax.broadcasted_iota(jnp.int32, (tb, tb), 1)
        tri = jnp.where(col <= row, 1.0, 0.0).astype(jnp.bfloat16)
        cs = _tri_dot(tri, ls) + carry[...]
        carry[...] = cs[tb - 1:tb, :]
        for h in range(FOX_HEADS):
            sl = slice(h * FOX_HEAD_DIM, (h + 1) * FOX_HEAD_DIM)
            out = slice(h * LANES, (h + 1) * LANES)
            c3 = _split3(cs[:, h:h + 1])
            qa_ref[:, out] = _aug_head(q_ref[:, sl].astype(_F32) * (FOX_HEAD_DIM ** -0.5), c3, _ONES3)
            ka_ref[:, out] = _aug_head(k_ref[:, sl], _ONES3, tuple(-t for t in c3))
            va_ref[:, out] = _aug_head(v_ref[:, sl], _ONES3, _ZEROS3)

    return pl.pallas_call(
        body, name="gate_fwd", grid=(T // tb,),
        in_specs=[_rows(tb, LANES), _rows(tb, FOX_WIDTH), _rows(tb, FOX_WIDTH), _rows(tb, FOX_WIDTH)],
        out_specs=[_rows(tb, wide)] * 3, out_shape=[_sds((T, wide), _MXU)] * 3,
        scratch_shapes=[pltpu.VMEM((1, LANES), _F32)],
    )(fl, qn, kn, v)


def _gate_bwd(drs, dcs, fl):
    T = fl.shape[0]
    tb = min(_TB, T)
    nb = T // tb

    def body(r_ref, d_ref, fl_ref, df_ref, carry):
        @pl.when(pl.program_id(0) == 0)
        def _():
            carry[...] = jnp.zeros_like(carry)

        lane = lax.broadcasted_iota(jnp.int32, (tb, LANES), 1)
        dc = jnp.zeros((tb, LANES), _F32)
        for p in range(FOX_HEADS // 2):
            pair = r_ref[p] - d_ref[p]
            for hh in range(2):
                dc = jnp.where(lane == 2 * p + hh, pair[:, hh:hh + 1], dc)
        row = lax.broadcasted_iota(jnp.int32, (tb, tb), 0)
        col = lax.broadcasted_iota(jnp.int32, (tb, tb), 1)
        tri = jnp.where(col >= row, 1.0, 0.0).astype(jnp.bfloat16)
        rc = _tri_dot(tri, dc) + carry[...]
        carry[...] = rc[0:1, :]
        df_ref[...] = rc * (1.0 / (1.0 + jnp.exp(fl_ref[...])))

    rev = pl.BlockSpec((tb, LANES), lambda i: (nb - 1 - i, 0))
    rev4 = pl.BlockSpec((FOX_HEADS // 2, tb, LANES), lambda i: (0, nb - 1 - i, 0))
    return pl.pallas_call(
        body, name="gate_bwd", grid=(nb,), in_specs=[rev4, rev4, rev], out_specs=rev,
        out_shape=_sds((T, LANES), _F32), scratch_shapes=[pltpu.VMEM((1, LANES), _F32)],
    )(drs, dcs, fl)


def _fox_scores(q_ref, k_ref, hh, masked):
    sl = slice(LANES * hh, LANES * (hh + 1))
    s = _dot_nt(q_ref[:, sl], k_ref[:, sl])
    if masked:
        row = lax.broadcasted_iota(jnp.int32, s.shape, 0)
        col = lax.broadcasted_iota(jnp.int32, s.shape, 1)
        s = jnp.where(col <= row, s, -jnp.inf)
    return s


def _causal_steps(nq, query_major):
    if query_major:
        steps = [(i, j) for i in range(nq) for j in range(i + 1)]
    else:
        steps = [(i, j) for j in range(nq) for i in range(j, nq)]
    return (jnp.asarray(np.array([s[0] for s in steps], np.int32)),
            jnp.asarray(np.array([s[1] for s in steps], np.int32)))


def _two_rows(a, b):
    row = lax.broadcasted_iota(jnp.int32, (8, a.shape[1]), 0)
    return jnp.where(row == 0, a, jnp.where(row == 1, b, 0.0))


def _two_lanes(a, b):
    lane = lax.broadcasted_iota(jnp.int32, (a.shape[0], LANES), 1)
    return jnp.where(lane == 0, a, jnp.where(lane == 1, b, 0.0))


def _gather_behind(pid, t, n_pairs, steps, srcs, outs, send_sems, recv_sems):
    x, y, c = _mesh_pos()
    me = 2 * x + y
    sibling = (x, y, 1 - c)
    chips = _other_chips(x, y)

    def first(a, j):
        src, out = srcs[a], outs[a]
        mine = _half_rows(src.shape[0], c)
        return _remote(src.at[mine], out.at[me, mine], send_sems, recv_sems, 6 * a + j, (*chips[j], c))

    def landed(a, j):
        cx, cy = chips[j]
        slab = outs[a].at[2 * cx + cy, _half_rows(outs[a].shape[1], c)]
        return _remote(slab, slab, send_sems, recv_sems, 6 * a + j, (cx, cy, c))

    def passed(a, j):
        cx, cy = chips[j]
        slab = outs[a].at[2 * cx + cy, _half_rows(outs[a].shape[1], c)]
        return _remote(slab, slab, send_sems, recv_sems, 6 * a + 3 + j, sibling)

    def from_sibling(a, j):
        cx, cy = chips[j]
        slab = outs[a].at[2 * cx + cy, _half_rows(outs[a].shape[1], 1 - c)]
        return _remote(slab, slab, send_sems, recv_sems, 6 * a + 3 + j, sibling)

    every = [(a, j) for a in range(len(srcs)) for j in range(3)]

    @pl.when((pid == 0) & (t == 0))
    def _():
        for a, j in every:
            first(a, j).start()

    @pl.when((pid == n_pairs // 2) & (t == 0))
    def _():
        for a, j in every:
            landed(a, j).wait_recv()
            passed(a, j).start()

    @pl.when((pid == n_pairs - 1) & (t == steps - 1))
    def _():
        for a, j in every:
            from_sibling(a, j).wait_recv()
        for a, j in every:
            first(a, j).wait_send()
            passed(a, j).wait_send()


def _fox_fwd(qa, ka, va, gather=()):
    T = qa.shape[0]
    tq = min(_TQ, T)
    nq = T // tq
    ii, jj = _causal_steps(nq, True)
    steps = int(ii.shape[0])
    n = len(gather)
    n_pairs = FOX_HEADS // 2

    def body(ii_ref, jj_ref, q_ref, k_ref, v_ref, *refs):
        srcs, (o_ref, p_ref, mb_ref, lse_ref), outs = refs[:n], refs[n:n + 4], refs[n + 4:2 * n + 4]
        m_s, acc_s = refs[2 * n + 4:2 * n + 6]
        t = pl.program_id(1)
        i = ii_ref[t]
        j = jj_ref[t]
        if n:
            _gather_behind(pl.program_id(0), t, n_pairs, steps, srcs, outs, *refs[2 * n + 6:])

        @pl.when(j == 0)
        def _():
            m_s[...] = jnp.full(m_s.shape, -jnp.inf, _F32)
            acc_s[...] = jnp.zeros_like(acc_s)

        def step(masked):
            for hh in range(2):
                sl = slice(LANES * hh, LANES * (hh + 1))
                st = _dot_nt(k_ref[:, sl], q_ref[:, sl])
                if masked:
                    key = lax.broadcasted_iota(jnp.int32, st.shape, 0)
                    qry = lax.broadcasted_iota(jnp.int32, st.shape, 1)
                    st = jnp.where(key <= qry, st, -jnp.inf)
                m_prev = m_s[hh]
                m_new = jnp.maximum(m_prev, jnp.max(st, axis=0, keepdims=True))
                p_ref[0, 0, hh] = jnp.exp(st - m_new).astype(_MXU)
                vt = v_ref[:, sl].astype(_F32).T.astype(_MXU)
                acc_s[hh] = jnp.exp(m_prev - m_new) * acc_s[hh] + _dot(vt, p_ref[0, 0, hh])
                m_s[hh] = m_new
            mb_ref[0, 0] = _two_rows(m_s[0], m_s[1])

        @pl.when(j < i)
        def _():
            step(False)

        @pl.when(j == i)
        def _():
            step(True)
            outs, lses = [], []
            for hh in range(2):
                acc = acc_s[hh]
                lses.append(m_s[hh] + jnp.log(acc[FOX_HEAD_DIM:FOX_HEAD_DIM + 1, :]))
                acc = acc.T
                outs.append(acc[:, :FOX_HEAD_DIM] / acc[:, FOX_HEAD_DIM:FOX_HEAD_DIM + 1])
            o_ref[...] = jnp.concatenate(outs, axis=-1).astype(_MXU)
            lse_ref[0] = _two_rows(lses[0], lses[1])

    qspec = pl.BlockSpec((tq, 2 * LANES), lambda p, t, ii, jj: (ii[t], p))
    kspec = pl.BlockSpec((tq, 2 * LANES), lambda p, t, ii, jj: (jj[t], p))
    sems = [pltpu.SemaphoreType.DMA((6 * n,)), pltpu.SemaphoreType.DMA((6 * n,))] if n else []
    res = pl.pallas_call(
        body, name="fox_fwd_gather" if n else "fox_fwd",
        grid_spec=pltpu.PrefetchScalarGridSpec(
            num_scalar_prefetch=2, grid=(n_pairs, steps),
            in_specs=[qspec, kspec, kspec] + [_ANY] * n,
            out_specs=[pl.BlockSpec((tq, 2 * FOX_HEAD_DIM), lambda p, t, ii, jj: (ii[t], p)),
                       pl.BlockSpec((1, 1, 2, tq, tq), lambda p, t, ii, jj: (p, t, 0, 0, 0)),
                       pl.BlockSpec((1, 1, 8, tq), lambda p, t, ii, jj: (p, t, 0, 0)),
                       pl.BlockSpec((1, 8, tq), lambda p, t, ii, jj: (p, 0, ii[t]))] + [_ANY] * n,
            scratch_shapes=[pltpu.VMEM((2, 1, tq), _F32), pltpu.VMEM((2, LANES, tq), _F32)] + sems),
        out_shape=[_sds((T, FOX_WIDTH), _MXU), _sds((n_pairs, steps, 2, tq, tq), _MXU),
                   _sds((n_pairs, steps, 8, tq), _F32), _sds((n_pairs, 8, T), _F32)]
        + [_sds((N_CHIPS,) + s.shape, s.dtype) for s in gather],
    )(ii, jj, qa, ka, va, *gather)
    return res[0], res[1], res[2], res[3], list(res[4:])


def _pool_window_sum(ext, w, forward):
    n = ext.shape[0]
    sm = ext
    k = 1
    while k < w:
        sm = sm + pltpu.roll(sm, (n - k) if forward else k, axis=0)
        k *= 2
    return sm


def _out_proj_fwd(h, o, pin, w_pool, pscale, w_out):
    T = h.shape[0]
    tm = min(_TM, T)
    hb = tm // HALO

    def body(h_ref, o_ref, pin_ref, halo_ref, wp_ref, ps_ref, wo_ref, h1_ref, mixed_ref, y_ref):
        i = pl.program_id(0)
        pin_t = pin_ref[...]
        halo = jnp.where(i == 0, 0.0, halo_ref[...])
        ext = jnp.concatenate([halo, pin_t], axis=0)
        t = (i * tm + lax.broadcasted_iota(jnp.int32, (tm, 1), 0) + 1).astype(_F32)
        mixed, ys = [], []
        for g, w in enumerate(POOL_WINDOWS):
            sl = slice(g * POOL_GROUP_DIM, (g + 1) * POOL_GROUP_DIM)
            win = _pool_window_sum(ext[:, sl], w, False)[HALO:, :]
            mg = (win / jnp.minimum(t, float(w)) - pin_t[:, sl]).astype(_MXU)
            mixed.append(mg)
            ys.append(_dot(mg, wp_ref[g]))
        mixed_ref[...] = jnp.concatenate(mixed, axis=-1)
        y = (jnp.concatenate(ys, axis=-1) * ps_ref[...]).astype(_MXU)
        y_ref[...] = y
        h1_ref[...] = h_ref[...] + _dot(o_ref[...], wo_ref[:FOX_WIDTH, :]) + _dot(y, wo_ref[FOX_WIDTH:, :])

    return pl.pallas_call(
        body, name="out_proj_fwd", grid=(T // tm,),
        in_specs=[_rows(tm, D_MODEL), _rows(tm, FOX_WIDTH), _rows(tm, POOL_WIDTH),
                  pl.BlockSpec((HALO, POOL_WIDTH), lambda i: (jnp.maximum(i * hb - 1, 0), 0)),
                  _resident(w_pool.shape), _resident((1, POOL_WIDTH)), _resident(w_out.shape)],
        out_specs=[_rows(tm, D_MODEL), _rows(tm, POOL_WIDTH), _rows(tm, POOL_WIDTH)],
        out_shape=[_sds((T, D_MODEL), _F32), _sds((T, POOL_WIDTH), _MXU), _sds((T, POOL_WIDTH), _MXU)],
    )(h, o, pin, pin, w_pool, pscale, w_out)


def _mem_kv_fwd(mem, g_kv, w_kv, gkm, bd128):
    M = mem.shape[0]

    def body(mem_ref, g_ref, w_ref, gk_ref, bd_ref, mn_ref, mkv_ref, mk_ref, mv_ref):
        x = mem_ref[...]
        mn = ((x * _rstd(x)) * g_ref[...]).astype(_MXU)
        mn_ref[...] = mn
        z = _dot(mn, w_ref[...])
        mkv_ref[...] = z
        k = z[:, :MEM_WIDTH]
        rk = lax.rsqrt(_group_sum(k * k, bd_ref[...]) * (1.0 / MEM_HEAD_DIM) + EPS)
        mk_ref[...] = ((k * rk) * gk_ref[...]).astype(_MXU)
        mv_ref[...] = z[:, MEM_WIDTH:].astype(_MXU)

    return pl.pallas_call(
        body, name="mem_kv_fwd",
        out_shape=[_sds((M, D_MODEL), _MXU), _sds((M, 2 * MEM_WIDTH), _F32), _sds((M, MEM_WIDTH), _MXU),
                   _sds((M, MEM_WIDTH), _MXU)],
    )(mem, g_kv, w_kv, gkm, bd128)


def _mem_softmax(qn, mk_ref, hd):
    sl = slice(hd * MEM_HEAD_DIM, (hd + 1) * MEM_HEAD_DIM)
    s = _dot_nt(qn[:, sl], mk_ref[:, sl]) * (MEM_HEAD_DIM ** -0.5)
    e = jnp.exp(s - jnp.max(s, axis=-1, keepdims=True))
    return e / jnp.sum(e, axis=-1, keepdims=True)


def _mem_attn_fwd(h1, g_q, w_q, gqm, bd128, mk, mv, w_mo):
    T = h1.shape[0]
    tm = min(_TM, T)

    def body(h_ref, g_ref, wq_ref, gq_ref, bd_ref, mk_ref, mv_ref, wo_ref, h2_ref, hn_ref, mo_ref):
        x = h_ref[...]
        hn = ((x * _rstd(x)) * g_ref[...]).astype(_MXU)
        hn_ref[...] = hn
        mq = _dot(hn, wq_ref[...])
        rq = lax.rsqrt(_group_sum(mq * mq, bd_ref[...]) * (1.0 / MEM_HEAD_DIM) + EPS)
        qn = ((mq * rq) * gq_ref[...]).astype(_MXU)
        outs = []
        for hd in range(MEM_HEADS):
            p = _mem_softmax(qn, mk_ref, hd).astype(_MXU)
            outs.append(_dot(p, mv_ref[:, hd * MEM_HEAD_DIM:(hd + 1) * MEM_HEAD_DIM]))
        mo = jnp.concatenate(outs, axis=-1).astype(_MXU)
        mo_ref[...] = mo
        h2_ref[...] = x + _dot(mo, wo_ref[...])

    return pl.pallas_call(
        body, name="mem_attn_fwd", grid=(T // tm,),
        in_specs=[_rows(tm, D_MODEL), _resident((1, D_MODEL)), _resident(w_q.shape), _resident((1, MEM_WIDTH)),
                  _resident(bd128.shape), _resident(mk.shape), _resident(mv.shape), _resident(w_mo.shape)],
        out_specs=[_rows(tm, D_MODEL), _rows(tm, D_MODEL), _rows(tm, MEM_WIDTH)],
        out_shape=[_sds((T, D_MODEL), _F32), _sds((T, D_MODEL), _MXU), _sds((T, MEM_WIDTH), _MXU)],
    )(h1, g_q, w_q, gqm, bd128, mk, mv, w_mo)


def _ffn_weight_specs(layer):
    nf = D_FF // _TF
    return [pl.BlockSpec((1, D_MODEL, _TF), lambda i, j: (j, layer, 0)),
            pl.BlockSpec((1, D_MODEL, _TF), lambda i, j: (j + nf, layer, 0)),
            pl.BlockSpec((2, _TF // 2, D_MODEL), lambda i, j: (j, layer, 0))]


def _ffn_fwd(h2, g_ffn, w_gu, w_d, layer):
    T = h2.shape[0]
    tm = min(_TMF, T)
    nf = D_FF // _TF

    def body(h_ref, g_ref, wg_ref, wu_ref, wd_ref, h3_ref, hn_ref, acc, xn_s):
        j = pl.program_id(1)

        @pl.when(j == 0)
        def _():
            x = h_ref[...]
            xn = ((x * _rstd(x)) * g_ref[...]).astype(_MXU)
            xn_s[...] = xn
            hn_ref[...] = xn
            acc[...] = jnp.zeros_like(acc)

        xn = xn_s[...]
        g = _dot(xn, wg_ref[0])
        u = _dot(xn, wu_ref[0])
        a = ((g * jax.nn.sigmoid(g)) * u).astype(_MXU)
        acc[...] += _dot(a, wd_ref[...].reshape(_TF, D_MODEL))

        @pl.when(j == nf - 1)
        def _():
            h3_ref[...] = h_ref[...] + acc[...]

    tok = pl.BlockSpec((tm, D_MODEL), lambda i, j: (i, 0))
    return pl.pallas_call(
        body, name="ffn_fwd", grid=(T // tm, nf),
        in_specs=[tok, pl.BlockSpec((1, D_MODEL), lambda i, j: (0, 0))] + _ffn_weight_specs(layer),
        out_specs=[tok, tok],
        out_shape=[_sds((T, D_MODEL), _F32), _sds((T, D_MODEL), _MXU)],
        scratch_shapes=[pltpu.VMEM((tm, D_MODEL), _F32), pltpu.VMEM((tm, D_MODEL), _MXU)],
    )(h2, g_ffn, w_gu, w_gu, w_d)


def _loss_grad(y, tgt):
    T = y.shape[0]
    tm = min(_TA, T)

    def body(y_ref, t_ref, dy_ref, loss_ref):
        @pl.when(pl.program_id(0) == 0)
        def _():
            loss_ref[...] = jnp.zeros_like(loss_ref)

        err = y_ref[...] - t_ref[...]
        dy_ref[...] = err * (1.0 / D_MODEL)
        part = jnp.sum(jnp.sum(err * err, axis=0, keepdims=True), axis=1, keepdims=True)
        loss_ref[...] += part * (0.5 / D_MODEL)

    return pl.pallas_call(
        body, name="loss_grad", grid=(T // tm,), in_specs=[_rows(tm, D_MODEL), _rows(tm, D_MODEL)],
        out_specs=[_rows(tm, D_MODEL), pl.BlockSpec((1, 1), lambda i: (0, 0))],
        out_shape=[_sds((T, D_MODEL), _F32), _sds((1, 1), _F32)],
    )(y, tgt)


def _pick_tile(n, candidates=(1408, 1024, 512, 256, 128)):
    for c in candidates:
        if n % c == 0:
            return c
    return n


def _matmul_tn(a, b, name, tk=None, tn=None, dst=None, dst_shape=None, dst_index=None, dst_chips=1):
    T, K = a.shape
    N = b.shape[1]
    tk, tn, tt = tk or _pick_tile(K), tn or _pick_tile(N), min(_TT, T)

    def body(*refs):
        a_ref, b_ref, o_ref = refs[0], refs[1], refs[-1]

        @pl.when(pl.program_id(2) == 0)
        def _():
            o_ref[...] = jnp.zeros_like(o_ref)

        o_ref[...] += _dot_tn(a_ref[...].astype(_MXU), b_ref[...].astype(_MXU)).reshape(o_ref.shape)

    in_specs = [pl.BlockSpec((tt, tk), lambda i, j, t: (t, i)), pl.BlockSpec((tt, tn), lambda i, j, t: (t, j))]
    if dst_shape is None:
        out_spec, out_shape, args, alias = pl.BlockSpec((tk, tn), lambda i, j, t: (i, j)), (K, N), (a, b), {}
    else:
        out_spec = pl.BlockSpec((dst_chips, tk // dst_chips, tn), lambda i, j, t: dst_index(i, j))
        out_shape, args, alias = dst_shape, (a, b), {}
        if dst is not None:
            in_specs, args, alias = in_specs + [_ANY], (a, b, dst), {2: 0}
    return pl.pallas_call(
        body, name=name, grid=(K // tk, N // tn, T // tt), in_specs=in_specs, out_specs=out_spec,
        out_shape=_sds(out_shape, _F32), input_output_aliases=alias,
    )(*args)


def _ffn_bwd(dh3, h2, hn, g_ffn, w_gu, w_d, layer):
    T = h2.shape[0]
    tm = min(_TMF, T)
    nf = D_FF // _TF

    def body(dh_ref, h_ref, hn_ref, g_ref, wg_ref, wu_ref, wd_ref, dh2_ref, act_ref, dg_ref, du_ref, dgain_ref,
             acc, dyb):
        i = pl.program_id(0)
        j = pl.program_id(1)

        @pl.when((i == 0) & (j == 0))
        def _():
            dgain_ref[...] = jnp.zeros_like(dgain_ref)

        @pl.when(j == 0)
        def _():
            dyb[...] = dh_ref[...].astype(_MXU)
            acc[...] = jnp.zeros_like(acc)

        xn = hn_ref[...]
        wg = wg_ref[0]
        wu = wu_ref[0]
        g = _dot(xn, wg)
        u = _dot(xn, wu)
        sg = jax.nn.sigmoid(g)
        sl = g * sg
        act_ref[...] = (sl * u).astype(_MXU)
        da = _dot_nt(dyb[...], wd_ref[...].reshape(_TF, D_MODEL))
        dgate = (da * u * (sg * (1.0 + g * (1.0 - sg)))).astype(_MXU)
        dup = (da * sl).astype(_MXU)
        dg_ref[...] = dgate
        du_ref[...] = dup
        acc[...] += _dot_nt(dgate, wg) + _dot_nt(dup, wu)

        @pl.when(j == nf - 1)
        def _():
            dhn = acc[...]
            dx, dgain = _norm_bwd(dhn, h_ref[...], g_ref[...])
            dh2_ref[...] = dh_ref[...] + dx
            dgain_ref[...] += dgain

    tok = pl.BlockSpec((tm, D_MODEL), lambda i, j: (i, 0))
    ffb = pl.BlockSpec((tm, _TF), lambda i, j: (i, j))
    row = pl.BlockSpec((1, D_MODEL), lambda i, j: (0, 0))
    return pl.pallas_call(
        body, name="ffn_bwd", grid=(T // tm, nf),
        in_specs=[tok, tok, tok, row] + _ffn_weight_specs(layer),
        out_specs=[tok, ffb, ffb, ffb, row],
        out_shape=[_sds((T, D_MODEL), _F32), _sds((T, D_FF), _MXU), _sds((T, D_FF), _MXU), _sds((T, D_FF), _MXU),
                   _sds((1, D_MODEL), _F32)],
        scratch_shapes=[pltpu.VMEM((tm, D_MODEL), _F32), pltpu.VMEM((tm, D_MODEL), _MXU)],
    )(dh3, h2, hn, g_ffn, w_gu, w_gu, w_d)


def _mem_attn_bwd(dh2, h1, g_q, w_q, gqm, bd128, mk, mv, w_mo):
    T = h1.shape[0]
    M = mk.shape[0]
    tm = min(_TM, T)
    nt = T // tm

    def body(dh_ref, h_ref, g_ref, wq_ref, gq_ref, bd_ref, mk_ref, mv_ref, wo_ref,
             dh1_ref, dmq_ref, dmk_ref, dmv_ref, dgq_ref, dgain_ref, dgq_acc):
        i = pl.program_id(0)

        @pl.when(i == 0)
        def _():
            dmk_ref[...] = jnp.zeros_like(dmk_ref)
            dmv_ref[...] = jnp.zeros_like(dmv_ref)
            dgain_ref[...] = jnp.zeros_like(dgain_ref)
            dgq_acc[...] = jnp.zeros_like(dgq_acc)

        x = h_ref[...]
        g = g_ref[...]
        bd = bd_ref[...]
        hn = ((x * _rstd(x)) * g).astype(_MXU)
        mq = _dot(hn, wq_ref[...])
        rq = lax.rsqrt(_group_sum(mq * mq, bd) * (1.0 / MEM_HEAD_DIM) + EPS)
        qn = ((mq * rq) * gq_ref[...]).astype(_MXU)
        dmo = _dot_nt(dh_ref[...].astype(_MXU), wo_ref[...])
        dqn = []
        for hd in range(MEM_HEADS):
            sl = slice(hd * MEM_HEAD_DIM, (hd + 1) * MEM_HEAD_DIM)
            p = _mem_softmax(qn, mk_ref, hd)
            dmo_h = dmo[:, sl].astype(_MXU)
            dp = _dot_nt(dmo_h, mv_ref[:, sl])
            ds = (p * (dp - jnp.sum(p * dp, axis=-1, keepdims=True)) * (MEM_HEAD_DIM ** -0.5)).astype(_MXU)
            dqn.append(_dot(ds, mk_ref[:, sl]))
            dmk_ref[:, sl] += _dot_tn(ds, qn[:, sl])
            dmv_ref[:, sl] += _dot_tn(p.astype(_MXU), dmo_h)
        dqn = jnp.concatenate(dqn, axis=-1)
        dmq, dgq = _headnorm_bwd(dqn, mq, gq_ref[...], bd, MEM_HEAD_DIM)
        dgq_acc[...] += dgq
        dmq = dmq.astype(_MXU)
        dmq_ref[...] = dmq
        dhn = _dot_nt(dmq, wq_ref[...])
        dx, dgain = _norm_bwd(dhn, x, g)
        dh1_ref[...] = dh_ref[...] + dx
        dgain_ref[...] += dgain

        @pl.when(i == nt - 1)
        def _():
            dgq_ref[...] = _fold_heads(dgq_acc[...], MEM_HEADS, MEM_HEAD_DIM)

    const2 = lambda shape: pl.BlockSpec(shape, lambda i: (0, 0))
    return pl.pallas_call(
        body, name="mem_attn_bwd", grid=(nt,),
        in_specs=[_rows(tm, D_MODEL), _rows(tm, D_MODEL), _resident((1, D_MODEL)), _resident(w_q.shape),
                  _resident((1, MEM_WIDTH)), _resident(bd128.shape), _resident(mk.shape), _resident(mv.shape),
                  _resident(w_mo.shape)],
        out_specs=[_rows(tm, D_MODEL), _rows(tm, MEM_WIDTH), const2((M, MEM_WIDTH)), const2((M, MEM_WIDTH)),
                   const2((1, MEM_HEAD_DIM)), const2((1, D_MODEL))],
        out_shape=[_sds((T, D_MODEL), _F32), _sds((T, MEM_WIDTH), _MXU), _sds((M, MEM_WIDTH), _F32),
                   _sds((M, MEM_WIDTH), _F32), _sds((1, MEM_HEAD_DIM), _F32), _sds((1, D_MODEL), _F32)],
        scratch_shapes=[pltpu.VMEM((1, MEM_WIDTH), _F32)],
    )(dh2, h1, g_q, w_q, gqm, bd128, mk, mv, w_mo)


def _mem_kv_bwd(dmk, dmv, mkv, mn, mem, g_kv, gkm, bd128, w_kv, dst, dst_shape, block):
    rows = D_MODEL // N_CHIPS

    def body(dmk_ref, dmv_ref, mkv_ref, mn_ref, mem_ref, g_ref, gk_ref, bd_ref, w_ref, *rest):
        dw_ref, dgk_ref, dgain_ref = rest[-3:]
        kraw = mkv_ref[:, :MEM_WIDTH]
        dk, dgk = _headnorm_bwd(dmk_ref[...], kraw, gk_ref[...], bd_ref[...], MEM_HEAD_DIM)
        dgk_ref[...] = _fold_heads(dgk, MEM_HEADS, MEM_HEAD_DIM)
        dmkv = jnp.concatenate([dk, dmv_ref[...]], axis=-1).astype(_MXU)
        dw_ref[...] = _dot_tn(mn_ref[...], dmkv).reshape(N_CHIPS, rows, 2 * MEM_WIDTH)
        dmn = _dot_nt(dmkv, w_ref[...])
        _, dgain = _norm_bwd(dmn, mem_ref[...], g_ref[...])
        dgain_ref[...] = dgain

    args = (dmk, dmv, mkv, mn, mem, g_kv, gkm, bd128, w_kv)
    whole = lambda a: pl.BlockSpec(a.shape, lambda i: (0,) * a.ndim)
    in_specs, alias = [whole(a) for a in args], {}
    if dst is not None:
        in_specs, args, alias = in_specs + [_ANY], args + (dst,), {len(args): 0}
    return pl.pallas_call(
        body, name="mem_kv_bwd", grid=(1,), in_specs=in_specs,
        out_specs=[pl.BlockSpec((N_CHIPS, rows, 2 * MEM_WIDTH), lambda i: (0, block, 0)),
                   pl.BlockSpec((1, MEM_HEAD_DIM), lambda i: (0, 0)), pl.BlockSpec((1, D_MODEL), lambda i: (0, 0))],
        out_shape=[_sds(dst_shape, _F32), _sds((1, MEM_HEAD_DIM), _F32), _sds((1, D_MODEL), _F32)],
        input_output_aliases=alias,
    )(*args)


def _out_proj_bwd(dh1, mixed, o, bd64, w_pool, pscale, w_out):
    T = dh1.shape[0]
    tm = min(_TM, T)
    hb = tm // HALO
    nt = T // tm

    def body(dh_ref, halo_ref, mx_ref, o_ref, bd_ref, wp_ref, ps_ref, wo_ref, doa_ref, dpin_ref, dwp_ref, dps_ref):
        i = pl.program_id(0)

        @pl.when(i == 0)
        def _():
            dwp_ref[...] = jnp.zeros_like(dwp_ref)
            dps_ref[...] = jnp.zeros_like(dps_ref)

        dcat = _dot_nt(dh_ref[...].astype(_MXU), wo_ref[...])
        do = dcat[:, :FOX_WIDTH].astype(_MXU)
        delta = _group_sum(do.astype(_F32) * o_ref[...].astype(_F32), bd_ref[...])
        for h in range(FOX_HEADS):
            sl = slice(h * FOX_HEAD_DIM, (h + 1) * FOX_HEAD_DIM)
            doa_ref[:, h * LANES:(h + 1) * LANES] = _aug_head(
                do[:, sl], _split3(-delta[:, h * FOX_HEAD_DIM:h * FOX_HEAD_DIM + 1]), _ZEROS3)
        dy = dcat[:, FOX_WIDTH:]
        dyh = _dot_nt(halo_ref[...].astype(_MXU), wo_ref[FOX_WIDTH:, :])
        dyh = jnp.where(i == nt - 1, 0.0, dyh)
        ps = ps_ref[...]
        t = (i * tm + lax.broadcasted_iota(jnp.int32, (tm + HALO, 1), 0) + 1).astype(_F32)
        mixed_t = mx_ref[...]
        dpin, dps = [], []
        for g, w in enumerate(POOL_WINDOWS):
            sl = slice(g * POOL_GROUP_DIM, (g + 1) * POOL_GROUP_DIM)
            mg = mixed_t[:, sl]
            wg = wp_ref[g]
            dps.append(jnp.sum(dy[:, sl] * _dot(mg, wg), axis=0, keepdims=True))
            dyl = (dy[:, sl] * ps[:, sl]).astype(_MXU)
            dylh = (dyh[:, sl] * ps[:, sl]).astype(_MXU)
            dwp_ref[g] += _dot_tn(mg, dyl)
            dmx = _dot_nt(dyl, wg)
            ext = jnp.concatenate([dmx, _dot_nt(dylh, wg)], axis=0) / jnp.minimum(t, float(w))
            dpin.append(_pool_window_sum(ext, w, True)[:tm, :] - dmx)
        dpin_ref[...] = jnp.concatenate(dpin, axis=-1)
        dps_ref[...] += jnp.concatenate(dps, axis=-1)

    return pl.pallas_call(
        body, name="out_proj_bwd", grid=(nt,),
        in_specs=[_rows(tm, D_MODEL),
                  pl.BlockSpec((HALO, D_MODEL), lambda i: (jnp.minimum((i + 1) * hb, T // HALO - 1), 0)),
                  _rows(tm, POOL_WIDTH), _rows(tm, FOX_WIDTH), _resident(bd64.shape), _resident(w_pool.shape),
                  _resident((1, POOL_WIDTH)), _resident(w_out.shape)],
        out_specs=[_rows(tm, FOX_HEADS * LANES), _rows(tm, POOL_WIDTH),
                   pl.BlockSpec(w_pool.shape, lambda i: (0, 0, 0)), pl.BlockSpec((1, POOL_WIDTH), lambda i: (0, 0))],
        out_shape=[_sds((T, FOX_HEADS * LANES), _MXU), _sds((T, POOL_WIDTH), _F32), _sds(w_pool.shape, _F32),
                   _sds((1, POOL_WIDTH), _F32)],
    )(dh1, dh1, mixed, o, bd64, w_pool, pscale, w_out)


def _scatter_behind(pid, t, n_pairs, steps, srcs, gots, send_sems, recv_sems):
    x, y, c = _mesh_pos()
    me = 2 * x + y
    chips = _other_chips(x, y)
    every = [(a, j) for a in range(len(srcs)) for j in range(3)]

    def send(a, j):
        cx, cy = chips[j]
        return _remote(srcs[a].at[2 * cx + cy], gots[a].at[me], send_sems, recv_sems, 3 * a + j, (cx, cy, c))

    def arrival(a, j):
        cx, cy = chips[j]
        slab = gots[a].at[2 * cx + cy]
        return _remote(slab, slab, send_sems, recv_sems, 3 * a + j, (cx, cy, c))

    @pl.when((pid == 0) & (t == 0))
    def _():
        for a, j in every:
            send(a, j).start()

    @pl.when((pid == n_pairs - 1) & (t == steps - 1))
    def _():
        for a, j in every:
            arrival(a, j).wait_recv()
        for a, j in every:
            send(a, j).wait_send()


def _fox_bwd(qa, ka, va, doa, pt, mb, lse, scatter=()):
    T = qa.shape[0]
    tq = min(_TQ, T)
    nq = T // tq
    pair = 2 * FOX_HEAD_DIM
    ii, jj = _causal_steps(nq, False)
    fwd_step = ii * (ii + 1) // 2 + jj
    row_lane, col_lane = FOX_HEAD_DIM, FOX_HEAD_DIM + 3
    n = len(scatter)
    n_pairs = FOX_HEADS // 2
    steps = int(ii.shape[0])

    def body(ii_ref, jj_ref, fs_ref, q_ref, k_ref, v_ref, do_ref, p_ref, mb_ref, lse_ref, *refs):
        srcs, (dq_ref, dk_ref, dv_ref, drs_ref, dcs_ref), gots = refs[:n], refs[n:n + 5], refs[n + 5:2 * n + 5]
        dk_acc, dv_acc = refs[2 * n + 5:2 * n + 7]
        t = pl.program_id(1)
        i = ii_ref[t]
        j = jj_ref[t]
        if n:
            _scatter_behind(pl.program_id(0), t, n_pairs, steps, srcs, gots, *refs[2 * n + 7:])

        @pl.when(t == 0)
        def _():
            dq_ref[...] = jnp.zeros_like(dq_ref)
            drs_ref[...] = jnp.zeros_like(drs_ref)

        @pl.when(i == j)
        def _():
            dk_acc[...] = jnp.zeros_like(dk_acc)
            dv_acc[...] = jnp.zeros_like(dv_acc)

        r = jnp.exp(mb_ref[0, 0] - lse_ref[0])
        dqs, rs = [], []
        for hh in range(2):
            sl = slice(LANES * hh, LANES * (hh + 1))
            dot_t = (do_ref[:, sl].astype(_F32).T * r[hh:hh + 1, :]).astype(_MXU)
            pt_h = p_ref[0, 0, hh]
            dv_acc[hh] += _dot_nt(pt_h, dot_t)
            dst = (pt_h.astype(_F32) * _dot(v_ref[:, sl], dot_t)).astype(_MXU)
            dk_acc[hh] += _dot(dst, q_ref[:, sl])
            dqa = _dot(k_ref[:, sl].astype(_F32).T.astype(_MXU), dst).T
            dqs.append(dqa[:, :FOX_HEAD_DIM] * (FOX_HEAD_DIM ** -0.5))
            rs.append(dqa[:, row_lane:row_lane + 1])
        rows = pl.ds(pl.multiple_of(i * tq, tq), tq)
        dq_ref[rows, :] += jnp.concatenate(dqs, axis=-1)
        drs_ref[0, rows, :] += _two_lanes(rs[0], rs[1])

        @pl.when(i == nq - 1)
        def _():
            dk = [dk_acc[hh] for hh in range(2)]
            dv = [dv_acc[hh] for hh in range(2)]
            dk_ref[...] = jnp.concatenate([dk[0][:, :FOX_HEAD_DIM], dk[1][:, :FOX_HEAD_DIM]], axis=-1)
            dv_ref[...] = jnp.concatenate([dv[0][:, :FOX_HEAD_DIM], dv[1][:, :FOX_HEAD_DIM]], axis=-1)
            dcs_ref[0] = _two_lanes(dk[0][:, col_lane:col_lane + 1], dk[1][:, col_lane:col_lane + 1])

    qspec = pl.BlockSpec((tq, 2 * LANES), lambda p, t, ii, jj, fs: (ii[t], p))
    kspec = pl.BlockSpec((tq, 2 * LANES), lambda p, t, ii, jj, fs: (jj[t], p))
    kout = pl.BlockSpec((tq, pair), lambda p, t, ii, jj, fs: (jj[t], p))
    sems = [pltpu.SemaphoreType.DMA((3 * n,)), pltpu.SemaphoreType.DMA((3 * n,))] if n else []
    res = pl.pallas_call(
        body, name="fox_bwd_scatter" if n else "fox_bwd",
        grid_spec=pltpu.PrefetchScalarGridSpec(
            num_scalar_prefetch=3, grid=(n_pairs, steps),
            in_specs=[qspec, kspec, kspec, qspec,
                      pl.BlockSpec((1, 1, 2, tq, tq), lambda p, t, ii, jj, fs: (p, fs[t], 0, 0, 0)),
                      pl.BlockSpec((1, 1, 8, tq), lambda p, t, ii, jj, fs: (p, fs[t], 0, 0)),
                      pl.BlockSpec((1, 8, tq), lambda p, t, ii, jj, fs: (p, 0, ii[t]))] + [_ANY] * n,
            out_specs=[pl.BlockSpec((T, pair), lambda p, t, ii, jj, fs: (0, p)), kout, kout,
                       pl.BlockSpec((1, T, LANES), lambda p, t, ii, jj, fs: (p, 0, 0)),
                       pl.BlockSpec((1, tq, LANES), lambda p, t, ii, jj, fs: (p, jj[t], 0))] + [_ANY] * n,
            scratch_shapes=[pltpu.VMEM((2, tq, LANES), _F32), pltpu.VMEM((2, tq, LANES), _F32)] + sems),
        out_shape=[_sds((T, FOX_WIDTH), _F32), _sds((T, FOX_WIDTH), _F32), _sds((T, FOX_WIDTH), _F32),
                   _sds((n_pairs, T, LANES), _F32), _sds((n_pairs, T, LANES), _F32)]
        + [_sds(p.shape, p.dtype) for p in scatter],
    )(ii, jj, fwd_step, qa, ka, va, doa, pt, mb, lse, *scatter)
    return res[0], res[1], res[2], res[3], res[4], list(res[5:])


def _mix_in_bwd(dh1, h, g_mix, zqk, dq, dk, dv, dpin, df, gq, gk, bd64, w_main, w_f):
    T = h.shape[0]
    tm = min(_TM, T)
    nt = T // tm

    def body(dh1_ref, h_ref, g_ref, zqk_ref, dq_ref, dk_ref, dv_ref, dpin_ref, df_ref, gq_ref, gk_ref, bd_ref,
             wm_ref, wf_ref, dh_ref, dz_ref, dzf_ref, dgq_ref, dgk_ref, dgain_ref, dbf_ref, dgq_acc, dgk_acc):
        i = pl.program_id(0)

        @pl.when(i == 0)
        def _():
            dgain_ref[...] = jnp.zeros_like(dgain_ref)
            dbf_ref[...] = jnp.zeros_like(dbf_ref)
            dgq_acc[...] = jnp.zeros_like(dgq_acc)
            dgk_acc[...] = jnp.zeros_like(dgk_acc)

        bd = bd_ref[...]
        dqr, dgq = _headnorm_bwd(dq_ref[...], zqk_ref[:, :FOX_WIDTH], gq_ref[...], bd, FOX_HEAD_DIM)
        dkr, dgk = _headnorm_bwd(dk_ref[...], zqk_ref[:, FOX_WIDTH:], gk_ref[...], bd, FOX_HEAD_DIM)
        dgq_acc[...] += dgq
        dgk_acc[...] += dgk
        dz = jnp.concatenate([dqr, dkr, dv_ref[...], dpin_ref[...]], axis=-1).astype(_MXU)
        dz_ref[...] = dz
        df = df_ref[...]
        dzf = df.astype(_MXU)
        dzf_ref[...] = dzf
        dbf_ref[...] += jnp.sum(df, axis=0, keepdims=True)
        dxn = _dot_nt(dz, wm_ref[...]) + _dot_nt(dzf, wf_ref[...])
        dx, dgain = _norm_bwd(dxn, h_ref[...], g_ref[...])
        dh_ref[...] = dh1_ref[...] + dx
        dgain_ref[...] += dgain

        @pl.when(i == nt - 1)
        def _():
            dgq_ref[...] = _fold_heads(dgq_acc[...], FOX_HEADS, FOX_HEAD_DIM)
            dgk_ref[...] = _fold_heads(dgk_acc[...], FOX_HEADS, FOX_HEAD_DIM)

    const2 = lambda shape: pl.BlockSpec(shape, lambda i: (0, 0))
    return pl.pallas_call(
        body, name="mix_in_bwd", grid=(nt,),
        in_specs=[_rows(tm, D_MODEL), _rows(tm, D_MODEL), _resident((1, D_MODEL)), _rows(tm, 2 * FOX_WIDTH),
                  _rows(tm, FOX_WIDTH), _rows(tm, FOX_WIDTH), _rows(tm, FOX_WIDTH), _rows(tm, POOL_WIDTH),
                  _rows(tm, LANES), _resident((1, FOX_WIDTH)), _resident((1, FOX_WIDTH)), _resident(bd64.shape),
                  _resident(w_main.shape), _resident(w_f.shape)],
        out_specs=[_rows(tm, D_MODEL), _rows(tm, 4 * FOX_WIDTH), _rows(tm, LANES), const2((1, FOX_HEAD_DIM)),
                   const2((1, FOX_HEAD_DIM)), const2((1, D_MODEL)), const2((1, LANES))],
        out_shape=[_sds((T, D_MODEL), _F32), _sds((T, 4 * FOX_WIDTH), _MXU), _sds((T, LANES), _MXU),
                   _sds((1, FOX_HEAD_DIM), _F32), _sds((1, FOX_HEAD_DIM), _F32), _sds((1, D_MODEL), _F32),
                   _sds((1, LANES), _F32)],
        scratch_shapes=[pltpu.VMEM((1, FOX_WIDTH), _F32), pltpu.VMEM((1, FOX_WIDTH), _F32)],
    )(dh1, h, g_mix, zqk, dq, dk, dv, dpin, df, gq, gk, bd64, w_main, w_f)


CLASSES = (("a", D_MODEL, ("w_out", "w_mem_kv")), ("d", D_MODEL, ("w_down",)), ("g", _TF, ("w_gate_up",)),
           ("i", 514, ("w_in",)), ("q", MEM_WIDTH, ("w_mem_q",)), ("o", 256, ("w_mem_out",)))
W_OUT_ROWS = D_MODEL // N_CHIPS
W_DOWN_ROWS = D_FF // N_CHIPS
W_MEM_OUT_COLS = D_MODEL // N_CHIPS


def _late_params(WC, l):
    rows = lambda buf, first, n: buf[:, first:first + n]
    return dict(
        w_out=rows(WC["a"], l * W_OUT_ROWS, W_OUT_ROWS).reshape(D_MODEL, D_MODEL),
        w_mem_q=rows(WC["q"], l * W_OUT_ROWS, W_OUT_ROWS).reshape(D_MODEL, MEM_WIDTH),
        w_mem_kv=rows(WC["a"], (DEPTH + l) * W_OUT_ROWS, W_OUT_ROWS).reshape(D_MODEL, 2 * MEM_WIDTH),
        w_mem_out=rows(WC["o"], l * MEM_WIDTH, MEM_WIDTH).transpose(1, 0, 2).reshape(MEM_WIDTH, D_MODEL),
        w_gu=WC["g"],
        w_d=WC["d"],
    )


def _layer_params(WC, WS, l):
    rows = lambda buf, first, n: buf[:, first:first + n]
    w_in = rows(WC["i"], l * D_MODEL, D_MODEL).transpose(1, 0, 2).reshape(D_MODEL, -1)
    n_main = 3 * FOX_WIDTH + POOL_WIDTH
    row = lambda a: a.reshape(1, -1).astype(_F32)
    return dict(
        layer=l,
        g_mix=row(WS["g_mix"][l]),
        w_main=w_in[:, :n_main],
        w_f=jnp.pad(w_in[:, n_main:], ((0, 0), (0, LANES - FOX_HEADS))),
        b_f=jnp.pad(row(WS["b_forget"][l]), ((0, 0), (0, LANES - FOX_HEADS))),
        gq=jnp.tile(row(WS["g_q_fox"][l]), (1, FOX_HEADS)),
        gk=jnp.tile(row(WS["g_k_fox"][l]), (1, FOX_HEADS)),
        w_pool=WS["w_pool"][l].astype(_MXU),
        pscale=row(WS["pool_scale"][l]),
        g_mem_q=row(WS["g_mem_q"][l]),
        g_mem_kv=row(WS["g_mem_kv"][l]),
        gqm=jnp.tile(row(WS["g_q_mem"][l]), (1, MEM_HEADS)),
        gkm=jnp.tile(row(WS["g_k_mem"][l]), (1, MEM_HEADS)),
        g_ffn=row(WS["g_ffn"][l]),
    )


def _layer_fwd(h, mem, P, bd64, bd128, pending=None):
    s = dict(h=h)
    s["xn"], s["zqk"], qn, kn, v, pin, s["fl"] = _mix_in_fwd(
        h, P["g_mix"], P["w_main"], P["w_f"], P["b_f"], P["gq"], P["gk"], bd64)
    s["qa"], s["ka"], s["va"] = _gate_fwd(s["fl"], qn, kn, v)
    s["o"], s["pt"], s["mb"], s["lse"], landed = _fox_fwd(s["qa"], s["ka"], s["va"], pending[0] if pending else ())
    if pending:
        pending[1](landed)
    s["h1"], s["mixed"], s["y"] = _out_proj_fwd(h, s["o"], pin, P["w_pool"], P["pscale"], P["w_out"])
    s["mn"], s["mkv"], s["mk"], s["mv"] = _mem_kv_fwd(mem, P["g_mem_kv"], P["w_mem_kv"], P["gkm"], bd128)
    s["h2"], s["hn_mem"], s["mo"] = _mem_attn_fwd(s["h1"], P["g_mem_q"], P["w_mem_q"], P["gqm"], bd128, s["mk"],
                                                  s["mv"], P["w_mem_out"])
    h3, s["hn_ffn"] = _ffn_fwd(s["h2"], P["g_ffn"], P["w_gu"], P["w_d"], P["layer"])
    return h3, s


def _layer_bwd(dh3, mem, P, s, bd64, bd128, G, shapes, early=None):
    l = P["layer"]
    g = {}

    def into(c, a, b, name, tk, tn, index, chips=1):
        G[c] = _matmul_tn(a, b, name, tk=tk, tn=tn, dst=G.get(c), dst_shape=shapes[c], dst_index=index,
                          dst_chips=chips)

    dh2, act, dgate, dup, g["g_ffn"] = _ffn_bwd(dh3, s["h2"], s["hn_ffn"], P["g_ffn"], P["w_gu"], P["w_d"], l)
    into("d", act, dh3, "dw_down", 2 * W_DOWN_ROWS, D_MODEL, lambda i, j: (i, l, 0), chips=2)
    into("g", s["hn_ffn"], dgate, "dw_gate", D_MODEL, _TF, lambda i, j: (j, l, 0))
    into("g", s["hn_ffn"], dup, "dw_up", D_MODEL, _TF, lambda i, j: (j + D_FF // _TF, l, 0))

    dh1, dmq, dmk, dmv, g["g_q_mem"], g["g_mem_q"] = _mem_attn_bwd(
        dh2, s["h1"], P["g_mem_q"], P["w_mem_q"], P["gqm"], bd128, s["mk"], s["mv"], P["w_mem_out"])
    into("o", s["mo"], dh2, "dw_mem_out", MEM_WIDTH, W_MEM_OUT_COLS, lambda i, j: (j, l, 0))
    into("q", s["hn_mem"], dmq, "dw_mem_q", W_OUT_ROWS, MEM_WIDTH, lambda i, j: (i, l, 0))
    G["a"], g["g_k_mem"], g["g_mem_kv"] = _mem_kv_bwd(dmk, dmv, s["mkv"], s["mn"], mem, P["g_mem_kv"], P["gkm"],
                                                      bd128, P["w_mem_kv"], G.get("a"), shapes["a"], DEPTH + l)

    doa, dpin, g["w_pool"], g["pool_scale"] = _out_proj_bwd(dh1, s["mixed"], s["o"], bd64, P["w_pool"], P["pscale"],
                                                            P["w_out"])
    half = FOX_WIDTH // W_OUT_ROWS
    into("a", s["o"], dh1, "dw_out_fox", W_OUT_ROWS, D_MODEL, lambda i, j: (i, l, 0))
    into("a", s["y"], dh1, "dw_out_pool", W_OUT_ROWS, D_MODEL, lambda i, j: (i + half, l, 0))
    sent = early(G) if early else ()
    dq, dk, dv, drs, dcs, landed = _fox_bwd(s["qa"], s["ka"], s["va"], doa, s["pt"], s["mb"], s["lse"], sent)
    df = _gate_bwd(drs, dcs, s["fl"])
    dh, dz, dzf, g["g_q_fox"], g["g_k_fox"], g["g_mix"], dbf = _mix_in_bwd(
        dh1, s["h"], P["g_mix"], s["zqk"], dq, dk, dv, dpin, df, P["gq"], P["gk"], bd64, P["w_main"], P["w_f"])
    g["b_forget"] = dbf[:, :FOX_HEADS]
    dw_in = jnp.concatenate([_matmul_tn(s["xn"], dz, "dw_in_main"),
                             _matmul_tn(s["xn"], dzf, "dw_in_gate")[:, :FOX_HEADS]], axis=1)
    return dh, G, g, _to_shards(dw_in, 1), (sent, landed)


def _device_grads(x, mem, tgt, WC, WS, pending=None, early=None):
    bd64 = _blockdiag_ones(FOX_HEADS, FOX_HEAD_DIM)
    bd128 = _blockdiag_ones(MEM_HEADS, MEM_HEAD_DIM)
    params = [_layer_params(WC, WS, l) for l in range(DEPTH)]
    shapes = {c: WC[c].shape for c in WC}

    def complete(buffers):
        for l in range(DEPTH):
            params[l].update(_late_params(buffers, l))

    first = None
    if pending is None:
        complete(WC)
    else:
        slabs, place = pending
        late = list(slabs)
        shapes.update({c: (N_CHIPS,) + slabs[c].shape for c in late})
        first = ([slabs[c] for c in late],
                 lambda landed: complete({c: place(buf, slabs[c]) for c, buf in zip(late, landed)}))
    h, saved = x, []
    for l in range(DEPTH):
        h, s = _layer_fwd(h, mem, params[l], bd64, bd128, first if l == 0 else None)
        saved.append(s)
    dh, loss = _loss_grad(h, tgt)
    G, small, dw_in = {}, [None] * DEPTH, [None] * DEPTH
    for l in reversed(range(DEPTH)):
        dh, G, small[l], dw_in[l], scattered = _layer_bwd(dh, mem, params[l], saved[l], bd64, bd128, G, shapes,
                                                          early if l == 0 else None)
    G["i"] = jnp.concatenate(dw_in, axis=1)
    gsmall = {n: jnp.stack([small[l][n].reshape(WS[n].shape[1:]) for l in range(DEPTH)]) for n in SMALL}
    return loss, dh, G, gsmall, scattered


def _class_slabs(shards):
    return {c: jnp.concatenate([shards[n].reshape(-1, width) for n in names]) for c, width, names in CLASSES}


def _class_rows(shards):
    where = {}
    for c, width, names in CLASSES:
        off = 0
        for n in names:
            rows = shards[n].shape[0] * shards[n].shape[1]
            where[n] = (c, off, rows)
            off += rows
    return where


def _to_shards(full, ax):
    shp = full.shape
    parts = full.reshape(shp[:ax] + (N_CHIPS, shp[ax] // N_CHIPS) + shp[ax + 1:])
    return jnp.moveaxis(parts, ax, 0)


def _from_shards(parts, ax):
    full = jnp.moveaxis(parts, 0, ax)
    shp = full.shape
    return full.reshape(shp[:ax] + (shp[ax] * shp[ax + 1],) + shp[ax + 2:])


def _pack_small(arrs):
    flat = jnp.concatenate([a.reshape(-1).astype(_F32) for a in arrs])
    rows = -(-flat.shape[0] // (8 * D_MODEL)) * 8
    return jnp.pad(flat, (0, rows * D_MODEL - flat.shape[0])).reshape(rows, D_MODEL)


def _unpack_small(flat, shapes):
    flat = flat.reshape(-1)
    out, off = [], 0
    for shp in shapes:
        n = 1
        for d in shp:
            n *= d
        out.append(flat[off:off + n].reshape(shp))
        off += n
    return out


def _mesh_pos():
    return lax.axis_index("x"), lax.axis_index("y"), lax.axis_index("c")


def _other_chips(x, y):
    return [(1 - x, y), (x, 1 - y), (1 - x, 1 - y)]


def _half_rows(ref_rows, which):
    half = ref_rows // 2
    return pl.ds(pl.multiple_of(which * half, 16), half)


def _remote(src_ref, dst_ref, send_sems, recv_sems, k, to):
    return pltpu.make_async_remote_copy(src_ref=src_ref, dst_ref=dst_ref, send_sem=send_sems.at[k],
                                        recv_sem=recv_sems.at[k], device_id=to, device_id_type=MESH)


def _allgather_weights(slabs):
    n = len(slabs)

    def body(*refs):
        srcs, outs, send_sems, recv_sems = refs[:n], refs[n:2 * n], refs[2 * n], refs[2 * n + 1]
        x, y, c = _mesh_pos()
        me = 2 * x + y
        sibling = (x, y, 1 - c)
        chips = _other_chips(x, y)
        first, passed = [], []
        for a, (src, out) in enumerate(zip(srcs, outs)):
            mine = _half_rows(src.shape[0], c)
            for j, chip in enumerate(chips):
                cp = _remote(src.at[mine], out.at[me, mine], send_sems, recv_sems, 6 * a + j, (*chip, c))
                cp.start()
                first.append(cp)
        for a, out in enumerate(outs):
            mine = _half_rows(out.shape[1], c)
            for j, (cx, cy) in enumerate(chips):
                slab = out.at[2 * cx + cy, mine]
                _remote(slab, slab, send_sems, recv_sems, 6 * a + j, (cx, cy, c)).wait_recv()
                fwd = _remote(slab, slab, send_sems, recv_sems, 6 * a + 3 + j, sibling)
                fwd.start()
                passed.append(fwd)
        for a, out in enumerate(outs):
            theirs = _half_rows(out.shape[1], 1 - c)
            for j, (cx, cy) in enumerate(chips):
                slab = out.at[2 * cx + cy, theirs]
                _remote(slab, slab, send_sems, recv_sems, 6 * a + 3 + j, sibling).wait_recv()
        for cp in first + passed:
            cp.wait_send()

    return pl.pallas_call(
        body, name="allgather_weights", in_specs=[_ANY] * n, out_specs=[_ANY] * n,
        out_shape=[_sds((N_CHIPS,) + s.shape, s.dtype) for s in slabs],
        scratch_shapes=[pltpu.SemaphoreType.DMA((6 * n,)), pltpu.SemaphoreType.DMA((6 * n,))],
    )(*slabs)


def _exchange_halves(grads, name):
    n = len(grads)

    def body(*refs):
        srcs, gots, send_sems, recv_sems = refs[:n], refs[n:2 * n], refs[2 * n], refs[2 * n + 1]
        x, y, c = _mesh_pos()
        copies = []
        for a, (src, got) in enumerate(zip(srcs, gots)):
            theirs = _half_rows(src.shape[1], 1 - c)
            for k in range(N_CHIPS):
                cp = _remote(src.at[k, theirs], got.at[k], send_sems, recv_sems, N_CHIPS * a + k, (x, y, 1 - c))
                cp.start()
                copies.append(cp)
        for cp in copies:
            cp.wait()

    return pl.pallas_call(
        body, name=name, in_specs=[_ANY] * n, out_specs=[_ANY] * n,
        out_shape=[_sds((N_CHIPS, g.shape[1] // 2, g.shape[2]), _F32) for g in grads],
        scratch_shapes=[pltpu.SemaphoreType.DMA((N_CHIPS * n,)), pltpu.SemaphoreType.DMA((N_CHIPS * n,))],
    )(*grads)


def _row_tile(rows):
    return _pick_tile(rows, (704, 512, 256))


def _add_halves(g, got, c_idx, name):
    _, half, width = got.shape
    ta = _row_tile(half)
    nb = half // ta

    def body(c_ref, a_ref, b_ref, o_ref):
        o_ref[...] = (a_ref[...] + b_ref[...]).astype(jnp.bfloat16)

    return pl.pallas_call(
        body, name=name,
        grid_spec=pltpu.PrefetchScalarGridSpec(
            num_scalar_prefetch=1, grid=(N_CHIPS, nb),
            in_specs=[pl.BlockSpec((1, ta, width), lambda k, i, c: (k, c[0] * nb + i, 0)),
                      pl.BlockSpec((1, ta, width), lambda k, i, c: (k, i, 0))],
            out_specs=pl.BlockSpec((1, ta, width), lambda k, i, c: (k, i, 0))),
        out_shape=_sds(got.shape, jnp.bfloat16),
    )(c_idx, g, got)


def _scatter_to_chips(parts):
    n = len(parts)

    def body(*refs):
        srcs, gots, send_sems, recv_sems = refs[:n], refs[n:2 * n], refs[2 * n], refs[2 * n + 1]
        x, y, c = _mesh_pos()
        me = 2 * x + y
        chips = _other_chips(x, y)
        copies = []
        for a, (src, got) in enumerate(zip(srcs, gots)):
            for j, (cx, cy) in enumerate(chips):
                cp = _remote(src.at[2 * cx + cy], got.at[me], send_sems, recv_sems, 3 * a + j, (cx, cy, c))
                cp.start()
                copies.append(cp)
        for a, got in enumerate(gots):
            for j, (cx, cy) in enumerate(chips):
                slab = got.at[2 * cx + cy]
                _remote(slab, slab, send_sems, recv_sems, 3 * a + j, (cx, cy, c)).wait_recv()
        for cp in copies:
            cp.wait_send()

    return pl.pallas_call(
        body, name="grad_scatter_chips", in_specs=[_ANY] * n, out_specs=[_ANY] * n,
        out_shape=[_sds(p.shape, p.dtype) for p in parts],
        scratch_shapes=[pltpu.SemaphoreType.DMA((3 * n,)), pltpu.SemaphoreType.DMA((3 * n,))],
    )(*parts)


def _sum_chips(got, c_idx, name):
    _, half, width = got.shape
    ta = _row_tile(half)
    nb = half // ta

    def body(c_ref, a_ref, o_ref):
        f = lambda k: a_ref[k].astype(_F32)
        o_ref[...] = ((f(0) + f(1)) + f(2)) + f(3)

    return pl.pallas_call(
        body, name=name,
        grid_spec=pltpu.PrefetchScalarGridSpec(
            num_scalar_prefetch=1, grid=(nb,),
            in_specs=[pl.BlockSpec((N_CHIPS, ta, width), lambda i, c: (0, i, 0))],
            out_specs=pl.BlockSpec((ta, width), lambda i, c: (c[0] * nb + i, 0))),
        out_shape=_sds((2 * half, width), _F32),
    )(c_idx, got)


def _share_with_sibling(bufs):
    n = len(bufs)

    def body(*refs):
        outs, send_sems, recv_sems = refs[n:2 * n], refs[2 * n], refs[2 * n + 1]
        x, y, c = _mesh_pos()
        copies = []
        for a, out in enumerate(outs):
            mine = out.at[_half_rows(out.shape[0], c)]
            cp = _remote(mine, mine, send_sems, recv_sems, a, (x, y, 1 - c))
            cp.start()
            copies.append(cp)
        for a, out in enumerate(outs):
            theirs = out.at[_half_rows(out.shape[0], 1 - c)]
            _remote(theirs, theirs, send_sems, recv_sems, a, (x, y, 1 - c)).wait_recv()
        for cp in copies:
            cp.wait_send()

    return pl.pallas_call(
        body, name="grad_share_sibling", in_specs=[_ANY] * n, out_specs=[_ANY] * n,
        out_shape=[_sds(b.shape, _F32) for b in bufs], input_output_aliases={a: a for a in range(n)},
        scratch_shapes=[pltpu.SemaphoreType.DMA((n,)), pltpu.SemaphoreType.DMA((n,))],
    )(*bufs)


def _allreduce_small(g):
    rows = g.shape[0]
    n_dev = 2 * N_CHIPS

    def body(g_ref, out_ref, gathered, local_sem, send_sems, recv_sems):
        x, y, c = _mesh_pos()
        me = 4 * x + 2 * y + c
        own = pltpu.make_async_copy(g_ref, gathered.at[me], local_sem)
        own.start()
        copies = []
        for k in range(1, n_dev):
            fx, fy, fc = (k >> 2) & 1, (k >> 1) & 1, k & 1
            cp = pltpu.make_async_remote_copy(
                src_ref=g_ref, dst_ref=gathered.at[me], send_sem=send_sems.at[k - 1], recv_sem=recv_sems.at[k - 1],
                device_id=(x ^ fx, y ^ fy, c ^ fc), device_id_type=MESH)
            cp.start()
            copies.append(cp)
        for k in range(1, n_dev):
            fx, fy, fc = (k >> 2) & 1, (k >> 1) & 1, k & 1
            px, py, pc = x ^ fx, y ^ fy, c ^ fc
            slab = gathered.at[4 * px + 2 * py + pc]
            pltpu.make_async_remote_copy(src_ref=slab, dst_ref=slab, send_sem=send_sems.at[k - 1],
                                         recv_sem=recv_sems.at[k - 1], device_id=(px, py, pc),
                                         device_id_type=MESH).wait_recv()
        for cp in copies:
            cp.wait_send()
        own.wait()
        acc = gathered[0]
        for d in range(1, n_dev):
            acc = acc + gathered[d]
        out_ref[...] = acc

    vmem = pl.BlockSpec(memory_space=pltpu.VMEM)
    return pl.pallas_call(
        body, name="allreduce_small", in_specs=[vmem], out_specs=vmem, out_shape=_sds((rows, D_MODEL), _F32),
        scratch_shapes=[pltpu.VMEM((n_dev, rows, D_MODEL), _F32), pltpu.SemaphoreType.DMA,
                        pltpu.SemaphoreType.DMA((n_dev - 1,)), pltpu.SemaphoreType.DMA((n_dev - 1,))],
    )(g)


def _adamw(w, g, m, v, name, g_first_row=0):
    shape = w.shape
    cols = shape[-1]
    rows = 1
    for d in shape[:-1]:
        rows *= d
    w2, m2, v2 = (a.reshape(rows, cols) for a in (w, m, v))
    tr = _pick_tile(rows, (256, 128, 64, 32, 16, 8))
    g0 = g_first_row // tr

    def body(w_ref, g_ref, m_ref, v_ref, go_ref, d_ref, nm_ref, nv_ref):
        gg = g_ref[...]
        go_ref[...] = gg
        nm = ADAM_B1 * m_ref[...] + (1.0 - ADAM_B1) * gg
        nv = ADAM_B2 * v_ref[...] + (1.0 - ADAM_B2) * (gg * gg)
        m_hat = nm / (1.0 - ADAM_B1 ** ADAM_STEP)
        v_hat = nv / (1.0 - ADAM_B2 ** ADAM_STEP)
        d_ref[...] = -ADAM_LR * (m_hat / (jnp.sqrt(v_hat) + ADAM_EPS) + ADAM_WD * w_ref[...])
        nm_ref[...] = nm
        nv_ref[...] = nv

    spec = pl.BlockSpec((tr, cols), lambda i: (i, 0))
    outs = pl.pallas_call(
        body, name=name, grid=(rows // tr,),
        in_specs=[spec, pl.BlockSpec((tr, cols), lambda i: (g0 + i, 0)), spec, spec], out_specs=[spec] * 4,
        out_shape=[_sds((rows, cols), _F32)] * 4,
    )(w2, g, m2, v2)
    return tuple(o.reshape(shape) for o in outs)


def kernel(x, mem, g_mix, w_in, b_forget, g_q_fox, g_k_fox, w_pool, pool_scale, w_out, g_mem_q, g_mem_kv, w_mem_q, w_mem_kv, g_q_mem, g_k_mem, w_mem_out, g_ffn, w_gate_up, w_down, loss_target, m_g_mix, m_w_in, m_b_forget, m_g_q_fox, m_g_k_fox, m_w_pool, m_pool_scale, m_w_out, m_g_mem_q, m_g_mem_kv, m_w_mem_q, m_w_mem_kv, m_g_q_mem, m_g_k_mem, m_w_mem_out, m_g_ffn, m_w_gate_up, m_w_down, v_g_mix, v_w_in, v_b_forget, v_g_q_fox, v_g_k_fox, v_w_pool, v_pool_scale, v_w_out, v_g_mem_q, v_g_mem_kv, v_w_mem_q, v_w_mem_kv, v_g_q_mem, v_g_k_mem, v_w_mem_out, v_g_ffn, v_w_gate_up, v_w_down):
    w = dict(g_mix=g_mix, w_in=w_in, b_forget=b_forget, g_q_fox=g_q_fox, g_k_fox=g_k_fox, w_pool=w_pool,
             pool_scale=pool_scale, w_out=w_out, g_mem_q=g_mem_q, g_mem_kv=g_mem_kv, w_mem_q=w_mem_q,
             w_mem_kv=w_mem_kv, g_q_mem=g_q_mem, g_k_mem=g_k_mem, w_mem_out=w_mem_out, g_ffn=g_ffn,
             w_gate_up=w_gate_up, w_down=w_down)
    m = dict(g_mix=m_g_mix, w_in=m_w_in, b_forget=m_b_forget, g_q_fox=m_g_q_fox, g_k_fox=m_g_k_fox, w_pool=m_w_pool,
             pool_scale=m_pool_scale, w_out=m_w_out, g_mem_q=m_g_mem_q, g_mem_kv=m_g_mem_kv, w_mem_q=m_w_mem_q,
             w_mem_kv=m_w_mem_kv, g_q_mem=m_g_q_mem, g_k_mem=m_g_k_mem, w_mem_out=m_w_mem_out, g_ffn=m_g_ffn,
             w_gate_up=m_w_gate_up, w_down=m_w_down)
    v = dict(g_mix=v_g_mix, w_in=v_w_in, b_forget=v_b_forget, g_q_fox=v_g_q_fox, g_k_fox=v_g_k_fox, w_pool=v_w_pool,
             pool_scale=v_pool_scale, w_out=v_w_out, g_mem_q=v_g_mem_q, g_mem_kv=v_g_mem_kv, w_mem_q=v_w_mem_q,
             w_mem_kv=v_w_mem_kv, g_q_mem=v_g_q_mem, g_k_mem=v_g_k_mem, w_mem_out=v_w_mem_out, g_ffn=v_g_ffn,
             w_gate_up=v_w_gate_up, w_down=v_w_down)

    classes = [c for c, _, _ in CLASSES]
    chip = 2 * lax.axis_index("x") + lax.axis_index("y")
    c_idx = lax.axis_index("c").astype(jnp.int32).reshape(1)
    own_slab = lambda bufs, own: lax.dynamic_update_slice(bufs, own[None], (chip, 0, 0))

    slabs = _class_slabs({n: w[n].astype(_MXU) for n in BIG})
    WC = {"i": own_slab(_allgather_weights([slabs["i"]])[0], slabs["i"])}
    later = ({c: slabs[c] for c in classes if c != "i"}, own_slab)

    def chip_sums(names, tag):
        grads = [G[c] for c in names]
        return [_add_halves(g, got, c_idx, "grad_add_halves_" + c)
                for c, g, got in zip(names, grads, _exchange_halves(grads, "grad_exchange_halves" + tag))]

    early = [c for c in classes if c != "i"]
    G = {}

    def send_early(buffers):
        G.update(buffers)
        return chip_sums(early, "")

    loss, grad_x, grads, gsmall, (sent, landed) = _device_grads(
        x[0], mem[0], loss_target[0], WC, {n: w[n] for n in SMALL}, later, send_early)
    loss = lax.psum(loss[0, 0], ("x", "y", "c"))
    G.update(grads)
    sent_i = chip_sums(["i"], "_w_in")
    partial = dict(zip(early + ["i"], list(sent) + sent_i))
    arrived = dict(zip(early + ["i"], list(landed) + list(_scatter_to_chips(sent_i))))
    partials = [own_slab(arrived[c], lax.dynamic_index_in_dim(partial[c], chip, 0, keepdims=False)) for c in classes]
    reduced = _share_with_sibling([_sum_chips(p, c_idx, "grad_sum_chips_" + c) for c, p in zip(classes, partials)])
    reduced = dict(zip(classes, reduced))

    small_shapes = [w[n].shape for n in SMALL]
    gsmall = _unpack_small(_allreduce_small(_pack_small([gsmall[n] for n in SMALL])), small_shapes)

    g_out, d_out, m_out, v_out = {}, {}, {}, {}
    for n, (c, first, _) in _class_rows({n: w[n] for n in BIG}).items():
        g_out[n], d_out[n], m_out[n], v_out[n] = _adamw(w[n], reduced[c], m[n], v[n], "adamw_" + n, first)
    packed = [_pack_small([t[n] for n in SMALL]) for t in (w, m, v)]
    _, ds, ms, vs = _adamw(packed[0], _pack_small(gsmall), packed[1], packed[2], "adamw_small")
    for n, gi, di, mi, vi in zip(SMALL, gsmall, _unpack_small(ds, small_shapes), _unpack_small(ms, small_shapes),
                                 _unpack_small(vs, small_shapes)):
        g_out[n], d_out[n], m_out[n], v_out[n] = gi, di, mi, vi

    return (loss, grad_x[None], *[g_out[n] for n in WEIGHTS], *[d_out[n] for n in WEIGHTS],
            *[m_out[n] for n in WEIGHTS], *[v_out[n] for n in WEIGHTS])
```

```python
import functools

import jax
import jax.numpy as jnp
import numpy as np
from jax import lax
from jax.experimental import pallas as pl
from jax.experimental.pallas import tpu as pltpu

_F32 = jnp.float32
_MXU = jnp.bfloat16

D_MODEL = 1024
DEPTH = 2
FOX_HEADS = 8
FOX_HEAD_DIM = 64
FOX_WIDTH = FOX_HEADS * FOX_HEAD_DIM
POOL_WINDOWS = (2, 4, 8, 16)
POOL_GROUP_DIM = 128
POOL_WIDTH = len(POOL_WINDOWS) * POOL_GROUP_DIM
MEM_HEADS = 4
MEM_HEAD_DIM = 128
MEM_WIDTH = MEM_HEADS * MEM_HEAD_DIM
D_FF = 2816
EPS = 1e-6
LANES = 128
HALO = 16

ADAM_LR = 0.001
ADAM_B1 = 0.9
ADAM_B2 = 0.999
ADAM_EPS = 1e-08
ADAM_WD = 0.01
ADAM_STEP = 10

N_CHIPS = 4
MESH = pl.DeviceIdType.MESH

_TM = 512
_TQ = 1024
_TMF = 256
_TF = 1408
_TT = 1024
_TB = 256
_TA = 512

BIG = ("w_in", "w_out", "w_mem_q", "w_mem_kv", "w_mem_out", "w_gate_up", "w_down")
SHARD_AXIS = {"w_in": 2, "w_out": 1, "w_mem_q": 1, "w_mem_kv": 1, "w_mem_out": 2, "w_gate_up": 2, "w_down": 1}
SMALL = ("g_mix", "b_forget", "g_q_fox", "g_k_fox", "w_pool", "pool_scale", "g_mem_q", "g_mem_kv", "g_q_mem",
         "g_k_mem", "g_ffn")
WEIGHTS = ("g_mix", "w_in", "b_forget", "g_q_fox", "g_k_fox", "w_pool", "pool_scale", "w_out", "g_mem_q", "g_mem_kv",
           "w_mem_q", "w_mem_kv", "g_q_mem", "g_k_mem", "w_mem_out", "g_ffn", "w_gate_up", "w_down")


def _dot(a, b):
    return jnp.dot(a, b, preferred_element_type=_F32)


def _dot_nt(a, b):
    return lax.dot_general(a, b, (((1,), (1,)), ((), ())), preferred_element_type=_F32)


def _dot_tn(a, b):
    return lax.dot_general(a, b, (((0,), (0,)), ((), ())), preferred_element_type=_F32)


def _group_sum(x, ones_blockdiag):
    hi = x.astype(_MXU)
    lo = (x - hi.astype(_F32)).astype(_MXU)
    return _dot(hi, ones_blockdiag) + _dot(lo, ones_blockdiag)


def _tri_dot(tri, x):
    h1 = x.astype(jnp.bfloat16)
    r1 = x - h1.astype(_F32)
    h2 = r1.astype(jnp.bfloat16)
    h3 = (r1 - h2.astype(_F32)).astype(jnp.bfloat16)
    return _dot(tri, h1) + _dot(tri, h2) + _dot(tri, h3)


def _rstd(x):
    return lax.rsqrt(jnp.mean(x * x, axis=-1, keepdims=True) + EPS)


def _norm_bwd(dy, x, g):
    r = _rstd(x)
    xhat = x * r
    u = dy * g
    dx = r * (u - xhat * jnp.mean(u * xhat, axis=-1, keepdims=True))
    return dx, jnp.sum(dy * xhat, axis=0, keepdims=True)


def _headnorm_bwd(dy, x, g, ones_blockdiag, width):
    r = lax.rsqrt(_group_sum(x * x, ones_blockdiag) * (1.0 / width) + EPS)
    xhat = x * r
    u = dy * g
    dx = r * (u - xhat * (_group_sum(u * xhat, ones_blockdiag) * (1.0 / width)))
    return dx, jnp.sum(dy * xhat, axis=0, keepdims=True)


def _fold_heads(row, heads, width):
    acc = row[:, 0:width]
    for h in range(1, heads):
        acc = acc + row[:, h * width:(h + 1) * width]
    return acc


_ANY = pl.BlockSpec(memory_space=pl.ANY)


def _resident(shape):
    return pl.BlockSpec(shape, lambda *_: (0,) * len(shape), pipeline_mode=pl.Buffered(1))


def _rows(tm, width):
    return pl.BlockSpec((tm, width), lambda i: (i, 0))


def _blockdiag_ones(groups, width):
    return jnp.kron(jnp.eye(groups, dtype=_F32), jnp.ones((width, width), _F32)).astype(_MXU)


def _sds(shape, dtype):
    return jax.ShapeDtypeStruct(shape, dtype)


def _mix_in_fwd(h, g_mix, w_main, w_f, b_f, gq, gk, bd64):
    T = h.shape[0]
    tm = min(_TM, T)

    def body(h_ref, g_ref, wm_ref, wf_ref, bf_ref, gq_ref, gk_ref, bd_ref,
             xn_ref, zqk_ref, qn_ref, kn_ref, v_ref, pin_ref, fl_ref):
        x = h_ref[...]
        xn = ((x * _rstd(x)) * g_ref[...]).astype(_MXU)
        xn_ref[...] = xn
        z = _dot(xn, wm_ref[...])
        q = z[:, :FOX_WIDTH]
        k = z[:, FOX_WIDTH:2 * FOX_WIDTH]
        zqk_ref[...] = z[:, :2 * FOX_WIDTH]
        bd = bd_ref[...]
        rq = lax.rsqrt(_group_sum(q * q, bd) * (1.0 / FOX_HEAD_DIM) + EPS)
        rk = lax.rsqrt(_group_sum(k * k, bd) * (1.0 / FOX_HEAD_DIM) + EPS)
        qn_ref[...] = ((q * rq) * gq_ref[...]).astype(_MXU)
        kn_ref[...] = ((k * rk) * gk_ref[...]).astype(_MXU)
        v_ref[...] = z[:, 2 * FOX_WIDTH:3 * FOX_WIDTH].astype(_MXU)
        pin_ref[...] = z[:, 3 * FOX_WIDTH:]
        fl_ref[...] = _dot(xn, wf_ref[...]) + bf_ref[...]

    return pl.pallas_call(
        body, name="mix_in_fwd", grid=(T // tm,),
        in_specs=[_rows(tm, D_MODEL), _resident((1, D_MODEL)), _resident(w_main.shape), _resident(w_f.shape),
                  _resident((1, LANES)), _resident((1, FOX_WIDTH)), _resident((1, FOX_WIDTH)), _resident(bd64.shape)],
        out_specs=[_rows(tm, D_MODEL), _rows(tm, 2 * FOX_WIDTH), _rows(tm, FOX_WIDTH), _rows(tm, FOX_WIDTH),
                   _rows(tm, FOX_WIDTH), _rows(tm, POOL_WIDTH), _rows(tm, LANES)],
        out_shape=[_sds((T, D_MODEL), _MXU), _sds((T, 2 * FOX_WIDTH), _F32), _sds((T, FOX_WIDTH), _MXU),
                   _sds((T, FOX_WIDTH), _MXU), _sds((T, FOX_WIDTH), _MXU), _sds((T, POOL_WIDTH), _F32),
                   _sds((T, LANES), _F32)],
    )(h, g_mix, w_main, w_f, b_f, gq, gk, bd64)


def _split3(x):
    h1 = x.astype(jnp.bfloat16).astype(_F32)
    r1 = x - h1
    h2 = r1.astype(jnp.bfloat16).astype(_F32)
    h3 = (r1 - h2).astype(jnp.bfloat16).astype(_F32)
    return h1, h2, h3


_ONES3 = (1.0, 1.0, 1.0)
_ZEROS3 = (0.0, 0.0, 0.0)


def _aug_head(feat, first, second):
    rows = feat.shape[0]
    lane = lax.broadcasted_iota(jnp.int32, (rows, LANES - FOX_HEAD_DIM), 1)
    aux = jnp.zeros((rows, LANES - FOX_HEAD_DIM), _F32)
    for k in range(3):
        aux = jnp.where(lane == k, first[k], aux)
        aux = jnp.where(lane == 3 + k, second[k], aux)
    return jnp.concatenate([feat.astype(_MXU), aux.astype(_MXU)], axis=-1)


def _gate_fwd(fl, qn, kn, v):
    T = fl.shape[0]
    tb = min(_TB, T)
    wide = FOX_HEADS * LANES

    sel = np.zeros((3, LANES, FOX_WIDTH), np.float32)
    for h in range(FOX_HEADS):
        for k in range(3):
            sel[0, h + 8 * k, FOX_HEAD_DIM * h + k] = 1.0
            sel[0, 24, FOX_HEAD_DIM * h + 3 + k] = 1.0
            sel[1, 24, FOX_HEAD_DIM * h + k] = 1.0
            sel[1, h + 8 * k, FOX_HEAD_DIM * h + 3 + k] = -1.0
            sel[2, 24, FOX_HEAD_DIM * h + k] = 1.0
    sel = jnp.asarray(sel).astype(_MXU)

    def body(fl_ref, q_ref, k_ref, v_ref, sel_ref, qa_ref, ka_ref, va_ref, carry):
        @pl.when(pl.program_id(0) == 0)
        def _():
            carry[...] = jnp.zeros_like(carry)

        x = fl_ref[...]
        ls = jnp.minimum(x, 0.0) - jnp.log1p(jnp.exp(-jnp.abs(x)))
        row = lax.broadcasted_iota(jnp.int32, (tb, tb), 0)
        col = lax.broadcasted_iota(jnp.int32, (tb, tb), 1)
        tri = jnp.where(col <= row, 1.0, 0.0).astype(jnp.bfloat16)
        cs = _tri_dot(tri, ls) + carry[...]
        carry[...] = cs[tb - 1:tb, :]
        h1, h2, h3 = _split3(cs)
        lane = lax.broadcasted_iota(jnp.int32, (tb, LANES), 1)
        terms = jnp.where(lane < 8, h1, jnp.where(lane < 16, pltpu.roll(h2, 8, axis=1), jnp.where(
            lane < 24, pltpu.roll(h3, 16, axis=1), jnp.where(lane == 24, 1.0, 0.0)))).astype(_MXU)
        extra = [_dot(terms, sel_ref[a]).astype(_MXU) for a in range(3)]
        scaled = (q_ref[...].astype(_F32) * (FOX_HEAD_DIM ** -0.5)).astype(_MXU)
        for h in range(FOX_HEADS):
            sl = slice(h * FOX_HEAD_DIM, (h + 1) * FOX_HEAD_DIM)
            out = slice(h * LANES, (h + 1) * LANES)
            qa_ref[:, out] = jnp.concatenate([scaled[:, sl], extra[0][:, sl]], axis=-1)
            ka_ref[:, out] = jnp.concatenate([k_ref[:, sl], extra[1][:, sl]], axis=-1)
            va_ref[:, out] = jnp.concatenate([v_ref[:, sl], extra[2][:, sl]], axis=-1)

    return pl.pallas_call(
        body, name="gate_fwd", grid=(T // tb,),
        in_specs=[_rows(tb, LANES), _rows(tb, FOX_WIDTH), _rows(tb, FOX_WIDTH), _rows(tb, FOX_WIDTH),
                  _resident(sel.shape)],
        out_specs=[_rows(tb, wide)] * 3, out_shape=[_sds((T, wide), _MXU)] * 3,
        scratch_shapes=[pltpu.VMEM((1, LANES), _F32)],
    )(fl, qn, kn, v, sel)


def _gate_bwd(drs, dcs, fl):
    T = fl.shape[0]
    tb = min(_TB, T)
    nb = T // tb

    def body(r_ref, d_ref, fl_ref, df_ref, carry):
        @pl.when(pl.program_id(0) == 0)
        def _():
            carry[...] = jnp.zeros_like(carry)

        lane = lax.broadcasted_iota(jnp.int32, (tb, LANES), 1)
        dc = jnp.zeros((tb, LANES), _F32)
        for p in range(FOX_HEADS // 2):
            pair = r_ref[p] - d_ref[p]
            for hh in range(2):
                dc = jnp.where(lane == 2 * p + hh, pair[:, hh:hh + 1], dc)
        row = lax.broadcasted_iota(jnp.int32, (tb, tb), 0)
        col = lax.broadcasted_iota(jnp.int32, (tb, tb), 1)
        tri = jnp.where(col >= row, 1.0, 0.0).astype(jnp.bfloat16)
        rc = _tri_dot(tri, dc) + carry[...]
        carry[...] = rc[0:1, :]
        df_ref[...] = rc * (1.0 / (1.0 + jnp.exp(fl_ref[...])))

    rev = pl.BlockSpec((tb, LANES), lambda i: (nb - 1 - i, 0))
    rev4 = pl.BlockSpec((FOX_HEADS // 2, tb, LANES), lambda i: (0, nb - 1 - i, 0))
    return pl.pallas_call(
        body, name="gate_bwd", grid=(nb,), in_specs=[rev4, rev4, rev], out_specs=rev,
        out_shape=_sds((T, LANES), _F32), scratch_shapes=[pltpu.VMEM((1, LANES), _F32)],
    )(drs, dcs, fl)


def _fox_scores(q_ref, k_ref, hh, masked):
    sl = slice(LANES * hh, LANES * (hh + 1))
    s = _dot_nt(q_ref[:, sl], k_ref[:, sl])
    if masked:
        row = lax.broadcasted_iota(jnp.int32, s.shape, 0)
        col = lax.broadcasted_iota(jnp.int32, s.shape, 1)
        s = jnp.where(col <= row, s, -jnp.inf)
    return s


def _causal_steps(nq, query_major):
    if query_major:
        steps = [(i, j) for i in range(nq) for j in range(i + 1)]
    else:
        steps = [(i, j) for j in range(nq) for i in range(j, nq)]
    return (jnp.asarray(np.array([s[0] for s in steps], np.int32)),
            jnp.asarray(np.array([s[1] for s in steps], np.int32)))


def _two_rows(a, b):
    row = lax.broadcasted_iota(jnp.int32, (8, a.shape[1]), 0)
    return jnp.where(row == 0, a, jnp.where(row == 1, b, 0.0))


def _two_lanes(a, b):
    lane = lax.broadcasted_iota(jnp.int32, (a.shape[0], LANES), 1)
    return jnp.where(lane == 0, a, jnp.where(lane == 1, b, 0.0))


def _gather_behind(pid, t, n_pairs, steps, srcs, outs, send_sems, recv_sems):
    x, y, c = _mesh_pos()
    me = 2 * x + y
    sibling = (x, y, 1 - c)
    chips = _other_chips(x, y)

    def first(a, j):
        src, out = srcs[a], outs[a]
        mine = _half_rows(src.shape[0], c)
        return _remote(src.at[mine], out.at[me, mine], send_sems, recv_sems, 6 * a + j, (*chips[j], c))

    def landed(a, j):
        cx, cy = chips[j]
        slab = outs[a].at[2 * cx + cy, _half_rows(outs[a].shape[1], c)]
        return _remote(slab, slab, send_sems, recv_sems, 6 * a + j, (cx, cy, c))

    def passed(a, j):
        cx, cy = chips[j]
        slab = outs[a].at[2 * cx + cy, _half_rows(outs[a].shape[1], c)]
        return _remote(slab, slab, send_sems, recv_sems, 6 * a + 3 + j, sibling)

    def from_sibling(a, j):
        cx, cy = chips[j]
        slab = outs[a].at[2 * cx + cy, _half_rows(outs[a].shape[1], 1 - c)]
        return _remote(slab, slab, send_sems, recv_sems, 6 * a + 3 + j, sibling)

    every = [(a, j) for a in range(len(srcs)) for j in range(3)]

    @pl.when((pid == 0) & (t == 0))
    def _():
        for a, j in every:
            first(a, j).start()

    @pl.when((pid == n_pairs // 2) & (t == 0))
    def _():
        for a, j in every:
            landed(a, j).wait_recv()
            passed(a, j).start()

    @pl.when((pid == n_pairs - 1) & (t == steps - 1))
    def _():
        for a, j in every:
            from_sibling(a, j).wait_recv()
        for a, j in every:
            first(a, j).wait_send()
            passed(a, j).wait_send()


def _fox_fwd(qa, ka, va, gather=()):
    T = qa.shape[0]
    tq = min(_TQ, T)
    nq = T // tq
    ii, jj = _causal_steps(nq, True)
    steps = int(ii.shape[0])
    n = len(gather)
    n_pairs = FOX_HEADS // 2

    def body(ii_ref, jj_ref, q_ref, k_ref, v_ref, *refs):
        srcs, (o_ref, p_ref, mb_ref, lse_ref), outs = refs[:n], refs[n:n + 4], refs[n + 4:2 * n + 4]
        m_s, acc_s = refs[2 * n + 4:2 * n + 6]
        t = pl.program_id(1)
        i = ii_ref[t]
        j = jj_ref[t]
        if n:
            _gather_behind(pl.program_id(0), t, n_pairs, steps, srcs, outs, *refs[2 * n + 6:])

        @pl.when(j == 0)
        def _():
            m_s[...] = jnp.full(m_s.shape, -jnp.inf, _F32)
            acc_s[...] = jnp.zeros_like(acc_s)

        def step(masked):
            for hh in range(2):
                sl = slice(LANES * hh, LANES * (hh + 1))
                st = _dot_nt(k_ref[:, sl], q_ref[:, sl])
                if masked:
                    key = lax.broadcasted_iota(jnp.int32, st.shape, 0)
                    qry = lax.broadcasted_iota(jnp.int32, st.shape, 1)
                    st = jnp.where(key <= qry, st, -jnp.inf)
                m_prev = m_s[hh]
                m_new = jnp.maximum(m_prev, jnp.max(st, axis=0, keepdims=True))
                p_ref[0, 0, hh] = jnp.exp(st - m_new).astype(_MXU)
                vt = v_ref[:, sl].astype(_F32).T.astype(_MXU)
                acc_s[hh] = jnp.exp(m_prev - m_new) * acc_s[hh] + _dot(vt, p_ref[0, 0, hh])
                m_s[hh] = m_new
            mb_ref[0, 0] = _two_rows(m_s[0], m_s[1])

        @pl.when(j < i)
        def _():
            step(False)

        @pl.when(j == i)
        def _():
            step(True)
            outs, lses = [], []
            for hh in range(2):
                acc = acc_s[hh]
                lses.append(m_s[hh] + jnp.log(acc[FOX_HEAD_DIM:FOX_HEAD_DIM + 1, :]))
                acc = acc.T
                outs.append(acc[:, :FOX_HEAD_DIM] / acc[:, FOX_HEAD_DIM:FOX_HEAD_DIM + 1])
            o_ref[...] = jnp.concatenate(outs, axis=-1).astype(_MXU)
            lse_ref[0] = _two_rows(lses[0], lses[1])

    qspec = pl.BlockSpec((tq, 2 * LANES), lambda p, t, ii, jj: (ii[t], p))
    kspec = pl.BlockSpec((tq, 2 * LANES), lambda p, t, ii, jj: (jj[t], p))
    sems = [pltpu.SemaphoreType.DMA((6 * n,)), pltpu.SemaphoreType.DMA((6 * n,))] if n else []
    res = pl.pallas_call(
        body, name="fox_fwd_gather" if n else "fox_fwd",
        grid_spec=pltpu.PrefetchScalarGridSpec(
            num_scalar_prefetch=2, grid=(n_pairs, steps),
            in_specs=[qspec, kspec, kspec] + [_ANY] * n,
            out_specs=[pl.BlockSpec((tq, 2 * FOX_HEAD_DIM), lambda p, t, ii, jj: (ii[t], p)),
                       pl.BlockSpec((1, 1, 2, tq, tq), lambda p, t, ii, jj: (p, t, 0, 0, 0)),
                       pl.BlockSpec((1, 1, 8, tq), lambda p, t, ii, jj: (p, t, 0, 0)),
                       pl.BlockSpec((1, 8, tq), lambda p, t, ii, jj: (p, 0, ii[t]))] + [_ANY] * n,
            scratch_shapes=[pltpu.VMEM((2, 1, tq), _F32), pltpu.VMEM((2, LANES, tq), _F32)] + sems),
        out_shape=[_sds((T, FOX_WIDTH), _MXU), _sds((n_pairs, steps, 2, tq, tq), _MXU),
                   _sds((n_pairs, steps, 8, tq), _F32), _sds((n_pairs, 8, T), _F32)]
        + [_sds((N_CHIPS,) + s.shape, s.dtype) for s in gather],
    )(ii, jj, qa, ka, va, *gather)
    return res[0], res[1], res[2], res[3], list(res[4:])


def _pool_window_sum(ext, w, forward):
    n = ext.shape[0]
    sm = ext
    k = 1
    while k < w:
        sm = sm + pltpu.roll(sm, (n - k) if forward else k, axis=0)
        k *= 2
    return sm


def _out_proj_fwd(h, o, pin, w_pool, pscale, w_out):
    T = h.shape[0]
    tm = min(_TM, T)
    hb = tm // HALO

    def body(h_ref, o_ref, pin_ref, halo_ref, wp_ref, ps_ref, wo_ref, h1_ref, mixed_ref, y_ref):
        i = pl.program_id(0)
        pin_t = pin_ref[...]
        halo = jnp.where(i == 0, 0.0, halo_ref[...])
        ext = jnp.concatenate([halo, pin_t], axis=0)
        t = (i * tm + lax.broadcasted_iota(jnp.int32, (tm, 1), 0) + 1).astype(_F32)
        mixed, ys = [], []
        for g, w in enumerate(POOL_WINDOWS):
            sl = slice(g * POOL_GROUP_DIM, (g + 1) * POOL_GROUP_DIM)
            win = _pool_window_sum(ext[:, sl], w, False)[HALO:, :]
            mg = (win / jnp.minimum(t, float(w)) - pin_t[:, sl]).astype(_MXU)
            mixed.append(mg)
            ys.append(_dot(mg, wp_ref[g]))
        mixed_ref[...] = jnp.concatenate(mixed, axis=-1)
        y = (jnp.concatenate(ys, axis=-1) * ps_ref[...]).astype(_MXU)
        y_ref[...] = y
        h1_ref[...] = h_ref[...] + _dot(o_ref[...], wo_ref[:FOX_WIDTH, :]) + _dot(y, wo_ref[FOX_WIDTH:, :])

    return pl.pallas_call(
        body, name="out_proj_fwd", grid=(T // tm,),
        in_specs=[_rows(tm, D_MODEL), _rows(tm, FOX_WIDTH), _rows(tm, POOL_WIDTH),
                  pl.BlockSpec((HALO, POOL_WIDTH), lambda i: (jnp.maximum(i * hb - 1, 0), 0)),
                  _resident(w_pool.shape), _resident((1, POOL_WIDTH)), _resident(w_out.shape)],
        out_specs=[_rows(tm, D_MODEL), _rows(tm, POOL_WIDTH), _rows(tm, POOL_WIDTH)],
        out_shape=[_sds((T, D_MODEL), _F32), _sds((T, POOL_WIDTH), _MXU), _sds((T, POOL_WIDTH), _MXU)],
    )(h, o, pin, pin, w_pool, pscale, w_out)


def _mem_kv_fwd(mem, g_kv, w_kv, gkm, bd128):
    M = mem.shape[0]

    def body(mem_ref, g_ref, w_ref, gk_ref, bd_ref, mn_ref, mkv_ref, mk_ref, mv_ref):
        x = mem_ref[...]
        mn = ((x * _rstd(x)) * g_ref[...]).astype(_MXU)
        mn_ref[...] = mn
        z = _dot(mn, w_ref[...])
        mkv_ref[...] = z
        k = z[:, :MEM_WIDTH]
        rk = lax.rsqrt(_group_sum(k * k, bd_ref[...]) * (1.0 / MEM_HEAD_DIM) + EPS)
        mk_ref[...] = ((k * rk) * gk_ref[...]).astype(_MXU)
        mv_ref[...] = z[:, MEM_WIDTH:].astype(_MXU)

    return pl.pallas_call(
        body, name="mem_kv_fwd",
        out_shape=[_sds((M, D_MODEL), _MXU), _sds((M, 2 * MEM_WIDTH), _F32), _sds((M, MEM_WIDTH), _MXU),
                   _sds((M, MEM_WIDTH), _MXU)],
    )(mem, g_kv, w_kv, gkm, bd128)


def _mem_softmax(qn, mk_ref, hd):
    sl = slice(hd * MEM_HEAD_DIM, (hd + 1) * MEM_HEAD_DIM)
    s = _dot_nt(qn[:, sl], mk_ref[:, sl]) * (MEM_HEAD_DIM ** -0.5)
    e = jnp.exp(s - jnp.max(s, axis=-1, keepdims=True))
    return e / jnp.sum(e, axis=-1, keepdims=True)


def _mem_attn_fwd(h1, g_q, w_q, gqm, bd128, mk, mv, w_mo):
    T = h1.shape[0]
    tm = min(_TM, T)

    def body(h_ref, g_ref, wq_ref, gq_ref, bd_ref, mk_ref, mv_ref, wo_ref, h2_ref, hn_ref, mo_ref):
        x = h_ref[...]
        hn = ((x * _rstd(x)) * g_ref[...]).astype(_MXU)
        hn_ref[...] = hn
        mq = _dot(hn, wq_ref[...])
        rq = lax.rsqrt(_group_sum(mq * mq, bd_ref[...]) * (1.0 / MEM_HEAD_DIM) + EPS)
        qn = ((mq * rq) * gq_ref[...]).astype(_MXU)
        outs = []
        for hd in range(MEM_HEADS):
            p = _mem_softmax(qn, mk_ref, hd).astype(_MXU)
            outs.append(_dot(p, mv_ref[:, hd * MEM_HEAD_DIM:(hd + 1) * MEM_HEAD_DIM]))
        mo = jnp.concatenate(outs, axis=-1).astype(_MXU)
        mo_ref[...] = mo
        h2_ref[...] = x + _dot(mo, wo_ref[...])

    return pl.pallas_call(
        body, name="mem_attn_fwd", grid=(T // tm,),
        in_specs=[_rows(tm, D_MODEL), _resident((1, D_MODEL)), _resident(w_q.shape), _resident((1, MEM_WIDTH)),
                  _resident(bd128.shape), _resident(mk.shape), _resident(mv.shape), _resident(w_mo.shape)],
        out_specs=[_rows(tm, D_MODEL), _rows(tm, D_MODEL), _rows(tm, MEM_WIDTH)],
        out_shape=[_sds((T, D_MODEL), _F32), _sds((T, D_MODEL), _MXU), _sds((T, MEM_WIDTH), _MXU)],
    )(h1, g_q, w_q, gqm, bd128, mk, mv, w_mo)


def _ffn_weight_specs(layer):
    nf = D_FF // _TF
    return [pl.BlockSpec((1, D_MODEL, _TF), lambda i, j: (j, layer, 0)),
            pl.BlockSpec((1, D_MODEL, _TF), lambda i, j: (j + nf, layer, 0)),
            pl.BlockSpec((2, _TF // 2, D_MODEL), lambda i, j: (j, layer, 0))]


def _ffn_fwd(h2, g_ffn, w_gu, w_d, layer):
    T = h2.shape[0]
    tm = min(_TMF, T)
    nf = D_FF // _TF

    def body(h_ref, g_ref, wg_ref, wu_ref, wd_ref, h3_ref, hn_ref, acc, xn_s):
        j = pl.program_id(1)

        @pl.when(j == 0)
        def _():
            x = h_ref[...]
            xn = ((x * _rstd(x)) * g_ref[...]).astype(_MXU)
            xn_s[...] = xn
            hn_ref[...] = xn
            acc[...] = jnp.zeros_like(acc)

        xn = xn_s[...]
        g = _dot(xn, wg_ref[0])
        u = _dot(xn, wu_ref[0])
        a = ((g * jax.nn.sigmoid(g)) * u).astype(_MXU)
        acc[...] += _dot(a, wd_ref[...].reshape(_TF, D_MODEL))

        @pl.when(j == nf - 1)
        def _():
            h3_ref[...] = h_ref[...] + acc[...]

    tok = pl.BlockSpec((tm, D_MODEL), lambda i, j: (i, 0))
    return pl.pallas_call(
        body, name="ffn_fwd", grid=(T // tm, nf),
        in_specs=[tok, pl.BlockSpec((1, D_MODEL), lambda i, j: (0, 0))] + _ffn_weight_specs(layer),
        out_specs=[tok, tok],
        out_shape=[_sds((T, D_MODEL), _F32), _sds((T, D_MODEL), _MXU)],
        scratch_shapes=[pltpu.VMEM((tm, D_MODEL), _F32), pltpu.VMEM((tm, D_MODEL), _MXU)],
    )(h2, g_ffn, w_gu, w_gu, w_d)


def _loss_grad(y, tgt):
    T = y.shape[0]
    tm = min(_TA, T)

    def body(y_ref, t_ref, dy_ref, loss_ref):
        @pl.when(pl.program_id(0) == 0)
        def _():
            loss_ref[...] = jnp.zeros_like(loss_ref)

        err = y_ref[...] - t_ref[...]
        dy_ref[...] = err * (1.0 / D_MODEL)
        part = jnp.sum(jnp.sum(err * err, axis=0, keepdims=True), axis=1, keepdims=True)
        loss_ref[...] += part * (0.5 / D_MODEL)

    return pl.pallas_call(
        body, name="loss_grad", grid=(T // tm,), in_specs=[_rows(tm, D_MODEL), _rows(tm, D_MODEL)],
        out_specs=[_rows(tm, D_MODEL), pl.BlockSpec((1, 1), lambda i: (0, 0))],
        out_shape=[_sds((T, D_MODEL), _F32), _sds((1, 1), _F32)],
    )(y, tgt)


def _pick_tile(n, candidates=(1408, 1024, 512, 256, 128)):
    for c in candidates:
        if n % c == 0:
            return c
    return n


def _matmul_tn(a, b, name, tk=None, tn=None, dst=None, dst_shape=None, dst_index=None, dst_chips=1):
    T, K = a.shape
    N = b.shape[1]
    tk, tn, tt = tk or _pick_tile(K), tn or _pick_tile(N), min(_TT, T)

    def body(*refs):
        a_ref, b_ref, o_ref = refs[0], refs[1], refs[-1]

        @pl.when(pl.program_id(2) == 0)
        def _():
            o_ref[...] = jnp.zeros_like(o_ref)

        o_ref[...] += _dot_tn(a_ref[...].astype(_MXU), b_ref[...].astype(_MXU)).reshape(o_ref.shape)

    in_specs = [pl.BlockSpec((tt, tk), lambda i, j, t: (t, i)), pl.BlockSpec((tt, tn), lambda i, j, t: (t, j))]
    if dst_shape is None:
        out_spec, out_shape, args, alias = pl.BlockSpec((tk, tn), lambda i, j, t: (i, j)), (K, N), (a, b), {}
    else:
        out_spec = pl.BlockSpec((dst_chips, tk // dst_chips, tn), lambda i, j, t: dst_index(i, j))
        out_shape, args, alias = dst_shape, (a, b), {}
        if dst is not None:
            in_specs, args, alias = in_specs + [_ANY], (a, b, dst), {2: 0}
    return pl.pallas_call(
        body, name=name, grid=(K // tk, N // tn, T // tt), in_specs=in_specs, out_specs=out_spec,
        out_shape=_sds(out_shape, _F32), input_output_aliases=alias,
    )(*args)


def _ffn_bwd(dh3, h2, hn, g_ffn, w_gu, w_d, layer):
    T = h2.shape[0]
    tm = min(_TMF, T)
    nf = D_FF // _TF

    def body(dh_ref, h_ref, hn_ref, g_ref, wg_ref, wu_ref, wd_ref, dh2_ref, act_ref, dg_ref, du_ref, dgain_ref,
             acc, dyb):
        i = pl.program_id(0)
        j = pl.program_id(1)

        @pl.when((i == 0) & (j == 0))
        def _():
            dgain_ref[...] = jnp.zeros_like(dgain_ref)

        @pl.when(j == 0)
        def _():
            dyb[...] = dh_ref[...].astype(_MXU)
            acc[...] = jnp.zeros_like(acc)

        xn = hn_ref[...]
        wg = wg_ref[0]
        wu = wu_ref[0]
        g = _dot(xn, wg)
        u = _dot(xn, wu)
        sg = jax.nn.sigmoid(g)
        sl = g * sg
        act_ref[...] = (sl * u).astype(_MXU)
        da = _dot_nt(dyb[...], wd_ref[...].reshape(_TF, D_MODEL))
        dgate = (da * u * (sg * (1.0 + g * (1.0 - sg)))).astype(_MXU)
        dup = (da * sl).astype(_MXU)
        dg_ref[...] = dgate
        du_ref[...] = dup
        acc[...] += _dot_nt(dgate, wg) + _dot_nt(dup, wu)

        @pl.when(j == nf - 1)
        def _():
            dhn = acc[...]
            dx, dgain = _norm_bwd(dhn, h_ref[...], g_ref[...])
            dh2_ref[...] = dh_ref[...] + dx
            dgain_ref[...] += dgain

    tok = pl.BlockSpec((tm, D_MODEL), lambda i, j: (i, 0))
    ffb = pl.BlockSpec((tm, _TF), lambda i, j: (i, j))
    row = pl.BlockSpec((1, D_MODEL), lambda i, j: (0, 0))
    return pl.pallas_call(
        body, name="ffn_bwd", grid=(T // tm, nf),
        in_specs=[tok, tok, tok, row] + _ffn_weight_specs(layer),
        out_specs=[tok, ffb, ffb, ffb, row],
        out_shape=[_sds((T, D_MODEL), _F32), _sds((T, D_FF), _MXU), _sds((T, D_FF), _MXU), _sds((T, D_FF), _MXU),
                   _sds((1, D_MODEL), _F32)],
        scratch_shapes=[pltpu.VMEM((tm, D_MODEL), _F32), pltpu.VMEM((tm, D_MODEL), _MXU)],
    )(dh3, h2, hn, g_ffn, w_gu, w_gu, w_d)


def _mem_attn_bwd(dh2, h1, g_q, w_q, gqm, bd128, mk, mv, w_mo):
    T = h1.shape[0]
    M = mk.shape[0]
    tm = min(_TM, T)
    nt = T // tm

    def body(dh_ref, h_ref, g_ref, wq_ref, gq_ref, bd_ref, mk_ref, mv_ref, wo_ref,
             dh1_ref, dmq_ref, dmk_ref, dmv_ref, dgq_ref, dgain_ref, dgq_acc):
        i = pl.program_id(0)

        @pl.when(i == 0)
        def _():
            dmk_ref[...] = jnp.zeros_like(dmk_ref)
            dmv_ref[...] = jnp.zeros_like(dmv_ref)
            dgain_ref[...] = jnp.zeros_like(dgain_ref)
            dgq_acc[...] = jnp.zeros_like(dgq_acc)

        x = h_ref[...]
        g = g_ref[...]
        bd = bd_ref[...]
        hn = ((x * _rstd(x)) * g).astype(_MXU)
        mq = _dot(hn, wq_ref[...])
        rq = lax.rsqrt(_group_sum(mq * mq, bd) * (1.0 / MEM_HEAD_DIM) + EPS)
        qn = ((mq * rq) * gq_ref[...]).astype(_MXU)
        dmo = _dot_nt(dh_ref[...].astype(_MXU), wo_ref[...])
        dqn = []
        for hd in range(MEM_HEADS):
            sl = slice(hd * MEM_HEAD_DIM, (hd + 1) * MEM_HEAD_DIM)
            p = _mem_softmax(qn, mk_ref, hd)
            dmo_h = dmo[:, sl].astype(_MXU)
            dp = _dot_nt(dmo_h, mv_ref[:, sl])
            ds = (p * (dp - jnp.sum(p * dp, axis=-1, keepdims=True)) * (MEM_HEAD_DIM ** -0.5)).astype(_MXU)
            dqn.append(_dot(ds, mk_ref[:, sl]))
            dmk_ref[:, sl] += _dot_tn(ds, qn[:, sl])
            dmv_ref[:, sl] += _dot_tn(p.astype(_MXU), dmo_h)
        dqn = jnp.concatenate(dqn, axis=-1)
        dmq, dgq = _headnorm_bwd(dqn, mq, gq_ref[...], bd, MEM_HEAD_DIM)
        dgq_acc[...] += dgq
        dmq = dmq.astype(_MXU)
        dmq_ref[...] = dmq
        dhn = _dot_nt(dmq, wq_ref[...])
        dx, dgain = _norm_bwd(dhn, x, g)
        dh1_ref[...] = dh_ref[...] + dx
        dgain_ref[...] += dgain

        @pl.when(i == nt - 1)
        def _():
            dgq_ref[...] = _fold_heads(dgq_acc[...], MEM_HEADS, MEM_HEAD_DIM)

    const2 = lambda shape: pl.BlockSpec(shape, lambda i: (0, 0))
    return pl.pallas_call(
        body, name="mem_attn_bwd", grid=(nt,),
        in_specs=[_rows(tm, D_MODEL), _rows(tm, D_MODEL), _resident((1, D_MODEL)), _resident(w_q.shape),
                  _resident((1, MEM_WIDTH)), _resident(bd128.shape), _resident(mk.shape), _resident(mv.shape),
                  _resident(w_mo.shape)],
        out_specs=[_rows(tm, D_MODEL), _rows(tm, MEM_WIDTH), const2((M, MEM_WIDTH)), const2((M, MEM_WIDTH)),
                   const2((1, MEM_HEAD_DIM)), const2((1, D_MODEL))],
        out_shape=[_sds((T, D_MODEL), _F32), _sds((T, MEM_WIDTH), _MXU), _sds((M, MEM_WIDTH), _F32),
                   _sds((M, MEM_WIDTH), _F32), _sds((1, MEM_HEAD_DIM), _F32), _sds((1, D_MODEL), _F32)],
        scratch_shapes=[pltpu.VMEM((1, MEM_WIDTH), _F32)],
    )(dh2, h1, g_q, w_q, gqm, bd128, mk, mv, w_mo)


def _mem_kv_bwd(dmk, dmv, mkv, mn, mem, g_kv, gkm, bd128, w_kv, dst, dst_shape, block):
    rows = D_MODEL // N_CHIPS

    def body(dmk_ref, dmv_ref, mkv_ref, mn_ref, mem_ref, g_ref, gk_ref, bd_ref, w_ref, *rest):
        dw_ref, dgk_ref, dgain_ref = rest[-3:]
        kraw = mkv_ref[:, :MEM_WIDTH]
        dk, dgk = _headnorm_bwd(dmk_ref[...], kraw, gk_ref[...], bd_ref[...], MEM_HEAD_DIM)
        dgk_ref[...] = _fold_heads(dgk, MEM_HEADS, MEM_HEAD_DIM)
        dmkv = jnp.concatenate([dk, dmv_ref[...]], axis=-1).astype(_MXU)
        dw_ref[...] = _dot_tn(mn_ref[...], dmkv).reshape(N_CHIPS, rows, 2 * MEM_WIDTH)
        dmn = _dot_nt(dmkv, w_ref[...])
        _, dgain = _norm_bwd(dmn, mem_ref[...], g_ref[...])
        dgain_ref[...] = dgain

    args = (dmk, dmv, mkv, mn, mem, g_kv, gkm, bd128, w_kv)
    whole = lambda a: pl.BlockSpec(a.shape, lambda i: (0,) * a.ndim)
    in_specs, alias = [whole(a) for a in args], {}
    if dst is not None:
        in_specs, args, alias = in_specs + [_ANY], args + (dst,), {len(args): 0}
    return pl.pallas_call(
        body, name="mem_kv_bwd", grid=(1,), in_specs=in_specs,
        out_specs=[pl.BlockSpec((N_CHIPS, rows, 2 * MEM_WIDTH), lambda i: (0, block, 0)),
                   pl.BlockSpec((1, MEM_HEAD_DIM), lambda i: (0, 0)), pl.BlockSpec((1, D_MODEL), lambda i: (0, 0))],
        out_shape=[_sds(dst_shape, _F32), _sds((1, MEM_HEAD_DIM), _F32), _sds((1, D_MODEL), _F32)],
        input_output_aliases=alias,
    )(*args)


def _out_proj_bwd(dh1, mixed, o, bd64, w_pool, pscale, w_out):
    T = dh1.shape[0]
    tm = min(_TM, T)
    hb = tm // HALO
    nt = T // tm

    def body(dh_ref, halo_ref, mx_ref, o_ref, bd_ref, wp_ref, ps_ref, wo_ref, doa_ref, dpin_ref, dwp_ref, dps_ref):
        i = pl.program_id(0)

        @pl.when(i == 0)
        def _():
            dwp_ref[...] = jnp.zeros_like(dwp_ref)
            dps_ref[...] = jnp.zeros_like(dps_ref)

        dcat = _dot_nt(dh_ref[...].astype(_MXU), wo_ref[...])
        do = dcat[:, :FOX_WIDTH].astype(_MXU)
        delta = _group_sum(do.astype(_F32) * o_ref[...].astype(_F32), bd_ref[...])
        for h in range(FOX_HEADS):
            sl = slice(h * FOX_HEAD_DIM, (h + 1) * FOX_HEAD_DIM)
            doa_ref[:, h * LANES:(h + 1) * LANES] = _aug_head(
                do[:, sl], _split3(-delta[:, h * FOX_HEAD_DIM:h * FOX_HEAD_DIM + 1]), _ZEROS3)
        dy = dcat[:, FOX_WIDTH:]
        dyh = _dot_nt(halo_ref[...].astype(_MXU), wo_ref[FOX_WIDTH:, :])
        dyh = jnp.where(i == nt - 1, 0.0, dyh)
        ps = ps_ref[...]
        t = (i * tm + lax.broadcasted_iota(jnp.int32, (tm + HALO, 1), 0) + 1).astype(_F32)
        mixed_t = mx_ref[...]
        dpin, dps = [], []
        for g, w in enumerate(POOL_WINDOWS):
            sl = slice(g * POOL_GROUP_DIM, (g + 1) * POOL_GROUP_DIM)
            mg = mixed_t[:, sl]
            wg = wp_ref[g]
            dps.append(jnp.sum(dy[:, sl] * _dot(mg, wg), axis=0, keepdims=True))
            dyl = (dy[:, sl] * ps[:, sl]).astype(_MXU)
            dylh = (dyh[:, sl] * ps[:, sl]).astype(_MXU)
            dwp_ref[g] += _dot_tn(mg, dyl)
            dmx = _dot_nt(dyl, wg)
            ext = jnp.concatenate([dmx, _dot_nt(dylh, wg)], axis=0) / jnp.minimum(t, float(w))
            dpin.append(_pool_window_sum(ext, w, True)[:tm, :] - dmx)
        dpin_ref[...] = jnp.concatenate(dpin, axis=-1)
        dps_ref[...] += jnp.concatenate(dps, axis=-1)

    return pl.pallas_call(
        body, name="out_proj_bwd", grid=(nt,),
        in_specs=[_rows(tm, D_MODEL),
                  pl.BlockSpec((HALO, D_MODEL), lambda i: (jnp.minimum((i + 1) * hb, T // HALO - 1), 0)),
                  _rows(tm, POOL_WIDTH), _rows(tm, FOX_WIDTH), _resident(bd64.shape), _resident(w_pool.shape),
                  _resident((1, POOL_WIDTH)), _resident(w_out.shape)],
        out_specs=[_rows(tm, FOX_HEADS * LANES), _rows(tm, POOL_WIDTH),
                   pl.BlockSpec(w_pool.shape, lambda i: (0, 0, 0)), pl.BlockSpec((1, POOL_WIDTH), lambda i: (0, 0))],
        out_shape=[_sds((T, FOX_HEADS * LANES), _MXU), _sds((T, POOL_WIDTH), _F32), _sds(w_pool.shape, _F32),
                   _sds((1, POOL_WIDTH), _F32)],
    )(dh1, dh1, mixed, o, bd64, w_pool, pscale, w_out)


def _scatter_behind(pid, t, n_pairs, steps, srcs, gots, send_sems, recv_sems):
    x, y, c = _mesh_pos()
    me = 2 * x + y
    chips = _other_chips(x, y)
    every = [(a, j) for a in range(len(srcs)) for j in range(3)]

    def send(a, j):
        cx, cy = chips[j]
        return _remote(srcs[a].at[2 * cx + cy], gots[a].at[me], send_sems, recv_sems, 3 * a + j, (cx, cy, c))

    def arrival(a, j):
        cx, cy = chips[j]
        slab = gots[a].at[2 * cx + cy]
        return _remote(slab, slab, send_sems, recv_sems, 3 * a + j, (cx, cy, c))

    @pl.when((pid == 0) & (t == 0))
    def _():
        for a, j in every:
            send(a, j).start()

    @pl.when((pid == n_pairs - 1) & (t == steps - 1))
    def _():
        for a, j in every:
            arrival(a, j).wait_recv()
        for a, j in every:
            send(a, j).wait_send()


def _fox_bwd(qa, ka, va, doa, pt, mb, lse, scatter=()):
    T = qa.shape[0]
    tq = min(_TQ, T)
    nq = T // tq
    pair = 2 * FOX_HEAD_DIM
    ii, jj = _causal_steps(nq, False)
    fwd_step = ii * (ii + 1) // 2 + jj
    row_lane, col_lane = FOX_HEAD_DIM, FOX_HEAD_DIM + 3
    n = len(scatter)
    n_pairs = FOX_HEADS // 2
    steps = int(ii.shape[0])

    def body(ii_ref, jj_ref, fs_ref, q_ref, k_ref, v_ref, do_ref, p_ref, mb_ref, lse_ref, *refs):
        srcs, (dq_ref, dk_ref, dv_ref, drs_ref, dcs_ref), gots = refs[:n], refs[n:n + 5], refs[n + 5:2 * n + 5]
        dk_acc, dv_acc = refs[2 * n + 5:2 * n + 7]
        t = pl.program_id(1)
        i = ii_ref[t]
        j = jj_ref[t]
        if n:
            _scatter_behind(pl.program_id(0), t, n_pairs, steps, srcs, gots, *refs[2 * n + 7:])

        @pl.when(t == 0)
        def _():
            dq_ref[...] = jnp.zeros_like(dq_ref)
            drs_ref[...] = jnp.zeros_like(drs_ref)

        @pl.when(i == j)
        def _():
            dk_acc[...] = jnp.zeros_like(dk_acc)
            dv_acc[...] = jnp.zeros_like(dv_acc)

        r = jnp.exp(mb_ref[0, 0] - lse_ref[0])
        dqs, rs = [], []
        for hh in range(2):
            sl = slice(LANES * hh, LANES * (hh + 1))
            dot_t = (do_ref[:, sl].astype(_F32).T * r[hh:hh + 1, :]).astype(_MXU)
            pt_h = p_ref[0, 0, hh]
            dv_acc[hh] += _dot_nt(pt_h, dot_t)
            dst = (pt_h.astype(_F32) * _dot(v_ref[:, sl], dot_t)).astype(_MXU)
            dk_acc[hh] += _dot(dst, q_ref[:, sl])
            dqa = _dot(k_ref[:, sl].astype(_F32).T.astype(_MXU), dst).T
            dqs.append(dqa[:, :FOX_HEAD_DIM] * (FOX_HEAD_DIM ** -0.5))
            rs.append(dqa[:, row_lane:row_lane + 1])
        rows = pl.ds(pl.multiple_of(i * tq, tq), tq)
        dq_ref[rows, :] += jnp.concatenate(dqs, axis=-1)
        drs_ref[0, rows, :] += _two_lanes(rs[0], rs[1])

        @pl.when(i == nq - 1)
        def _():
            dk = [dk_acc[hh] for hh in range(2)]
            dv = [dv_acc[hh] for hh in range(2)]
            dk_ref[...] = jnp.concatenate([dk[0][:, :FOX_HEAD_DIM], dk[1][:, :FOX_HEAD_DIM]], axis=-1)
            dv_ref[...] = jnp.concatenate([dv[0][:, :FOX_HEAD_DIM], dv[1][:, :FOX_HEAD_DIM]], axis=-1)
            dcs_ref[0] = _two_lanes(dk[0][:, col_lane:col_lane + 1], dk[1][:, col_lane:col_lane + 1])

    qspec = pl.BlockSpec((tq, 2 * LANES), lambda p, t, ii, jj, fs: (ii[t], p))
    kspec = pl.BlockSpec((tq, 2 * LANES), lambda p, t, ii, jj, fs: (jj[t], p))
    kout = pl.BlockSpec((tq, pair), lambda p, t, ii, jj, fs: (jj[t], p))
    sems = [pltpu.SemaphoreType.DMA((3 * n,)), pltpu.SemaphoreType.DMA((3 * n,))] if n else []
    res = pl.pallas_call(
        body, name="fox_bwd_scatter" if n else "fox_bwd",
        grid_spec=pltpu.PrefetchScalarGridSpec(
            num_scalar_prefetch=3, grid=(n_pairs, steps),
            in_specs=[qspec, kspec, kspec, qspec,
                      pl.BlockSpec((1, 1, 2, tq, tq), lambda p, t, ii, jj, fs: (p, fs[t], 0, 0, 0)),
                      pl.BlockSpec((1, 1, 8, tq), lambda p, t, ii, jj, fs: (p, fs[t], 0, 0)),
                      pl.BlockSpec((1, 8, tq), lambda p, t, ii, jj, fs: (p, 0, ii[t]))] + [_ANY] * n,
            out_specs=[pl.BlockSpec((T, pair), lambda p, t, ii, jj, fs: (0, p)), kout, kout,
                       pl.BlockSpec((1, T, LANES), lambda p, t, ii, jj, fs: (p, 0, 0)),
                       pl.BlockSpec((1, tq, LANES), lambda p, t, ii, jj, fs: (p, jj[t], 0))] + [_ANY] * n,
            scratch_shapes=[pltpu.VMEM((2, tq, LANES), _F32), pltpu.VMEM((2, tq, LANES), _F32)] + sems),
        out_shape=[_sds((T, FOX_WIDTH), _F32), _sds((T, FOX_WIDTH), _F32), _sds((T, FOX_WIDTH), _F32),
                   _sds((n_pairs, T, LANES), _F32), _sds((n_pairs, T, LANES), _F32)]
        + [_sds(p.shape, p.dtype) for p in scatter],
    )(ii, jj, fwd_step, qa, ka, va, doa, pt, mb, lse, *scatter)
    return res[0], res[1], res[2], res[3], res[4], list(res[5:])


def _mix_in_bwd(dh1, h, g_mix, zqk, dq, dk, dv, dpin, df, gq, gk, bd64, w_main, w_f):
    T = h.shape[0]
    tm = min(_TM, T)
    nt = T // tm

    def body(dh1_ref, h_ref, g_ref, zqk_ref, dq_ref, dk_ref, dv_ref, dpin_ref, df_ref, gq_ref, gk_ref, bd_ref,
             wm_ref, wf_ref, dh_ref, dz_ref, dzf_ref, dgq_ref, dgk_ref, dgain_ref, dbf_ref, dgq_acc, dgk_acc):
        i = pl.program_id(0)

        @pl.when(i == 0)
        def _():
            dgain_ref[...] = jnp.zeros_like(dgain_ref)
            dbf_ref[...] = jnp.zeros_like(dbf_ref)
            dgq_acc[...] = jnp.zeros_like(dgq_acc)
            dgk_acc[...] = jnp.zeros_like(dgk_acc)

        bd = bd_ref[...]
        dqr, dgq = _headnorm_bwd(dq_ref[...], zqk_ref[:, :FOX_WIDTH], gq_ref[...], bd, FOX_HEAD_DIM)
        dkr, dgk = _headnorm_bwd(dk_ref[...], zqk_ref[:, FOX_WIDTH:], gk_ref[...], bd, FOX_HEAD_DIM)
        dgq_acc[...] += dgq
        dgk_acc[...] += dgk
        dz = jnp.concatenate([dqr, dkr, dv_ref[...], dpin_ref[...]], axis=-1).astype(_MXU)
        dz_ref[...] = dz
        df = df_ref[...]
        dzf = df.astype(_MXU)
        dzf_ref[...] = dzf
        dbf_ref[...] += jnp.sum(df, axis=0, keepdims=True)
        dxn = _dot_nt(dz, wm_ref[...]) + _dot_nt(dzf, wf_ref[...])
        dx, dgain = _norm_bwd(dxn, h_ref[...], g_ref[...])
        dh_ref[...] = dh1_ref[...] + dx
        dgain_ref[...] += dgain

        @pl.when(i == nt - 1)
        def _():
            dgq_ref[...] = _fold_heads(dgq_acc[...], FOX_HEADS, FOX_HEAD_DIM)
            dgk_ref[...] = _fold_heads(dgk_acc[...], FOX_HEADS, FOX_HEAD_DIM)

    const2 = lambda shape: pl.BlockSpec(shape, lambda i: (0, 0))
    return pl.pallas_call(
        body, name="mix_in_bwd", grid=(nt,),
        in_specs=[_rows(tm, D_MODEL), _rows(tm, D_MODEL), _resident((1, D_MODEL)), _rows(tm, 2 * FOX_WIDTH),
                  _rows(tm, FOX_WIDTH), _rows(tm, FOX_WIDTH), _rows(tm, FOX_WIDTH), _rows(tm, POOL_WIDTH),
                  _rows(tm, LANES), _resident((1, FOX_WIDTH)), _resident((1, FOX_WIDTH)), _resident(bd64.shape),
                  _resident(w_main.shape), _resident(w_f.shape)],
        out_specs=[_rows(tm, D_MODEL), _rows(tm, 4 * FOX_WIDTH), _rows(tm, LANES), const2((1, FOX_HEAD_DIM)),
                   const2((1, FOX_HEAD_DIM)), const2((1, D_MODEL)), const2((1, LANES))],
        out_shape=[_sds((T, D_MODEL), _F32), _sds((T, 4 * FOX_WIDTH), _MXU), _sds((T, LANES), _MXU),
                   _sds((1, FOX_HEAD_DIM), _F32), _sds((1, FOX_HEAD_DIM), _F32), _sds((1, D_MODEL), _F32),
                   _sds((1, LANES), _F32)],
        scratch_shapes=[pltpu.VMEM((1, FOX_WIDTH), _F32), pltpu.VMEM((1, FOX_WIDTH), _F32)],
    )(dh1, h, g_mix, zqk, dq, dk, dv, dpin, df, gq, gk, bd64, w_main, w_f)


CLASSES = (("a", D_MODEL, ("w_out", "w_mem_kv")), ("d", D_MODEL, ("w_down",)), ("g", _TF, ("w_gate_up",)),
           ("i", 514, ("w_in",)), ("q", MEM_WIDTH, ("w_mem_q",)), ("o", 256, ("w_mem_out",)))
W_OUT_ROWS = D_MODEL // N_CHIPS
W_DOWN_ROWS = D_FF // N_CHIPS
W_MEM_OUT_COLS = D_MODEL // N_CHIPS


def _late_params(WC, l):
    rows = lambda buf, first, n: buf[:, first:first + n]
    return dict(
        w_out=rows(WC["a"], l * W_OUT_ROWS, W_OUT_ROWS).reshape(D_MODEL, D_MODEL),
        w_mem_q=rows(WC["q"], l * W_OUT_ROWS, W_OUT_ROWS).reshape(D_MODEL, MEM_WIDTH),
        w_mem_kv=rows(WC["a"], (DEPTH + l) * W_OUT_ROWS, W_OUT_ROWS).reshape(D_MODEL, 2 * MEM_WIDTH),
        w_mem_out=rows(WC["o"], l * MEM_WIDTH, MEM_WIDTH).transpose(1, 0, 2).reshape(MEM_WIDTH, D_MODEL),
        w_gu=WC["g"],
        w_d=WC["d"],
    )


def _layer_params(WC, WS, l):
    rows = lambda buf, first, n: buf[:, first:first + n]
    w_in = rows(WC["i"], l * D_MODEL, D_MODEL).transpose(1, 0, 2).reshape(D_MODEL, -1)
    n_main = 3 * FOX_WIDTH + POOL_WIDTH
    row = lambda a: a.reshape(1, -1).astype(_F32)
    return dict(
        layer=l,
        g_mix=row(WS["g_mix"][l]),
        w_main=w_in[:, :n_main],
        w_f=jnp.pad(w_in[:, n_main:], ((0, 0), (0, LANES - FOX_HEADS))),
        b_f=jnp.pad(row(WS["b_forget"][l]), ((0, 0), (0, LANES - FOX_HEADS))),
        gq=jnp.tile(row(WS["g_q_fox"][l]), (1, FOX_HEADS)),
        gk=jnp.tile(row(WS["g_k_fox"][l]), (1, FOX_HEADS)),
        w_pool=WS["w_pool"][l].astype(_MXU),
        pscale=row(WS["pool_scale"][l]),
        g_mem_q=row(WS["g_mem_q"][l]),
        g_mem_kv=row(WS["g_mem_kv"][l]),
        gqm=jnp.tile(row(WS["g_q_mem"][l]), (1, MEM_HEADS)),
        gkm=jnp.tile(row(WS["g_k_mem"][l]), (1, MEM_HEADS)),
        g_ffn=row(WS["g_ffn"][l]),
    )


def _layer_fwd(h, mem, P, bd64, bd128, pending=None):
    s = dict(h=h)
    s["xn"], s["zqk"], qn, kn, v, pin, s["fl"] = _mix_in_fwd(
        h, P["g_mix"], P["w_main"], P["w_f"], P["b_f"], P["gq"], P["gk"], bd64)
    s["qa"], s["ka"], s["va"] = _gate_fwd(s["fl"], qn, kn, v)
    s["o"], s["pt"], s["mb"], s["lse"], landed = _fox_fwd(s["qa"], s["ka"], s["va"], pending[0] if pending else ())
    if pending:
        pending[1](landed)
    s["h1"], s["mixed"], s["y"] = _out_proj_fwd(h, s["o"], pin, P["w_pool"], P["pscale"], P["w_out"])
    s["mn"], s["mkv"], s["mk"], s["mv"] = _mem_kv_fwd(mem, P["g_mem_kv"], P["w_mem_kv"], P["gkm"], bd128)
    s["h2"], s["hn_mem"], s["mo"] = _mem_attn_fwd(s["h1"], P["g_mem_q"], P["w_mem_q"], P["gqm"], bd128, s["mk"],
                                                  s["mv"], P["w_mem_out"])
    h3, s["hn_ffn"] = _ffn_fwd(s["h2"], P["g_ffn"], P["w_gu"], P["w_d"], P["layer"])
    return h3, s


def _layer_bwd(dh3, mem, P, s, bd64, bd128, G, shapes, early=None):
    l = P["layer"]
    g = {}

    def into(c, a, b, name, tk, tn, index, chips=1):
        G[c] = _matmul_tn(a, b, name, tk=tk, tn=tn, dst=G.get(c), dst_shape=shapes[c], dst_index=index,
                          dst_chips=chips)

    dh2, act, dgate, dup, g["g_ffn"] = _ffn_bwd(dh3, s["h2"], s["hn_ffn"], P["g_ffn"], P["w_gu"], P["w_d"], l)
    into("d", act, dh3, "dw_down", 2 * W_DOWN_ROWS, D_MODEL, lambda i, j: (i, l, 0), chips=2)
    into("g", s["hn_ffn"], dgate, "dw_gate", D_MODEL, _TF, lambda i, j: (j, l, 0))
    into("g", s["hn_ffn"], dup, "dw_up", D_MODEL, _TF, lambda i, j: (j + D_FF // _TF, l, 0))

    dh1, dmq, dmk, dmv, g["g_q_mem"], g["g_mem_q"] = _mem_attn_bwd(
        dh2, s["h1"], P["g_mem_q"], P["w_mem_q"], P["gqm"], bd128, s["mk"], s["mv"], P["w_mem_out"])
    into("o", s["mo"], dh2, "dw_mem_out", MEM_WIDTH, W_MEM_OUT_COLS, lambda i, j: (j, l, 0))
    into("q", s["hn_mem"], dmq, "dw_mem_q", W_OUT_ROWS, MEM_WIDTH, lambda i, j: (i, l, 0))
    G["a"], g["g_k_mem"], g["g_mem_kv"] = _mem_kv_bwd(dmk, dmv, s["mkv"], s["mn"], mem, P["g_mem_kv"], P["gkm"],
                                                      bd128, P["w_mem_kv"], G.get("a"), shapes["a"], DEPTH + l)

    doa, dpin, g["w_pool"], g["pool_scale"] = _out_proj_bwd(dh1, s["mixed"], s["o"], bd64, P["w_pool"], P["pscale"],
                                                            P["w_out"])
    half = FOX_WIDTH // W_OUT_ROWS
    into("a", s["o"], dh1, "dw_out_fox", W_OUT_ROWS, D_MODEL, lambda i, j: (i, l, 0))
    into("a", s["y"], dh1, "dw_out_pool", W_OUT_ROWS, D_MODEL, lambda i, j: (i + half, l, 0))
    sent = early(G) if early else ()
    dq, dk, dv, drs, dcs, landed = _fox_bwd(s["qa"], s["ka"], s["va"], doa, s["pt"], s["mb"], s["lse"], sent)
    df = _gate_bwd(drs, dcs, s["fl"])
    dh, dz, dzf, g["g_q_fox"], g["g_k_fox"], g["g_mix"], dbf = _mix_in_bwd(
        dh1, s["h"], P["g_mix"], s["zqk"], dq, dk, dv, dpin, df, P["gq"], P["gk"], bd64, P["w_main"], P["w_f"])
    g["b_forget"] = dbf[:, :FOX_HEADS]
    dw_in = jnp.concatenate([_matmul_tn(s["xn"], dz, "dw_in_main"),
                             _matmul_tn(s["xn"], dzf, "dw_in_gate")[:, :FOX_HEADS]], axis=1)
    return dh, G, g, _to_shards(dw_in, 1), (sent, landed)


def _device_grads(x, mem, tgt, WC, WS, pending=None, early=None):
    bd64 = _blockdiag_ones(FOX_HEADS, FOX_HEAD_DIM)
    bd128 = _blockdiag_ones(MEM_HEADS, MEM_HEAD_DIM)
    params = [_layer_params(WC, WS, l) for l in range(DEPTH)]
    shapes = {c: WC[c].shape for c in WC}

    def complete(buffers):
        for l in range(DEPTH):
            params[l].update(_late_params(buffers, l))

    first = None
    if pending is None:
        complete(WC)
    else:
        slabs, place = pending
        late = list(slabs)
        shapes.update({c: (N_CHIPS,) + slabs[c].shape for c in late})
        first = ([slabs[c] for c in late],
                 lambda landed: complete({c: place(buf, slabs[c]) for c, buf in zip(late, landed)}))
    h, saved = x, []
    for l in range(DEPTH):
        h, s = _layer_fwd(h, mem, params[l], bd64, bd128, first if l == 0 else None)
        saved.append(s)
    dh, loss = _loss_grad(h, tgt)
    G, small, dw_in = {}, [None] * DEPTH, [None] * DEPTH
    for l in reversed(range(DEPTH)):
        dh, G, small[l], dw_in[l], scattered = _layer_bwd(dh, mem, params[l], saved[l], bd64, bd128, G, shapes,
                                                          early if l == 0 else None)
    G["i"] = jnp.concatenate(dw_in, axis=1)
    gsmall = {n: jnp.stack([small[l][n].reshape(WS[n].shape[1:]) for l in range(DEPTH)]) for n in SMALL}
    return loss, dh, G, gsmall, scattered


def _class_slabs(shards):
    return {c: jnp.concatenate([shards[n].reshape(-1, width) for n in names]) for c, width, names in CLASSES}


def _class_rows(shards):
    where = {}
    for c, width, names in CLASSES:
        off = 0
        for n in names:
            rows = shards[n].shape[0] * shards[n].shape[1]
            where[n] = (c, off, rows)
            off += rows
    return where


def _to_shards(full, ax):
    shp = full.shape
    parts = full.reshape(shp[:ax] + (N_CHIPS, shp[ax] // N_CHIPS) + shp[ax + 1:])
    return jnp.moveaxis(parts, ax, 0)


def _from_shards(parts, ax):
    full = jnp.moveaxis(parts, 0, ax)
    shp = full.shape
    return full.reshape(shp[:ax] + (shp[ax] * shp[ax + 1],) + shp[ax + 2:])


def _pack_small(arrs):
    flat = jnp.concatenate([a.reshape(-1).astype(_F32) for a in arrs])
    rows = -(-flat.shape[0] // (8 * D_MODEL)) * 8
    return jnp.pad(flat, (0, rows * D_MODEL - flat.shape[0])).reshape(rows, D_MODEL)


def _unpack_small(flat, shapes):
    flat = flat.reshape(-1)
    out, off = [], 0
    for shp in shapes:
        n = 1
        for d in shp:
            n *= d
        out.append(flat[off:off + n].reshape(shp))
        off += n
    return out


def _mesh_pos():
    return lax.axis_index("x"), lax.axis_index("y"), lax.axis_index("c")


def _other_chips(x, y):
    return [(1 - x, y), (x, 1 - y), (1 - x, 1 - y)]


def _half_rows(ref_rows, which):
    half = ref_rows // 2
    return pl.ds(pl.multiple_of(which * half, 16), half)


def _remote(src_ref, dst_ref, send_sems, recv_sems, k, to):
    return pltpu.make_async_remote_copy(src_ref=src_ref, dst_ref=dst_ref, send_sem=send_sems.at[k],
                                        recv_sem=recv_sems.at[k], device_id=to, device_id_type=MESH)


def _allgather_weights(slabs):
    n = len(slabs)

    def body(*refs):
        srcs, outs, send_sems, recv_sems = refs[:n], refs[n:2 * n], refs[2 * n], refs[2 * n + 1]
        x, y, c = _mesh_pos()
        me = 2 * x + y
        sibling = (x, y, 1 - c)
        chips = _other_chips(x, y)
        first, passed = [], []
        for a, (src, out) in enumerate(zip(srcs, outs)):
            mine = _half_rows(src.shape[0], c)
            for j, chip in enumerate(chips):
                cp = _remote(src.at[mine], out.at[me, mine], send_sems, recv_sems, 6 * a + j, (*chip, c))
                cp.start()
                first.append(cp)
        for a, out in enumerate(outs):
            mine = _half_rows(out.shape[1], c)
            for j, (cx, cy) in enumerate(chips):
                slab = out.at[2 * cx + cy, mine]
                _remote(slab, slab, send_sems, recv_sems, 6 * a + j, (cx, cy, c)).wait_recv()
                fwd = _remote(slab, slab, send_sems, recv_sems, 6 * a + 3 + j, sibling)
                fwd.start()
                passed.append(fwd)
        for a, out in enumerate(outs):
            theirs = _half_rows(out.shape[1], 1 - c)
            for j, (cx, cy) in enumerate(chips):
                slab = out.at[2 * cx + cy, theirs]
                _remote(slab, slab, send_sems, recv_sems, 6 * a + 3 + j, sibling).wait_recv()
        for cp in first + passed:
            cp.wait_send()

    return pl.pallas_call(
        body, name="allgather_weights", in_specs=[_ANY] * n, out_specs=[_ANY] * n,
        out_shape=[_sds((N_CHIPS,) + s.shape, s.dtype) for s in slabs],
        scratch_shapes=[pltpu.SemaphoreType.DMA((6 * n,)), pltpu.SemaphoreType.DMA((6 * n,))],
    )(*slabs)


def _exchange_halves(grads, name):
    n = len(grads)

    def body(*refs):
        srcs, gots, send_sems, recv_sems = refs[:n], refs[n:2 * n], refs[2 * n], refs[2 * n + 1]
        x, y, c = _mesh_pos()
        copies = []
        for a, (src, got) in enumerate(zip(srcs, gots)):
            theirs = _half_rows(src.shape[1], 1 - c)
            for k in range(N_CHIPS):
                cp = _remote(src.at[k, theirs], got.at[k], send_sems, recv_sems, N_CHIPS * a + k, (x, y, 1 - c))
                cp.start()
                copies.append(cp)
        for cp in copies:
            cp.wait()

    return pl.pallas_call(
        body, name=name, in_specs=[_ANY] * n, out_specs=[_ANY] * n,
        out_shape=[_sds((N_CHIPS, g.shape[1] // 2, g.shape[2]), _F32) for g in grads],
        scratch_shapes=[pltpu.SemaphoreType.DMA((N_CHIPS * n,)), pltpu.SemaphoreType.DMA((N_CHIPS * n,))],
    )(*grads)


def _row_tile(rows):
    return _pick_tile(rows, (704, 512, 256))


def _add_halves(g, got, c_idx, name):
    _, half, width = got.shape
    ta = _row_tile(half)
    nb = half // ta

    def body(c_ref, a_ref, b_ref, o_ref):
        o_ref[...] = (a_ref[...] + b_ref[...]).astype(jnp.bfloat16)

    return pl.pallas_call(
        body, name=name,
        grid_spec=pltpu.PrefetchScalarGridSpec(
            num_scalar_prefetch=1, grid=(N_CHIPS, nb),
            in_specs=[pl.BlockSpec((1, ta, width), lambda k, i, c: (k, c[0] * nb + i, 0)),
                      pl.BlockSpec((1, ta, width), lambda k, i, c: (k, i, 0))],
            out_specs=pl.BlockSpec((1, ta, width), lambda k, i, c: (k, i, 0))),
        out_shape=_sds(got.shape, jnp.bfloat16),
    )(c_idx, g, got)


def _scatter_to_chips(parts):
    n = len(parts)

    def body(*refs):
        srcs, gots, send_sems, recv_sems = refs[:n], refs[n:2 * n], refs[2 * n], refs[2 * n + 1]
        x, y, c = _mesh_pos()
        me = 2 * x + y
        chips = _other_chips(x, y)
        copies = []
        for a, (src, got) in enumerate(zip(srcs, gots)):
            for j, (cx, cy) in enumerate(chips):
                cp = _remote(src.at[2 * cx + cy], got.at[me], send_sems, recv_sems, 3 * a + j, (cx, cy, c))
                cp.start()
                copies.append(cp)
        for a, got in enumerate(gots):
            for j, (cx, cy) in enumerate(chips):
                slab = got.at[2 * cx + cy]
                _remote(slab, slab, send_sems, recv_sems, 3 * a + j, (cx, cy, c)).wait_recv()
        for cp in copies:
            cp.wait_send()

    return pl.pallas_call(
        body, name="grad_scatter_chips", in_specs=[_ANY] * n, out_specs=[_ANY] * n,
        out_shape=[_sds(p.shape, p.dtype) for p in parts],
        scratch_shapes=[pltpu.SemaphoreType.DMA((3 * n,)), pltpu.SemaphoreType.DMA((3 * n,))],
    )(*parts)


def _sum_chips(got, c_idx, name):
    _, half, width = got.shape
    ta = _row_tile(half)
    nb = half // ta

    def body(c_ref, a_ref, o_ref):
        f = lambda k: a_ref[k].astype(_F32)
        o_ref[...] = ((f(0) + f(1)) + f(2)) + f(3)

    return pl.pallas_call(
        body, name=name,
        grid_spec=pltpu.PrefetchScalarGridSpec(
            num_scalar_prefetch=1, grid=(nb,),
            in_specs=[pl.BlockSpec((N_CHIPS, ta, width), lambda i, c: (0, i, 0))],
            out_specs=pl.BlockSpec((ta, width), lambda i, c: (c[0] * nb + i, 0))),
        out_shape=_sds((2 * half, width), _F32),
    )(c_idx, got)


def _share_with_sibling(bufs):
    n = len(bufs)

    def body(*refs):
        outs, send_sems, recv_sems = refs[n:2 * n], refs[2 * n], refs[2 * n + 1]
        x, y, c = _mesh_pos()
        copies = []
        for a, out in enumerate(outs):
            mine = out.at[_half_rows(out.shape[0], c)]
            cp = _remote(mine, mine, send_sems, recv_sems, a, (x, y, 1 - c))
            cp.start()
            copies.append(cp)
        for a, out in enumerate(outs):
            theirs = out.at[_half_rows(out.shape[0], 1 - c)]
            _remote(theirs, theirs, send_sems, recv_sems, a, (x, y, 1 - c)).wait_recv()
        for cp in copies:
            cp.wait_send()

    return pl.pallas_call(
        body, name="grad_share_sibling", in_specs=[_ANY] * n, out_specs=[_ANY] * n,
        out_shape=[_sds(b.shape, _F32) for b in bufs], input_output_aliases={a: a for a in range(n)},
        scratch_shapes=[pltpu.SemaphoreType.DMA((n,)), pltpu.SemaphoreType.DMA((n,))],
    )(*bufs)


def _allreduce_small(g):
    rows = g.shape[0]
    n_dev = 2 * N_CHIPS

    def body(g_ref, out_ref, gathered, local_sem, send_sems, recv_sems):
        x, y, c = _mesh_pos()
        me = 4 * x + 2 * y + c
        own = pltpu.make_async_copy(g_ref, gathered.at[me], local_sem)
        own.start()
        copies = []
        for k in range(1, n_dev):
            fx, fy, fc = (k >> 2) & 1, (k >> 1) & 1, k & 1
            cp = pltpu.make_async_remote_copy(
                src_ref=g_ref, dst_ref=gathered.at[me], send_sem=send_sems.at[k - 1], recv_sem=recv_sems.at[k - 1],
                device_id=(x ^ fx, y ^ fy, c ^ fc), device_id_type=MESH)
            cp.start()
            copies.append(cp)
        for k in range(1, n_dev):
            fx, fy, fc = (k >> 2) & 1, (k >> 1) & 1, k & 1
            px, py, pc = x ^ fx, y ^ fy, c ^ fc
            slab = gathered.at[4 * px + 2 * py + pc]
            pltpu.make_async_remote_copy(src_ref=slab, dst_ref=slab, send_sem=send_sems.at[k - 1],
                                         recv_sem=recv_sems.at[k - 1], device_id=(px, py, pc),
                                         device_id_type=MESH).wait_recv()
        for cp in copies:
            cp.wait_send()
        own.wait()
        acc = gathered[0]
        for d in range(1, n_dev):
            acc = acc + gathered[d]
        out_ref[...] = acc

    vmem = pl.BlockSpec(memory_space=pltpu.VMEM)
    return pl.pallas_call(
        body, name="allreduce_small", in_specs=[vmem], out_specs=vmem, out_shape=_sds((rows, D_MODEL), _F32),
        scratch_shapes=[pltpu.VMEM((n_dev, rows, D_MODEL), _F32), pltpu.SemaphoreType.DMA,
                        pltpu.SemaphoreType.DMA((n_dev - 1,)), pltpu.SemaphoreType.DMA((n_dev - 1,))],
    )(g)


def _adamw(w, g, m, v, name, g_first_row=0):
    shape = w.shape
    cols = shape[-1]
    rows = 1
    for d in shape[:-1]:
        rows *= d
    w2, m2, v2 = (a.reshape(rows, cols) for a in (w, m, v))
    tr = _pick_tile(rows, (256, 128, 64, 32, 16, 8))
    g0 = g_first_row // tr

    def body(w_ref, g_ref, m_ref, v_ref, go_ref, d_ref, nm_ref, nv_ref):
        gg = g_ref[...]
        go_ref[...] = gg
        nm = ADAM_B1 * m_ref[...] + (1.0 - ADAM_B1) * gg
        nv = ADAM_B2 * v_ref[...] + (1.0 - ADAM_B2) * (gg * gg)
        m_hat = nm / (1.0 - ADAM_B1 ** ADAM_STEP)
        v_hat = nv / (1.0 - ADAM_B2 ** ADAM_STEP)
        d_ref[...] = -ADAM_LR * (m_hat / (jnp.sqrt(v_hat) + ADAM_EPS) + ADAM_WD * w_ref[...])
        nm_ref[...] = nm
        nv_ref[...] = nv

    spec = pl.BlockSpec((tr, cols), lambda i: (i, 0))
    outs = pl.pallas_call(
        body, name=name, grid=(rows // tr,),
        in_specs=[spec, pl.BlockSpec((tr, cols), lambda i: (g0 + i, 0)), spec, spec], out_specs=[spec] * 4,
        out_shape=[_sds((rows, cols), _F32)] * 4,
    )(w2, g, m2, v2)
    return tuple(o.reshape(shape) for o in outs)


def kernel(x, mem, g_mix, w_in, b_forget, g_q_fox, g_k_fox, w_pool, pool_scale, w_out, g_mem_q, g_mem_kv, w_mem_q, w_mem_kv, g_q_mem, g_k_mem, w_mem_out, g_ffn, w_gate_up, w_down, loss_target, m_g_mix, m_w_in, m_b_forget, m_g_q_fox, m_g_k_fox, m_w_pool, m_pool_scale, m_w_out, m_g_mem_q, m_g_mem_kv, m_w_mem_q, m_w_mem_kv, m_g_q_mem, m_g_k_mem, m_w_mem_out, m_g_ffn, m_w_gate_up, m_w_down, v_g_mix, v_w_in, v_b_forget, v_g_q_fox, v_g_k_fox, v_w_pool, v_pool_scale, v_w_out, v_g_mem_q, v_g_mem_kv, v_w_mem_q, v_w_mem_kv, v_g_q_mem, v_g_k_mem, v_w_mem_out, v_g_ffn, v_w_gate_up, v_w_down):
    w = dict(g_mix=g_mix, w_in=w_in, b_forget=b_forget, g_q_fox=g_q_fox, g_k_fox=g_k_fox, w_pool=w_pool,
             pool_scale=pool_scale, w_out=w_out, g_mem_q=g_mem_q, g_mem_kv=g_mem_kv, w_mem_q=w_mem_q,
             w_mem_kv=w_mem_kv, g_q_mem=g_q_mem, g_k_mem=g_k_mem, w_mem_out=w_mem_out, g_ffn=g_ffn,
             w_gate_up=w_gate_up, w_down=w_down)
    m = dict(g_mix=m_g_mix, w_in=m_w_in, b_forget=m_b_forget, g_q_fox=m_g_q_fox, g_k_fox=m_g_k_fox, w_pool=m_w_pool,
             pool_scale=m_pool_scale, w_out=m_w_out, g_mem_q=m_g_mem_q, g_mem_kv=m_g_mem_kv, w_mem_q=m_w_mem_q,
             w_mem_kv=m_w_mem_kv, g_q_mem=m_g_q_mem, g_k_mem=m_g_k_mem, w_mem_out=m_w_mem_out, g_ffn=m_g_ffn,
             w_gate_up=m_w_gate_up, w_down=m_w_down)
    v = dict(g_mix=v_g_mix, w_in=v_w_in, b_forget=v_b_forget, g_q_fox=v_g_q_fox, g_k_fox=v_g_k_fox, w_pool=v_w_pool,
             pool_scale=v_pool_scale, w_out=v_w_out, g_mem_q=v_g_mem_q, g_mem_kv=v_g_mem_kv, w_mem_q=v_w_mem_q,
             w_mem_kv=v_w_mem_kv, g_q_mem=v_g_q_mem, g_k_mem=v_g_k_mem, w_mem_out=v_w_mem_out, g_ffn=v_g_ffn,
             w_gate_up=v_w_gate_up, w_down=v_w_down)

    classes = [c for c, _, _ in CLASSES]
    chip = 2 * lax.axis_index("x") + lax.axis_index("y")
    c_idx = lax.axis_index("c").astype(jnp.int32).reshape(1)
    own_slab = lambda bufs, own: lax.dynamic_update_slice(bufs, own[None], (chip, 0, 0))

    slabs = _class_slabs({n: w[n].astype(_MXU) for n in BIG})
    WC = {"i": own_slab(_allgather_weights([slabs["i"]])[0], slabs["i"])}
    later = ({c: slabs[c] for c in classes if c != "i"}, own_slab)

    def chip_sums(names, tag):
        grads = [G[c] for c in names]
        return [_add_halves(g, got, c_idx, "grad_add_halves_" + c)
                for c, g, got in zip(names, grads, _exchange_halves(grads, "grad_exchange_halves" + tag))]

    early = [c for c in classes if c != "i"]
    G = {}

    def send_early(buffers):
        G.update(buffers)
        return chip_sums(early, "")

    loss, grad_x, grads, gsmall, (sent, landed) = _device_grads(
        x[0], mem[0], loss_target[0], WC, {n: w[n] for n in SMALL}, later, send_early)
    loss = lax.psum(loss[0, 0], ("x", "y", "c"))
    G.update(grads)
    sent_i = chip_sums(["i"], "_w_in")
    partial = dict(zip(early + ["i"], list(sent) + sent_i))
    arrived = dict(zip(early + ["i"], list(landed) + list(_scatter_to_chips(sent_i))))
    partials = [own_slab(arrived[c], lax.dynamic_index_in_dim(partial[c], chip, 0, keepdims=False)) for c in classes]
    reduced = _share_with_sibling([_sum_chips(p, c_idx, "grad_sum_chips_" + c) for c, p in zip(classes, partials)])
    reduced = dict(zip(classes, reduced))

    small_shapes = [w[n].shape for n in SMALL]
    gsmall = _unpack_small(_allreduce_small(_pack_small([gsmall[n] for n in SMALL])), small_shapes)

    g_out, d_out, m_out, v_out = {}, {}, {}, {}
    for n, (c, first, _) in _class_rows({n: w[n] for n in BIG}).items():
        g_out[n], d_out[n], m_out[n], v_out[n] = _adamw(w[n], reduced[c], m[n], v[n], "adamw_" + n, first)
    packed = [_pack_small([t[n] for n in SMALL]) for t in (w, m, v)]
    _, ds, ms, vs = _adamw(packed[0], _pack_small(gsmall), packed[1], packed[2], "adamw_small")
    for n, gi, di, mi, vi in zip(SMALL, gsmall, _unpack_small(ds, small_shapes), _unpack_small(ms, small_shapes),
                                 _unpack_small(vs, small_shapes)):
        g_out[n], d_out[n], m_out[n], v_out[n] = gi, di, mi, vi

    return (loss, grad_x[None], *[g_out[n] for n in WEIGHTS], *[d_out[n] for n in WEIGHTS],
            *[m_out[n] for n in WEIGHTS], *[v_out[n] for n in WEIGHTS])
```
